```python
import jax, jax.numpy as jnp
from jax import lax
import numpy as np

D_MODEL = 1024
BATCH = 32
SEQ = 2048
DEPTH = 2

N_MIXERS = 2
N_A_LAYERS = (DEPTH + 1) // 2
N_B_LAYERS = DEPTH // 2
RMS_EPS = 1e-6
LRU_WIDTH = D_MODEL
LRU_HEADS = 4
LRU_BLOCK = LRU_WIDTH // LRU_HEADS
CONV_WIDTH = 4
LRU_C = 8.0
RWKV_HEAD = 64
RWKV_HEADS = D_MODEL // RWKV_HEAD
DECAY_LORA = 64
AAA_LORA = 64
GATE_LORA = 128
RWKV_GN_EPS = 64e-5
MEM_LEN = 256
MEM_HEADS = 4
MEM_HEAD_DIM = D_MODEL // MEM_HEADS
D_FF = 4 * D_MODEL

kernel_name = "hybrid_rglru_rwkv7_memxattn"


def rms_norm(x, g):
    xf = x.astype(jnp.float32)
    y = xf * lax.rsqrt(jnp.mean(xf * xf, axis=-1, keepdims=True) + RMS_EPS)
    return (y * g.astype(jnp.float32)).astype(x.dtype)


def _lru_combine(c1, c2):
    a1, b1 = c1
    a2, b2 = c2
    return a1 * a2, a2 * b1 + b2


def rglru_mixer(x, conv_w, conv_b, w_in, b_in, gate_w, gate_b, lam, w_out, b_out):
    B, S, _ = x.shape
    proj = x @ w_in + b_in
    y_branch, u = jnp.split(proj, 2, axis=-1)
    y_branch = jax.nn.gelu(y_branch, approximate=True)
    u_pad = jnp.pad(u, ((0, 0), (CONV_WIDTH - 1, 0), (0, 0)))
    conv = conv_b + u_pad[:, 0:S] * conv_w[0]
    for tap in range(1, CONV_WIDTH):
        conv = conv + u_pad[:, tap:tap + S] * conv_w[tap]
    ub = conv.reshape(B, S, LRU_HEADS, LRU_BLOCK)
    gates = jax.nn.sigmoid(jnp.einsum('bshi,ghij->gbshj', ub, gate_w) + gate_b[:, None, None])
    r_gate = gates[0].reshape(B, S, LRU_WIDTH).astype(jnp.float32)
    i_gate = gates[1].reshape(B, S, LRU_WIDTH).astype(jnp.float32)
    log_a = -LRU_C * r_gate * jax.nn.softplus(-lam.astype(jnp.float32))
    a = jnp.exp(log_a)
    mult = jnp.sqrt(-jnp.expm1(2.0 * log_a))
    b = mult * i_gate * conv.astype(jnp.float32)
    _, h = lax.associative_scan(_lru_combine, (a, b), axis=1)
    return (h.astype(x.dtype) * y_branch) @ w_out + b_out


def rwkv7_mixer(x, mu, w_rkv, w0, w1, w2, a0, a1, a2, g1, g2, k_k, k_a, r_k, gn_g, gn_b, w_o):
    B, S, D = x.shape
    H, N = RWKV_HEADS, RWKV_HEAD
    x_prev = jnp.pad(x, ((0, 0), (1, 0), (0, 0)))[:, :S]
    xx = x_prev - x
    r = (x + xx * mu[0]) @ w_rkv[0]
    xw = x + xx * mu[1]
    k = (x + xx * mu[2]) @ w_rkv[1]
    v = (x + xx * mu[3]) @ w_rkv[2]
    xa = x + xx * mu[4]
    xg = x + xx * mu[5]
    w_log = -jax.nn.softplus(-(w0 + jnp.tanh(xw @ w1) @ w2).astype(jnp.float32)) - 0.5
    decay = jnp.exp(-jnp.exp(w_log))
    a = jax.nn.sigmoid(a0 + (xa @ a1) @ a2)
    g = jax.nn.sigmoid(xg @ g1) @ g2
    kk = (k * k_k).reshape(B, S, H, N).astype(jnp.float32)
    kk = kk / jnp.maximum(jnp.linalg.norm(kk, axis=-1, keepdims=True), 1e-12)
    k = k * (1.0 + (a - 1.0) * k_a)

    rh = r.reshape(B, S, H, N).astype(jnp.float32)
    kh = k.reshape(B, S, H, N).astype(jnp.float32)
    vh = v.reshape(B, S, H, N).astype(jnp.float32)
    wh = decay.reshape(B, S, H, N)
    ah = a.reshape(B, S, H, N).astype(jnp.float32)
    rem_a = -kk
    rem_b = kk * ah

    def step(state, inp):
        r_t, w_t, k_t, v_t, a_t, b_t = inp
        sa = jnp.einsum('bhij,bhj->bhi', state, a_t)
        state = (state * w_t[:, :, None, :] + sa[..., None] * b_t[:, :, None, :]
                 + v_t[..., None] * k_t[:, :, None, :])
        y_t = jnp.einsum('bhij,bhj->bhi', state, r_t)
        return state, y_t

    seq_inputs = tuple(jnp.moveaxis(t, 1, 0) for t in (rh, wh, kh, vh, rem_a, rem_b))
    state0 = jnp.zeros((B, H, N, N), jnp.float32)
    _, ys = lax.scan(step, state0, seq_inputs)
    y = jnp.moveaxis(ys, 0, 1)
    mean = jnp.mean(y, axis=-1, keepdims=True)
    var = jnp.mean(jnp.square(y - mean), axis=-1, keepdims=True)
    yn = ((y - mean) * lax.rsqrt(var + RWKV_GN_EPS)).reshape(B, S, D)
    yn = yn * gn_g.astype(jnp.float32) + gn_b.astype(jnp.float32)
    bonus = jnp.sum(rh * kh * r_k.astype(jnp.float32), axis=-1, keepdims=True) * vh
    out = (yn + bonus.reshape(B, S, D)).astype(x.dtype)
    return (out * g) @ w_o


def mem_cross_attention(h, mem_n, w_q, w_kv, w_o):
    B, S, D = h.shape
    M = mem_n.shape[1]
    q = (h @ w_q).reshape(B, S, MEM_HEADS, MEM_HEAD_DIM)
    kv = (mem_n @ w_kv).reshape(B, M, 2, MEM_HEADS, MEM_HEAD_DIM)
    k, v = kv[:, :, 0], kv[:, :, 1]
    s = jnp.einsum('bqhd,bkhd->bhqk', q, k).astype(jnp.float32) * (MEM_HEAD_DIM ** -0.5)
    p = jax.nn.softmax(s, axis=-1).astype(h.dtype)
    o = jnp.einsum('bhqk,bkhd->bqhd', p, v).reshape(B, S, D)
    return o @ w_o


def sqrelu_mlp(h, w_up, w_down):
    return jnp.square(jax.nn.relu(h @ w_up)) @ w_down


def _fwd_setup_inputs(seed: int = 0) -> dict:
    key = jax.random.key(seed)
    ks = iter(jax.random.split(key, 48))
    f32 = jnp.float32

    def nrm(shape, scale):
        return jax.random.normal(next(ks), shape, f32) * scale

    def unif(shape, lo, hi):
        return jax.random.uniform(next(ks), shape, f32, lo, hi)

    D, NA, NB = D_MODEL, N_A_LAYERS, N_B_LAYERS
    x = nrm((BATCH, SEQ, D), 1.0)
    mem = nrm((BATCH, MEM_LEN, D), 1.0)
    ln_gains = 1.0 + nrm((DEPTH, 6, D), 0.05)
    mem_norm = 1.0 + nrm((D,), 0.05)
    a_conv_w = nrm((NA, CONV_WIDTH, LRU_WIDTH), CONV_WIDTH ** -0.5)
    a_conv_b = nrm((NA, LRU_WIDTH), 0.01)
    a_w_in = nrm((NA, D, 2 * LRU_WIDTH), D ** -0.5)
    a_b_in = nrm((NA, 2 * LRU_WIDTH), 0.01)
    a_gate_w = nrm((NA, 2, LRU_HEADS, LRU_BLOCK, LRU_BLOCK), LRU_BLOCK ** -0.5)
    a_gate_b = nrm((NA, 2, LRU_HEADS, LRU_BLOCK), 0.01)
    u = unif((NA, LRU_WIDTH), 0.81, 0.998)
    sp = -0.5 * jnp.log(u)
    a_lambda = -jnp.log(jnp.expm1(sp))
    a_w_out = nrm((NA, LRU_WIDTH, D), LRU_WIDTH ** -0.5)
    a_b_out = nrm((NA, D), 0.01)
    b_mu = unif((NB, 6, D), 0.0, 1.0)
    b_w_rkv = nrm((NB, 3, D, D), D ** -0.5)
    b_w0 = unif((NB, D), -6.0, -1.0)
    b_w1 = nrm((NB, D, DECAY_LORA), D ** -0.5)
    b_w2 = nrm((NB, DECAY_LORA, D), 0.1 * DECAY_LORA ** -0.5)
    b_a0 = nrm((NB, D), 0.1)
    b_a1 = nrm((NB, D, AAA_LORA), D ** -0.5)
    b_a2 = nrm((NB, AAA_LORA, D), 0.1 * AAA_LORA ** -0.5)
    b_g1 = nrm((NB, D, GATE_LORA), D ** -0.5)
    b_g2 = nrm((NB, GATE_LORA, D), GATE_LORA ** -0.5)
    b_k_k = 0.85 + nrm((NB, D), 0.05)
    b_k_a = 1.0 + nrm((NB, D), 0.05)
    b_r_k = nrm((NB, RWKV_HEADS, RWKV_HEAD), 0.1)
    b_gn_g = 1.0 + nrm((NB, D), 0.05)
    b_gn_b = nrm((NB, D), 0.01)
    b_w_o = nrm((NB, D, D), D ** -0.5)
    c_w_q = nrm((DEPTH, D, D), D ** -0.5)
    c_w_kv = nrm((DEPTH, D, 2 * D), D ** -0.5)
    c_w_o = nrm((DEPTH, D, D), D ** -0.5)
    m_w_up = nrm((DEPTH, D, D_FF), D ** -0.5)
    m_w_down = nrm((DEPTH, D_FF, D), D_FF ** -0.5)
    return {"x": x, "mem": mem, "ln_gains": ln_gains, "mem_norm": mem_norm,
            "a_conv_w": a_conv_w, "a_conv_b": a_conv_b, "a_w_in": a_w_in, "a_b_in": a_b_in,
            "a_gate_w": a_gate_w, "a_gate_b": a_gate_b, "a_lambda": a_lambda,
            "a_w_out": a_w_out, "a_b_out": a_b_out,
            "b_mu": b_mu, "b_w_rkv": b_w_rkv, "b_w0": b_w0, "b_w1": b_w1, "b_w2": b_w2,
            "b_a0": b_a0, "b_a1": b_a1, "b_a2": b_a2, "b_g1": b_g1, "b_g2": b_g2,
            "b_k_k": b_k_k, "b_k_a": b_k_a, "b_r_k": b_r_k, "b_gn_g": b_gn_g, "b_gn_b": b_gn_b,
            "b_w_o": b_w_o,
            "c_w_q": c_w_q, "c_w_kv": c_w_kv, "c_w_o": c_w_o,
            "m_w_up": m_w_up, "m_w_down": m_w_down}


def _fwd_reference(x, mem, ln_gains, mem_norm,
              a_conv_w, a_conv_b, a_w_in, a_b_in, a_gate_w, a_gate_b, a_lambda, a_w_out, a_b_out,
              b_mu, b_w_rkv, b_w0, b_w1, b_w2, b_a0, b_a1, b_a2, b_g1, b_g2,
              b_k_k, b_k_a, b_r_k, b_gn_g, b_gn_b, b_w_o,
              c_w_q, c_w_kv, c_w_o, m_w_up, m_w_down):
    mem_n = rms_norm(mem, mem_norm)
    for i in range(DEPTH):
        g = ln_gains[i]
        j = i // N_MIXERS
        hn = rms_norm(x, g[0])
        if i % N_MIXERS == 0:
            t = rglru_mixer(hn, a_conv_w[j], a_conv_b[j], a_w_in[j], a_b_in[j], a_gate_w[j],
                            a_gate_b[j], a_lambda[j], a_w_out[j], a_b_out[j])
        else:
            t = rwkv7_mixer(hn, b_mu[j], b_w_rkv[j], b_w0[j], b_w1[j], b_w2[j], b_a0[j],
                            b_a1[j], b_a2[j], b_g1[j], b_g2[j], b_k_k[j], b_k_a[j], b_r_k[j],
                            b_gn_g[j], b_gn_b[j], b_w_o[j])
        x = x + rms_norm(t, g[1])
        c = mem_cross_attention(rms_norm(x, g[2]), mem_n, c_w_q[i], c_w_kv[i], c_w_o[i])
        x = x + rms_norm(c, g[3])
        m = sqrelu_mlp(rms_norm(x, g[4]), m_w_up[i], m_w_down[i])
        x = x + rms_norm(m, g[5])
    return x


import jax as _jax
import jax.numpy as _jnp

TWIN_FORMAT = 'train_step'
FWD_PARAMS = ['x', 'mem', 'ln_gains', 'mem_norm', 'a_conv_w', 'a_conv_b', 'a_w_in', 'a_b_in', 'a_gate_w', 'a_gate_b', 'a_lambda', 'a_w_out', 'a_b_out', 'b_mu', 'b_w_rkv', 'b_w0', 'b_w1', 'b_w2', 'b_a0', 'b_a1', 'b_a2', 'b_g1', 'b_g2', 'b_k_k', 'b_k_a', 'b_r_k', 'b_gn_g', 'b_gn_b', 'b_w_o', 'c_w_q', 'c_w_kv', 'c_w_o', 'm_w_up', 'm_w_down']
TWIN_WEIGHTS = ['ln_gains', 'mem_norm', 'a_conv_w', 'a_conv_b', 'a_w_in', 'a_b_in', 'a_gate_w', 'a_gate_b', 'a_lambda', 'a_w_out', 'a_b_out', 'b_mu', 'b_w_rkv', 'b_w0', 'b_w1', 'b_w2', 'b_a0', 'b_a1', 'b_a2', 'b_g1', 'b_g2', 'b_k_k', 'b_k_a', 'b_r_k', 'b_gn_g', 'b_gn_b', 'b_w_o', 'c_w_q', 'c_w_kv', 'c_w_o', 'm_w_up', 'm_w_down']
TWIN_DIFF_INPUT = 'x'
TWIN_INPUTS = ['x', 'mem', 'ln_gains', 'mem_norm', 'a_conv_w', 'a_conv_b', 'a_w_in', 'a_b_in', 'a_gate_w', 'a_gate_b', 'a_lambda', 'a_w_out', 'a_b_out', 'b_mu', 'b_w_rkv', 'b_w0', 'b_w1', 'b_w2', 'b_a0', 'b_a1', 'b_a2', 'b_g1', 'b_g2', 'b_k_k', 'b_k_a', 'b_r_k', 'b_gn_g', 'b_gn_b', 'b_w_o', 'c_w_q', 'c_w_kv', 'c_w_o', 'm_w_up', 'm_w_down', 'loss_target', 'm_ln_gains', 'm_mem_norm', 'm_a_conv_w', 'm_a_conv_b', 'm_a_w_in', 'm_a_b_in', 'm_a_gate_w', 'm_a_gate_b', 'm_a_lambda', 'm_a_w_out', 'm_a_b_out', 'm_b_mu', 'm_b_w_rkv', 'm_b_w0', 'm_b_w1', 'm_b_w2', 'm_b_a0', 'm_b_a1', 'm_b_a2', 'm_b_g1', 'm_b_g2', 'm_b_k_k', 'm_b_k_a', 'm_b_r_k', 'm_b_gn_g', 'm_b_gn_b', 'm_b_w_o', 'm_c_w_q', 'm_c_w_kv', 'm_c_w_o', 'm_m_w_up', 'm_m_w_down', 'v_ln_gains', 'v_mem_norm', 'v_a_conv_w', 'v_a_conv_b', 'v_a_w_in', 'v_a_b_in', 'v_a_gate_w', 'v_a_gate_b', 'v_a_lambda', 'v_a_w_out', 'v_a_b_out', 'v_b_mu', 'v_b_w_rkv', 'v_b_w0', 'v_b_w1', 'v_b_w2', 'v_b_a0', 'v_b_a1', 'v_b_a2', 'v_b_g1', 'v_b_g2', 'v_b_k_k', 'v_b_k_a', 'v_b_r_k', 'v_b_gn_g', 'v_b_gn_b', 'v_b_w_o', 'v_c_w_q', 'v_c_w_kv', 'v_c_w_o', 'v_m_w_up', 'v_m_w_down']
TWIN_OUTPUTS = ['loss', 'grad_x', 'grad_ln_gains', 'grad_mem_norm', 'grad_a_conv_w', 'grad_a_conv_b', 'grad_a_w_in', 'grad_a_b_in', 'grad_a_gate_w', 'grad_a_gate_b', 'grad_a_lambda', 'grad_a_w_out', 'grad_a_b_out', 'grad_b_mu', 'grad_b_w_rkv', 'grad_b_w0', 'grad_b_w1', 'grad_b_w2', 'grad_b_a0', 'grad_b_a1', 'grad_b_a2', 'grad_b_g1', 'grad_b_g2', 'grad_b_k_k', 'grad_b_k_a', 'grad_b_r_k', 'grad_b_gn_g', 'grad_b_gn_b', 'grad_b_w_o', 'grad_c_w_q', 'grad_c_w_kv', 'grad_c_w_o', 'grad_m_w_up', 'grad_m_w_down', 'delta_ln_gains', 'delta_mem_norm', 'delta_a_conv_w', 'delta_a_conv_b', 'delta_a_w_in', 'delta_a_b_in', 'delta_a_gate_w', 'delta_a_gate_b', 'delta_a_lambda', 'delta_a_w_out', 'delta_a_b_out', 'delta_b_mu', 'delta_b_w_rkv', 'delta_b_w0', 'delta_b_w1', 'delta_b_w2', 'delta_b_a0', 'delta_b_a1', 'delta_b_a2', 'delta_b_g1', 'delta_b_g2', 'delta_b_k_k', 'delta_b_k_a', 'delta_b_r_k', 'delta_b_gn_g', 'delta_b_gn_b', 'delta_b_w_o', 'delta_c_w_q', 'delta_c_w_kv', 'delta_c_w_o', 'delta_m_w_up', 'delta_m_w_down', 'new_m_ln_gains', 'new_m_mem_norm', 'new_m_a_conv_w', 'new_m_a_conv_b', 'new_m_a_w_in', 'new_m_a_b_in', 'new_m_a_gate_w', 'new_m_a_gate_b', 'new_m_a_lambda', 'new_m_a_w_out', 'new_m_a_b_out', 'new_m_b_mu', 'new_m_b_w_rkv', 'new_m_b_w0', 'new_m_b_w1', 'new_m_b_w2', 'new_m_b_a0', 'new_m_b_a1', 'new_m_b_a2', 'new_m_b_g1', 'new_m_b_g2', 'new_m_b_k_k', 'new_m_b_k_a', 'new_m_b_r_k', 'new_m_b_gn_g', 'new_m_b_gn_b', 'new_m_b_w_o', 'new_m_c_w_q', 'new_m_c_w_kv', 'new_m_c_w_o', 'new_m_m_w_up', 'new_m_m_w_down', 'new_v_ln_gains', 'new_v_mem_norm', 'new_v_a_conv_w', 'new_v_a_conv_b', 'new_v_a_w_in', 'new_v_a_b_in', 'new_v_a_gate_w', 'new_v_a_gate_b', 'new_v_a_lambda', 'new_v_a_w_out', 'new_v_a_b_out', 'new_v_b_mu', 'new_v_b_w_rkv', 'new_v_b_w0', 'new_v_b_w1', 'new_v_b_w2', 'new_v_b_a0', 'new_v_b_a1', 'new_v_b_a2', 'new_v_b_g1', 'new_v_b_g2', 'new_v_b_k_k', 'new_v_b_k_a', 'new_v_b_r_k', 'new_v_b_gn_g', 'new_v_b_gn_b', 'new_v_b_w_o', 'new_v_c_w_q', 'new_v_c_w_kv', 'new_v_c_w_o', 'new_v_m_w_up', 'new_v_m_w_down']
TWIN_LEAF_KINDS = {'loss': 'loss', 'grad_x': 'grad_x', 'grad_ln_gains': 'grad_w', 'grad_mem_norm': 'grad_w', 'grad_a_conv_w': 'grad_w', 'grad_a_conv_b': 'grad_w', 'grad_a_w_in': 'grad_w', 'grad_a_b_in': 'grad_w', 'grad_a_gate_w': 'grad_w', 'grad_a_gate_b': 'grad_w', 'grad_a_lambda': 'grad_w', 'grad_a_w_out': 'grad_w', 'grad_a_b_out': 'grad_w', 'grad_b_mu': 'grad_w', 'grad_b_w_rkv': 'grad_w', 'grad_b_w0': 'grad_w', 'grad_b_w1': 'grad_w', 'grad_b_w2': 'grad_w', 'grad_b_a0': 'grad_w', 'grad_b_a1': 'grad_w', 'grad_b_a2': 'grad_w', 'grad_b_g1': 'grad_w', 'grad_b_g2': 'grad_w', 'grad_b_k_k': 'grad_w', 'grad_b_k_a': 'grad_w', 'grad_b_r_k': 'grad_w', 'grad_b_gn_g': 'grad_w', 'grad_b_gn_b': 'grad_w', 'grad_b_w_o': 'grad_w', 'grad_c_w_q': 'grad_w', 'grad_c_w_kv': 'grad_w', 'grad_c_w_o': 'grad_w', 'grad_m_w_up': 'grad_w', 'grad_m_w_down': 'grad_w', 'delta_ln_gains': 'delta_w', 'delta_mem_norm': 'delta_w', 'delta_a_conv_w': 'delta_w', 'delta_a_conv_b': 'delta_w', 'delta_a_w_in': 'delta_w', 'delta_a_b_in': 'delta_w', 'delta_a_gate_w': 'delta_w', 'delta_a_gate_b': 'delta_w', 'delta_a_lambda': 'delta_w', 'delta_a_w_out': 'delta_w', 'delta_a_b_out': 'delta_w', 'delta_b_mu': 'delta_w', 'delta_b_w_rkv': 'delta_w', 'delta_b_w0': 'delta_w', 'delta_b_w1': 'delta_w', 'delta_b_w2': 'delta_w', 'delta_b_a0': 'delta_w', 'delta_b_a1': 'delta_w', 'delta_b_a2': 'delta_w', 'delta_b_g1': 'delta_w', 'delta_b_g2': 'delta_w', 'delta_b_k_k': 'delta_w', 'delta_b_k_a': 'delta_w', 'delta_b_r_k': 'delta_w', 'delta_b_gn_g': 'delta_w', 'delta_b_gn_b': 'delta_w', 'delta_b_w_o': 'delta_w', 'delta_c_w_q': 'delta_w', 'delta_c_w_kv': 'delta_w', 'delta_c_w_o': 'delta_w', 'delta_m_w_up': 'delta_w', 'delta_m_w_down': 'delta_w', 'new_m_ln_gains': 'new_m', 'new_m_mem_norm': 'new_m', 'new_m_a_conv_w': 'new_m', 'new_m_a_conv_b': 'new_m', 'new_m_a_w_in': 'new_m', 'new_m_a_b_in': 'new_m', 'new_m_a_gate_w': 'new_m', 'new_m_a_gate_b': 'new_m', 'new_m_a_lambda': 'new_m', 'new_m_a_w_out': 'new_m', 'new_m_a_b_out': 'new_m', 'new_m_b_mu': 'new_m', 'new_m_b_w_rkv': 'new_m', 'new_m_b_w0': 'new_m', 'new_m_b_w1': 'new_m', 'new_m_b_w2': 'new_m', 'new_m_b_a0': 'new_m', 'new_m_b_a1': 'new_m', 'new_m_b_a2': 'new_m', 'new_m_b_g1': 'new_m', 'new_m_b_g2': 'new_m', 'new_m_b_k_k': 'new_m', 'new_m_b_k_a': 'new_m', 'new_m_b_r_k': 'new_m', 'new_m_b_gn_g': 'new_m', 'new_m_b_gn_b': 'new_m', 'new_m_b_w_o': 'new_m', 'new_m_c_w_q': 'new_m', 'new_m_c_w_kv': 'new_m', 'new_m_c_w_o': 'new_m', 'new_m_m_w_up': 'new_m', 'new_m_m_w_down': 'new_m', 'new_v_ln_gains': 'new_v', 'new_v_mem_norm': 'new_v', 'new_v_a_conv_w': 'new_v', 'new_v_a_conv_b': 'new_v', 'new_v_a_w_in': 'new_v', 'new_v_a_b_in': 'new_v', 'new_v_a_gate_w': 'new_v', 'new_v_a_gate_b': 'new_v', 'new_v_a_lambda': 'new_v', 'new_v_a_w_out': 'new_v', 'new_v_a_b_out': 'new_v', 'new_v_b_mu': 'new_v', 'new_v_b_w_rkv': 'new_v', 'new_v_b_w0': 'new_v', 'new_v_b_w1': 'new_v', 'new_v_b_w2': 'new_v', 'new_v_b_a0': 'new_v', 'new_v_b_a1': 'new_v', 'new_v_b_a2': 'new_v', 'new_v_b_g1': 'new_v', 'new_v_b_g2': 'new_v', 'new_v_b_k_k': 'new_v', 'new_v_b_k_a': 'new_v', 'new_v_b_r_k': 'new_v', 'new_v_b_gn_g': 'new_v', 'new_v_b_gn_b': 'new_v', 'new_v_b_w_o': 'new_v', 'new_v_c_w_q': 'new_v', 'new_v_c_w_kv': 'new_v', 'new_v_c_w_o': 'new_v', 'new_v_m_w_up': 'new_v', 'new_v_m_w_down': 'new_v'}


def _forward(args):
    return _fwd_reference(*[args[k] for k in FWD_PARAMS])


def _output_shape():
    out = _jax.eval_shape(lambda: _forward(_fwd_setup_inputs(0)))
    return out.shape, out.dtype

N_MICROBATCH = 1
ADAM_LR = 0.001
ADAM_B1 = 0.9
ADAM_B2 = 0.999
ADAM_EPS = 1e-08
ADAM_WD = 0.01
ADAM_STEP = 10
PER_EXAMPLE_BATCH_AXIS = {'x': 0, 'mem': 0, 'loss_target': 0}
SHARED_INPUTS = []
_WEIGHT_DTYPES = {'ln_gains': _jnp.float32, 'mem_norm': _jnp.float32, 'a_conv_w': _jnp.float32, 'a_conv_b': _jnp.float32, 'a_w_in': _jnp.float32, 'a_b_in': _jnp.float32, 'a_gate_w': _jnp.float32, 'a_gate_b': _jnp.float32, 'a_lambda': _jnp.float32, 'a_w_out': _jnp.float32, 'a_b_out': _jnp.float32, 'b_mu': _jnp.float32, 'b_w_rkv': _jnp.float32, 'b_w0': _jnp.float32, 'b_w1': _jnp.float32, 'b_w2': _jnp.float32, 'b_a0': _jnp.float32, 'b_a1': _jnp.float32, 'b_a2': _jnp.float32, 'b_g1': _jnp.float32, 'b_g2': _jnp.float32, 'b_k_k': _jnp.float32, 'b_k_a': _jnp.float32, 'b_r_k': _jnp.float32, 'b_gn_g': _jnp.float32, 'b_gn_b': _jnp.float32, 'b_w_o': _jnp.float32, 'c_w_q': _jnp.float32, 'c_w_kv': _jnp.float32, 'c_w_o': _jnp.float32, 'm_w_up': _jnp.float32, 'm_w_down': _jnp.float32}
MOMENT_SCALE = {'ln_gains': 4.664384e+01, 'mem_norm': 2.507860e+01, 'a_conv_w': 3.874623e+00, 'a_conv_b': 9.518693e+01, 'a_w_in': 2.949471e+00, 'a_b_in': 5.524169e+01, 'a_gate_w': 1.707514e+00, 'a_gate_b': 1.330626e+00, 'a_lambda': 1.586744e+00, 'a_w_out': 4.349199e+00, 'a_b_out': 1.427861e+02, 'b_mu': 3.538732e+00, 'b_w_rkv': 5.690704e+00, 'b_w0': 4.033648e+00, 'b_w1': 4.172545e-01, 'b_w2': 9.297039e-01, 'b_a0': 1.265590e+00, 'b_a1': 4.850384e-01, 'b_a2': 1.191464e+00, 'b_g1': 2.448878e+00, 'b_g2': 5.910403e+00, 'b_k_k': 2.953143e+01, 'b_k_a': 2.965900e+01, 'b_r_k': 3.847755e+00, 'b_gn_g': 5.874227e+00, 'b_gn_b': 4.591005e+01, 'b_w_o': 5.419966e+00, 'c_w_q': 4.069149e+00, 'c_w_kv': 1.248388e+01, 'c_w_o': 1.847290e+01, 'm_w_up': 4.013674e+00, 'm_w_down': 1.762688e+01}


def _to_microbatches(a, axis):
    t = _jnp.moveaxis(a, axis, 0)
    t = t.reshape((N_MICROBATCH, t.shape[0] // N_MICROBATCH) + t.shape[1:])
    return _jnp.moveaxis(t, 1, axis + 1)


def setup_inputs(seed: int = 0) -> dict:
    inp = _fwd_setup_inputs(seed)
    key = _jax.random.fold_in(_jax.random.key(seed), 7919)
    shape, _ = _output_shape()
    out = dict(inp)
    out["loss_target"] = _jax.random.normal(_jax.random.fold_in(key, 0), shape, _jnp.float32)
    for i, name in enumerate(TWIN_WEIGHTS):
        w = inp[name].astype(_jnp.float32)
        if MOMENT_SCALE is None:
            s = _jnp.sqrt(_jnp.mean(_jnp.square(w)) + 1e-30)
        else:
            s = MOMENT_SCALE[name]
        km, kv = _jax.random.split(_jax.random.fold_in(key, i + 1))
        out[name] = w
        out["m_" + name] = s * _jax.random.normal(km, w.shape, _jnp.float32)
        out["v_" + name] = (s * s) * _jax.random.uniform(kv, w.shape, _jnp.float32, 0.5, 1.5)
    if N_MICROBATCH > 1:
        for name, axis in PER_EXAMPLE_BATCH_AXIS.items():
            out[name] = _to_microbatches(out[name], axis)
    return {'x': out['x'], 'mem': out['mem'], 'ln_gains': out['ln_gains'], 'mem_norm': out['mem_norm'], 'a_conv_w': out['a_conv_w'], 'a_conv_b': out['a_conv_b'], 'a_w_in': out['a_w_in'], 'a_b_in': out['a_b_in'], 'a_gate_w': out['a_gate_w'], 'a_gate_b': out['a_gate_b'], 'a_lambda': out['a_lambda'], 'a_w_out': out['a_w_out'], 'a_b_out': out['a_b_out'], 'b_mu': out['b_mu'], 'b_w_rkv': out['b_w_rkv'], 'b_w0': out['b_w0'], 'b_w1': out['b_w1'], 'b_w2': out['b_w2'], 'b_a0': out['b_a0'], 'b_a1': out['b_a1'], 'b_a2': out['b_a2'], 'b_g1': out['b_g1'], 'b_g2': out['b_g2'], 'b_k_k': out['b_k_k'], 'b_k_a': out['b_k_a'], 'b_r_k': out['b_r_k'], 'b_gn_g': out['b_gn_g'], 'b_gn_b': out['b_gn_b'], 'b_w_o': out['b_w_o'], 'c_w_q': out['c_w_q'], 'c_w_kv': out['c_w_kv'], 'c_w_o': out['c_w_o'], 'm_w_up': out['m_w_up'], 'm_w_down': out['m_w_down'], 'loss_target': out['loss_target'], 'm_ln_gains': out['m_ln_gains'], 'm_mem_norm': out['m_mem_norm'], 'm_a_conv_w': out['m_a_conv_w'], 'm_a_conv_b': out['m_a_conv_b'], 'm_a_w_in': out['m_a_w_in'], 'm_a_b_in': out['m_a_b_in'], 'm_a_gate_w': out['m_a_gate_w'], 'm_a_gate_b': out['m_a_gate_b'], 'm_a_lambda': out['m_a_lambda'], 'm_a_w_out': out['m_a_w_out'], 'm_a_b_out': out['m_a_b_out'], 'm_b_mu': out['m_b_mu'], 'm_b_w_rkv': out['m_b_w_rkv'], 'm_b_w0': out['m_b_w0'], 'm_b_w1': out['m_b_w1'], 'm_b_w2': out['m_b_w2'], 'm_b_a0': out['m_b_a0'], 'm_b_a1': out['m_b_a1'], 'm_b_a2': out['m_b_a2'], 'm_b_g1': out['m_b_g1'], 'm_b_g2': out['m_b_g2'], 'm_b_k_k': out['m_b_k_k'], 'm_b_k_a': out['m_b_k_a'], 'm_b_r_k': out['m_b_r_k'], 'm_b_gn_g': out['m_b_gn_g'], 'm_b_gn_b': out['m_b_gn_b'], 'm_b_w_o': out['m_b_w_o'], 'm_c_w_q': out['m_c_w_q'], 'm_c_w_kv': out['m_c_w_kv'], 'm_c_w_o': out['m_c_w_o'], 'm_m_w_up': out['m_m_w_up'], 'm_m_w_down': out['m_m_w_down'], 'v_ln_gains': out['v_ln_gains'], 'v_mem_norm': out['v_mem_norm'], 'v_a_conv_w': out['v_a_conv_w'], 'v_a_conv_b': out['v_a_conv_b'], 'v_a_w_in': out['v_a_w_in'], 'v_a_b_in': out['v_a_b_in'], 'v_a_gate_w': out['v_a_gate_w'], 'v_a_gate_b': out['v_a_gate_b'], 'v_a_lambda': out['v_a_lambda'], 'v_a_w_out': out['v_a_w_out'], 'v_a_b_out': out['v_a_b_out'], 'v_b_mu': out['v_b_mu'], 'v_b_w_rkv': out['v_b_w_rkv'], 'v_b_w0': out['v_b_w0'], 'v_b_w1': out['v_b_w1'], 'v_b_w2': out['v_b_w2'], 'v_b_a0': out['v_b_a0'], 'v_b_a1': out['v_b_a1'], 'v_b_a2': out['v_b_a2'], 'v_b_g1': out['v_b_g1'], 'v_b_g2': out['v_b_g2'], 'v_b_k_k': out['v_b_k_k'], 'v_b_k_a': out['v_b_k_a'], 'v_b_r_k': out['v_b_r_k'], 'v_b_gn_g': out['v_b_gn_g'], 'v_b_gn_b': out['v_b_gn_b'], 'v_b_w_o': out['v_b_w_o'], 'v_c_w_q': out['v_c_w_q'], 'v_c_w_kv': out['v_c_w_kv'], 'v_c_w_o': out['v_c_w_o'], 'v_m_w_up': out['v_m_w_up'], 'v_m_w_down': out['v_m_w_down']}


def _loss(weights, diff, rest, loss_target):
    with _jax.named_scope("forward"):
        args = {**rest, TWIN_DIFF_INPUT: diff, **{k: w.astype(_WEIGHT_DTYPES[k]) for k, w in weights.items()}}
        y = _forward(args)
    with _jax.named_scope("loss_head"):
        err = _jnp.square(y.astype(_jnp.float32) - loss_target)
        return 0.5 * _jnp.sum(_jnp.mean(err, axis=-1)) if err.ndim else 0.5 * err


def _adamw(w, g, m, v):
    m = ADAM_B1 * m + (1.0 - ADAM_B1) * g
    v = ADAM_B2 * v + (1.0 - ADAM_B2) * _jnp.square(g)
    m_hat = m / (1.0 - ADAM_B1 ** ADAM_STEP)
    v_hat = v / (1.0 - ADAM_B2 ** ADAM_STEP)
    delta = -ADAM_LR * (m_hat / (_jnp.sqrt(v_hat) + ADAM_EPS) + ADAM_WD * w)
    return delta, m, v


def reference(x, mem, ln_gains, mem_norm, a_conv_w, a_conv_b, a_w_in, a_b_in, a_gate_w, a_gate_b, a_lambda, a_w_out, a_b_out, b_mu, b_w_rkv, b_w0, b_w1, b_w2, b_a0, b_a1, b_a2, b_g1, b_g2, b_k_k, b_k_a, b_r_k, b_gn_g, b_gn_b, b_w_o, c_w_q, c_w_kv, c_w_o, m_w_up, m_w_down, loss_target, m_ln_gains, m_mem_norm, m_a_conv_w, m_a_conv_b, m_a_w_in, m_a_b_in, m_a_gate_w, m_a_gate_b, m_a_lambda, m_a_w_out, m_a_b_out, m_b_mu, m_b_w_rkv, m_b_w0, m_b_w1, m_b_w2, m_b_a0, m_b_a1, m_b_a2, m_b_g1, m_b_g2, m_b_k_k, m_b_k_a, m_b_r_k, m_b_gn_g, m_b_gn_b, m_b_w_o, m_c_w_q, m_c_w_kv, m_c_w_o, m_m_w_up, m_m_w_down, v_ln_gains, v_mem_norm, v_a_conv_w, v_a_conv_b, v_a_w_in, v_a_b_in, v_a_gate_w, v_a_gate_b, v_a_lambda, v_a_w_out, v_a_b_out, v_b_mu, v_b_w_rkv, v_b_w0, v_b_w1, v_b_w2, v_b_a0, v_b_a1, v_b_a2, v_b_g1, v_b_g2, v_b_k_k, v_b_k_a, v_b_r_k, v_b_gn_g, v_b_gn_b, v_b_w_o, v_c_w_q, v_c_w_kv, v_c_w_o, v_m_w_up, v_m_w_down):
    given = dict(x=x, mem=mem, ln_gains=ln_gains, mem_norm=mem_norm, a_conv_w=a_conv_w, a_conv_b=a_conv_b, a_w_in=a_w_in, a_b_in=a_b_in, a_gate_w=a_gate_w, a_gate_b=a_gate_b, a_lambda=a_lambda, a_w_out=a_w_out, a_b_out=a_b_out, b_mu=b_mu, b_w_rkv=b_w_rkv, b_w0=b_w0, b_w1=b_w1, b_w2=b_w2, b_a0=b_a0, b_a1=b_a1, b_a2=b_a2, b_g1=b_g1, b_g2=b_g2, b_k_k=b_k_k, b_k_a=b_k_a, b_r_k=b_r_k, b_gn_g=b_gn_g, b_gn_b=b_gn_b, b_w_o=b_w_o, c_w_q=c_w_q, c_w_kv=c_w_kv, c_w_o=c_w_o, m_w_up=m_w_up, m_w_down=m_w_down, loss_target=loss_target, m_ln_gains=m_ln_gains, m_mem_norm=m_mem_norm, m_a_conv_w=m_a_conv_w, m_a_conv_b=m_a_conv_b, m_a_w_in=m_a_w_in, m_a_b_in=m_a_b_in, m_a_gate_w=m_a_gate_w, m_a_gate_b=m_a_gate_b, m_a_lambda=m_a_lambda, m_a_w_out=m_a_w_out, m_a_b_out=m_a_b_out, m_b_mu=m_b_mu, m_b_w_rkv=m_b_w_rkv, m_b_w0=m_b_w0, m_b_w1=m_b_w1, m_b_w2=m_b_w2, m_b_a0=m_b_a0, m_b_a1=m_b_a1, m_b_a2=m_b_a2, m_b_g1=m_b_g1, m_b_g2=m_b_g2, m_b_k_k=m_b_k_k, m_b_k_a=m_b_k_a, m_b_r_k=m_b_r_k, m_b_gn_g=m_b_gn_g, m_b_gn_b=m_b_gn_b, m_b_w_o=m_b_w_o, m_c_w_q=m_c_w_q, m_c_w_kv=m_c_w_kv, m_c_w_o=m_c_w_o, m_m_w_up=m_m_w_up, m_m_w_down=m_m_w_down, v_ln_gains=v_ln_gains, v_mem_norm=v_mem_norm, v_a_conv_w=v_a_conv_w, v_a_conv_b=v_a_conv_b, v_a_w_in=v_a_w_in, v_a_b_in=v_a_b_in, v_a_gate_w=v_a_gate_w, v_a_gate_b=v_a_gate_b, v_a_lambda=v_a_lambda, v_a_w_out=v_a_w_out, v_a_b_out=v_a_b_out, v_b_mu=v_b_mu, v_b_w_rkv=v_b_w_rkv, v_b_w0=v_b_w0, v_b_w1=v_b_w1, v_b_w2=v_b_w2, v_b_a0=v_b_a0, v_b_a1=v_b_a1, v_b_a2=v_b_a2, v_b_g1=v_b_g1, v_b_g2=v_b_g2, v_b_k_k=v_b_k_k, v_b_k_a=v_b_k_a, v_b_r_k=v_b_r_k, v_b_gn_g=v_b_gn_g, v_b_gn_b=v_b_gn_b, v_b_w_o=v_b_w_o, v_c_w_q=v_c_w_q, v_c_w_kv=v_c_w_kv, v_c_w_o=v_c_w_o, v_m_w_up=v_m_w_up, v_m_w_down=v_m_w_down)
    weights = {n: given[n] for n in TWIN_WEIGHTS}
    shared = {n: given[n] for n in SHARED_INPUTS}
    per_example = {n: given[n] for n in ['x', 'mem']}
    grad_fn = _jax.value_and_grad(_loss, argnums=(0, 1))

    def one_microbatch(ex, loss_target):
        ex = dict(ex)
        diff = ex.pop(TWIN_DIFF_INPUT)
        return grad_fn(weights, diff, {**shared, **ex}, loss_target)

    if N_MICROBATCH == 1:
        loss, (grad_w, grad_x) = one_microbatch(per_example, given["loss_target"])
    else:
        def body(carry, xs):
            loss_sum, grad_sum = carry
            l_k, (gw_k, gx_k) = one_microbatch(xs[0], xs[1])
            with _jax.named_scope("update"):
                return (loss_sum + l_k, _jax.tree.map(_jnp.add, grad_sum, gw_k)), gx_k

        init = (_jnp.zeros((), _jnp.float32), _jax.tree.map(_jnp.zeros_like, weights))
        (loss, grad_w), grad_x = _jax.lax.scan(body, init, (per_example, given["loss_target"]))
    with _jax.named_scope("update"):
        delta_w, new_m, new_v = {}, {}, {}
        for n in TWIN_WEIGHTS:
            delta_w[n], new_m[n], new_v[n] = _adamw(weights[n], grad_w[n], given["m_" + n], given["v_" + n])
    return (loss, grad_x, *[grad_w[n] for n in TWIN_WEIGHTS], *[delta_w[n] for n in TWIN_WEIGHTS],
            *[new_m[n] for n in TWIN_WEIGHTS], *[new_v[n] for n in TWIN_WEIGHTS])
```

```python
import functools

import jax
import jax.numpy as jnp
from jax import lax
from jax.experimental import pallas as pl
from jax.experimental.pallas import tpu as pltpu

F32 = jnp.float32
BF16 = jnp.bfloat16
N_DEV = 8
MESH = pl.DeviceIdType.MESH

V7X_VMEM_LIMIT_BYTES = 56 * 1024 * 1024
SUB = 8
HALO_ROWS = SUB
LANES = 128
RWKV_N = 64
RMS_EPS = 1e-6
GN_EPS = 64e-5
LRU_C = 8.0
XATTN_HEADS = 4
LRU_HEADS = 4

ADAM_LR, ADAM_B1, ADAM_B2, ADAM_EPS, ADAM_WD, ADAM_STEP = 0.001, 0.9, 0.999, 1e-8, 0.01, 10

WEIGHTS = ['ln_gains', 'mem_norm', 'a_conv_w', 'a_conv_b', 'a_w_in', 'a_b_in', 'a_gate_w', 'a_gate_b',
           'a_lambda', 'a_w_out', 'a_b_out', 'b_mu', 'b_w_rkv', 'b_w0', 'b_w1', 'b_w2', 'b_a0', 'b_a1',
           'b_a2', 'b_g1', 'b_g2', 'b_k_k', 'b_k_a', 'b_r_k', 'b_gn_g', 'b_gn_b', 'b_w_o', 'c_w_q',
           'c_w_kv', 'c_w_o', 'm_w_up', 'm_w_down']
SHARD_AXIS = {'ln_gains': 2, 'mem_norm': None, 'a_conv_w': 2, 'a_conv_b': None, 'a_w_in': 2, 'a_b_in': None,
              'a_gate_w': 3, 'a_gate_b': 3, 'a_lambda': None, 'a_w_out': 1, 'a_b_out': None, 'b_mu': 2,
              'b_w_rkv': 2, 'b_w0': 1, 'b_w1': 1, 'b_w2': 2, 'b_a0': 1, 'b_a1': 1, 'b_a2': 2, 'b_g1': 1,
              'b_g2': 2, 'b_k_k': 1, 'b_k_a': 1, 'b_r_k': None, 'b_gn_g': 1, 'b_gn_b': 1, 'b_w_o': 1,
              'c_w_q': 1, 'c_w_kv': 2, 'c_w_o': 1, 'm_w_up': 2, 'm_w_down': 1}
BIG = ['a_w_in', 'a_gate_w', 'a_w_out', 'b_w_rkv', 'b_w1', 'b_w2', 'b_a1', 'b_a2', 'b_g1', 'b_g2', 'b_w_o',
       'c_w_q', 'c_w_kv', 'c_w_o', 'm_w_up', 'm_w_down']
SMALL = ['ln_gains', 'a_conv_w', 'a_gate_b', 'b_mu', 'b_w0', 'b_a0', 'b_k_k', 'b_k_a', 'b_gn_g', 'b_gn_b']
REPL = ['mem_norm', 'a_conv_b', 'a_b_in', 'a_lambda', 'a_b_out', 'b_r_k']
PACK_W = 1024


def _cparams(*sem):
    return pltpu.CompilerParams(dimension_semantics=sem, vmem_limit_bytes=V7X_VMEM_LIMIT_BYTES)


def _shift_tile(tile, halo, k, pos, seq, tm):
    if k > 0:
        ext = jnp.concatenate([halo, tile], axis=0)
        r = pltpu.roll(ext, k, 0)[HALO_ROWS:HALO_ROWS + tm]
        return jnp.where(pos >= k, r, 0.0)
    kk = -k
    ext = jnp.concatenate([tile, halo], axis=0)
    r = pltpu.roll(ext, tm + HALO_ROWS - kk, 0)[0:tm]
    return jnp.where(pos + kk < seq, r, 0.0)


def _row_specs(rows, shifts, params, per_b, seq, tm):
    T = rows[0].shape[0]
    ns = seq // tm
    specs = [pl.BlockSpec((tm, r.shape[1]), lambda b, i: (b * ns + i, 0)) for r in rows]
    halo_keys = []
    for idx, k in shifts:
        key = (idx, k > 0)
        if key not in halo_keys:
            halo_keys.append(key)
    per8 = tm // HALO_ROWS
    last8 = T // HALO_ROWS - 1
    for idx, prev in halo_keys:
        c = rows[idx].shape[1]
        if prev:
            specs.append(pl.BlockSpec((HALO_ROWS, c), lambda b, i: (jnp.maximum((b * ns + i) * per8 - 1, 0), 0)))
        else:
            specs.append(pl.BlockSpec((HALO_ROWS, c), lambda b, i: (jnp.minimum((b * ns + i + 1) * per8, last8), 0)))
    for j, p in enumerate(params):
        if j in per_b:
            specs.append(pl.BlockSpec((None,) + p.shape[1:], lambda b, i: (b, 0, 0)))
        else:
            specs.append(pl.BlockSpec(p.shape, lambda b, i, nd=p.ndim: (0,) * nd))
    return specs, halo_keys


def _load_tiles(refs, rows, shifts, halo_keys, params, seq, tm):
    nr, nh = len(rows), len(halo_keys)
    i = pl.program_id(1)
    pos = i * tm + lax.broadcasted_iota(jnp.int32, (tm, 1), 0)
    row_t = [r[...] for r in refs[:nr]]
    halo_t = {key: refs[nr + j][...] for j, key in enumerate(halo_keys)}
    sh_t = [_shift_tile(row_t[idx], halo_t[(idx, k > 0)], k, pos, seq, tm) for idx, k in shifts]
    par_t = [p[...] for p in refs[nr + nh:nr + nh + len(params)]]
    return pos, row_t, sh_t, par_t


def _rw(name, fn, rows, params=(), outs=(), accs=(), shifts=(), seq=None, per_b=(), tm=256):
    rows, params = list(rows), list(params)
    T = rows[0].shape[0]
    seq = seq or T
    tm = min(tm, seq)
    nb, ns = T // seq, seq // tm
    in_specs, halo_keys = _row_specs(rows, shifts, params, per_b, seq, tm)
    n_in, no = len(in_specs), len(outs)

    def body(*refs):
        pos, row_t, sh_t, par_t = _load_tiles(refs, rows, shifts, halo_keys, params, seq, tm)
        res = fn(pos, *row_t, *sh_t, *par_t)
        out_refs = refs[n_in:n_in + no]
        acc_refs = refs[n_in + no:]
        for o_ref, v in zip(out_refs, res[:no]):
            o_ref[...] = v.astype(o_ref.dtype)
        if acc_refs:
            @pl.when((pl.program_id(0) == 0) & (pl.program_id(1) == 0))
            def _():
                for a_ref in acc_refs:
                    a_ref[...] = jnp.zeros_like(a_ref)
            for a_ref, v in zip(acc_refs, res[no:]):
                a_ref[...] += v

    out_shape = [jax.ShapeDtypeStruct((T, c), dt) for c, dt in outs]
    out_shape += [jax.ShapeDtypeStruct(s, F32) for s in accs]
    out_specs = [pl.BlockSpec((tm, c), lambda b, i: (b * ns + i, 0)) for c, _ in outs]
    out_specs += [pl.BlockSpec(s, lambda b, i: (0, 0)) for s in accs]
    return pl.pallas_call(body, name=name, grid=(nb, ns), in_specs=in_specs, out_specs=out_specs,
                          out_shape=out_shape, compiler_params=_cparams("arbitrary", "arbitrary"))(*rows, *[
                              rows[idx] for idx, _ in halo_keys], *params)


def _rw_vjp(name, fn, rows, params, cts, n_out, d_rows=(), d_shifts=(), d_params=(), adds=(), ct_adds=(),
            shifts=(), seq=None, per_b=(), tm=256):
    rows, params, cts = list(rows), list(params), list(cts)
    T = rows[0].shape[0]
    seq = seq or T
    tm = min(tm, seq)
    nb, ns = T // seq, seq // tm
    n_ga = len(adds)
    adds = list(adds) + list(ct_adds)
    add_rows = [a for _, a, _ in adds]
    add_shifts = [(j, k) for j, (_, _, k) in enumerate(adds) if k != 0]
    all_rows = rows + cts + add_rows
    off_ct, off_add = len(rows), len(rows) + len(cts)
    all_shifts = list(shifts) + [(off_add + j, k) for j, k in add_shifts]
    in_specs, halo_keys = _row_specs(all_rows, all_shifts, params, per_b, seq, tm)
    n_in = len(in_specs)
    n_dr, n_ds = len(d_rows), len(d_shifts)

    def body(*refs):
        pos, row_t, sh_t, par_t = _load_tiles(refs, all_rows, all_shifts, halo_keys, params, seq, tm)
        fn_rows, ct_t, add_t = row_t[:off_ct], row_t[off_ct:off_add], row_t[off_add:]
        fn_sh, add_sh = sh_t[:len(shifts)], sh_t[len(shifts):]
        diff = [fn_rows[j] for j, _ in d_rows] + [fn_sh[j] for j, _ in d_shifts] + [par_t[j] for j in d_params]

        def f(*d):
            r2, s2, p2 = list(fn_rows), list(fn_sh), list(par_t)
            for (j, _), v in zip(d_rows, d[:n_dr]):
                r2[j] = v
            for (j, _), v in zip(d_shifts, d[n_dr:n_dr + n_ds]):
                s2[j] = v
            for j, v in zip(d_params, d[n_dr + n_ds:]):
                p2[j] = v
            return tuple(fn(pos, *r2, *s2, *p2)[:n_out])

        sh_iter = iter(add_sh)
        add_v = [add_t[j] if k == 0 else next(sh_iter) for j, (_, _, k) in enumerate(adds)]
        ct_t = list(ct_t)
        for (tgt, _, _), v in zip(adds[n_ga:], add_v[n_ga:]):
            ct_t[tgt] = ct_t[tgt] + v
        outs_v, vjp_fn = jax.vjp(f, *diff)
        grads = list(vjp_fn(tuple(c.astype(o.dtype) for c, o in zip(ct_t, outs_v))))
        for (tgt, _, _), v in zip(adds[:n_ga], add_v[:n_ga]):
            grads[tgt] = grads[tgt] + v
        out_refs = refs[n_in:]
        for o_ref, g in zip(out_refs[:n_dr + n_ds], grads[:n_dr + n_ds]):
            o_ref[...] = g.astype(o_ref.dtype)
        for o_ref, g, j in zip(out_refs[n_dr + n_ds:], grads[n_dr + n_ds:], d_params):
            first = (pl.program_id(1) == 0) if j in per_b else ((pl.program_id(0) == 0) & (pl.program_id(1) == 0))

            @pl.when(first)
            def _(o_ref=o_ref):
                o_ref[...] = jnp.zeros_like(o_ref)
            o_ref[...] += g.astype(F32)

    out_shape, out_specs = [], []
    for j, dt in d_rows:
        out_shape.append(jax.ShapeDtypeStruct(rows[j].shape, dt))
        out_specs.append(pl.BlockSpec((tm, rows[j].shape[1]), lambda b, i: (b * ns + i, 0)))
    for j, dt in d_shifts:
        src = rows[shifts[j][0]]
        out_shape.append(jax.ShapeDtypeStruct(src.shape, dt))
        out_specs.append(pl.BlockSpec((tm, src.shape[1]), lambda b, i: (b * ns + i, 0)))
    for j in d_params:
        p = params[j]
        out_shape.append(jax.ShapeDtypeStruct(p.shape, F32))
        if j in per_b:
            out_specs.append(pl.BlockSpec((None,) + p.shape[1:], lambda b, i: (b, 0, 0)))
        else:
            out_specs.append(pl.BlockSpec(p.shape, lambda b, i, nd=p.ndim: (0,) * nd))
    return pl.pallas_call(body, name=name, grid=(nb, ns), in_specs=in_specs, out_specs=out_specs,
                          out_shape=out_shape, compiler_params=_cparams("arbitrary", "arbitrary"))(*all_rows, *[
                              all_rows[idx] for idx, _ in halo_keys], *params)


def _mm_nn(name, x, w, out_dtype, tm=512, tn=512):
    M, K = x.shape
    N = w.shape[1]
    tm, tn = min(tm, M), min(tn, N)

    def body(x_ref, w_ref, o_ref):
        o_ref[...] = jnp.dot(x_ref[...].astype(BF16), w_ref[...].astype(BF16),
                             preferred_element_type=F32).astype(o_ref.dtype)

    return pl.pallas_call(body, name=name, grid=(M // tm, N // tn),
                          in_specs=[pl.BlockSpec((tm, K), lambda i, j: (i, 0)), pl.BlockSpec((K, tn), lambda i, j: (0, j))],
                          out_specs=pl.BlockSpec((tm, tn), lambda i, j: (i, j)),
                          out_shape=jax.ShapeDtypeStruct((M, N), out_dtype),
                          compiler_params=_cparams("parallel", "parallel"))(x, w)


def _mm_nt(name, dy, w, out_dtype, tm=512, tk=512):
    M, N = dy.shape
    K = w.shape[0]
    tm, tk = min(tm, M), min(tk, K)

    def body(dy_ref, w_ref, o_ref):
        o_ref[...] = lax.dot_general(dy_ref[...].astype(BF16), w_ref[...].astype(BF16), (((1,), (1,)), ((), ())),
                                     preferred_element_type=F32).astype(o_ref.dtype)

    return pl.pallas_call(body, name=name, grid=(M // tm, K // tk),
                          in_specs=[pl.BlockSpec((tm, N), lambda i, j: (i, 0)), pl.BlockSpec((tk, N), lambda i, j: (j, 0))],
                          out_specs=pl.BlockSpec((tm, tk), lambda i, j: (i, j)),
                          out_shape=jax.ShapeDtypeStruct((M, K), out_dtype),
                          compiler_params=_cparams("parallel", "parallel"))(dy, w)


def _mm_tn(name, x, dy, tm=1024, tk=512, tn=512):
    M, K = x.shape
    N = dy.shape[1]
    tm, tk, tn = min(tm, M), min(tk, K), min(tn, N)

    def body(x_ref, dy_ref, o_ref):
        @pl.when(pl.program_id(2) == 0)
        def _():
            o_ref[...] = jnp.zeros_like(o_ref)
        o_ref[...] += lax.dot_general(x_ref[...].astype(BF16), dy_ref[...].astype(BF16), (((0,), (0,)), ((), ())),
                                      preferred_element_type=F32)

    return pl.pallas_call(body, name=name, grid=(K // tk, N // tn, M // tm),
                          in_specs=[pl.BlockSpec((tm, tk), lambda i, j, m: (m, i)), pl.BlockSpec((tm, tn), lambda i, j, m: (m, j))],
                          out_specs=pl.BlockSpec((tk, tn), lambda i, j, m: (i, j)),
                          out_shape=jax.ShapeDtypeStruct((K, N), F32),
                          compiler_params=_cparams("parallel", "parallel", "arbitrary"))(x, dy)


def _lru_scan(name, a, b, seq, ct=128):
    T, C = a.shape

    def body(a_ref, b_ref, h_ref):
        A, Bv = a_ref[...], b_ref[...]
        row = lax.broadcasted_iota(jnp.int32, (seq, 1), 0)
        k = 1
        while k < seq:
            keep = row >= k
            Bv = A * jnp.where(keep, pltpu.roll(Bv, k, 0), 0.0) + Bv
            A = A * jnp.where(keep, pltpu.roll(A, k, 0), 1.0)
            k *= 2
        h_ref[...] = Bv

    spec = pl.BlockSpec((seq, ct), lambda b_, j: (b_, j))
    return pl.pallas_call(body, name=name, grid=(T // seq, C // ct), in_specs=[spec, spec], out_specs=spec,
                          out_shape=jax.ShapeDtypeStruct((T, C), F32),
                          compiler_params=_cparams("parallel", "parallel"))(a, b)


def _lru_scan_bwd(name, a, h, dh, seq, ct=128):
    T, C = a.shape

    def body(a_ref, h_ref, dh_ref, da_ref, db_ref):
        row = lax.broadcasted_iota(jnp.int32, (seq, 1), 0)
        Cf = jnp.where(row < seq - 1, pltpu.roll(a_ref[...], seq - 1, 0), 0.0)
        G = dh_ref[...]
        k = 1
        while k < seq:
            keep = row + k < seq
            G = Cf * jnp.where(keep, pltpu.roll(G, seq - k, 0), 0.0) + G
            Cf = Cf * jnp.where(keep, pltpu.roll(Cf, seq - k, 0), 1.0)
            k *= 2
        db_ref[...] = G
        da_ref[...] = G * jnp.where(row >= 1, pltpu.roll(h_ref[...], 1, 0), 0.0)

    spec = pl.BlockSpec((seq, ct), lambda b_, j: (b_, j))
    return pl.pallas_call(body, name=name, grid=(T // seq, C // ct), in_specs=[spec] * 3, out_specs=[spec] * 2,
                          out_shape=[jax.ShapeDtypeStruct((T, C), F32)] * 2,
                          compiler_params=_cparams("parallel", "parallel"))(a, h, dh)


def _pair_consts():
    jj = lax.broadcasted_iota(jnp.int32, (RWKV_N, LANES), 0)
    ll = lax.broadcasted_iota(jnp.int32, (RWKV_N, LANES), 1)
    pick = ((ll & (RWKV_N - 1)) == jj).astype(F32)
    l0 = lax.broadcasted_iota(jnp.int32, (LANES, LANES), 0)
    l1 = lax.broadcasted_iota(jnp.int32, (LANES, LANES), 1)
    same = ((l0 >> 6) == (l1 >> 6)).astype(BF16)
    return pick, same


def _col_raw(x, pick, same):
    return jnp.dot((pick * x).astype(BF16), same, preferred_element_type=F32)


def _col2_raw(x, pick, same):
    hi = x.astype(BF16).astype(F32)
    return _col_raw(hi, pick, same) + _col_raw(x - hi, pick, same)


def _col_bwd(res, g):
    pick, same = res
    m = jnp.dot(g.astype(BF16), same, preferred_element_type=F32)
    return jnp.sum(pick * m, axis=0, keepdims=True), jnp.zeros_like(pick), jnp.zeros_like(same)


_col = jax.custom_vjp(_col_raw)
_col.defvjp(lambda x, pick, same: (_col_raw(x, pick, same), (pick, same)), _col_bwd)
_col2 = jax.custom_vjp(_col2_raw)
_col2.defvjp(lambda x, pick, same: (_col2_raw(x, pick, same), (pick, same)), _col_bwd)


def _make_col(pick, same, diff):
    col, col2 = (_col, _col2) if diff else (_col_raw, _col2_raw)
    return (lambda x: col(x, pick, same)), (lambda x: col2(x, pick, same))


def _rwkv_step(col, col2, s0, r, w, k, v, a, b):
    sa = jnp.sum(s0 * col(a), axis=0, keepdims=True)
    s1 = s0 * col2(w) + col(b) * sa + col(k) * v
    y = jnp.sum(s1 * col(r), axis=0, keepdims=True)
    return s1, y


def _rwkv_specs(C, seq, tc, G, reverse):
    nc = seq // tc
    if reverse:
        row = pl.BlockSpec((tc, LANES * G), lambda b, g, c: (b * nc + nc - 1 - c, g))
        st = pl.BlockSpec((tc, G, RWKV_N, LANES), lambda b, g, c: (b * nc + nc - 1 - c, g, 0, 0))
    else:
        row = pl.BlockSpec((tc, LANES * G), lambda b, g, c: (b * nc + c, g))
        st = pl.BlockSpec((tc, G, RWKV_N, LANES), lambda b, g, c: (b * nc + c, g, 0, 0))
    return row, st


def _rwkv_fwd(name, r, w, k, v, a, b, seq, tc=32, G=4):
    T, C = r.shape
    tc = min(tc, seq)
    NP = C // LANES
    row, st = _rwkv_specs(C, seq, tc, G, False)

    def body(r_ref, w_ref, k_ref, v_ref, a_ref, b_ref, y_ref, st_ref, s_scr):
        @pl.when(pl.program_id(2) == 0)
        def _():
            s_scr[...] = jnp.zeros_like(s_scr)
        pick, same = _pair_consts()
        col, col2 = _make_col(pick, same, False)

        rid = lax.broadcasted_iota(jnp.int32, (SUB, LANES), 0)

        def group(t8, carry):
            base = pl.multiple_of(t8 * SUB, SUB)
            lanes = [pl.ds(LANES * p, LANES) for p in range(G)]
            blk = [[x[pl.ds(base, SUB), ln] for x in (r_ref, w_ref, k_ref, v_ref, a_ref, b_ref)] for ln in lanes]
            s = [s_scr[p] for p in range(G)]
            ys = [jnp.zeros((SUB, LANES), F32) for _ in range(G)]
            for i in range(SUB):
                for p in range(G):
                    st_ref[base + i, p] = s[p]
                    s[p], y = _rwkv_step(col, col2, s[p], *[b_[i:i + 1] for b_ in blk[p]])
                    ys[p] = jnp.where(rid == i, y, ys[p])
            for p in range(G):
                s_scr[p] = s[p]
                y_ref[pl.ds(base, SUB), lanes[p]] = ys[p]
            return carry

        lax.fori_loop(0, tc // SUB, group, 0)

    return pl.pallas_call(body, name=name, grid=(T // seq, NP // G, seq // tc), in_specs=[row] * 6, out_specs=[row, st],
                          out_shape=[jax.ShapeDtypeStruct((T, C), F32), jax.ShapeDtypeStruct((T, NP, RWKV_N, LANES), F32)],
                          scratch_shapes=[pltpu.VMEM((G, RWKV_N, LANES), F32)],
                          compiler_params=_cparams("parallel", "parallel", "arbitrary"))(r, w, k, v, a, b)


def _rwkv_bwd(name, r, w, k, v, a, b, states, dy, dr_p, dk_p, dv_p, seq, tc=32, G=4):
    T, C = r.shape
    tc = min(tc, seq)
    NP = C // LANES
    row, st = _rwkv_specs(C, seq, tc, G, True)

    def body(r_ref, w_ref, k_ref, v_ref, a_ref, b_ref, st_ref, dy_ref, drp_ref, dkp_ref, dvp_ref,
             dr_ref, dw_ref, dk_ref, dv_ref, da_ref, db_ref, ds_scr):
        @pl.when(pl.program_id(2) == 0)
        def _():
            ds_scr[...] = jnp.zeros_like(ds_scr)
        pick, same = _pair_consts()
        col, col2 = _make_col(pick, same, True)
        step_fn = functools.partial(_rwkv_step, col, col2)

        rid = lax.broadcasted_iota(jnp.int32, (SUB, LANES), 0)
        out_refs = (dr_ref, dw_ref, dk_ref, dv_ref, da_ref, db_ref)

        def group(n, carry):
            base = pl.multiple_of((tc // SUB - 1 - n) * SUB, SUB)
            lanes = [pl.ds(LANES * p, LANES) for p in range(G)]
            blk = [[x[pl.ds(base, SUB), ln] for x in (r_ref, w_ref, k_ref, v_ref, a_ref, b_ref, dy_ref)] for ln in lanes]
            ds = [ds_scr[p] for p in range(G)]
            gs = [[drp_ref[pl.ds(base, SUB), ln], jnp.zeros((SUB, LANES), F32), dkp_ref[pl.ds(base, SUB), ln],
                   dvp_ref[pl.ds(base, SUB), ln], jnp.zeros((SUB, LANES), F32), jnp.zeros((SUB, LANES), F32)]
                  for ln in lanes]
            for i in reversed(range(SUB)):
                for p in range(G):
                    rows = [b_[i:i + 1] for b_ in blk[p]]
                    _, vjp_fn = jax.vjp(step_fn, st_ref[base + i, p], *rows[:6])
                    got = vjp_fn((ds[p], rows[6]))
                    ds[p] = got[0]
                    for j in range(6):
                        gs[p][j] = gs[p][j] + jnp.where(rid == i, got[1 + j], 0.0)
            for p in range(G):
                ds_scr[p] = ds[p]
                for j in range(6):
                    out_refs[j][pl.ds(base, SUB), lanes[p]] = gs[p][j]
            return carry

        lax.fori_loop(0, tc // SUB, group, 0)

    return pl.pallas_call(body, name=name, grid=(T // seq, NP // G, seq // tc),
                          in_specs=[row] * 6 + [st] + [row] * 4, out_specs=[row] * 6,
                          out_shape=[jax.ShapeDtypeStruct((T, C), F32)] * 6,
                          scratch_shapes=[pltpu.VMEM((G, RWKV_N, LANES), F32)],
                          compiler_params=_cparams("parallel", "parallel", "arbitrary"))(
                              r, w, k, v, a, b, states, dy, dr_p, dk_p, dv_p)


def _my_index():
    return 4 * lax.axis_index("x") + 2 * lax.axis_index("y") + lax.axis_index("c")


def _peer(k):
    x, y, c = lax.axis_index("x"), lax.axis_index("y"), lax.axis_index("c")
    return (x ^ ((k >> 2) & 1), y ^ ((k >> 1) & 1), c ^ (k & 1))


def _all_gather(name, shard):
    def body(x_ref, o_ref, send_sems, recv_sems, local_sem):
        me = _my_index()
        mine = pltpu.make_async_copy(x_ref, o_ref.at[me], local_sem)
        mine.start()
        copies = []
        for k in range(1, N_DEV):
            cp = pltpu.make_async_remote_copy(src_ref=x_ref, dst_ref=o_ref.at[me], send_sem=send_sems.at[k - 1],
                                              recv_sem=recv_sems.at[k - 1], device_id=_peer(k), device_id_type=MESH)
            cp.start()
            copies.append(cp)
        for k in range(1, N_DEV):
            pltpu.make_async_remote_copy(src_ref=x_ref, dst_ref=o_ref.at[me ^ k], send_sem=send_sems.at[k - 1],
                                         recv_sem=recv_sems.at[k - 1], device_id=_peer(k), device_id_type=MESH).wait_recv()
        for cp in copies:
            cp.wait_send()
        mine.wait()

    hbm = pl.BlockSpec(memory_space=pltpu.HBM)
    return pl.pallas_call(body, name=name, in_specs=[hbm], out_specs=hbm,
                          out_shape=jax.ShapeDtypeStruct((N_DEV,) + shard.shape, shard.dtype),
                          scratch_shapes=[pltpu.SemaphoreType.DMA((N_DEV - 1,)), pltpu.SemaphoreType.DMA((N_DEV - 1,)),
                                          pltpu.SemaphoreType.DMA])(shard)


def _all_to_all(name, blocks):
    def body(x_ref, o_ref, send_sems, recv_sems, local_sem):
        me = _my_index()
        mine = pltpu.make_async_copy(x_ref.at[me], o_ref.at[me], local_sem)
        mine.start()
        copies = []
        for k in range(1, N_DEV):
            cp = pltpu.make_async_remote_copy(src_ref=x_ref.at[me ^ k], dst_ref=o_ref.at[me], send_sem=send_sems.at[k - 1],
                                              recv_sem=recv_sems.at[k - 1], device_id=_peer(k), device_id_type=MESH)
            cp.start()
            copies.append(cp)
        for k in range(1, N_DEV):
            pltpu.make_async_remote_copy(src_ref=x_ref.at[me], dst_ref=o_ref.at[me ^ k], send_sem=send_sems.at[k - 1],
                                         recv_sem=recv_sems.at[k - 1], device_id=_peer(k), device_id_type=MESH).wait_recv()
        for cp in copies:
            cp.wait_send()
        mine.wait()

    hbm = pl.BlockSpec(memory_space=pltpu.HBM)
    return pl.pallas_call(body, name=name, in_specs=[hbm], out_specs=hbm,
                          out_shape=jax.ShapeDtypeStruct(blocks.shape, blocks.dtype),
                          scratch_shapes=[pltpu.SemaphoreType.DMA((N_DEV - 1,)), pltpu.SemaphoreType.DMA((N_DEV - 1,)),
                                          pltpu.SemaphoreType.DMA])(blocks)


def _adamw(name, parts, w, m, v, tr):
    R, W = w.shape

    def body(p_ref, w_ref, m_ref, v_ref, g_ref, d_ref, nm_ref, nv_ref):
        g = p_ref[0]
        for s in range(1, N_DEV):
            g = g + p_ref[s]
        nm = ADAM_B1 * m_ref[...] + (1.0 - ADAM_B1) * g
        nv = ADAM_B2 * v_ref[...] + (1.0 - ADAM_B2) * (g * g)
        m_hat = nm / (1.0 - ADAM_B1 ** ADAM_STEP)
        v_hat = nv / (1.0 - ADAM_B2 ** ADAM_STEP)
        g_ref[...] = g
        d_ref[...] = -ADAM_LR * (m_hat / (jnp.sqrt(v_hat) + ADAM_EPS) + ADAM_WD * w_ref[...])
        nm_ref[...] = nm
        nv_ref[...] = nv

    spec = pl.BlockSpec((tr, W), lambda i: (i, 0))
    return pl.pallas_call(body, name=name, grid=(R // tr,),
                          in_specs=[pl.BlockSpec((N_DEV, tr, W), lambda i: (0, i, 0)), spec, spec, spec],
                          out_specs=[spec] * 4, out_shape=[jax.ShapeDtypeStruct((R, W), F32)] * 4,
                          compiler_params=_cparams("parallel"))(parts, w, m, v)


def _rms(x, g):
    return x * lax.rsqrt(jnp.mean(x * x, axis=-1, keepdims=True) + RMS_EPS) * g


def _softplus(z):
    return jnp.maximum(z, 0.0) + jnp.log(1.0 + jnp.exp(-jnp.abs(z)))


def _neg_expm1(z):
    series = -z * (1.0 + z * (0.5 + z * (1.0 / 6.0)))
    return jnp.where(z > -1e-3, series, 1.0 - jnp.exp(z))


def _gelu_tanh(x):
    return 0.5 * x * (1.0 + jnp.tanh(0.7978845608028654 * (x + 0.044715 * x * x * x)))


def _bdot(x, w):
    return jnp.dot(x.astype(BF16), w.astype(BF16), preferred_element_type=F32)


def _seg_consts(C):
    c0 = lax.broadcasted_iota(jnp.int32, (C, LANES), 0)
    h1 = lax.broadcasted_iota(jnp.int32, (C, LANES), 1)
    red = ((c0 >> 6) == h1).astype(BF16)
    h0 = lax.broadcasted_iota(jnp.int32, (LANES, C), 0)
    c1 = lax.broadcasted_iota(jnp.int32, (LANES, C), 1)
    exp = ((c1 >> 6) == h0).astype(BF16)
    return red, exp


def _split_dot(x, m):
    hi = x.astype(BF16)
    lo = (x - hi.astype(F32)).astype(BF16)
    return jnp.dot(hi, m, preferred_element_type=F32) + jnp.dot(lo, m, preferred_element_type=F32)


def _segsum_raw(x):
    red, exp = _seg_consts(x.shape[1])
    return _split_dot(_split_dot(x, red), exp)


@jax.custom_vjp
def _segsum(x):
    return _segsum_raw(x)


_segsum.defvjp(lambda x: (_segsum_raw(x), None), lambda _, g: (_segsum_raw(g),))


def _f_pre(out_dtype):
    def fn(pos, x, g):
        return (_rms(x, g).astype(out_dtype),)
    return fn


def _f_post(pos, x, t, g):
    return (x + _rms(t, g),)


def _f_post_bias(pos, x, t, g, bias):
    return (x + _rms(t + bias, g),)


def _f_conv(pos, proj, p1, p2, p3, b_u, w0, w1, w2, w3, cb):
    D = b_u.shape[1]
    u0 = proj[:, D:] + b_u
    u1 = jnp.where(pos >= 1, p1[:, D:] + b_u, 0.0)
    u2 = jnp.where(pos >= 2, p2[:, D:] + b_u, 0.0)
    u3 = jnp.where(pos >= 3, p3[:, D:] + b_u, 0.0)
    return (cb + u3 * w0 + u2 * w1 + u1 * w2 + u0 * w3,)


def _f_conv_bwd(pos, dconv, proj, dyb, n1, n2, n3, p1, p2, p3, b_u, w0, w1, w2, w3):
    D = b_u.shape[1]
    du = dconv * w3 + n1 * w2 + n2 * w1 + n3 * w0
    dproj = jnp.concatenate([dyb[:, :D], du], axis=1)
    u0 = proj[:, D:] + b_u
    u1 = jnp.where(pos >= 1, p1[:, D:] + b_u, 0.0)
    u2 = jnp.where(pos >= 2, p2[:, D:] + b_u, 0.0)
    u3 = jnp.where(pos >= 3, p3[:, D:] + b_u, 0.0)
    rs = lambda z: jnp.sum(z, axis=0, keepdims=True)
    return (dproj, rs(dproj), rs(dconv * u3), rs(dconv * u2), rs(dconv * u1), rs(dconv * u0), rs(dconv))


def _f_gates(pos, conv, proj, gate_w, gb_r, gb_i, lam, b_y):
    D = conv.shape[1]
    blk = D // LRU_HEADS
    cb = conv.astype(BF16)

    def gate(g, bias):
        z = [jnp.dot(cb[:, h * blk:(h + 1) * blk], gate_w[(g * LRU_HEADS + h) * blk:(g * LRU_HEADS + h + 1) * blk, :].astype(BF16),
                     preferred_element_type=F32) for h in range(LRU_HEADS)]
        return jax.nn.sigmoid(jnp.concatenate(z, axis=1) + bias)

    r_gate, i_gate = gate(0, gb_r), gate(1, gb_i)
    log_a = -LRU_C * r_gate * _softplus(-lam)
    a = jnp.exp(log_a)
    b = jnp.sqrt(_neg_expm1(2.0 * log_a)) * i_gate * conv
    yb = _gelu_tanh(proj[:, :D] + b_y)
    return a, b, yb


def _f_hy(pos, h, yb):
    return ((h * yb).astype(BF16),)


def _f_mix(pos, hn, hp, mu_r, mu_w, mu_k, mu_v, mu_a, mu_g, w0, w1, w2, a0, a1, a2, g1, g2):
    xx = hp - hn
    xr, xw, xk, xv, xa, xg = (hn + xx * m for m in (mu_r, mu_w, mu_k, mu_v, mu_a, mu_g))
    lw = w0 + _bdot(jnp.tanh(_bdot(xw, w1)), w2)
    decay = jnp.exp(-jnp.exp(-_softplus(-lw) - 0.5))
    a = jax.nn.sigmoid(a0 + _bdot(_bdot(xa, a1), a2))
    g = _bdot(jax.nn.sigmoid(_bdot(xg, g1)), g2)
    return xr.astype(BF16), xk.astype(BF16), xv.astype(BF16), decay, a, g


def _f_kk(pos, k, a, k_k, k_a):
    kk = k * k_k
    kk = kk / jnp.maximum(jnp.sqrt(_segsum(kk * kk)), 1e-12)
    return k * (1.0 + (a - 1.0) * k_a), -kk, kk * a


def _f_gn(pos, y, r, k2, v, g, gn_g, gn_b, r_k):
    inv_n = 1.0 / RWKV_N
    yc = y - _segsum(y) * inv_n
    var = _segsum(yc * yc) * inv_n
    yn = yc * lax.rsqrt(var + GN_EPS) * gn_g + gn_b
    bonus = _segsum(r * k2 * r_k) * v
    return (((yn + bonus) * g).astype(BF16),)


def _f_attn(pos, q, k, v):
    D = q.shape[1]
    hd = D // XATTN_HEADS
    qb, kb, vb = q.astype(BF16), k.astype(BF16), v.astype(BF16)
    outs = []
    for h in range(XATTN_HEADS):
        sl = slice(h * hd, (h + 1) * hd)
        s = lax.dot_general(qb[:, sl], kb[:, sl], (((1,), (1,)), ((), ())), preferred_element_type=F32) * (hd ** -0.5)
        s = s - jnp.max(s, axis=-1, keepdims=True)
        e = jnp.exp(s)
        p = e / jnp.sum(e, axis=-1, keepdims=True)
        outs.append(jnp.dot(p.astype(BF16), vb[:, sl], preferred_element_type=F32))
    return (jnp.concatenate(outs, axis=1).astype(BF16),)


def _f_act(pos, up):
    return (jnp.square(jnp.maximum(up, 0.0)).astype(BF16),)


def _f_loss(pos, y, target):
    d = y - target
    inv = 1.0 / y.shape[1]
    part = 0.5 * inv * jnp.sum(jnp.sum(d * d, axis=1, keepdims=True), axis=0, keepdims=True)
    return d * inv, jnp.broadcast_to(part, (8, LANES))


def _full_from_blocks(blocks, shard_shape, ax):
    g = jnp.moveaxis(blocks.reshape((N_DEV,) + tuple(shard_shape)), 0, ax)
    return g.reshape(tuple(shard_shape[:ax]) + (N_DEV * shard_shape[ax],) + tuple(shard_shape[ax + 1:]))


def _blocks_from_full(full, shard_shape, ax):
    g = full.reshape(tuple(shard_shape[:ax]) + (N_DEV, shard_shape[ax]) + tuple(shard_shape[ax + 1:]))
    return jnp.moveaxis(g, ax, 0).reshape(N_DEV, -1)


def _numel(shape):
    n = 1
    for s in shape:
        n *= s
    return n


def _pack_flat(arrs, total):
    flat = jnp.concatenate([a.reshape(-1).astype(F32) for a in arrs])
    return jnp.pad(flat, (0, total - flat.shape[0]))


def _train_step(x3, mem3, target3, P, M_, V_):
    Bl, S, D = x3.shape
    T = Bl * S
    ML = mem3.shape[1]
    x = x3.reshape(T, D)
    mem = mem3.reshape(Bl * ML, D)
    target = target3.reshape(T, D)
    shp = {n: P[n].shape for n in WEIGHTS}

    n_big = sum(_numel(shp[n]) for n in BIG)
    n_small = sum(_numel(shp[n]) for n in SMALL)
    n_small_pad = -(-n_small // (8 * PACK_W)) * (8 * PACK_W)
    big_rows, small_rows = n_big // PACK_W, n_small_pad // PACK_W
    big_pack = _pack_flat([P[n] for n in BIG], n_big).reshape(big_rows, PACK_W).astype(BF16)
    small_pack = _pack_flat([P[n] for n in SMALL], n_small_pad).reshape(small_rows, PACK_W)
    big_all = _all_gather("gather_matrices", big_pack).reshape(N_DEV, n_big)
    small_all = _all_gather("gather_vectors", small_pack).reshape(N_DEV, n_small_pad)
    Wf = {}
    off = 0
    for n in BIG:
        ne = _numel(shp[n])
        Wf[n] = _full_from_blocks(big_all[:, off:off + ne], shp[n], SHARD_AXIS[n])
        off += ne
    off = 0
    for n in SMALL:
        ne = _numel(shp[n])
        Wf[n] = _full_from_blocks(small_all[:, off:off + ne], shp[n], SHARD_AXIS[n])
        off += ne
    for n in REPL:
        Wf[n] = P[n]

    row = lambda v: v.reshape(1, -1).astype(F32)
    gains = [[row(Wf['ln_gains'][i, j]) for j in range(6)] for i in range(2)]
    LW = 128

    def pad_cols(w):
        return jnp.pad(w, ((0, 0), (0, LW - w.shape[1]))).astype(F32)

    def pad_rows(w):
        return jnp.pad(w, ((0, LW - w.shape[0]), (0, 0))).astype(F32)

    G = {}
    saved = {}

    mem_n, = _rw("mem_norm_fwd", _f_pre(BF16), [mem], [row(Wf['mem_norm'])], outs=[(D, BF16)])

    def xattn_fwd(i, xin):
        hq, = _rw(f"xattn{i}_pre", _f_pre(BF16), [xin], [gains[i][2]], outs=[(D, BF16)], seq=S)
        q = _mm_nn(f"xattn{i}_q", hq, Wf['c_w_q'][i], BF16)
        kv = _mm_nn(f"xattn{i}_kv", mem_n, Wf['c_w_kv'][i], F32)
        kb = kv[:, :D].reshape(Bl, ML, D)
        vb = kv[:, D:].reshape(Bl, ML, D)
        o, = _rw(f"xattn{i}_attn", _f_attn, [q], [kb, vb], outs=[(D, BF16)], seq=S, per_b=(0, 1))
        c = _mm_nn(f"xattn{i}_o", o, Wf['c_w_o'][i], F32)
        xo, = _rw(f"xattn{i}_post", _f_post, [xin, c], [gains[i][3]], outs=[(D, F32)], seq=S)
        saved[f"xattn{i}"] = (xin, hq, q, kb, vb, o, c)
        return xo

    def mlp_fwd(i, xin):
        hm, = _rw(f"mlp{i}_pre", _f_pre(BF16), [xin], [gains[i][4]], outs=[(D, BF16)], seq=S)
        up = _mm_nn(f"mlp{i}_up", hm, Wf['m_w_up'][i], F32)
        act, = _rw(f"mlp{i}_act", _f_act, [up], outs=[(up.shape[1], BF16)], seq=S)
        m = _mm_nn(f"mlp{i}_down", act, Wf['m_w_down'][i], F32)
        xo, = _rw(f"mlp{i}_post", _f_post, [xin, m], [gains[i][5]], outs=[(D, F32)], seq=S)
        saved[f"mlp{i}"] = (xin, hm, up, act, m)
        return xo

    b_in = row(Wf['a_b_in'][0])
    b_y, b_u = b_in[:, :D], b_in[:, D:]
    cw = [row(Wf['a_conv_w'][0, t]) for t in range(4)]
    cb = row(Wf['a_conv_b'][0])
    gate_w = Wf['a_gate_w'][0].reshape(2 * LRU_HEADS * (D // LRU_HEADS), D // LRU_HEADS).astype(F32)
    gate_b = Wf['a_gate_b'][0].reshape(2, D).astype(F32)
    gb_r, gb_i = gate_b[0:1], gate_b[1:2]
    lam = row(Wf['a_lambda'][0])
    b_out = row(Wf['a_b_out'][0])

    hn0, = _rw("lru_pre", _f_pre(BF16), [x], [gains[0][0]], outs=[(D, BF16)], seq=S)
    proj = _mm_nn("lru_in", hn0, Wf['a_w_in'][0], F32)
    prev3 = [(0, 1), (0, 2), (0, 3)]
    conv, = _rw("lru_conv", _f_conv, [proj], [b_u] + cw + [cb], outs=[(D, F32)], shifts=prev3, seq=S)
    gate_par = [gate_w, gb_r, gb_i, lam, b_y]
    a_l, b_l, yb = _rw("lru_gates", _f_gates, [conv, proj], gate_par, outs=[(D, F32)] * 3, seq=S)
    h_l = _lru_scan("lru_scan", a_l, b_l, S)
    hy, = _rw("lru_hy", _f_hy, [h_l, yb], outs=[(D, BF16)], seq=S)
    t0 = _mm_nn("lru_out", hy, Wf['a_w_out'][0], F32)
    x1, = _rw("lru_post", _f_post_bias, [x, t0], [gains[0][1], b_out], outs=[(D, F32)], seq=S)
    x2 = xattn_fwd(0, x1)
    x3_ = mlp_fwd(0, x2)

    mu = [row(Wf['b_mu'][0, j]) for j in range(6)]
    lora = [row(Wf['b_w0'][0]), pad_cols(Wf['b_w1'][0]), pad_rows(Wf['b_w2'][0]),
            row(Wf['b_a0'][0]), pad_cols(Wf['b_a1'][0]), pad_rows(Wf['b_a2'][0]),
            Wf['b_g1'][0].astype(F32), Wf['b_g2'][0].astype(F32)]
    k_k, k_a = row(Wf['b_k_k'][0]), row(Wf['b_k_a'][0])
    gn_g, gn_b, r_k = row(Wf['b_gn_g'][0]), row(Wf['b_gn_b'][0]), row(Wf['b_r_k'][0])

    hn1, = _rw("rwkv_pre", _f_pre(F32), [x3_], [gains[1][0]], outs=[(D, F32)], seq=S)
    xr, xk, xv, decay, a_r, g_r = _rw("rwkv_mix", _f_mix, [hn1], mu + lora,
                                      outs=[(D, BF16)] * 3 + [(D, F32)] * 3, shifts=[(0, 1)], seq=S)
    r_ = _mm_nn("rwkv_r", xr, Wf['b_w_rkv'][0, 0], F32)
    k_ = _mm_nn("rwkv_k", xk, Wf['b_w_rkv'][0, 1], F32)
    v_ = _mm_nn("rwkv_v", xv, Wf['b_w_rkv'][0, 2], F32)
    k2, ra, rb = _rw("rwkv_kk", _f_kk, [k_, a_r], [k_k, k_a], outs=[(D, F32)] * 3, seq=S)
    y_r, states = _rwkv_fwd("rwkv_scan", r_, decay, k2, v_, ra, rb, S)
    gn_par = [gn_g, gn_b, r_k]
    og, = _rw("rwkv_gn", _f_gn, [y_r, r_, k2, v_, g_r], gn_par, outs=[(D, BF16)], seq=S)
    t1 = _mm_nn("rwkv_out", og, Wf['b_w_o'][0], F32)
    x4, = _rw("rwkv_post", _f_post, [x3_, t1], [gains[1][1]], outs=[(D, F32)], seq=S)
    x5 = xattn_fwd(1, x4)
    x6 = mlp_fwd(1, x5)

    dx, loss_acc = _rw("loss", _f_loss, [x6, target], outs=[(D, F32)], accs=[(8, LANES)], seq=S)
    loss = lax.psum(loss_acc[0, 0], ("x", "y", "c"))

    dgain = [[None] * 6 for _ in range(2)]
    dmem_n = None

    def mlp_bwd(i, dxo):
        xin, hm, up, act, m = saved[f"mlp{i}"]
        dm, dgain[i][5] = _rw_vjp(f"mlp{i}_post_bwd", _f_post, [xin, m], [gains[i][5]], [dxo], 1,
                                  d_rows=[(1, BF16)], d_params=[0], seq=S)
        G_down = _mm_tn(f"mlp{i}_down_dw", act, dm)
        dact = _mm_nt(f"mlp{i}_down_dx", dm, Wf['m_w_down'][i], F32)
        dup, = _rw_vjp(f"mlp{i}_act_bwd", _f_act, [up], [], [dact], 1, d_rows=[(0, BF16)], seq=S)
        G_up = _mm_tn(f"mlp{i}_up_dw", hm, dup)
        dhm = _mm_nt(f"mlp{i}_up_dx", dup, Wf['m_w_up'][i], F32)
        dxi, dgain[i][4] = _rw_vjp(f"mlp{i}_pre_bwd", _f_pre(F32), [xin], [gains[i][4]], [dhm], 1,
                                   d_rows=[(0, F32)], d_params=[0], adds=[(0, dxo, 0)], seq=S)
        return dxi, G_up, G_down

    def xattn_bwd(i, dxo):
        xin, hq, q, kb, vb, o, c = saved[f"xattn{i}"]
        dc, dgain[i][3] = _rw_vjp(f"xattn{i}_post_bwd", _f_post, [xin, c], [gains[i][3]], [dxo], 1,
                                  d_rows=[(1, BF16)], d_params=[0], seq=S)
        G_o = _mm_tn(f"xattn{i}_o_dw", o, dc)
        do = _mm_nt(f"xattn{i}_o_dx", dc, Wf['c_w_o'][i], F32)
        dq, dkb, dvb = _rw_vjp(f"xattn{i}_attn_bwd", _f_attn, [q], [kb, vb], [do], 1, d_rows=[(0, BF16)],
                               d_params=[0, 1], seq=S, per_b=(0, 1))
        dkv = jnp.concatenate([dkb.reshape(Bl * ML, D), dvb.reshape(Bl * ML, D)], axis=1)
        G_kv = _mm_tn(f"xattn{i}_kv_dw", mem_n, dkv)
        dmn = _mm_nt(f"xattn{i}_kv_dx", dkv, Wf['c_w_kv'][i], F32)
        G_q = _mm_tn(f"xattn{i}_q_dw", hq, dq)
        dhq = _mm_nt(f"xattn{i}_q_dx", dq, Wf['c_w_q'][i], F32)
        dxi, dgain[i][2] = _rw_vjp(f"xattn{i}_pre_bwd", _f_pre(F32), [xin], [gains[i][2]], [dhq], 1,
                                   d_rows=[(0, F32)], d_params=[0], adds=[(0, dxo, 0)], seq=S)
        return dxi, G_q, G_kv, G_o, dmn

    G_up, G_down, G_q, G_kv, G_o = [None] * 2, [None] * 2, [None] * 2, [None] * 2, [None] * 2
    dx, G_up[1], G_down[1] = mlp_bwd(1, dx)
    dx, G_q[1], G_kv[1], G_o[1], dmn1 = xattn_bwd(1, dx)

    dt1, dgain[1][1] = _rw_vjp("rwkv_post_bwd", _f_post, [x3_, t1], [gains[1][1]], [dx], 1,
                               d_rows=[(1, BF16)], d_params=[0], seq=S)
    G['b_w_o'] = _mm_tn("rwkv_out_dw", og, dt1)[None]
    dog = _mm_nt("rwkv_out_dx", dt1, Wf['b_w_o'][0], F32)
    dy_r, dr_p, dk2_p, dv_p, dg_r, d_gn_g, d_gn_b, d_r_k = _rw_vjp(
        "rwkv_gn_bwd", _f_gn, [y_r, r_, k2, v_, g_r], gn_par, [dog], 1,
        d_rows=[(j, F32) for j in range(5)], d_params=[0, 1, 2], seq=S)
    dr_, ddecay, dk2, dv_, dra, drb = _rwkv_bwd("rwkv_scan_bwd", r_, decay, k2, v_, ra, rb, states, dy_r,
                                                dr_p, dk2_p, dv_p, S)
    dk_, da_r, d_k_k, d_k_a = _rw_vjp("rwkv_kk_bwd", _f_kk, [k_, a_r], [k_k, k_a], [dk2, dra, drb], 3,
                                      d_rows=[(0, F32), (1, F32)], d_params=[0, 1], seq=S)
    G_rkv = [_mm_tn("rwkv_r_dw", xr, dr_), _mm_tn("rwkv_k_dw", xk, dk_), _mm_tn("rwkv_v_dw", xv, dv_)]
    G['b_w_rkv'] = jnp.stack(G_rkv)[None]
    dxr = _mm_nt("rwkv_r_dx", dr_, Wf['b_w_rkv'][0, 0], F32)
    dxk = _mm_nt("rwkv_k_dx", dk_, Wf['b_w_rkv'][0, 1], F32)
    dxv = _mm_nt("rwkv_v_dx", dv_, Wf['b_w_rkv'][0, 2], F32)
    mix_out = _rw_vjp("rwkv_mix_bwd", _f_mix, [hn1], mu + lora, [dxr, dxk, dxv, ddecay, da_r, dg_r], 6,
                      d_rows=[(0, F32)], d_shifts=[(0, F32)], d_params=list(range(14)), shifts=[(0, 1)], seq=S)
    dhn1, dhp1 = mix_out[0], mix_out[1]
    d_mu = mix_out[2:8]
    d_lora = mix_out[8:16]
    dx, dgain[1][0] = _rw_vjp("rwkv_pre_bwd", _f_pre(F32), [x3_], [gains[1][0]], [dhn1], 1, d_rows=[(0, F32)],
                              d_params=[0], adds=[(0, dx, 0)], ct_adds=[(0, dhp1, -1)], seq=S)
    G['b_mu'] = jnp.concatenate(d_mu, axis=0)[None]
    G['b_w0'], G['b_a0'] = d_lora[0], d_lora[3]
    G['b_w1'] = d_lora[1][:, :shp['b_w1'][2]][None]
    G['b_w2'] = d_lora[2][:shp['b_w2'][1]][None]
    G['b_a1'] = d_lora[4][:, :shp['b_a1'][2]][None]
    G['b_a2'] = d_lora[5][:shp['b_a2'][1]][None]
    G['b_g1'], G['b_g2'] = d_lora[6][None], d_lora[7][None]
    G['b_k_k'], G['b_k_a'], G['b_gn_g'], G['b_gn_b'] = d_k_k, d_k_a, d_gn_g, d_gn_b
    G['b_r_k'] = d_r_k.reshape(P['b_r_k'].shape)

    dx, G_up[0], G_down[0] = mlp_bwd(0, dx)
    dx, G_q[0], G_kv[0], G_o[0], dmn0 = xattn_bwd(0, dx)
    G['m_w_up'], G['m_w_down'] = jnp.stack(G_up), jnp.stack(G_down)
    G['c_w_q'], G['c_w_kv'], G['c_w_o'] = jnp.stack(G_q), jnp.stack(G_kv), jnp.stack(G_o)

    dt0, dgain[0][1], G['a_b_out'] = _rw_vjp("lru_post_bwd", _f_post_bias, [x, t0], [gains[0][1], b_out], [dx], 1,
                                             d_rows=[(1, BF16)], d_params=[0, 1], seq=S)
    G['a_w_out'] = _mm_tn("lru_out_dw", hy, dt0)[None]
    dhy = _mm_nt("lru_out_dx", dt0, Wf['a_w_out'][0], F32)
    dh_l, dyb = _rw_vjp("lru_hy_bwd", _f_hy, [h_l, yb], [], [dhy], 1, d_rows=[(0, F32), (1, F32)], seq=S)
    da_l, db_l = _lru_scan_bwd("lru_scan_bwd", a_l, h_l, dh_l, S)
    dconv, dproj_y, d_gate_w, d_gb_r, d_gb_i, G['a_lambda'] = _rw_vjp(
        "lru_gates_bwd", _f_gates, [conv, proj], gate_par, [da_l, db_l, dyb], 3,
        d_rows=[(0, F32), (1, F32)], d_params=[0, 1, 2, 3], seq=S)
    next3 = [(0, -1), (0, -2), (0, -3)]
    dproj, G['a_b_in'], dw0, dw1, dw2, dw3, G['a_conv_b'] = _rw(
        "lru_conv_bwd", _f_conv_bwd, [dconv, proj, dproj_y], [b_u] + cw, outs=[(2 * D, BF16)],
        accs=[(1, 2 * D)] + [(1, D)] * 5, shifts=next3 + [(1, 1), (1, 2), (1, 3)], seq=S)
    G['a_conv_w'] = jnp.concatenate([dw0, dw1, dw2, dw3], axis=0)[None]
    G['a_gate_w'] = d_gate_w.reshape(P['a_gate_w'].shape[:3] + (D // LRU_HEADS, D // LRU_HEADS))
    G['a_gate_b'] = jnp.concatenate([d_gb_r, d_gb_i], axis=0).reshape(1, 2, LRU_HEADS, D // LRU_HEADS)
    G['a_w_in'] = _mm_tn("lru_in_dw", hn0, dproj)[None]
    dhn0 = _mm_nt("lru_in_dx", dproj, Wf['a_w_in'][0], F32)
    grad_x, dgain[0][0] = _rw_vjp("lru_pre_bwd", _f_pre(F32), [x], [gains[0][0]], [dhn0], 1, d_rows=[(0, F32)],
                                  d_params=[0], adds=[(0, dx, 0)], seq=S)
    d_mem_norm, = _rw_vjp("mem_norm_bwd", _f_pre(F32), [mem], [row(Wf['mem_norm'])], [dmn0], 1, d_params=[0],
                          ct_adds=[(0, dmn1, 0)])
    G['mem_norm'] = d_mem_norm.reshape(-1)
    G['ln_gains'] = jnp.stack([jnp.concatenate(dgain[i], axis=0) for i in range(2)])

    sharded = BIG + SMALL
    g_blocks = jnp.concatenate([_blocks_from_full(G[n], shp[n], SHARD_AXIS[n]) for n in sharded], axis=1)
    g_blocks = jnp.pad(g_blocks, ((0, 0), (0, n_big + n_small_pad - g_blocks.shape[1])))
    rows_s = big_rows + small_rows
    parts_s = _all_to_all("exchange_grads", g_blocks.reshape(N_DEV, rows_s, PACK_W))
    n_repl = sum(_numel(shp[n]) for n in REPL)
    n_repl_pad = -(-n_repl // (8 * PACK_W)) * (8 * PACK_W)
    rows_r = n_repl_pad // PACK_W
    parts_r = _all_gather("gather_replicated_grads", _pack_flat([G[n] for n in REPL], n_repl_pad).reshape(rows_r, PACK_W))

    def pack(src, names, total, rows_):
        return _pack_flat([src[n] for n in names], total).reshape(rows_, PACK_W)

    tr = max(t for t in range(SUB, 161, SUB) if rows_s % t == 0)
    out_s = _adamw("adamw_sharded", parts_s, pack(P, sharded, n_big + n_small_pad, rows_s),
                   pack(M_, sharded, n_big + n_small_pad, rows_s), pack(V_, sharded, n_big + n_small_pad, rows_s), tr)
    out_r = _adamw("adamw_replicated", parts_r, pack(P, REPL, n_repl_pad, rows_r), pack(M_, REPL, n_repl_pad, rows_r),
                   pack(V_, REPL, n_repl_pad, rows_r), rows_r)

    def unpack(packed, names):
        flat = packed.reshape(-1)
        res, o = {}, 0
        for n in names:
            ne = _numel(shp[n])
            res[n] = flat[o:o + ne].reshape(shp[n])
            o += ne
        return res

    results = []
    for j in range(4):
        both = {**unpack(out_s[j], sharded), **unpack(out_r[j], REPL)}
        results += [both[n] for n in WEIGHTS]
    return (loss, grad_x.reshape(Bl, S, D), *results)


def kernel(x, mem, ln_gains, mem_norm, a_conv_w, a_conv_b, a_w_in, a_b_in, a_gate_w, a_gate_b, a_lambda, a_w_out, a_b_out, b_mu, b_w_rkv, b_w0, b_w1, b_w2, b_a0, b_a1, b_a2, b_g1, b_g2, b_k_k, b_k_a, b_r_k, b_gn_g, b_gn_b, b_w_o, c_w_q, c_w_kv, c_w_o, m_w_up, m_w_down, loss_target, m_ln_gains, m_mem_norm, m_a_conv_w, m_a_conv_b, m_a_w_in, m_a_b_in, m_a_gate_w, m_a_gate_b, m_a_lambda, m_a_w_out, m_a_b_out, m_b_mu, m_b_w_rkv, m_b_w0, m_b_w1, m_b_w2, m_b_a0, m_b_a1, m_b_a2, m_b_g1, m_b_g2, m_b_k_k, m_b_k_a, m_b_r_k, m_b_gn_g, m_b_gn_b, m_b_w_o, m_c_w_q, m_c_w_kv, m_c_w_o, m_m_w_up, m_m_w_down, v_ln_gains, v_mem_norm, v_a_conv_w, v_a_conv_b, v_a_w_in, v_a_b_in, v_a_gate_w, v_a_gate_b, v_a_lambda, v_a_w_out, v_a_b_out, v_b_mu, v_b_w_rkv, v_b_w0, v_b_w1, v_b_w2, v_b_a0, v_b_a1, v_b_a2, v_b_g1, v_b_g2, v_b_k_k, v_b_k_a, v_b_r_k, v_b_gn_g, v_b_gn_b, v_b_w_o, v_c_w_q, v_c_w_kv, v_c_w_o, v_m_w_up, v_m_w_down):
    weights = (ln_gains, mem_norm, a_conv_w, a_conv_b, a_w_in, a_b_in, a_gate_w, a_gate_b, a_lambda, a_w_out, a_b_out, b_mu, b_w_rkv, b_w0, b_w1, b_w2, b_a0, b_a1, b_a2, b_g1, b_g2, b_k_k, b_k_a, b_r_k, b_gn_g, b_gn_b, b_w_o, c_w_q, c_w_kv, c_w_o, m_w_up, m_w_down)
    moments1 = (m_ln_gains, m_mem_norm, m_a_conv_w, m_a_conv_b, m_a_w_in, m_a_b_in, m_a_gate_w, m_a_gate_b, m_a_lambda, m_a_w_out, m_a_b_out, m_b_mu, m_b_w_rkv, m_b_w0, m_b_w1, m_b_w2, m_b_a0, m_b_a1, m_b_a2, m_b_g1, m_b_g2, m_b_k_k, m_b_k_a, m_b_r_k, m_b_gn_g, m_b_gn_b, m_b_w_o, m_c_w_q, m_c_w_kv, m_c_w_o, m_m_w_up, m_m_w_down)
    moments2 = (v_ln_gains, v_mem_norm, v_a_conv_w, v_a_conv_b, v_a_w_in, v_a_b_in, v_a_gate_w, v_a_gate_b, v_a_lambda, v_a_w_out, v_a_b_out, v_b_mu, v_b_w_rkv, v_b_w0, v_b_w1, v_b_w2, v_b_a0, v_b_a1, v_b_a2, v_b_g1, v_b_g2, v_b_k_k, v_b_k_a, v_b_r_k, v_b_gn_g, v_b_gn_b, v_b_w_o, v_c_w_q, v_c_w_kv, v_c_w_o, v_m_w_up, v_m_w_down)
    return _train_step(x, mem, loss_target, dict(zip(WEIGHTS, weights)), dict(zip(WEIGHTS, moments1)),
                       dict(zip(WEIGHTS, moments2)))
```

```python
import functools

import jax
import jax.numpy as jnp
from jax import lax
from jax.experimental import pallas as pl
from jax.experimental.pallas import tpu as pltpu

F32 = jnp.float32
BF16 = jnp.bfloat16
N_DEV = 8
MESH = pl.DeviceIdType.MESH

V7X_VMEM_LIMIT_BYTES = 56 * 1024 * 1024
SUB = 8
HALO_ROWS = SUB
LANES = 128
RWKV_N = 64
RMS_EPS = 1e-6
GN_EPS = 64e-5
LRU_C = 8.0
XATTN_HEADS = 4
LRU_HEADS = 4

ADAM_LR, ADAM_B1, ADAM_B2, ADAM_EPS, ADAM_WD, ADAM_STEP = 0.001, 0.9, 0.999, 1e-8, 0.01, 10

WEIGHTS = ['ln_gains', 'mem_norm', 'a_conv_w', 'a_conv_b', 'a_w_in', 'a_b_in', 'a_gate_w', 'a_gate_b',
           'a_lambda', 'a_w_out', 'a_b_out', 'b_mu', 'b_w_rkv', 'b_w0', 'b_w1', 'b_w2', 'b_a0', 'b_a1',
           'b_a2', 'b_g1', 'b_g2', 'b_k_k', 'b_k_a', 'b_r_k', 'b_gn_g', 'b_gn_b', 'b_w_o', 'c_w_q',
           'c_w_kv', 'c_w_o', 'm_w_up', 'm_w_down']
SHARD_AXIS = {'ln_gains': 2, 'mem_norm': None, 'a_conv_w': 2, 'a_conv_b': None, 'a_w_in': 2, 'a_b_in': None,
              'a_gate_w': 3, 'a_gate_b': 3, 'a_lambda': None, 'a_w_out': 1, 'a_b_out': None, 'b_mu': 2,
              'b_w_rkv': 2, 'b_w0': 1, 'b_w1': 1, 'b_w2': 2, 'b_a0': 1, 'b_a1': 1, 'b_a2': 2, 'b_g1': 1,
              'b_g2': 2, 'b_k_k': 1, 'b_k_a': 1, 'b_r_k': None, 'b_gn_g': 1, 'b_gn_b': 1, 'b_w_o': 1,
              'c_w_q': 1, 'c_w_kv': 2, 'c_w_o': 1, 'm_w_up': 2, 'm_w_down': 1}
BIG = ['a_w_in', 'a_gate_w', 'a_w_out', 'b_w_rkv', 'b_w1', 'b_w2', 'b_a1', 'b_a2', 'b_g1', 'b_g2', 'b_w_o',
       'c_w_q', 'c_w_kv', 'c_w_o', 'm_w_up', 'm_w_down']
SMALL = ['ln_gains', 'a_conv_w', 'a_gate_b', 'b_mu', 'b_w0', 'b_a0', 'b_k_k', 'b_k_a', 'b_gn_g', 'b_gn_b']
REPL = ['mem_norm', 'a_conv_b', 'a_b_in', 'a_lambda', 'a_b_out', 'b_r_k']
PACK_W = 1024


def _cparams(*sem):
    return pltpu.CompilerParams(dimension_semantics=sem, vmem_limit_bytes=V7X_VMEM_LIMIT_BYTES)


def _shift_tile(tile, halo, k, pos, seq, tm):
    if k > 0:
        ext = jnp.concatenate([halo, tile], axis=0)
        r = pltpu.roll(ext, k, 0)[HALO_ROWS:HALO_ROWS + tm]
        return jnp.where(pos >= k, r, 0.0)
    kk = -k
    ext = jnp.concatenate([tile, halo], axis=0)
    r = pltpu.roll(ext, tm + HALO_ROWS - kk, 0)[0:tm]
    return jnp.where(pos + kk < seq, r, 0.0)


def _row_specs(rows, shifts, params, per_b, seq, tm):
    T = rows[0].shape[0]
    ns = seq // tm
    specs = [pl.BlockSpec((tm, r.shape[1]), lambda b, i: (b * ns + i, 0)) for r in rows]
    halo_keys = []
    for idx, k in shifts:
        key = (idx, k > 0)
        if key not in halo_keys:
            halo_keys.append(key)
    per8 = tm // HALO_ROWS
    last8 = T // HALO_ROWS - 1
    for idx, prev in halo_keys:
        c = rows[idx].shape[1]
        if prev:
            specs.append(pl.BlockSpec((HALO_ROWS, c), lambda b, i: (jnp.maximum((b * ns + i) * per8 - 1, 0), 0)))
        else:
            specs.append(pl.BlockSpec((HALO_ROWS, c), lambda b, i: (jnp.minimum((b * ns + i + 1) * per8, last8), 0)))
    for j, p in enumerate(params):
        if j in per_b:
            specs.append(pl.BlockSpec((None,) + p.shape[1:], lambda b, i: (b, 0, 0)))
        else:
            specs.append(pl.BlockSpec(p.shape, lambda b, i, nd=p.ndim: (0,) * nd))
    return specs, halo_keys


def _load_tiles(refs, rows, shifts, halo_keys, params, seq, tm):
    nr, nh = len(rows), len(halo_keys)
    i = pl.program_id(1)
    pos = i * tm + lax.broadcasted_iota(jnp.int32, (tm, 1), 0)
    row_t = [r[...] for r in refs[:nr]]
    halo_t = {key: refs[nr + j][...] for j, key in enumerate(halo_keys)}
    sh_t = [_shift_tile(row_t[idx], halo_t[(idx, k > 0)], k, pos, seq, tm) for idx, k in shifts]
    par_t = [p[...] for p in refs[nr + nh:nr + nh + len(params)]]
    return pos, row_t, sh_t, par_t


def _rw(name, fn, rows, params=(), outs=(), accs=(), shifts=(), seq=None, per_b=(), tm=256):
    rows, params = list(rows), list(params)
    T = rows[0].shape[0]
    seq = seq or T
    tm = min(tm, seq)
    nb, ns = T // seq, seq // tm
    in_specs, halo_keys = _row_specs(rows, shifts, params, per_b, seq, tm)
    n_in, no = len(in_specs), len(outs)

    def body(*refs):
        pos, row_t, sh_t, par_t = _load_tiles(refs, rows, shifts, halo_keys, params, seq, tm)
        res = fn(pos, *row_t, *sh_t, *par_t)
        out_refs = refs[n_in:n_in + no]
        acc_refs = refs[n_in + no:]
        for o_ref, v in zip(out_refs, res[:no]):
            o_ref[...] = v.astype(o_ref.dtype)
        if acc_refs:
            @pl.when((pl.program_id(0) == 0) & (pl.program_id(1) == 0))
            def _():
                for a_ref in acc_refs:
                    a_ref[...] = jnp.zeros_like(a_ref)
            for a_ref, v in zip(acc_refs, res[no:]):
                a_ref[...] += v

    out_shape = [jax.ShapeDtypeStruct((T, c), dt) for c, dt in outs]
    out_shape += [jax.ShapeDtypeStruct(s, F32) for s in accs]
    out_specs = [pl.BlockSpec((tm, c), lambda b, i: (b * ns + i, 0)) for c, _ in outs]
    out_specs += [pl.BlockSpec(s, lambda b, i: (0, 0)) for s in accs]
    return pl.pallas_call(body, name=name, grid=(nb, ns), in_specs=in_specs, out_specs=out_specs,
                          out_shape=out_shape, compiler_params=_cparams("arbitrary", "arbitrary"))(*rows, *[
                              rows[idx] for idx, _ in halo_keys], *params)


def _rw_vjp(name, fn, rows, params, cts, n_out, d_rows=(), d_shifts=(), d_params=(), adds=(), ct_adds=(),
            shifts=(), seq=None, per_b=(), tm=256):
    rows, params, cts = list(rows), list(params), list(cts)
    T = rows[0].shape[0]
    seq = seq or T
    tm = min(tm, seq)
    nb, ns = T // seq, seq // tm
    n_ga = len(adds)
    adds = list(adds) + list(ct_adds)
    add_rows = [a for _, a, _ in adds]
    add_shifts = [(j, k) for j, (_, _, k) in enumerate(adds) if k != 0]
    all_rows = rows + cts + add_rows
    off_ct, off_add = len(rows), len(rows) + len(cts)
    all_shifts = list(shifts) + [(off_add + j, k) for j, k in add_shifts]
    in_specs, halo_keys = _row_specs(all_rows, all_shifts, params, per_b, seq, tm)
    n_in = len(in_specs)
    n_dr, n_ds = len(d_rows), len(d_shifts)

    def body(*refs):
        pos, row_t, sh_t, par_t = _load_tiles(refs, all_rows, all_shifts, halo_keys, params, seq, tm)
        fn_rows, ct_t, add_t = row_t[:off_ct], row_t[off_ct:off_add], row_t[off_add:]
        fn_sh, add_sh = sh_t[:len(shifts)], sh_t[len(shifts):]
        diff = [fn_rows[j] for j, _ in d_rows] + [fn_sh[j] for j, _ in d_shifts] + [par_t[j] for j in d_params]

        def f(*d):
            r2, s2, p2 = list(fn_rows), list(fn_sh), list(par_t)
            for (j, _), v in zip(d_rows, d[:n_dr]):
                r2[j] = v
            for (j, _), v in zip(d_shifts, d[n_dr:n_dr + n_ds]):
                s2[j] = v
            for j, v in zip(d_params, d[n_dr + n_ds:]):
                p2[j] = v
            return tuple(fn(pos, *r2, *s2, *p2)[:n_out])

        sh_iter = iter(add_sh)
        add_v = [add_t[j] if k == 0 else next(sh_iter) for j, (_, _, k) in enumerate(adds)]
        ct_t = list(ct_t)
        for (tgt, _, _), v in zip(adds[n_ga:], add_v[n_ga:]):
            ct_t[tgt] = ct_t[tgt] + v
        outs_v, vjp_fn = jax.vjp(f, *diff)
        grads = list(vjp_fn(tuple(c.astype(o.dtype) for c, o in zip(ct_t, outs_v))))
        for (tgt, _, _), v in zip(adds[:n_ga], add_v[:n_ga]):
            grads[tgt] = grads[tgt] + v
        out_refs = refs[n_in:]
        for o_ref, g in zip(out_refs[:n_dr + n_ds], grads[:n_dr + n_ds]):
            o_ref[...] = g.astype(o_ref.dtype)
        for o_ref, g, j in zip(out_refs[n_dr + n_ds:], grads[n_dr + n_ds:], d_params):
            first = (pl.program_id(1) == 0) if j in per_b else ((pl.program_id(0) == 0) & (pl.program_id(1) == 0))

            @pl.when(first)
            def _(o_ref=o_ref):
                o_ref[...] = jnp.zeros_like(o_ref)
            o_ref[...] += g.astype(F32)

    out_shape, out_specs = [], []
    for j, dt in d_rows:
        out_shape.append(jax.ShapeDtypeStruct(rows[j].shape, dt))
        out_specs.append(pl.BlockSpec((tm, rows[j].shape[1]), lambda b, i: (b * ns + i, 0)))
    for j, dt in d_shifts:
        src = rows[shifts[j][0]]
        out_shape.append(jax.ShapeDtypeStruct(src.shape, dt))
        out_specs.append(pl.BlockSpec((tm, src.shape[1]), lambda b, i: (b * ns + i, 0)))
    for j in d_params:
        p = params[j]
        out_shape.append(jax.ShapeDtypeStruct(p.shape, F32))
        if j in per_b:
            out_specs.append(pl.BlockSpec((None,) + p.shape[1:], lambda b, i: (b, 0, 0)))
        else:
            out_specs.append(pl.BlockSpec(p.shape, lambda b, i, nd=p.ndim: (0,) * nd))
    return pl.pallas_call(body, name=name, grid=(nb, ns), in_specs=in_specs, out_specs=out_specs,
                          out_shape=out_shape, compiler_params=_cparams("arbitrary", "arbitrary"))(*all_rows, *[
                              all_rows[idx] for idx, _ in halo_keys], *params)


def _mm_nn(name, x, w, out_dtype, tm=512, tn=512):
    M, K = x.shape
    N = w.shape[1]
    tm, tn = min(tm, M), min(tn, N)

    def body(x_ref, w_ref, o_ref):
        o_ref[...] = jnp.dot(x_ref[...].astype(BF16), w_ref[...].astype(BF16),
                             preferred_element_type=F32).astype(o_ref.dtype)

    return pl.pallas_call(body, name=name, grid=(M // tm, N // tn),
                          in_specs=[pl.BlockSpec((tm, K), lambda i, j: (i, 0)), pl.BlockSpec((K, tn), lambda i, j: (0, j))],
                          out_specs=pl.BlockSpec((tm, tn), lambda i, j: (i, j)),
                          out_shape=jax.ShapeDtypeStruct((M, N), out_dtype),
                          compiler_params=_cparams("parallel", "parallel"))(x, w)


def _mm_nt(name, dy, w, out_dtype, tm=512, tk=512):
    M, N = dy.shape
    K = w.shape[0]
    tm, tk = min(tm, M), min(tk, K)

    def body(dy_ref, w_ref, o_ref):
        o_ref[...] = lax.dot_general(dy_ref[...].astype(BF16), w_ref[...].astype(BF16), (((1,), (1,)), ((), ())),
                                     preferred_element_type=F32).astype(o_ref.dtype)

    return pl.pallas_call(body, name=name, grid=(M // tm, K // tk),
                          in_specs=[pl.BlockSpec((tm, N), lambda i, j: (i, 0)), pl.BlockSpec((tk, N), lambda i, j: (j, 0))],
                          out_specs=pl.BlockSpec((tm, tk), lambda i, j: (i, j)),
                          out_shape=jax.ShapeDtypeStruct((M, K), out_dtype),
                          compiler_params=_cparams("parallel", "parallel"))(dy, w)


def _mm_tn(name, x, dy, tm=1024, tk=512, tn=512):
    M, K = x.shape
    N = dy.shape[1]
    tm, tk, tn = min(tm, M), min(tk, K), min(tn, N)

    def body(x_ref, dy_ref, o_ref):
        @pl.when(pl.program_id(2) == 0)
        def _():
            o_ref[...] = jnp.zeros_like(o_ref)
        o_ref[...] += lax.dot_general(x_ref[...].astype(BF16), dy_ref[...].astype(BF16), (((0,), (0,)), ((), ())),
                                      preferred_element_type=F32)

    return pl.pallas_call(body, name=name, grid=(K // tk, N // tn, M // tm),
                          in_specs=[pl.BlockSpec((tm, tk), lambda i, j, m: (m, i)), pl.BlockSpec((tm, tn), lambda i, j, m: (m, j))],
                          out_specs=pl.BlockSpec((tk, tn), lambda i, j, m: (i, j)),
                          out_shape=jax.ShapeDtypeStruct((K, N), F32),
                          compiler_params=_cparams("parallel", "parallel", "arbitrary"))(x, dy)


def _lru_scan(name, a, b, seq, ct=128):
    T, C = a.shape

    def body(a_ref, b_ref, h_ref):
        A, Bv = a_ref[...], b_ref[...]
        row = lax.broadcasted_iota(jnp.int32, (seq, 1), 0)
        k = 1
        while k < seq:
            keep = row >= k
            Bv = A * jnp.where(keep, pltpu.roll(Bv, k, 0), 0.0) + Bv
            A = A * jnp.where(keep, pltpu.roll(A, k, 0), 1.0)
            k *= 2
        h_ref[...] = Bv

    spec = pl.BlockSpec((seq, ct), lambda b_, j: (b_, j))
    return pl.pallas_call(body, name=name, grid=(T // seq, C // ct), in_specs=[spec, spec], out_specs=spec,
                          out_shape=jax.ShapeDtypeStruct((T, C), F32),
                          compiler_params=_cparams("parallel", "parallel"))(a, b)


def _lru_scan_bwd(name, a, h, dh, seq, ct=128):
    T, C = a.shape

    def body(a_ref, h_ref, dh_ref, da_ref, db_ref):
        row = lax.broadcasted_iota(jnp.int32, (seq, 1), 0)
        Cf = jnp.where(row < seq - 1, pltpu.roll(a_ref[...], seq - 1, 0), 0.0)
        G = dh_ref[...]
        k = 1
        while k < seq:
            keep = row + k < seq
            G = Cf * jnp.where(keep, pltpu.roll(G, seq - k, 0), 0.0) + G
            Cf = Cf * jnp.where(keep, pltpu.roll(Cf, seq - k, 0), 1.0)
            k *= 2
        db_ref[...] = G
        da_ref[...] = G * jnp.where(row >= 1, pltpu.roll(h_ref[...], 1, 0), 0.0)

    spec = pl.BlockSpec((seq, ct), lambda b_, j: (b_, j))
    return pl.pallas_call(body, name=name, grid=(T // seq, C // ct), in_specs=[spec] * 3, out_specs=[spec] * 2,
                          out_shape=[jax.ShapeDtypeStruct((T, C), F32)] * 2,
                          compiler_params=_cparams("parallel", "parallel"))(a, h, dh)


def _pair_consts():
    jj = lax.broadcasted_iota(jnp.int32, (RWKV_N, LANES), 0)
    ll = lax.broadcasted_iota(jnp.int32, (RWKV_N, LANES), 1)
    pick = ((ll & (RWKV_N - 1)) == jj).astype(BF16)
    l0 = lax.broadcasted_iota(jnp.int32, (LANES, LANES), 0)
    l1 = lax.broadcasted_iota(jnp.int32, (LANES, LANES), 1)
    same = ((l0 >> 6) == (l1 >> 6)).astype(BF16)
    return pick, same


N_COL = 5


def _cols_raw(rows, pick, same):
    lhs = [pick * jnp.broadcast_to(x, (RWKV_N, LANES)).astype(BF16) for x in rows]
    out = jnp.dot(jnp.concatenate(lhs, axis=0), same, preferred_element_type=F32)
    return tuple(out[i * RWKV_N:(i + 1) * RWKV_N] for i in range(len(rows)))


def _cols_bwd(res, g):
    pick, same = res
    m = jnp.dot(jnp.concatenate([t.astype(BF16) for t in g], axis=0), same, preferred_element_type=F32)
    pick_f = pick.astype(F32)
    rows = tuple(jnp.sum(pick_f * m[i * RWKV_N:(i + 1) * RWKV_N], axis=0, keepdims=True) for i in range(len(g)))
    return rows, jnp.zeros_like(pick), jnp.zeros_like(same)


_cols = jax.custom_vjp(_cols_raw)
_cols.defvjp(lambda rows, pick, same: (_cols_raw(rows, pick, same), (pick, same)), _cols_bwd)


def _rwkv_step(cols, s0s, rows):
    flat = []
    for r, w, k, v, a, b in rows:
        flat += [a, 1.0 - w, b, k, r]
    tiles = cols(tuple(flat))
    s1s, ys = [], []
    for p, (s0, (r, w, k, v, a, b)) in enumerate(zip(s0s, rows)):
        ca, ce, cb, ck, cr = tiles[N_COL * p:N_COL * (p + 1)]
        sa = jnp.sum(s0 * ca, axis=0, keepdims=True)
        s1 = s0 - s0 * ce + cb * sa + ck * v
        s1s.append(s1)
        ys.append(jnp.sum(s1 * cr, axis=0, keepdims=True))
    return tuple(s1s), tuple(ys)


def _rwkv_specs(C, seq, tc, G, reverse):
    nc = seq // tc
    if reverse:
        row = pl.BlockSpec((tc, LANES * G), lambda b, g, c: (b * nc + nc - 1 - c, g))
        st = pl.BlockSpec((tc, G, RWKV_N, LANES), lambda b, g, c: (b * nc + nc - 1 - c, g, 0, 0))
    else:
        row = pl.BlockSpec((tc, LANES * G), lambda b, g, c: (b * nc + c, g))
        st = pl.BlockSpec((tc, G, RWKV_N, LANES), lambda b, g, c: (b * nc + c, g, 0, 0))
    return row, st


def _rwkv_fwd(name, r, w, k, v, a, b, seq, tc=32, G=8):
    T, C = r.shape
    tc = min(tc, seq)
    NP = C // LANES
    row, st = _rwkv_specs(C, seq, tc, G, False)

    def body(r_ref, w_ref, k_ref, v_ref, a_ref, b_ref, y_ref, st_ref, s_scr):
        @pl.when(pl.program_id(2) == 0)
        def _():
            s_scr[...] = jnp.zeros_like(s_scr)
        pick, same = _pair_consts()
        cols = lambda rows: _cols_raw(rows, pick, same)
        rid = lax.broadcasted_iota(jnp.int32, (SUB, LANES), 0)

        def group(t8, carry):
            base = pl.multiple_of(t8 * SUB, SUB)
            lanes = [pl.ds(LANES * p, LANES) for p in range(G)]
            blk = [[x[pl.ds(base, SUB), ln] for x in (r_ref, w_ref, k_ref, v_ref, a_ref, b_ref)] for ln in lanes]
            s = tuple(s_scr[p] for p in range(G))
            ys = [jnp.zeros((SUB, LANES), F32) for _ in range(G)]
            for i in range(SUB):
                for p in range(G):
                    st_ref[base + i, p] = s[p]
                s, y = _rwkv_step(cols, s, tuple(tuple(b_[i:i + 1] for b_ in blk[p]) for p in range(G)))
                for p in range(G):
                    ys[p] = jnp.where(rid == i, y[p], ys[p])
            for p in range(G):
                s_scr[p] = s[p]
                y_ref[pl.ds(base, SUB), lanes[p]] = ys[p]
            return carry

        lax.fori_loop(0, tc // SUB, group, 0)

    return pl.pallas_call(body, name=name, grid=(T // seq, NP // G, seq // tc), in_specs=[row] * 6, out_specs=[row, st],
                          out_shape=[jax.ShapeDtypeStruct((T, C), F32), jax.ShapeDtypeStruct((T, NP, RWKV_N, LANES), F32)],
                          scratch_shapes=[pltpu.VMEM((G, RWKV_N, LANES), F32)],
                          compiler_params=_cparams("parallel", "parallel", "arbitrary"))(r, w, k, v, a, b)


def _rwkv_bwd(name, r, w, k, v, a, b, states, dy, dr_p, dk_p, dv_p, seq, tc=32, G=8):
    T, C = r.shape
    tc = min(tc, seq)
    NP = C // LANES
    row, st = _rwkv_specs(C, seq, tc, G, True)

    def body(r_ref, w_ref, k_ref, v_ref, a_ref, b_ref, st_ref, dy_ref, drp_ref, dkp_ref, dvp_ref,
             dr_ref, dw_ref, dk_ref, dv_ref, da_ref, db_ref, ds_scr):
        @pl.when(pl.program_id(2) == 0)
        def _():
            ds_scr[...] = jnp.zeros_like(ds_scr)
        pick, same = _pair_consts()
        step_fn = functools.partial(_rwkv_step, lambda rows: _cols(rows, pick, same))

        rid = lax.broadcasted_iota(jnp.int32, (SUB, LANES), 0)
        out_refs = (dr_ref, dw_ref, dk_ref, dv_ref, da_ref, db_ref)

        def group(n, carry):
            base = pl.multiple_of((tc // SUB - 1 - n) * SUB, SUB)
            lanes = [pl.ds(LANES * p, LANES) for p in range(G)]
            blk = [[x[pl.ds(base, SUB), ln] for x in (r_ref, w_ref, k_ref, v_ref, a_ref, b_ref, dy_ref)] for ln in lanes]
            ds = [ds_scr[p] for p in range(G)]
            gs = [[drp_ref[pl.ds(base, SUB), ln], jnp.zeros((SUB, LANES), F32), dkp_ref[pl.ds(base, SUB), ln],
                   dvp_ref[pl.ds(base, SUB), ln], jnp.zeros((SUB, LANES), F32), jnp.zeros((SUB, LANES), F32)]
                  for ln in lanes]
            for i in reversed(range(SUB)):
                rows = tuple(tuple(b_[i:i + 1] for b_ in blk[p][:6]) for p in range(G))
                _, vjp_fn = jax.vjp(step_fn, tuple(st_ref[base + i, p] for p in range(G)), rows)
                ds, drows = vjp_fn((tuple(ds), tuple(blk[p][6][i:i + 1] for p in range(G))))
                for p in range(G):
                    for j in range(6):
                        gs[p][j] = gs[p][j] + jnp.where(rid == i, drows[p][j], 0.0)
            for p in range(G):
                ds_scr[p] = ds[p]
                for j in range(6):
                    out_refs[j][pl.ds(base, SUB), lanes[p]] = gs[p][j]
            return carry

        lax.fori_loop(0, tc // SUB, group, 0)

    return pl.pallas_call(body, name=name, grid=(T // seq, NP // G, seq // tc),
                          in_specs=[row] * 6 + [st] + [row] * 4, out_specs=[row] * 6,
                          out_shape=[jax.ShapeDtypeStruct((T, C), F32)] * 6,
                          scratch_shapes=[pltpu.VMEM((G, RWKV_N, LANES), F32)],
                          compiler_params=_cparams("parallel", "parallel", "arbitrary"))(
                              r, w, k, v, a, b, states, dy, dr_p, dk_p, dv_p)


def _all_gather(name, shard):
    def body(x_ref, o_ref, send_sems, recv_sems, local_sem):
        x, y, c = lax.axis_index("x"), lax.axis_index("y"), lax.axis_index("c")
        me, sibling = (x, y, c), (x, y, 1 - c)
        chips = [(1 - x, y), (x, 1 - y), (1 - x, 1 - y)]

        def copy(k, block, to, src=None):
            dst = o_ref.at[4 * block[0] + 2 * block[1] + block[2]]
            return pltpu.make_async_remote_copy(src_ref=dst if src is None else src, dst_ref=dst, send_sem=send_sems.at[k],
                                                recv_sem=recv_sems.at[k], device_id=to, device_id_type=MESH)

        mine = pltpu.make_async_copy(x_ref, o_ref.at[4 * x + 2 * y + c], local_sem)
        mine.start()
        first = [copy(0, me, sibling, src=x_ref)] + [copy(1 + j, me, (*chip, c), src=x_ref) for j, chip in enumerate(chips)]
        for cp in first:
            cp.start()
        passed = [copy(4 + j, (*chip, c), sibling) for j, chip in enumerate(chips)]
        for j, chip in enumerate(chips):
            copy(1 + j, (*chip, c), me).wait_recv()
            passed[j].start()
        copy(0, sibling, me).wait_recv()
        for j, chip in enumerate(chips):
            copy(4 + j, (*chip, 1 - c), me).wait_recv()
        for cp in first + passed:
            cp.wait_send()
        mine.wait()

    hbm = pl.BlockSpec(memory_space=pltpu.HBM)
    return pl.pallas_call(body, name=name, in_specs=[hbm], out_specs=hbm,
                          out_shape=jax.ShapeDtypeStruct((N_DEV,) + shard.shape, shard.dtype),
                          scratch_shapes=[pltpu.SemaphoreType.DMA((N_DEV - 1,)), pltpu.SemaphoreType.DMA((N_DEV - 1,)),
                                          pltpu.SemaphoreType.DMA])(shard)


N_CHIP = 4


def _swap_with_sibling(name, blocks):
    def body(x_ref, keep_ref, got_ref, send_sem, recv_sem, local_sem):
        x, y, c = lax.axis_index("x"), lax.axis_index("y"), lax.axis_index("c")
        mine = pltpu.make_async_copy(x_ref.at[c], keep_ref, local_sem)
        mine.start()
        cp = pltpu.make_async_remote_copy(src_ref=x_ref.at[1 - c], dst_ref=got_ref, send_sem=send_sem, recv_sem=recv_sem,
                                          device_id=(x, y, 1 - c), device_id_type=MESH)
        cp.start()
        cp.wait()
        mine.wait()

    hbm = pl.BlockSpec(memory_space=pltpu.HBM)
    out = jax.ShapeDtypeStruct(blocks.shape[1:], blocks.dtype)
    return pl.pallas_call(body, name=name, in_specs=[hbm], out_specs=[hbm, hbm], out_shape=[out, out],
                          scratch_shapes=[pltpu.SemaphoreType.DMA, pltpu.SemaphoreType.DMA, pltpu.SemaphoreType.DMA])(blocks)


def _add_blocks(name, a, b, tr):
    N, R, W = a.shape

    def body(a_ref, b_ref, o_ref):
        o_ref[...] = a_ref[...] + b_ref[...]

    spec = pl.BlockSpec((None, tr, W), lambda n, i: (n, i, 0))
    return pl.pallas_call(body, name=name, grid=(N, R // tr), in_specs=[spec, spec], out_specs=spec,
                          out_shape=jax.ShapeDtypeStruct(a.shape, a.dtype),
                          compiler_params=_cparams("parallel", "parallel"))(a, b)


def _exchange_chips(name, blocks):
    def body(z_ref, o_ref, send_sems, recv_sems, local_sem):
        x, y, c = lax.axis_index("x"), lax.axis_index("y"), lax.axis_index("c")
        chip = 2 * x + y
        mine = pltpu.make_async_copy(z_ref.at[chip], o_ref.at[chip], local_sem)
        mine.start()
        copies = []
        for k in range(1, N_CHIP):
            to = (x ^ (k >> 1), y ^ (k & 1), c)
            cp = pltpu.make_async_remote_copy(src_ref=z_ref.at[chip ^ k], dst_ref=o_ref.at[chip], send_sem=send_sems.at[k - 1],
                                              recv_sem=recv_sems.at[k - 1], device_id=to, device_id_type=MESH)
            cp.start()
            copies.append(cp)
        for k in range(1, N_CHIP):
            pltpu.make_async_remote_copy(src_ref=z_ref.at[chip], dst_ref=o_ref.at[chip ^ k], send_sem=send_sems.at[k - 1],
                                         recv_sem=recv_sems.at[k - 1], device_id=(x ^ (k >> 1), y ^ (k & 1), c),
                                         device_id_type=MESH).wait_recv()
        for cp in copies:
            cp.wait_send()
        mine.wait()

    hbm = pl.BlockSpec(memory_space=pltpu.HBM)
    return pl.pallas_call(body, name=name, in_specs=[hbm], out_specs=hbm,
                          out_shape=jax.ShapeDtypeStruct(blocks.shape, blocks.dtype),
                          scratch_shapes=[pltpu.SemaphoreType.DMA((N_CHIP - 1,)), pltpu.SemaphoreType.DMA((N_CHIP - 1,)),
                                          pltpu.SemaphoreType.DMA])(blocks)


def _adamw(name, parts, w, m, v, tr):
    R, W = w.shape
    n_parts = parts.shape[0]

    def body(p_ref, w_ref, m_ref, v_ref, g_ref, d_ref, nm_ref, nv_ref):
        g = p_ref[0]
        for s in range(1, n_parts):
            g = g + p_ref[s]
        nm = ADAM_B1 * m_ref[...] + (1.0 - ADAM_B1) * g
        nv = ADAM_B2 * v_ref[...] + (1.0 - ADAM_B2) * (g * g)
        m_hat = nm / (1.0 - ADAM_B1 ** ADAM_STEP)
        v_hat = nv / (1.0 - ADAM_B2 ** ADAM_STEP)
        g_ref[...] = g
        d_ref[...] = -ADAM_LR * (m_hat / (jnp.sqrt(v_hat) + ADAM_EPS) + ADAM_WD * w_ref[...])
        nm_ref[...] = nm
        nv_ref[...] = nv

    spec = pl.BlockSpec((tr, W), lambda i: (i, 0))
    return pl.pallas_call(body, name=name, grid=(R // tr,),
                          in_specs=[pl.BlockSpec((n_parts, tr, W), lambda i: (0, i, 0)), spec, spec, spec],
                          out_specs=[spec] * 4, out_shape=[jax.ShapeDtypeStruct((R, W), F32)] * 4,
                          compiler_params=_cparams("parallel"))(parts, w, m, v)


def _rms(x, g):
    return x * lax.rsqrt(jnp.mean(x * x, axis=-1, keepdims=True) + RMS_EPS) * g


def _softplus(z):
    return jnp.maximum(z, 0.0) + jnp.log(1.0 + jnp.exp(-jnp.abs(z)))


def _neg_expm1(z):
    series = -z * (1.0 + z * (0.5 + z * (1.0 / 6.0)))
    return jnp.where(z > -1e-3, series, 1.0 - jnp.exp(z))


def _gelu_tanh(x):
    return 0.5 * x * (1.0 + jnp.tanh(0.7978845608028654 * (x + 0.044715 * x * x * x)))


def _bdot(x, w):
    return jnp.dot(x.astype(BF16), w.astype(BF16), preferred_element_type=F32)


def _seg_consts(C):
    c0 = lax.broadcasted_iota(jnp.int32, (C, LANES), 0)
    h1 = lax.broadcasted_iota(jnp.int32, (C, LANES), 1)
    red = ((c0 >> 6) == h1).astype(BF16)
    h0 = lax.broadcasted_iota(jnp.int32, (LANES, C), 0)
    c1 = lax.broadcasted_iota(jnp.int32, (LANES, C), 1)
    exp = ((c1 >> 6) == h0).astype(BF16)
    return red, exp


def _split_dot(x, m):
    hi = x.astype(BF16)
    lo = (x - hi.astype(F32)).astype(BF16)
    return jnp.dot(hi, m, preferred_element_type=F32) + jnp.dot(lo, m, preferred_element_type=F32)


def _segsum_raw(x):
    red, exp = _seg_consts(x.shape[1])
    return _split_dot(_split_dot(x, red), exp)


@jax.custom_vjp
def _segsum(x):
    return _segsum_raw(x)


_segsum.defvjp(lambda x: (_segsum_raw(x), None), lambda _, g: (_segsum_raw(g),))


def _f_pre(out_dtype):
    def fn(pos, x, g):
        return (_rms(x, g).astype(out_dtype),)
    return fn


def _f_post(pos, x, t, g):
    return (x + _rms(t, g),)


def _f_post_bias(pos, x, t, g, bias):
    return (x + _rms(t + bias, g),)


def _f_conv(pos, proj, p1, p2, p3, b_u, w0, w1, w2, w3, cb):
    D = b_u.shape[1]
    u0 = proj[:, D:] + b_u
    u1 = jnp.where(pos >= 1, p1[:, D:] + b_u, 0.0)
    u2 = jnp.where(pos >= 2, p2[:, D:] + b_u, 0.0)
    u3 = jnp.where(pos >= 3, p3[:, D:] + b_u, 0.0)
    return (cb + u3 * w0 + u2 * w1 + u1 * w2 + u0 * w3,)


def _f_conv_bwd(pos, dconv, proj, dyb, n1, n2, n3, p1, p2, p3, b_u, w0, w1, w2, w3):
    D = b_u.shape[1]
    du = dconv * w3 + n1 * w2 + n2 * w1 + n3 * w0
    dproj = jnp.concatenate([dyb[:, :D], du], axis=1)
    u0 = proj[:, D:] + b_u
    u1 = jnp.where(pos >= 1, p1[:, D:] + b_u, 0.0)
    u2 = jnp.where(pos >= 2, p2[:, D:] + b_u, 0.0)
    u3 = jnp.where(pos >= 3, p3[:, D:] + b_u, 0.0)
    rs = lambda z: jnp.sum(z, axis=0, keepdims=True)
    return (dproj, rs(dproj), rs(dconv * u3), rs(dconv * u2), rs(dconv * u1), rs(dconv * u0), rs(dconv))


def _f_gates(pos, conv, proj, gate_w, gb_r, gb_i, lam, b_y):
    D = conv.shape[1]
    blk = D // LRU_HEADS
    cb = conv.astype(BF16)

    def gate(g, bias):
        z = [jnp.dot(cb[:, h * blk:(h + 1) * blk], gate_w[(g * LRU_HEADS + h) * blk:(g * LRU_HEADS + h + 1) * blk, :].astype(BF16),
                     preferred_element_type=F32) for h in range(LRU_HEADS)]
        return jax.nn.sigmoid(jnp.concatenate(z, axis=1) + bias)

    r_gate, i_gate = gate(0, gb_r), gate(1, gb_i)
    log_a = -LRU_C * r_gate * _softplus(-lam)
    a = jnp.exp(log_a)
    b = jnp.sqrt(_neg_expm1(2.0 * log_a)) * i_gate * conv
    yb = _gelu_tanh(proj[:, :D] + b_y)
    return a, b, yb


def _f_hy(pos, h, yb):
    return ((h * yb).astype(BF16),)


def _f_mix(pos, hn, hp, mu_r, mu_w, mu_k, mu_v, mu_a, mu_g, w0, w1, w2, a0, a1, a2, g1, g2):
    xx = hp - hn
    xr, xw, xk, xv, xa, xg = (hn + xx * m for m in (mu_r, mu_w, mu_k, mu_v, mu_a, mu_g))
    lw = w0 + _bdot(jnp.tanh(_bdot(xw, w1)), w2)
    decay = jnp.exp(-jnp.exp(-_softplus(-lw) - 0.5))
    a = jax.nn.sigmoid(a0 + _bdot(_bdot(xa, a1), a2))
    g = _bdot(jax.nn.sigmoid(_bdot(xg, g1)), g2)
    return xr.astype(BF16), xk.astype(BF16), xv.astype(BF16), decay, a, g


def _f_kk(pos, k, a, k_k, k_a):
    kk = k * k_k
    kk = kk / jnp.maximum(jnp.sqrt(_segsum(kk * kk)), 1e-12)
    return k * (1.0 + (a - 1.0) * k_a), -kk, kk * a


def _f_gn(pos, y, r, k2, v, g, gn_g, gn_b, r_k):
    inv_n = 1.0 / RWKV_N
    yc = y - _segsum(y) * inv_n
    var = _segsum(yc * yc) * inv_n
    yn = yc * lax.rsqrt(var + GN_EPS) * gn_g + gn_b
    bonus = _segsum(r * k2 * r_k) * v
    return (((yn + bonus) * g).astype(BF16),)


def _f_attn(pos, q, k, v):
    D = q.shape[1]
    hd = D // XATTN_HEADS
    qb, kb, vb = q.astype(BF16), k.astype(BF16), v.astype(BF16)
    outs = []
    for h in range(XATTN_HEADS):
        sl = slice(h * hd, (h + 1) * hd)
        s = lax.dot_general(qb[:, sl], kb[:, sl], (((1,), (1,)), ((), ())), preferred_element_type=F32) * (hd ** -0.5)
        s = s - jnp.max(s, axis=-1, keepdims=True)
        e = jnp.exp(s)
        p = e / jnp.sum(e, axis=-1, keepdims=True)
        outs.append(jnp.dot(p.astype(BF16), vb[:, sl], preferred_element_type=F32))
    return (jnp.concatenate(outs, axis=1).astype(BF16),)


def _f_act(pos, up):
    return (jnp.square(jnp.maximum(up, 0.0)).astype(BF16),)


def _f_loss(pos, y, target):
    d = y - target
    inv = 1.0 / y.shape[1]
    part = 0.5 * inv * jnp.sum(jnp.sum(d * d, axis=1, keepdims=True), axis=0, keepdims=True)
    return d * inv, jnp.broadcast_to(part, (8, LANES))


def _full_from_blocks(blocks, shard_shape, ax):
    g = jnp.moveaxis(blocks.reshape((N_DEV,) + tuple(shard_shape)), 0, ax)
    return g.reshape(tuple(shard_shape[:ax]) + (N_DEV * shard_shape[ax],) + tuple(shard_shape[ax + 1:]))


def _blocks_from_full(full, shard_shape, ax):
    g = full.reshape(tuple(shard_shape[:ax]) + (N_DEV, shard_shape[ax]) + tuple(shard_shape[ax + 1:]))
    return jnp.moveaxis(g, ax, 0).reshape(N_DEV, -1)


def _numel(shape):
    n = 1
    for s in shape:
        n *= s
    return n


def _pack_flat(arrs, total):
    flat = jnp.concatenate([a.reshape(-1).astype(F32) for a in arrs])
    return jnp.pad(flat, (0, total - flat.shape[0]))


def _train_step(x3, mem3, target3, P, M_, V_):
    Bl, S, D = x3.shape
    T = Bl * S
    ML = mem3.shape[1]
    x = x3.reshape(T, D)
    mem = mem3.reshape(Bl * ML, D)
    target = target3.reshape(T, D)
    shp = {n: P[n].shape for n in WEIGHTS}

    n_big = sum(_numel(shp[n]) for n in BIG)
    n_small = sum(_numel(shp[n]) for n in SMALL)
    n_small_pad = -(-n_small // (8 * PACK_W)) * (8 * PACK_W)
    big_rows, small_rows = n_big // PACK_W, n_small_pad // PACK_W
    big_pack = _pack_flat([P[n] for n in BIG], n_big).reshape(big_rows, PACK_W).astype(BF16)
    small_pack = _pack_flat([P[n] for n in SMALL], n_small_pad).reshape(small_rows, PACK_W)
    big_all = _all_gather("gather_matrices", big_pack).reshape(N_DEV, n_big)
    small_all = _all_gather("gather_vectors", small_pack).reshape(N_DEV, n_small_pad)
    Wf = {}
    off = 0
    for n in BIG:
        ne = _numel(shp[n])
        Wf[n] = _full_from_blocks(big_all[:, off:off + ne], shp[n], SHARD_AXIS[n])
        off += ne
    off = 0
    for n in SMALL:
        ne = _numel(shp[n])
        Wf[n] = _full_from_blocks(small_all[:, off:off + ne], shp[n], SHARD_AXIS[n])
        off += ne
    for n in REPL:
        Wf[n] = P[n]

    row = lambda v: v.reshape(1, -1).astype(F32)
    gains = [[row(Wf['ln_gains'][i, j]) for j in range(6)] for i in range(2)]
    LW = 128

    def pad_cols(w):
        return jnp.pad(w, ((0, 0), (0, LW - w.shape[1]))).astype(F32)

    def pad_rows(w):
        return jnp.pad(w, ((0, LW - w.shape[0]), (0, 0))).astype(F32)

    G = {}
    saved = {}

    mem_n, = _rw("mem_norm_fwd", _f_pre(BF16), [mem], [row(Wf['mem_norm'])], outs=[(D, BF16)])

    def xattn_fwd(i, xin):
        hq, = _rw(f"xattn{i}_pre", _f_pre(BF16), [xin], [gains[i][2]], outs=[(D, BF16)], seq=S)
        q = _mm_nn(f"xattn{i}_q", hq, Wf['c_w_q'][i], BF16)
        kv = _mm_nn(f"xattn{i}_kv", mem_n, Wf['c_w_kv'][i], F32)
        kb = kv[:, :D].reshape(Bl, ML, D)
        vb = kv[:, D:].reshape(Bl, ML, D)
        o, = _rw(f"xattn{i}_attn", _f_attn, [q], [kb, vb], outs=[(D, BF16)], seq=S, per_b=(0, 1))
        c = _mm_nn(f"xattn{i}_o", o, Wf['c_w_o'][i], F32)
        xo, = _rw(f"xattn{i}_post", _f_post, [xin, c], [gains[i][3]], outs=[(D, F32)], seq=S)
        saved[f"xattn{i}"] = (xin, hq, q, kb, vb, o, c)
        return xo

    def mlp_fwd(i, xin):
        hm, = _rw(f"mlp{i}_pre", _f_pre(BF16), [xin], [gains[i][4]], outs=[(D, BF16)], seq=S)
        up = _mm_nn(f"mlp{i}_up", hm, Wf['m_w_up'][i], F32)
        act, = _rw(f"mlp{i}_act", _f_act, [up], outs=[(up.shape[1], BF16)], seq=S)
        m = _mm_nn(f"mlp{i}_down", act, Wf['m_w_down'][i], F32)
        xo, = _rw(f"mlp{i}_post", _f_post, [xin, m], [gains[i][5]], outs=[(D, F32)], seq=S)
        saved[f"mlp{i}"] = (xin, hm, up, act, m)
        return xo

    b_in = row(Wf['a_b_in'][0])
    b_y, b_u = b_in[:, :D], b_in[:, D:]
    cw = [row(Wf['a_conv_w'][0, t]) for t in range(4)]
    cb = row(Wf['a_conv_b'][0])
    gate_w = Wf['a_gate_w'][0].reshape(2 * LRU_HEADS * (D // LRU_HEADS), D // LRU_HEADS).astype(F32)
    gate_b = Wf['a_gate_b'][0].reshape(2, D).astype(F32)
    gb_r, gb_i = gate_b[0:1], gate_b[1:2]
    lam = row(Wf['a_lambda'][0])
    b_out = row(Wf['a_b_out'][0])

    hn0, = _rw("lru_pre", _f_pre(BF16), [x], [gains[0][0]], outs=[(D, BF16)], seq=S)
    proj = _mm_nn("lru_in", hn0, Wf['a_w_in'][0], F32)
    prev3 = [(0, 1), (0, 2), (0, 3)]
    conv, = _rw("lru_conv", _f_conv, [proj], [b_u] + cw + [cb], outs=[(D, F32)], shifts=prev3, seq=S)
    gate_par = [gate_w, gb_r, gb_i, lam, b_y]
    a_l, b_l, yb = _rw("lru_gates", _f_gates, [conv, proj], gate_par, outs=[(D, F32)] * 3, seq=S)
    h_l = _lru_scan("lru_scan", a_l, b_l, S)
    hy, = _rw("lru_hy", _f_hy, [h_l, yb], outs=[(D, BF16)], seq=S)
    t0 = _mm_nn("lru_out", hy, Wf['a_w_out'][0], F32)
    x1, = _rw("lru_post", _f_post_bias, [x, t0], [gains[0][1], b_out], outs=[(D, F32)], seq=S)
    x2 = xattn_fwd(0, x1)
    x3_ = mlp_fwd(0, x2)

    mu = [row(Wf['b_mu'][0, j]) for j in range(6)]
    lora = [row(Wf['b_w0'][0]), pad_cols(Wf['b_w1'][0]), pad_rows(Wf['b_w2'][0]),
            row(Wf['b_a0'][0]), pad_cols(Wf['b_a1'][0]), pad_rows(Wf['b_a2'][0]),
            Wf['b_g1'][0].astype(F32), Wf['b_g2'][0].astype(F32)]
    k_k, k_a = row(Wf['b_k_k'][0]), row(Wf['b_k_a'][0])
    gn_g, gn_b, r_k = row(Wf['b_gn_g'][0]), row(Wf['b_gn_b'][0]), row(Wf['b_r_k'][0])

    hn1, = _rw("rwkv_pre", _f_pre(F32), [x3_], [gains[1][0]], outs=[(D, F32)], seq=S)
    xr, xk, xv, decay, a_r, g_r = _rw("rwkv_mix", _f_mix, [hn1], mu + lora,
                                      outs=[(D, BF16)] * 3 + [(D, F32)] * 3, shifts=[(0, 1)], seq=S)
    r_ = _mm_nn("rwkv_r", xr, Wf['b_w_rkv'][0, 0], F32)
    k_ = _mm_nn("rwkv_k", xk, Wf['b_w_rkv'][0, 1], F32)
    v_ = _mm_nn("rwkv_v", xv, Wf['b_w_rkv'][0, 2], F32)
    k2, ra, rb = _rw("rwkv_kk", _f_kk, [k_, a_r], [k_k, k_a], outs=[(D, F32)] * 3, seq=S)
    y_r, states = _rwkv_fwd("rwkv_scan", r_, decay, k2, v_, ra, rb, S)
    gn_par = [gn_g, gn_b, r_k]
    og, = _rw("rwkv_gn", _f_gn, [y_r, r_, k2, v_, g_r], gn_par, outs=[(D, BF16)], seq=S)
    t1 = _mm_nn("rwkv_out", og, Wf['b_w_o'][0], F32)
    x4, = _rw("rwkv_post", _f_post, [x3_, t1], [gains[1][1]], outs=[(D, F32)], seq=S)
    x5 = xattn_fwd(1, x4)
    x6 = mlp_fwd(1, x5)

    dx, loss_acc = _rw("loss", _f_loss, [x6, target], outs=[(D, F32)], accs=[(8, LANES)], seq=S)
    loss = lax.psum(loss_acc[0, 0], ("x", "y", "c"))

    dgain = [[None] * 6 for _ in range(2)]
    dmem_n = None

    def mlp_bwd(i, dxo):
        xin, hm, up, act, m = saved[f"mlp{i}"]
        dm, dgain[i][5] = _rw_vjp(f"mlp{i}_post_bwd", _f_post, [xin, m], [gains[i][5]], [dxo], 1,
                                  d_rows=[(1, BF16)], d_params=[0], seq=S)
        G_down = _mm_tn(f"mlp{i}_down_dw", act, dm)
        dact = _mm_nt(f"mlp{i}_down_dx", dm, Wf['m_w_down'][i], F32)
        dup, = _rw_vjp(f"mlp{i}_act_bwd", _f_act, [up], [], [dact], 1, d_rows=[(0, BF16)], seq=S)
        G_up = _mm_tn(f"mlp{i}_up_dw", hm, dup)
        dhm = _mm_nt(f"mlp{i}_up_dx", dup, Wf['m_w_up'][i], F32)
        dxi, dgain[i][4] = _rw_vjp(f"mlp{i}_pre_bwd", _f_pre(F32), [xin], [gains[i][4]], [dhm], 1,
                                   d_rows=[(0, F32)], d_params=[0], adds=[(0, dxo, 0)], seq=S)
        return dxi, G_up, G_down

    def xattn_bwd(i, dxo):
        xin, hq, q, kb, vb, o, c = saved[f"xattn{i}"]
        dc, dgain[i][3] = _rw_vjp(f"xattn{i}_post_bwd", _f_post, [xin, c], [gains[i][3]], [dxo], 1,
                                  d_rows=[(1, BF16)], d_params=[0], seq=S)
        G_o = _mm_tn(f"xattn{i}_o_dw", o, dc)
        do = _mm_nt(f"xattn{i}_o_dx", dc, Wf['c_w_o'][i], F32)
        dq, dkb, dvb = _rw_vjp(f"xattn{i}_attn_bwd", _f_attn, [q], [kb, vb], [do], 1, d_rows=[(0, BF16)],
                               d_params=[0, 1], seq=S, per_b=(0, 1))
        dkv = jnp.concatenate([dkb.reshape(Bl * ML, D), dvb.reshape(Bl * ML, D)], axis=1)
        G_kv = _mm_tn(f"xattn{i}_kv_dw", mem_n, dkv)
        dmn = _mm_nt(f"xattn{i}_kv_dx", dkv, Wf['c_w_kv'][i], F32)
        G_q = _mm_tn(f"xattn{i}_q_dw", hq, dq)
        dhq = _mm_nt(f"xattn{i}_q_dx", dq, Wf['c_w_q'][i], F32)
        dxi, dgain[i][2] = _rw_vjp(f"xattn{i}_pre_bwd", _f_pre(F32), [xin], [gains[i][2]], [dhq], 1,
                                   d_rows=[(0, F32)], d_params=[0], adds=[(0, dxo, 0)], seq=S)
        return dxi, G_q, G_kv, G_o, dmn

    G_up, G_down, G_q, G_kv, G_o = [None] * 2, [None] * 2, [None] * 2, [None] * 2, [None] * 2
    dx, G_up[1], G_down[1] = mlp_bwd(1, dx)
    dx, G_q[1], G_kv[1], G_o[1], dmn1 = xattn_bwd(1, dx)

    dt1, dgain[1][1] = _rw_vjp("rwkv_post_bwd", _f_post, [x3_, t1], [gains[1][1]], [dx], 1,
                               d_rows=[(1, BF16)], d_params=[0], seq=S)
    G['b_w_o'] = _mm_tn("rwkv_out_dw", og, dt1)[None]
    dog = _mm_nt("rwkv_out_dx", dt1, Wf['b_w_o'][0], F32)
    dy_r, dr_p, dk2_p, dv_p, dg_r, d_gn_g, d_gn_b, d_r_k = _rw_vjp(
        "rwkv_gn_bwd", _f_gn, [y_r, r_, k2, v_, g_r], gn_par, [dog], 1,
        d_rows=[(j, F32) for j in range(5)], d_params=[0, 1, 2], seq=S)
    dr_, ddecay, dk2, dv_, dra, drb = _rwkv_bwd("rwkv_scan_bwd", r_, decay, k2, v_, ra, rb, states, dy_r,
                                                dr_p, dk2_p, dv_p, S)
    dk_, da_r, d_k_k, d_k_a = _rw_vjp("rwkv_kk_bwd", _f_kk, [k_, a_r], [k_k, k_a], [dk2, dra, drb], 3,
                                      d_rows=[(0, F32), (1, F32)], d_params=[0, 1], seq=S)
    G_rkv = [_mm_tn("rwkv_r_dw", xr, dr_), _mm_tn("rwkv_k_dw", xk, dk_), _mm_tn("rwkv_v_dw", xv, dv_)]
    G['b_w_rkv'] = jnp.stack(G_rkv)[None]
    dxr = _mm_nt("rwkv_r_dx", dr_, Wf['b_w_rkv'][0, 0], F32)
    dxk = _mm_nt("rwkv_k_dx", dk_, Wf['b_w_rkv'][0, 1], F32)
    dxv = _mm_nt("rwkv_v_dx", dv_, Wf['b_w_rkv'][0, 2], F32)
    mix_out = _rw_vjp("rwkv_mix_bwd", _f_mix, [hn1], mu + lora, [dxr, dxk, dxv, ddecay, da_r, dg_r], 6,
                      d_rows=[(0, F32)], d_shifts=[(0, F32)], d_params=list(range(14)), shifts=[(0, 1)], seq=S)
    dhn1, dhp1 = mix_out[0], mix_out[1]
    d_mu = mix_out[2:8]
    d_lora = mix_out[8:16]
    dx, dgain[1][0] = _rw_vjp("rwkv_pre_bwd", _f_pre(F32), [x3_], [gains[1][0]], [dhn1], 1, d_rows=[(0, F32)],
                              d_params=[0], adds=[(0, dx, 0)], ct_adds=[(0, dhp1, -1)], seq=S)
    G['b_mu'] = jnp.concatenate(d_mu, axis=0)[None]
    G['b_w0'], G['b_a0'] = d_lora[0], d_lora[3]
    G['b_w1'] = d_lora[1][:, :shp['b_w1'][2]][None]
    G['b_w2'] = d_lora[2][:shp['b_w2'][1]][None]
    G['b_a1'] = d_lora[4][:, :shp['b_a1'][2]][None]
    G['b_a2'] = d_lora[5][:shp['b_a2'][1]][None]
    G['b_g1'], G['b_g2'] = d_lora[6][None], d_lora[7][None]
    G['b_k_k'], G['b_k_a'], G['b_gn_g'], G['b_gn_b'] = d_k_k, d_k_a, d_gn_g, d_gn_b
    G['b_r_k'] = d_r_k.reshape(P['b_r_k'].shape)

    dx, G_up[0], G_down[0] = mlp_bwd(0, dx)
    dx, G_q[0], G_kv[0], G_o[0], dmn0 = xattn_bwd(0, dx)
    G['m_w_up'], G['m_w_down'] = jnp.stack(G_up), jnp.stack(G_down)
    G['c_w_q'], G['c_w_kv'], G['c_w_o'] = jnp.stack(G_q), jnp.stack(G_kv), jnp.stack(G_o)

    dt0, dgain[0][1], G['a_b_out'] = _rw_vjp("lru_post_bwd", _f_post_bias, [x, t0], [gains[0][1], b_out], [dx], 1,
                                             d_rows=[(1, BF16)], d_params=[0, 1], seq=S)
    G['a_w_out'] = _mm_tn("lru_out_dw", hy, dt0)[None]
    dhy = _mm_nt("lru_out_dx", dt0, Wf['a_w_out'][0], F32)
    dh_l, dyb = _rw_vjp("lru_hy_bwd", _f_hy, [h_l, yb], [], [dhy], 1, d_rows=[(0, F32), (1, F32)], seq=S)
    da_l, db_l = _lru_scan_bwd("lru_scan_bwd", a_l, h_l, dh_l, S)
    dconv, dproj_y, d_gate_w, d_gb_r, d_gb_i, G['a_lambda'] = _rw_vjp(
        "lru_gates_bwd", _f_gates, [conv, proj], gate_par, [da_l, db_l, dyb], 3,
        d_rows=[(0, F32), (1, F32)], d_params=[0, 1, 2, 3], seq=S)
    next3 = [(0, -1), (0, -2), (0, -3)]
    dproj, G['a_b_in'], dw0, dw1, dw2, dw3, G['a_conv_b'] = _rw(
        "lru_conv_bwd", _f_conv_bwd, [dconv, proj, dproj_y], [b_u] + cw, outs=[(2 * D, BF16)],
        accs=[(1, 2 * D)] + [(1, D)] * 5, shifts=next3 + [(1, 1), (1, 2), (1, 3)], seq=S)
    G['a_conv_w'] = jnp.concatenate([dw0, dw1, dw2, dw3], axis=0)[None]
    G['a_gate_w'] = d_gate_w.reshape(P['a_gate_w'].shape[:3] + (D // LRU_HEADS, D // LRU_HEADS))
    G['a_gate_b'] = jnp.concatenate([d_gb_r, d_gb_i], axis=0).reshape(1, 2, LRU_HEADS, D // LRU_HEADS)
    G['a_w_in'] = _mm_tn("lru_in_dw", hn0, dproj)[None]
    dhn0 = _mm_nt("lru_in_dx", dproj, Wf['a_w_in'][0], F32)
    grad_x, dgain[0][0] = _rw_vjp("lru_pre_bwd", _f_pre(F32), [x], [gains[0][0]], [dhn0], 1, d_rows=[(0, F32)],
                                  d_params=[0], adds=[(0, dx, 0)], seq=S)
    d_mem_norm, = _rw_vjp("mem_norm_bwd", _f_pre(F32), [mem], [row(Wf['mem_norm'])], [dmn0], 1, d_params=[0],
                          ct_adds=[(0, dmn1, 0)])
    G['mem_norm'] = d_mem_norm.reshape(-1)
    G['ln_gains'] = jnp.stack([jnp.concatenate(dgain[i], axis=0) for i in range(2)])

    sharded = BIG + SMALL
    g_blocks = jnp.concatenate([_blocks_from_full(G[n], shp[n], SHARD_AXIS[n]) for n in sharded], axis=1)
    g_blocks = jnp.pad(g_blocks, ((0, 0), (0, n_big + n_small_pad - g_blocks.shape[1])))
    rows_s = big_rows + small_rows
    tr = max(t for t in range(SUB, 161, SUB) if rows_s % t == 0)
    by_core = jnp.swapaxes(g_blocks.reshape(N_CHIP, 2, rows_s, PACK_W), 0, 1)
    kept, got = _swap_with_sibling("exchange_grads_sibling", by_core)
    parts_s = _exchange_chips("exchange_grads_chips", _add_blocks("sum_with_sibling", kept, got, tr))
    n_repl = sum(_numel(shp[n]) for n in REPL)
    n_repl_pad = -(-n_repl // (8 * PACK_W)) * (8 * PACK_W)
    rows_r = n_repl_pad // PACK_W
    parts_r = _all_gather("gather_replicated_grads", _pack_flat([G[n] for n in REPL], n_repl_pad).reshape(rows_r, PACK_W))

    def pack(src, names, total, rows_):
        return _pack_flat([src[n] for n in names], total).reshape(rows_, PACK_W)

    out_s = _adamw("adamw_sharded", parts_s, pack(P, sharded, n_big + n_small_pad, rows_s),
                   pack(M_, sharded, n_big + n_small_pad, rows_s), pack(V_, sharded, n_big + n_small_pad, rows_s), tr)
    out_r = _adamw("adamw_replicated", parts_r, pack(P, REPL, n_repl_pad, rows_r), pack(M_, REPL, n_repl_pad, rows_r),
                   pack(V_, REPL, n_repl_pad, rows_r), rows_r)

    def unpack(packed, names):
        flat = packed.reshape(-1)
        res, o = {}, 0
        for n in names:
            ne = _numel(shp[n])
            res[n] = flat[o:o + ne].reshape(shp[n])
            o += ne
        return res

    results = []
    for j in range(4):
        both = {**unpack(out_s[j], sharded), **unpack(out_r[j], REPL)}
        results += [both[n] for n in WEIGHTS]
    return (loss, grad_x.reshape(Bl, S, D), *results)


def kernel(x, mem, ln_gains, mem_norm, a_conv_w, a_conv_b, a_w_in, a_b_in, a_gate_w, a_gate_b, a_lambda, a_w_out, a_b_out, b_mu, b_w_rkv, b_w0, b_w1, b_w2, b_a0, b_a1, b_a2, b_g1, b_g2, b_k_k, b_k_a, b_r_k, b_gn_g, b_gn_b, b_w_o, c_w_q, c_w_kv, c_w_o, m_w_up, m_w_down, loss_target, m_ln_gains, m_mem_norm, m_a_conv_w, m_a_conv_b, m_a_w_in, m_a_b_in, m_a_gate_w, m_a_gate_b, m_a_lambda, m_a_w_out, m_a_b_out, m_b_mu, m_b_w_rkv, m_b_w0, m_b_w1, m_b_w2, m_b_a0, m_b_a1, m_b_a2, m_b_g1, m_b_g2, m_b_k_k, m_b_k_a, m_b_r_k, m_b_gn_g, m_b_gn_b, m_b_w_o, m_c_w_q, m_c_w_kv, m_c_w_o, m_m_w_up, m_m_w_down, v_ln_gains, v_mem_norm, v_a_conv_w, v_a_conv_b, v_a_w_in, v_a_b_in, v_a_gate_w, v_a_gate_b, v_a_lambda, v_a_w_out, v_a_b_out, v_b_mu, v_b_w_rkv, v_b_w0, v_b_w1, v_b_w2, v_b_a0, v_b_a1, v_b_a2, v_b_g1, v_b_g2, v_b_k_k, v_b_k_a, v_b_r_k, v_b_gn_g, v_b_gn_b, v_b_w_o, v_c_w_q, v_c_w_kv, v_c_w_o, v_m_w_up, v_m_w_down):
    weights = (ln_gains, mem_norm, a_conv_w, a_conv_b, a_w_in, a_b_in, a_gate_w, a_gate_b, a_lambda, a_w_out, a_b_out, b_mu, b_w_rkv, b_w0, b_w1, b_w2, b_a0, b_a1, b_a2, b_g1, b_g2, b_k_k, b_k_a, b_r_k, b_gn_g, b_gn_b, b_w_o, c_w_q, c_w_kv, c_w_o, m_w_up, m_w_down)
    moments1 = (m_ln_gains, m_mem_norm, m_a_conv_w, m_a_conv_b, m_a_w_in, m_a_b_in, m_a_gate_w, m_a_gate_b, m_a_lambda, m_a_w_out, m_a_b_out, m_b_mu, m_b_w_rkv, m_b_w0, m_b_w1, m_b_w2, m_b_a0, m_b_a1, m_b_a2, m_b_g1, m_b_g2, m_b_k_k, m_b_k_a, m_b_r_k, m_b_gn_g, m_b_gn_b, m_b_w_o, m_c_w_q, m_c_w_kv, m_c_w_o, m_m_w_up, m_m_w_down)
    moments2 = (v_ln_gains, v_mem_norm, v_a_conv_w, v_a_conv_b, v_a_w_in, v_a_b_in, v_a_gate_w, v_a_gate_b, v_a_lambda, v_a_w_out, v_a_b_out, v_b_mu, v_b_w_rkv, v_b_w0, v_b_w1, v_b_w2, v_b_a0, v_b_a1, v_b_a2, v_b_g1, v_b_g2, v_b_k_k, v_b_k_a, v_b_r_k, v_b_gn_g, v_b_gn_b, v_b_w_o, v_c_w_q, v_c_w_kv, v_c_w_o, v_m_w_up, v_m_w_down)
    return _train_step(x, mem, loss_target, dict(zip(WEIGHTS, weights)), dict(zip(WEIGHTS, moments1)),
                       dict(zip(WEIGHTS, moments2)))
```

```python
import jax
import jax.numpy as jnp
from jax import lax
from jax.experimental import pallas as pl
from jax.experimental.pallas import tpu as pltpu

F32 = jnp.float32
BF16 = jnp.bfloat16
N_DEV = 8
MESH = pl.DeviceIdType.MESH

V7X_VMEM_LIMIT_BYTES = 56 * 1024 * 1024
SUB = 8
HALO_ROWS = SUB
LANES = 128
RWKV_N = 64
RMS_EPS = 1e-6
GN_EPS = 64e-5
LRU_C = 8.0
XATTN_HEADS = 4
LRU_HEADS = 4

ADAM_LR, ADAM_B1, ADAM_B2, ADAM_EPS, ADAM_WD, ADAM_STEP = 0.001, 0.9, 0.999, 1e-8, 0.01, 10

WEIGHTS = ['ln_gains', 'mem_norm', 'a_conv_w', 'a_conv_b', 'a_w_in', 'a_b_in', 'a_gate_w', 'a_gate_b',
           'a_lambda', 'a_w_out', 'a_b_out', 'b_mu', 'b_w_rkv', 'b_w0', 'b_w1', 'b_w2', 'b_a0', 'b_a1',
           'b_a2', 'b_g1', 'b_g2', 'b_k_k', 'b_k_a', 'b_r_k', 'b_gn_g', 'b_gn_b', 'b_w_o', 'c_w_q',
           'c_w_kv', 'c_w_o', 'm_w_up', 'm_w_down']
SHARD_AXIS = {'ln_gains': 2, 'mem_norm': None, 'a_conv_w': 2, 'a_conv_b': None, 'a_w_in': 2, 'a_b_in': None,
              'a_gate_w': 3, 'a_gate_b': 3, 'a_lambda': None, 'a_w_out': 1, 'a_b_out': None, 'b_mu': 2,
              'b_w_rkv': 2, 'b_w0': 1, 'b_w1': 1, 'b_w2': 2, 'b_a0': 1, 'b_a1': 1, 'b_a2': 2, 'b_g1': 1,
              'b_g2': 2, 'b_k_k': 1, 'b_k_a': 1, 'b_r_k': None, 'b_gn_g': 1, 'b_gn_b': 1, 'b_w_o': 1,
              'c_w_q': 1, 'c_w_kv': 2, 'c_w_o': 1, 'm_w_up': 2, 'm_w_down': 1}
BIG = ['a_w_in', 'a_gate_w', 'a_w_out', 'b_w_rkv', 'b_w1', 'b_w2', 'b_a1', 'b_a2', 'b_g1', 'b_g2', 'b_w_o',
       'c_w_q', 'c_w_kv', 'c_w_o', 'm_w_up', 'm_w_down']
SMALL = ['ln_gains', 'a_conv_w', 'a_gate_b', 'b_mu', 'b_w0', 'b_a0', 'b_k_k', 'b_k_a', 'b_gn_g', 'b_gn_b']
REPL = ['mem_norm', 'a_conv_b', 'a_b_in', 'a_lambda', 'a_b_out', 'b_r_k']
PACK_W = 1024


def _cparams(*sem):
    return pltpu.CompilerParams(dimension_semantics=sem, vmem_limit_bytes=V7X_VMEM_LIMIT_BYTES)


def _shift_tile(tile, halo, k, pos, seq, tm):
    if k > 0:
        ext = jnp.concatenate([halo, tile], axis=0)
        r = pltpu.roll(ext, k, 0)[HALO_ROWS:HALO_ROWS + tm]
        return jnp.where(pos >= k, r, 0.0)
    kk = -k
    ext = jnp.concatenate([tile, halo], axis=0)
    r = pltpu.roll(ext, tm + HALO_ROWS - kk, 0)[0:tm]
    return jnp.where(pos + kk < seq, r, 0.0)


def _row_specs(rows, shifts, params, per_b, seq, tm):
    T = rows[0].shape[0]
    ns = seq // tm
    specs = [pl.BlockSpec((tm, r.shape[1]), lambda b, i: (b * ns + i, 0)) for r in rows]
    halo_keys = []
    for idx, k in shifts:
        key = (idx, k > 0)
        if key not in halo_keys:
            halo_keys.append(key)
    per8 = tm // HALO_ROWS
    last8 = T // HALO_ROWS - 1
    for idx, prev in halo_keys:
        c = rows[idx].shape[1]
        if prev:
            specs.append(pl.BlockSpec((HALO_ROWS, c), lambda b, i: (jnp.maximum((b * ns + i) * per8 - 1, 0), 0)))
        else:
            specs.append(pl.BlockSpec((HALO_ROWS, c), lambda b, i: (jnp.minimum((b * ns + i + 1) * per8, last8), 0)))
    for j, p in enumerate(params):
        if j in per_b:
            specs.append(pl.BlockSpec((None,) + p.shape[1:], lambda b, i: (b, 0, 0)))
        else:
            specs.append(pl.BlockSpec(p.shape, lambda b, i, nd=p.ndim: (0,) * nd))
    return specs, halo_keys


def _load_tiles(refs, rows, shifts, halo_keys, params, seq, tm):
    nr, nh = len(rows), len(halo_keys)
    i = pl.program_id(1)
    pos = i * tm + lax.broadcasted_iota(jnp.int32, (tm, 1), 0)
    row_t = [r[...] for r in refs[:nr]]
    halo_t = {key: refs[nr + j][...] for j, key in enumerate(halo_keys)}
    sh_t = [_shift_tile(row_t[idx], halo_t[(idx, k > 0)], k, pos, seq, tm) for idx, k in shifts]
    par_t = [p[...] for p in refs[nr + nh:nr + nh + len(params)]]
    return pos, row_t, sh_t, par_t


ROW_TILES = (512, 256)
ROW_TILE_BUDGET_BYTES = 16 * 1024 * 1024


def _row_tile(seq, row_bytes):
    for tm in ROW_TILES:
        if 2 * tm * row_bytes <= ROW_TILE_BUDGET_BYTES:
            return min(tm, seq)
    return min(ROW_TILES[-1], seq)


def _rw(name, fn, rows, params=(), outs=(), accs=(), shifts=(), seq=None, per_b=()):
    rows, params = list(rows), list(params)
    T = rows[0].shape[0]
    seq = seq or T
    tm = _row_tile(seq, sum(r.shape[1] * r.dtype.itemsize for r in rows) + sum(c * jnp.dtype(dt).itemsize for c, dt in outs))
    nb, ns = T // seq, seq // tm
    in_specs, halo_keys = _row_specs(rows, shifts, params, per_b, seq, tm)
    n_in, no = len(in_specs), len(outs)

    def body(*refs):
        pos, row_t, sh_t, par_t = _load_tiles(refs, rows, shifts, halo_keys, params, seq, tm)
        res = fn(pos, *row_t, *sh_t, *par_t)
        out_refs = refs[n_in:n_in + no]
        acc_refs = refs[n_in + no:]
        for o_ref, v in zip(out_refs, res[:no]):
            o_ref[...] = v.astype(o_ref.dtype)
        if acc_refs:
            @pl.when((pl.program_id(0) == 0) & (pl.program_id(1) == 0))
            def _():
                for a_ref in acc_refs:
                    a_ref[...] = jnp.zeros_like(a_ref)
            for a_ref, v in zip(acc_refs, res[no:]):
                a_ref[...] += v

    out_shape = [jax.ShapeDtypeStruct((T, c), dt) for c, dt in outs]
    out_shape += [jax.ShapeDtypeStruct(s, F32) for s in accs]
    out_specs = [pl.BlockSpec((tm, c), lambda b, i: (b * ns + i, 0)) for c, _ in outs]
    out_specs += [pl.BlockSpec(s, lambda b, i: (0, 0)) for s in accs]
    return pl.pallas_call(body, name=name, grid=(nb, ns), in_specs=in_specs, out_specs=out_specs,
                          out_shape=out_shape, compiler_params=_cparams("arbitrary", "arbitrary"))(*rows, *[
                              rows[idx] for idx, _ in halo_keys], *params)


def _rw_vjp(name, fn, rows, params, cts, n_out, d_rows=(), d_shifts=(), d_params=(), adds=(), ct_adds=(),
            shifts=(), seq=None, per_b=()):
    rows, params, cts = list(rows), list(params), list(cts)
    T = rows[0].shape[0]
    seq = seq or T
    row_bytes = sum(r.shape[1] * r.dtype.itemsize for r in rows + cts + [a for _, a, _ in list(adds) + list(ct_adds)])
    row_bytes += sum(rows[j].shape[1] * jnp.dtype(dt).itemsize for j, dt in d_rows)
    row_bytes += sum(rows[shifts[j][0]].shape[1] * jnp.dtype(dt).itemsize for j, dt in d_shifts)
    tm = _row_tile(seq, row_bytes)
    nb, ns = T // seq, seq // tm
    n_ga = len(adds)
    adds = list(adds) + list(ct_adds)
    add_rows = [a for _, a, _ in adds]
    add_shifts = [(j, k) for j, (_, _, k) in enumerate(adds) if k != 0]
    all_rows = rows + cts + add_rows
    off_ct, off_add = len(rows), len(rows) + len(cts)
    all_shifts = list(shifts) + [(off_add + j, k) for j, k in add_shifts]
    in_specs, halo_keys = _row_specs(all_rows, all_shifts, params, per_b, seq, tm)
    n_in = len(in_specs)
    n_dr, n_ds = len(d_rows), len(d_shifts)

    def body(*refs):
        pos, row_t, sh_t, par_t = _load_tiles(refs, all_rows, all_shifts, halo_keys, params, seq, tm)
        fn_rows, ct_t, add_t = row_t[:off_ct], row_t[off_ct:off_add], row_t[off_add:]
        fn_sh, add_sh = sh_t[:len(shifts)], sh_t[len(shifts):]
        diff = [fn_rows[j] for j, _ in d_rows] + [fn_sh[j] for j, _ in d_shifts] + [par_t[j] for j in d_params]

        def f(*d):
            r2, s2, p2 = list(fn_rows), list(fn_sh), list(par_t)
            for (j, _), v in zip(d_rows, d[:n_dr]):
                r2[j] = v
            for (j, _), v in zip(d_shifts, d[n_dr:n_dr + n_ds]):
                s2[j] = v
            for j, v in zip(d_params, d[n_dr + n_ds:]):
                p2[j] = v
            return tuple(fn(pos, *r2, *s2, *p2)[:n_out])

        sh_iter = iter(add_sh)
        add_v = [add_t[j] if k == 0 else next(sh_iter) for j, (_, _, k) in enumerate(adds)]
        ct_t = list(ct_t)
        for (tgt, _, _), v in zip(adds[n_ga:], add_v[n_ga:]):
            ct_t[tgt] = ct_t[tgt] + v
        outs_v, vjp_fn = jax.vjp(f, *diff)
        grads = list(vjp_fn(tuple(c.astype(o.dtype) for c, o in zip(ct_t, outs_v))))
        for (tgt, _, _), v in zip(adds[:n_ga], add_v[:n_ga]):
            grads[tgt] = grads[tgt] + v
        out_refs = refs[n_in:]
        for o_ref, g in zip(out_refs[:n_dr + n_ds], grads[:n_dr + n_ds]):
            o_ref[...] = g.astype(o_ref.dtype)
        for o_ref, g, j in zip(out_refs[n_dr + n_ds:], grads[n_dr + n_ds:], d_params):
            first = (pl.program_id(1) == 0) if j in per_b else ((pl.program_id(0) == 0) & (pl.program_id(1) == 0))

            @pl.when(first)
            def _(o_ref=o_ref):
                o_ref[...] = jnp.zeros_like(o_ref)
            o_ref[...] += g.astype(F32)

    out_shape, out_specs = [], []
    for j, dt in d_rows:
        out_shape.append(jax.ShapeDtypeStruct(rows[j].shape, dt))
        out_specs.append(pl.BlockSpec((tm, rows[j].shape[1]), lambda b, i: (b * ns + i, 0)))
    for j, dt in d_shifts:
        src = rows[shifts[j][0]]
        out_shape.append(jax.ShapeDtypeStruct(src.shape, dt))
        out_specs.append(pl.BlockSpec((tm, src.shape[1]), lambda b, i: (b * ns + i, 0)))
    for j in d_params:
        p = params[j]
        out_shape.append(jax.ShapeDtypeStruct(p.shape, F32))
        if j in per_b:
            out_specs.append(pl.BlockSpec((None,) + p.shape[1:], lambda b, i: (b, 0, 0)))
        else:
            out_specs.append(pl.BlockSpec(p.shape, lambda b, i, nd=p.ndim: (0,) * nd))
    return pl.pallas_call(body, name=name, grid=(nb, ns), in_specs=in_specs, out_specs=out_specs,
                          out_shape=out_shape, compiler_params=_cparams("arbitrary", "arbitrary"))(*all_rows, *[
                              all_rows[idx] for idx, _ in halo_keys], *params)


MM_TILE = 1024
MM_DEEP = 2048


def _mm_nn(name, x, w, out_dtype):
    M, K = x.shape
    N = w.shape[1]
    tm, tn = min(MM_TILE if K <= MM_DEEP else MM_TILE // 2, M), min(MM_TILE, N)

    def body(x_ref, w_ref, o_ref):
        o_ref[...] = jnp.dot(x_ref[...].astype(BF16), w_ref[...].astype(BF16),
                             preferred_element_type=F32).astype(o_ref.dtype)

    return pl.pallas_call(body, name=name, grid=(M // tm, N // tn),
                          in_specs=[pl.BlockSpec((tm, K), lambda i, j: (i, 0)), pl.BlockSpec((K, tn), lambda i, j: (0, j))],
                          out_specs=pl.BlockSpec((tm, tn), lambda i, j: (i, j)),
                          out_shape=jax.ShapeDtypeStruct((M, N), out_dtype),
                          compiler_params=_cparams("parallel", "parallel"))(x, w)


def _mm_nt(name, dy, w, out_dtype):
    M, N = dy.shape
    K = w.shape[0]
    tm, tk = min(MM_TILE if N <= MM_DEEP else MM_TILE // 2, M), min(MM_TILE, K)

    def body(dy_ref, w_ref, o_ref):
        o_ref[...] = lax.dot_general(dy_ref[...].astype(BF16), w_ref[...].astype(BF16), (((1,), (1,)), ((), ())),
                                     preferred_element_type=F32).astype(o_ref.dtype)

    return pl.pallas_call(body, name=name, grid=(M // tm, K // tk),
                          in_specs=[pl.BlockSpec((tm, N), lambda i, j: (i, 0)), pl.BlockSpec((tk, N), lambda i, j: (j, 0))],
                          out_specs=pl.BlockSpec((tm, tk), lambda i, j: (i, j)),
                          out_shape=jax.ShapeDtypeStruct((M, K), out_dtype),
                          compiler_params=_cparams("parallel", "parallel"))(dy, w)


def _mm_tn(name, x, dy):
    M, K = x.shape
    N = dy.shape[1]
    tm, tk, tn = min(MM_TILE, M), min(MM_TILE, K), min(MM_TILE, N)

    def body(x_ref, dy_ref, o_ref):
        @pl.when(pl.program_id(2) == 0)
        def _():
            o_ref[...] = jnp.zeros_like(o_ref)
        o_ref[...] += lax.dot_general(x_ref[...].astype(BF16), dy_ref[...].astype(BF16), (((0,), (0,)), ((), ())),
                                      preferred_element_type=F32)

    return pl.pallas_call(body, name=name, grid=(K // tk, N // tn, M // tm),
                          in_specs=[pl.BlockSpec((tm, tk), lambda i, j, m: (m, i)), pl.BlockSpec((tm, tn), lambda i, j, m: (m, j))],
                          out_specs=pl.BlockSpec((tk, tn), lambda i, j, m: (i, j)),
                          out_shape=jax.ShapeDtypeStruct((K, N), F32),
                          compiler_params=_cparams("parallel", "parallel", "arbitrary"))(x, dy)


def _lru_scan(name, a, b, seq, ct=128):
    T, C = a.shape

    def body(a_ref, b_ref, h_ref):
        A, Bv = a_ref[...], b_ref[...]
        row = lax.broadcasted_iota(jnp.int32, (seq, 1), 0)
        k = 1
        while k < seq:
            keep = row >= k
            Bv = A * jnp.where(keep, pltpu.roll(Bv, k, 0), 0.0) + Bv
            A = A * jnp.where(keep, pltpu.roll(A, k, 0), 1.0)
            k *= 2
        h_ref[...] = Bv

    spec = pl.BlockSpec((seq, ct), lambda b_, j: (b_, j))
    return pl.pallas_call(body, name=name, grid=(T // seq, C // ct), in_specs=[spec, spec], out_specs=spec,
                          out_shape=jax.ShapeDtypeStruct((T, C), F32),
                          compiler_params=_cparams("parallel", "parallel"))(a, b)


def _lru_scan_bwd(name, a, h, dh, seq, ct=128):
    T, C = a.shape

    def body(a_ref, h_ref, dh_ref, da_ref, db_ref):
        row = lax.broadcasted_iota(jnp.int32, (seq, 1), 0)
        Cf = jnp.where(row < seq - 1, pltpu.roll(a_ref[...], seq - 1, 0), 0.0)
        G = dh_ref[...]
        k = 1
        while k < seq:
            keep = row + k < seq
            G = Cf * jnp.where(keep, pltpu.roll(G, seq - k, 0), 0.0) + G
            Cf = Cf * jnp.where(keep, pltpu.roll(Cf, seq - k, 0), 1.0)
            k *= 2
        db_ref[...] = G
        da_ref[...] = G * jnp.where(row >= 1, pltpu.roll(h_ref[...], 1, 0), 0.0)

    spec = pl.BlockSpec((seq, ct), lambda b_, j: (b_, j))
    return pl.pallas_call(body, name=name, grid=(T // seq, C // ct), in_specs=[spec] * 3, out_specs=[spec] * 2,
                          out_shape=[jax.ShapeDtypeStruct((T, C), F32)] * 2,
                          compiler_params=_cparams("parallel", "parallel"))(a, h, dh)


def _pair_consts():
    jj = lax.broadcasted_iota(jnp.int32, (RWKV_N, LANES), 0)
    ll = lax.broadcasted_iota(jnp.int32, (RWKV_N, LANES), 1)
    pick = ((ll & (RWKV_N - 1)) == jj).astype(BF16)
    l0 = lax.broadcasted_iota(jnp.int32, (LANES, LANES), 0)
    l1 = lax.broadcasted_iota(jnp.int32, (LANES, LANES), 1)
    same = ((l0 >> 6) == (l1 >> 6)).astype(BF16)
    return pick, same


N_COL = 5


def _cols_raw(rows, pick, same):
    lhs = [pick * jnp.broadcast_to(x, (RWKV_N, LANES)).astype(BF16) for x in rows]
    out = jnp.dot(jnp.concatenate(lhs, axis=0), same, preferred_element_type=F32)
    return tuple(out[i * RWKV_N:(i + 1) * RWKV_N] for i in range(len(rows)))


def _cols_adjoint(g):
    n = len(g)
    g = list(g) + [jnp.zeros_like(g[0])] * (n % 2)
    head = (lax.broadcasted_iota(jnp.int32, (SUB, LANES), 1) >> 6) == lax.broadcasted_iota(jnp.int32, (SUB, LANES), 0)
    sums = lax.dot_general(head.astype(BF16), jnp.concatenate([t.astype(BF16) for t in g], axis=0),
                           (((1,), (1,)), ((), ())), preferred_element_type=F32)
    low = lax.broadcasted_iota(jnp.int32, (1, LANES), 1) < RWKV_N
    rows = []
    for i in range(0, n, 2):
        blk = sums[:, i * RWKV_N:(i + 2) * RWKV_N]
        swapped = pltpu.roll(blk, RWKV_N, 1)
        rows.append(jnp.where(low, blk[0:1], swapped[1:2]))
        rows.append(jnp.where(low, swapped[0:1], blk[1:2]))
    return tuple(rows[:n])


def _rwkv_cols(rows, pick, same):
    flat = []
    for r, w, k, v, a, b in rows:
        flat += [a, 1.0 - w, b, k, r]
    tiles = _cols_raw(tuple(flat), pick, same)
    return [tiles[N_COL * p:N_COL * (p + 1)] for p in range(len(rows))]


def _rwkv_step(s0, tiles, v):
    ca, ce, cb, ck, cr = tiles
    sa = jnp.sum(s0 * ca, axis=0, keepdims=True)
    s1 = s0 - s0 * ce + cb * sa + ck * v
    return s1, jnp.sum(s1 * cr, axis=0, keepdims=True), sa


def _rwkv_step_bwd(s0, s1, sa, tiles, v, dy, ds1):
    ca, ce, cb, ck, cr = tiles
    ds1 = ds1 + cr * dy
    dsa = jnp.sum(ds1 * cb, axis=0, keepdims=True)
    dv = jnp.sum(ds1 * ck, axis=0, keepdims=True)
    ds0 = ds1 - ds1 * ce + ca * dsa
    return ds0, (s0 * dsa, ds1 * s0, ds1 * sa, ds1 * v, s1 * dy), dv


def _rwkv_specs(C, seq, tc, G, reverse):
    nc = seq // tc
    if reverse:
        row = pl.BlockSpec((tc, LANES * G), lambda b, g, c: (b * nc + nc - 1 - c, g))
        st = pl.BlockSpec((tc, G, RWKV_N, LANES), lambda b, g, c: (b * nc + nc - 1 - c, g, 0, 0))
    else:
        row = pl.BlockSpec((tc, LANES * G), lambda b, g, c: (b * nc + c, g))
        st = pl.BlockSpec((tc, G, RWKV_N, LANES), lambda b, g, c: (b * nc + c, g, 0, 0))
    return row, st


def _rwkv_fwd(name, r, w, k, v, a, b, seq, tc=32, G=8):
    T, C = r.shape
    tc = min(tc, seq)
    NP = C // LANES
    row, st = _rwkv_specs(C, seq, tc, G, False)
    last = pl.BlockSpec((None, G, RWKV_N, LANES), lambda b_, g, c: (b_, g, 0, 0))

    def body(r_ref, w_ref, k_ref, v_ref, a_ref, b_ref, y_ref, st_ref, sa_ref, last_ref, s_scr):
        @pl.when(pl.program_id(2) == 0)
        def _():
            s_scr[...] = jnp.zeros_like(s_scr)
        pick, same = _pair_consts()
        rid = lax.broadcasted_iota(jnp.int32, (SUB, LANES), 0)

        def group(t8, carry):
            base = pl.multiple_of(t8 * SUB, SUB)
            lanes = [pl.ds(LANES * p, LANES) for p in range(G)]
            blk = [[x[pl.ds(base, SUB), ln] for x in (r_ref, w_ref, k_ref, v_ref, a_ref, b_ref)] for ln in lanes]
            s = [s_scr[p] for p in range(G)]
            ys = [jnp.zeros((SUB, LANES), F32) for _ in range(G)]
            sas = [jnp.zeros((SUB, LANES), F32) for _ in range(G)]
            for i in range(SUB):
                tiles = _rwkv_cols([tuple(b_[i:i + 1] for b_ in blk[p]) for p in range(G)], pick, same)
                for p in range(G):
                    st_ref[base + i, p] = s[p]
                    s[p], y, sa = _rwkv_step(s[p], tiles[p], blk[p][3][i:i + 1])
                    ys[p] = jnp.where(rid == i, y, ys[p])
                    sas[p] = jnp.where(rid == i, sa, sas[p])
            for p in range(G):
                s_scr[p] = s[p]
                y_ref[pl.ds(base, SUB), lanes[p]] = ys[p]
                sa_ref[pl.ds(base, SUB), lanes[p]] = sas[p]
            return carry

        lax.fori_loop(0, tc // SUB, group, 0)

        @pl.when(pl.program_id(2) == seq // tc - 1)
        def _():
            last_ref[...] = s_scr[...]

    return pl.pallas_call(body, name=name, grid=(T // seq, NP // G, seq // tc), in_specs=[row] * 6,
                          out_specs=[row, st, row, last],
                          out_shape=[jax.ShapeDtypeStruct((T, C), F32), jax.ShapeDtypeStruct((T, NP, RWKV_N, LANES), F32),
                                     jax.ShapeDtypeStruct((T, C), F32),
                                     jax.ShapeDtypeStruct((T // seq, NP, RWKV_N, LANES), F32)],
                          scratch_shapes=[pltpu.VMEM((G, RWKV_N, LANES), F32)],
                          compiler_params=_cparams("parallel", "parallel", "arbitrary"))(r, w, k, v, a, b)


def _rwkv_bwd(name, r, w, k, v, a, b, states, sa, last, dy, dr_p, dk_p, dv_p, seq, tc=32, G=8):
    T, C = r.shape
    tc = min(tc, seq)
    NP = C // LANES
    row, st = _rwkv_specs(C, seq, tc, G, True)
    last_spec = pl.BlockSpec((None, G, RWKV_N, LANES), lambda b_, g, c: (b_, g, 0, 0))

    def body(r_ref, w_ref, k_ref, v_ref, a_ref, b_ref, st_ref, sa_ref, last_ref, dy_ref, drp_ref, dkp_ref, dvp_ref,
             dr_ref, dw_ref, dk_ref, dv_ref, da_ref, db_ref, ds_scr, s1_scr):
        @pl.when(pl.program_id(2) == 0)
        def _():
            ds_scr[...] = jnp.zeros_like(ds_scr)
            s1_scr[...] = last_ref[...]
        pick, same = _pair_consts()
        rid = lax.broadcasted_iota(jnp.int32, (SUB, LANES), 0)
        out_refs = (da_ref, dw_ref, db_ref, dk_ref, dr_ref)

        def group(n, carry):
            base = pl.multiple_of((tc // SUB - 1 - n) * SUB, SUB)
            lanes = [pl.ds(LANES * p, LANES) for p in range(G)]
            blk = [[x[pl.ds(base, SUB), ln] for x in (r_ref, w_ref, k_ref, v_ref, a_ref, b_ref)] for ln in lanes]
            dys = [dy_ref[pl.ds(base, SUB), ln] for ln in lanes]
            sas = [sa_ref[pl.ds(base, SUB), ln] for ln in lanes]
            ds = [ds_scr[p] for p in range(G)]
            s1 = [s1_scr[p] for p in range(G)]
            zero = jnp.zeros((SUB, LANES), F32)
            gs = [[zero, zero, zero, dkp_ref[pl.ds(base, SUB), ln], drp_ref[pl.ds(base, SUB), ln]] for ln in lanes]
            gv = [dvp_ref[pl.ds(base, SUB), ln] for ln in lanes]
            for i in reversed(range(SUB)):
                tiles = _rwkv_cols([tuple(b_[i:i + 1] for b_ in blk[p]) for p in range(G)], pick, same)
                adj = []
                for p in range(G):
                    s0 = st_ref[base + i, p]
                    ds[p], g_tiles, dv = _rwkv_step_bwd(s0, s1[p], sas[p][i:i + 1], tiles[p], blk[p][3][i:i + 1],
                                                        dys[p][i:i + 1], ds[p])
                    s1[p] = s0
                    adj += list(g_tiles)
                    gv[p] = gv[p] + jnp.where(rid == i, dv, 0.0)
                rows = _cols_adjoint(adj)
                for p in range(G):
                    for j in range(N_COL):
                        gs[p][j] = gs[p][j] + jnp.where(rid == i, rows[N_COL * p + j], 0.0)
            for p in range(G):
                ds_scr[p] = ds[p]
                s1_scr[p] = s1[p]
                dv_ref[pl.ds(base, SUB), lanes[p]] = gv[p]
                for j in range(N_COL):
                    out_refs[j][pl.ds(base, SUB), lanes[p]] = gs[p][j]
            return carry

        lax.fori_loop(0, tc // SUB, group, 0)

    return pl.pallas_call(body, name=name, grid=(T // seq, NP // G, seq // tc),
                          in_specs=[row] * 6 + [st, row, last_spec] + [row] * 4, out_specs=[row] * 6,
                          out_shape=[jax.ShapeDtypeStruct((T, C), F32)] * 6,
                          scratch_shapes=[pltpu.VMEM((G, RWKV_N, LANES), F32)] * 2,
                          compiler_params=_cparams("parallel", "parallel", "arbitrary"))(
                              r, w, k, v, a, b, states, sa, last, dy, dr_p, dk_p, dv_p)


def _all_gather(name, shard):
    def body(x_ref, o_ref, send_sems, recv_sems, local_sem):
        x, y, c = lax.axis_index("x"), lax.axis_index("y"), lax.axis_index("c")
        me, sibling = (x, y, c), (x, y, 1 - c)
        chips = [(1 - x, y), (x, 1 - y), (1 - x, 1 - y)]

        def copy(k, block, to, src=None):
            dst = o_ref.at[4 * block[0] + 2 * block[1] + block[2]]
            return pltpu.make_async_remote_copy(src_ref=dst if src is None else src, dst_ref=dst, send_sem=send_sems.at[k],
                                                recv_sem=recv_sems.at[k], device_id=to, device_id_type=MESH)

        mine = pltpu.make_async_copy(x_ref, o_ref.at[4 * x + 2 * y + c], local_sem)
        mine.start()
        first = [copy(0, me, sibling, src=x_ref)] + [copy(1 + j, me, (*chip, c), src=x_ref) for j, chip in enumerate(chips)]
        for cp in first:
            cp.start()
        passed = [copy(4 + j, (*chip, c), sibling) for j, chip in enumerate(chips)]
        for j, chip in enumerate(chips):
            copy(1 + j, (*chip, c), me).wait_recv()
            passed[j].start()
        copy(0, sibling, me).wait_recv()
        for j, chip in enumerate(chips):
            copy(4 + j, (*chip, 1 - c), me).wait_recv()
        for cp in first + passed:
            cp.wait_send()
        mine.wait()

    hbm = pl.BlockSpec(memory_space=pltpu.HBM)
    return pl.pallas_call(body, name=name, in_specs=[hbm], out_specs=hbm,
                          out_shape=jax.ShapeDtypeStruct((N_DEV,) + shard.shape, shard.dtype),
                          scratch_shapes=[pltpu.SemaphoreType.DMA((N_DEV - 1,)), pltpu.SemaphoreType.DMA((N_DEV - 1,)),
                                          pltpu.SemaphoreType.DMA])(shard)


N_CHIP = 4


def _swap_with_sibling(name, blocks, n_split):
    _, N, R, W = blocks.shape
    rq = R // n_split

    def body(x_ref, got_ref, send_sems, recv_sems):
        x, y, c = lax.axis_index("x"), lax.axis_index("y"), lax.axis_index("c")
        copies = []
        for j in range(N):
            for q in range(n_split):
                rows = pl.ds(q * rq, rq)
                cp = pltpu.make_async_remote_copy(src_ref=x_ref.at[1 - c, j, rows], dst_ref=got_ref.at[j, rows],
                                                  send_sem=send_sems.at[j * n_split + q], recv_sem=recv_sems.at[j * n_split + q],
                                                  device_id=(x, y, 1 - c), device_id_type=MESH)
                cp.start()
                copies.append(cp)
        for cp in copies:
            cp.wait()

    hbm = pl.BlockSpec(memory_space=pltpu.HBM)
    return pl.pallas_call(body, name=name, in_specs=[hbm], out_specs=hbm,
                          out_shape=jax.ShapeDtypeStruct(blocks.shape[1:], blocks.dtype),
                          scratch_shapes=[pltpu.SemaphoreType.DMA((N * n_split,)), pltpu.SemaphoreType.DMA((N * n_split,))])(blocks)


def _add_own_blocks(name, blocks, got, tr):
    _, N, R, W = blocks.shape

    def body(c_ref, a_ref, b_ref, o_ref):
        o_ref[...] = a_ref[...] + b_ref[...]

    grid_spec = pltpu.PrefetchScalarGridSpec(
        num_scalar_prefetch=1, grid=(N, R // tr),
        in_specs=[pl.BlockSpec((None, None, tr, W), lambda n, i, c: (c[0], n, i, 0)),
                  pl.BlockSpec((None, tr, W), lambda n, i, c: (n, i, 0))],
        out_specs=pl.BlockSpec((None, tr, W), lambda n, i, c: (n, i, 0)))
    core = lax.axis_index("c").astype(jnp.int32).reshape(1)
    return pl.pallas_call(body, name=name, grid_spec=grid_spec, out_shape=jax.ShapeDtypeStruct(got.shape, got.dtype),
                          compiler_params=_cparams("parallel", "parallel"))(core, blocks, got)


def _exchange_chips(name, blocks):
    def body(z_ref, o_ref, send_sems, recv_sems, local_sem):
        x, y, c = lax.axis_index("x"), lax.axis_index("y"), lax.axis_index("c")
        chip = 2 * x + y
        mine = pltpu.make_async_copy(z_ref.at[chip], o_ref.at[chip], local_sem)
        mine.start()
        copies = []
        for k in range(1, N_CHIP):
            to = (x ^ (k >> 1), y ^ (k & 1), c)
            cp = pltpu.make_async_remote_copy(src_ref=z_ref.at[chip ^ k], dst_ref=o_ref.at[chip], send_sem=send_sems.at[k - 1],
                                              recv_sem=recv_sems.at[k - 1], device_id=to, device_id_type=MESH)
            cp.start()
            copies.append(cp)
        for k in range(1, N_CHIP):
            pltpu.make_async_remote_copy(src_ref=z_ref.at[chip], dst_ref=o_ref.at[chip ^ k], send_sem=send_sems.at[k - 1],
                                         recv_sem=recv_sems.at[k - 1], device_id=(x ^ (k >> 1), y ^ (k & 1), c),
                                         device_id_type=MESH).wait_recv()
        for cp in copies:
            cp.wait_send()
        mine.wait()

    hbm = pl.BlockSpec(memory_space=pltpu.HBM)
    return pl.pallas_call(body, name=name, in_specs=[hbm], out_specs=hbm,
                          out_shape=jax.ShapeDtypeStruct(blocks.shape, blocks.dtype),
                          scratch_shapes=[pltpu.SemaphoreType.DMA((N_CHIP - 1,)), pltpu.SemaphoreType.DMA((N_CHIP - 1,)),
                                          pltpu.SemaphoreType.DMA])(blocks)


def _adamw(name, parts, w, m, v, tr):
    R, W = w.shape
    n_parts = parts.shape[0]

    def body(p_ref, w_ref, m_ref, v_ref, g_ref, d_ref, nm_ref, nv_ref):
        g = p_ref[0]
        for s in range(1, n_parts):
            g = g + p_ref[s]
        nm = ADAM_B1 * m_ref[...] + (1.0 - ADAM_B1) * g
        nv = ADAM_B2 * v_ref[...] + (1.0 - ADAM_B2) * (g * g)
        m_hat = nm / (1.0 - ADAM_B1 ** ADAM_STEP)
        v_hat = nv / (1.0 - ADAM_B2 ** ADAM_STEP)
        g_ref[...] = g
        d_ref[...] = -ADAM_LR * (m_hat / (jnp.sqrt(v_hat) + ADAM_EPS) + ADAM_WD * w_ref[...])
        nm_ref[...] = nm
        nv_ref[...] = nv

    spec = pl.BlockSpec((tr, W), lambda i: (i, 0))
    return pl.pallas_call(body, name=name, grid=(R // tr,),
                          in_specs=[pl.BlockSpec((n_parts, tr, W), lambda i: (0, i, 0)), spec, spec, spec],
                          out_specs=[spec] * 4, out_shape=[jax.ShapeDtypeStruct((R, W), F32)] * 4,
                          compiler_params=_cparams("parallel"))(parts, w, m, v)


def _rms(x, g):
    return x * lax.rsqrt(jnp.mean(x * x, axis=-1, keepdims=True) + RMS_EPS) * g


def _softplus(z):
    return jnp.maximum(z, 0.0) + jnp.log(1.0 + jnp.exp(-jnp.abs(z)))


def _neg_expm1(z):
    series = -z * (1.0 + z * (0.5 + z * (1.0 / 6.0)))
    return jnp.where(z > -1e-3, series, 1.0 - jnp.exp(z))


def _gelu_tanh(x):
    return 0.5 * x * (1.0 + jnp.tanh(0.7978845608028654 * (x + 0.044715 * x * x * x)))


def _bdot(x, w):
    return jnp.dot(x.astype(BF16), w.astype(BF16), preferred_element_type=F32)


def _seg_consts(C):
    c0 = lax.broadcasted_iota(jnp.int32, (C, LANES), 0)
    h1 = lax.broadcasted_iota(jnp.int32, (C, LANES), 1)
    red = ((c0 >> 6) == h1).astype(BF16)
    h0 = lax.broadcasted_iota(jnp.int32, (LANES, C), 0)
    c1 = lax.broadcasted_iota(jnp.int32, (LANES, C), 1)
    exp = ((c1 >> 6) == h0).astype(BF16)
    return red, exp


def _split_dot(x, m):
    hi = x.astype(BF16)
    lo = (x - hi.astype(F32)).astype(BF16)
    return jnp.dot(hi, m, preferred_element_type=F32) + jnp.dot(lo, m, preferred_element_type=F32)


def _segsum_raw(x):
    red, exp = _seg_consts(x.shape[1])
    return _split_dot(_split_dot(x, red), exp)


@jax.custom_vjp
def _segsum(x):
    return _segsum_raw(x)


_segsum.defvjp(lambda x: (_segsum_raw(x), None), lambda _, g: (_segsum_raw(g),))


def _f_pre(out_dtype):
    def fn(pos, x, g):
        return (_rms(x, g).astype(out_dtype),)
    return fn


def _f_post(pos, x, t, g):
    return (x + _rms(t, g),)


def _f_post_bias(pos, x, t, g, bias):
    return (x + _rms(t + bias, g),)


def _f_conv(pos, proj, p1, p2, p3, b_u, w0, w1, w2, w3, cb):
    D = b_u.shape[1]
    u0 = proj[:, D:] + b_u
    u1 = jnp.where(pos >= 1, p1[:, D:] + b_u, 0.0)
    u2 = jnp.where(pos >= 2, p2[:, D:] + b_u, 0.0)
    u3 = jnp.where(pos >= 3, p3[:, D:] + b_u, 0.0)
    return (cb + u3 * w0 + u2 * w1 + u1 * w2 + u0 * w3,)


def _f_conv_bwd(pos, dconv, proj, dyb, n1, n2, n3, p1, p2, p3, b_u, w0, w1, w2, w3):
    D = b_u.shape[1]
    du = dconv * w3 + n1 * w2 + n2 * w1 + n3 * w0
    dproj = jnp.concatenate([dyb[:, :D], du], axis=1)
    u0 = proj[:, D:] + b_u
    u1 = jnp.where(pos >= 1, p1[:, D:] + b_u, 0.0)
    u2 = jnp.where(pos >= 2, p2[:, D:] + b_u, 0.0)
    u3 = jnp.where(pos >= 3, p3[:, D:] + b_u, 0.0)
    rs = lambda z: jnp.sum(z, axis=0, keepdims=True)
    return (dproj, rs(dproj), rs(dconv * u3), rs(dconv * u2), rs(dconv * u1), rs(dconv * u0), rs(dconv))


def _f_gates(pos, conv, proj, gate_w, gb_r, gb_i, lam, b_y):
    D = conv.shape[1]
    blk = D // LRU_HEADS
    cb = conv.astype(BF16)

    def gate(g, bias):
        z = [jnp.dot(cb[:, h * blk:(h + 1) * blk], gate_w[(g * LRU_HEADS + h) * blk:(g * LRU_HEADS + h + 1) * blk, :].astype(BF16),
                     preferred_element_type=F32) for h in range(LRU_HEADS)]
        return jax.nn.sigmoid(jnp.concatenate(z, axis=1) + bias)

    r_gate, i_gate = gate(0, gb_r), gate(1, gb_i)
    log_a = -LRU_C * r_gate * _softplus(-lam)
    a = jnp.exp(log_a)
    b = jnp.sqrt(_neg_expm1(2.0 * log_a)) * i_gate * conv
    yb = _gelu_tanh(proj[:, :D] + b_y)
    return a, b, yb


def _f_hy(pos, h, yb):
    return ((h * yb).astype(BF16),)


def _f_mix(pos, hn, hp, mu_r, mu_w, mu_k, mu_v, mu_a, mu_g, w0, w1, w2, a0, a1, a2, g1, g2):
    xx = hp - hn
    xr, xw, xk, xv, xa, xg = (hn + xx * m for m in (mu_r, mu_w, mu_k, mu_v, mu_a, mu_g))
    lw = w0 + _bdot(jnp.tanh(_bdot(xw, w1)), w2)
    decay = jnp.exp(-jnp.exp(-_softplus(-lw) - 0.5))
    a = jax.nn.sigmoid(a0 + _bdot(_bdot(xa, a1), a2))
    g = _bdot(jax.nn.sigmoid(_bdot(xg, g1)), g2)
    return xr.astype(BF16), xk.astype(BF16), xv.astype(BF16), decay, a, g


def _f_kk(pos, k, a, k_k, k_a):
    kk = k * k_k
    kk = kk / jnp.maximum(jnp.sqrt(_segsum(kk * kk)), 1e-12)
    return k * (1.0 + (a - 1.0) * k_a), -kk, kk * a


def _f_gn(pos, y, r, k2, v, g, gn_g, gn_b, r_k):
    inv_n = 1.0 / RWKV_N
    yc = y - _segsum(y) * inv_n
    var = _segsum(yc * yc) * inv_n
    yn = yc * lax.rsqrt(var + GN_EPS) * gn_g + gn_b
    bonus = _segsum(r * k2 * r_k) * v
    return (((yn + bonus) * g).astype(BF16),)


def _f_attn(pos, q, k, v):
    D = q.shape[1]
    hd = D // XATTN_HEADS
    qb, kb, vb = q.astype(BF16), k.astype(BF16), v.astype(BF16)
    outs = []
    for h in range(XATTN_HEADS):
        sl = slice(h * hd, (h + 1) * hd)
        s = lax.dot_general(qb[:, sl], kb[:, sl], (((1,), (1,)), ((), ())), preferred_element_type=F32) * (hd ** -0.5)
        s = s - jnp.max(s, axis=-1, keepdims=True)
        e = jnp.exp(s)
        p = e / jnp.sum(e, axis=-1, keepdims=True)
        outs.append(jnp.dot(p.astype(BF16), vb[:, sl], preferred_element_type=F32))
    return (jnp.concatenate(outs, axis=1).astype(BF16),)


def _f_act(pos, up):
    return (jnp.square(jnp.maximum(up, 0.0)).astype(BF16),)


def _f_loss(pos, y, target):
    d = y - target
    inv = 1.0 / y.shape[1]
    part = 0.5 * inv * jnp.sum(jnp.sum(d * d, axis=1, keepdims=True), axis=0, keepdims=True)
    return d * inv, jnp.broadcast_to(part, (8, LANES))


def _full_from_blocks(blocks, shard_shape, ax):
    g = jnp.moveaxis(blocks.reshape((N_DEV,) + tuple(shard_shape)), 0, ax)
    return g.reshape(tuple(shard_shape[:ax]) + (N_DEV * shard_shape[ax],) + tuple(shard_shape[ax + 1:]))


def _blocks_from_full(full, shard_shape, ax):
    g = full.reshape(tuple(shard_shape[:ax]) + (N_DEV, shard_shape[ax]) + tuple(shard_shape[ax + 1:]))
    return jnp.moveaxis(g, ax, 0).reshape(N_DEV, -1)


def _numel(shape):
    n = 1
    for s in shape:
        n *= s
    return n


def _pack_flat(arrs, total):
    flat = jnp.concatenate([a.reshape(-1).astype(F32) for a in arrs])
    return jnp.pad(flat, (0, total - flat.shape[0]))


def _train_step(x3, mem3, target3, P, M_, V_):
    Bl, S, D = x3.shape
    T = Bl * S
    ML = mem3.shape[1]
    x = x3.reshape(T, D)
    mem = mem3.reshape(Bl * ML, D)
    target = target3.reshape(T, D)
    shp = {n: P[n].shape for n in WEIGHTS}

    n_big = sum(_numel(shp[n]) for n in BIG)
    n_small = sum(_numel(shp[n]) for n in SMALL)
    n_small_pad = -(-n_small // (8 * PACK_W)) * (8 * PACK_W)
    big_rows, small_rows = n_big // PACK_W, n_small_pad // PACK_W
    big_pack = _pack_flat([P[n] for n in BIG], n_big).reshape(big_rows, PACK_W).astype(BF16)
    small_pack = _pack_flat([P[n] for n in SMALL], n_small_pad).reshape(small_rows, PACK_W)
    big_all = _all_gather("gather_matrices", big_pack).reshape(N_DEV, n_big)
    small_all = _all_gather("gather_vectors", small_pack).reshape(N_DEV, n_small_pad)
    Wf = {}
    off = 0
    for n in BIG:
        ne = _numel(shp[n])
        Wf[n] = _full_from_blocks(big_all[:, off:off + ne], shp[n], SHARD_AXIS[n])
        off += ne
    off = 0
    for n in SMALL:
        ne = _numel(shp[n])
        Wf[n] = _full_from_blocks(small_all[:, off:off + ne], shp[n], SHARD_AXIS[n])
        off += ne
    for n in REPL:
        Wf[n] = P[n]

    row = lambda v: v.reshape(1, -1).astype(F32)
    gains = [[row(Wf['ln_gains'][i, j]) for j in range(6)] for i in range(2)]
    LW = 128

    def pad_cols(w):
        return jnp.pad(w, ((0, 0), (0, LW - w.shape[1]))).astype(F32)

    def pad_rows(w):
        return jnp.pad(w, ((0, LW - w.shape[0]), (0, 0))).astype(F32)

    G = {}
    saved = {}

    mem_n, = _rw("mem_norm_fwd", _f_pre(BF16), [mem], [row(Wf['mem_norm'])], outs=[(D, BF16)])

    def xattn_fwd(i, xin):
        hq, = _rw(f"xattn{i}_pre", _f_pre(BF16), [xin], [gains[i][2]], outs=[(D, BF16)], seq=S)
        q = _mm_nn(f"xattn{i}_q", hq, Wf['c_w_q'][i], BF16)
        kv = _mm_nn(f"xattn{i}_kv", mem_n, Wf['c_w_kv'][i], F32)
        kb = kv[:, :D].reshape(Bl, ML, D)
        vb = kv[:, D:].reshape(Bl, ML, D)
        o, = _rw(f"xattn{i}_attn", _f_attn, [q], [kb, vb], outs=[(D, BF16)], seq=S, per_b=(0, 1))
        c = _mm_nn(f"xattn{i}_o", o, Wf['c_w_o'][i], F32)
        xo, = _rw(f"xattn{i}_post", _f_post, [xin, c], [gains[i][3]], outs=[(D, F32)], seq=S)
        saved[f"xattn{i}"] = (xin, hq, q, kb, vb, o, c)
        return xo

    def mlp_fwd(i, xin):
        hm, = _rw(f"mlp{i}_pre", _f_pre(BF16), [xin], [gains[i][4]], outs=[(D, BF16)], seq=S)
        up = _mm_nn(f"mlp{i}_up", hm, Wf['m_w_up'][i], F32)
        act, = _rw(f"mlp{i}_act", _f_act, [up], outs=[(up.shape[1], BF16)], seq=S)
        m = _mm_nn(f"mlp{i}_down", act, Wf['m_w_down'][i], F32)
        xo, = _rw(f"mlp{i}_post", _f_post, [xin, m], [gains[i][5]], outs=[(D, F32)], seq=S)
        saved[f"mlp{i}"] = (xin, hm, up, act, m)
        return xo

    b_in = row(Wf['a_b_in'][0])
    b_y, b_u = b_in[:, :D], b_in[:, D:]
    cw = [row(Wf['a_conv_w'][0, t]) for t in range(4)]
    cb = row(Wf['a_conv_b'][0])
    gate_w = Wf['a_gate_w'][0].reshape(2 * LRU_HEADS * (D // LRU_HEADS), D // LRU_HEADS).astype(F32)
    gate_b = Wf['a_gate_b'][0].reshape(2, D).astype(F32)
    gb_r, gb_i = gate_b[0:1], gate_b[1:2]
    lam = row(Wf['a_lambda'][0])
    b_out = row(Wf['a_b_out'][0])

    hn0, = _rw("lru_pre", _f_pre(BF16), [x], [gains[0][0]], outs=[(D, BF16)], seq=S)
    proj = _mm_nn("lru_in", hn0, Wf['a_w_in'][0], F32)
    prev3 = [(0, 1), (0, 2), (0, 3)]
    conv, = _rw("lru_conv", _f_conv, [proj], [b_u] + cw + [cb], outs=[(D, F32)], shifts=prev3, seq=S)
    gate_par = [gate_w, gb_r, gb_i, lam, b_y]
    a_l, b_l, yb = _rw("lru_gates", _f_gates, [conv, proj], gate_par, outs=[(D, F32)] * 3, seq=S)
    h_l = _lru_scan("lru_scan", a_l, b_l, S)
    hy, = _rw("lru_hy", _f_hy, [h_l, yb], outs=[(D, BF16)], seq=S)
    t0 = _mm_nn("lru_out", hy, Wf['a_w_out'][0], F32)
    x1, = _rw("lru_post", _f_post_bias, [x, t0], [gains[0][1], b_out], outs=[(D, F32)], seq=S)
    x2 = xattn_fwd(0, x1)
    x3_ = mlp_fwd(0, x2)

    mu = [row(Wf['b_mu'][0, j]) for j in range(6)]
    lora = [row(Wf['b_w0'][0]), pad_cols(Wf['b_w1'][0]), pad_rows(Wf['b_w2'][0]),
            row(Wf['b_a0'][0]), pad_cols(Wf['b_a1'][0]), pad_rows(Wf['b_a2'][0]),
            Wf['b_g1'][0].astype(F32), Wf['b_g2'][0].astype(F32)]
    k_k, k_a = row(Wf['b_k_k'][0]), row(Wf['b_k_a'][0])
    gn_g, gn_b, r_k = row(Wf['b_gn_g'][0]), row(Wf['b_gn_b'][0]), row(Wf['b_r_k'][0])

    hn1, = _rw("rwkv_pre", _f_pre(F32), [x3_], [gains[1][0]], outs=[(D, F32)], seq=S)
    xr, xk, xv, decay, a_r, g_r = _rw("rwkv_mix", _f_mix, [hn1], mu + lora,
                                      outs=[(D, BF16)] * 3 + [(D, F32)] * 3, shifts=[(0, 1)], seq=S)
    r_ = _mm_nn("rwkv_r", xr, Wf['b_w_rkv'][0, 0], F32)
    k_ = _mm_nn("rwkv_k", xk, Wf['b_w_rkv'][0, 1], F32)
    v_ = _mm_nn("rwkv_v", xv, Wf['b_w_rkv'][0, 2], F32)
    k2, ra, rb = _rw("rwkv_kk", _f_kk, [k_, a_r], [k_k, k_a], outs=[(D, F32)] * 3, seq=S)
    y_r, states, sa_r, last_r = _rwkv_fwd("rwkv_scan", r_, decay, k2, v_, ra, rb, S)
    gn_par = [gn_g, gn_b, r_k]
    og, = _rw("rwkv_gn", _f_gn, [y_r, r_, k2, v_, g_r], gn_par, outs=[(D, BF16)], seq=S)
    t1 = _mm_nn("rwkv_out", og, Wf['b_w_o'][0], F32)
    x4, = _rw("rwkv_post", _f_post, [x3_, t1], [gains[1][1]], outs=[(D, F32)], seq=S)
    x5 = xattn_fwd(1, x4)
    x6 = mlp_fwd(1, x5)

    dx, loss_acc = _rw("loss", _f_loss, [x6, target], outs=[(D, F32)], accs=[(8, LANES)], seq=S)
    loss = lax.psum(loss_acc[0, 0], ("x", "y", "c"))

    dgain = [[None] * 6 for _ in range(2)]
    dmem_n = None

    def mlp_bwd(i, dxo):
        xin, hm, up, act, m = saved[f"mlp{i}"]
        dm, dgain[i][5] = _rw_vjp(f"mlp{i}_post_bwd", _f_post, [xin, m], [gains[i][5]], [dxo], 1,
                                  d_rows=[(1, BF16)], d_params=[0], seq=S)
        G_down = _mm_tn(f"mlp{i}_down_dw", act, dm)
        dact = _mm_nt(f"mlp{i}_down_dx", dm, Wf['m_w_down'][i], F32)
        dup, = _rw_vjp(f"mlp{i}_act_bwd", _f_act, [up], [], [dact], 1, d_rows=[(0, BF16)], seq=S)
        G_up = _mm_tn(f"mlp{i}_up_dw", hm, dup)
        dhm = _mm_nt(f"mlp{i}_up_dx", dup, Wf['m_w_up'][i], F32)
        dxi, dgain[i][4] = _rw_vjp(f"mlp{i}_pre_bwd", _f_pre(F32), [xin], [gains[i][4]], [dhm], 1,
                                   d_rows=[(0, F32)], d_params=[0], adds=[(0, dxo, 0)], seq=S)
        return dxi, G_up, G_down

    def xattn_bwd(i, dxo):
        xin, hq, q, kb, vb, o, c = saved[f"xattn{i}"]
        dc, dgain[i][3] = _rw_vjp(f"xattn{i}_post_bwd", _f_post, [xin, c], [gains[i][3]], [dxo], 1,
                                  d_rows=[(1, BF16)], d_params=[0], seq=S)
        G_o = _mm_tn(f"xattn{i}_o_dw", o, dc)
        do = _mm_nt(f"xattn{i}_o_dx", dc, Wf['c_w_o'][i], F32)
        dq, dkb, dvb = _rw_vjp(f"xattn{i}_attn_bwd", _f_attn, [q], [kb, vb], [do], 1, d_rows=[(0, BF16)],
                               d_params=[0, 1], seq=S, per_b=(0, 1))
        dkv = jnp.concatenate([dkb.reshape(Bl * ML, D), dvb.reshape(Bl * ML, D)], axis=1)
        G_kv = _mm_tn(f"xattn{i}_kv_dw", mem_n, dkv)
        dmn = _mm_nt(f"xattn{i}_kv_dx", dkv, Wf['c_w_kv'][i], F32)
        G_q = _mm_tn(f"xattn{i}_q_dw", hq, dq)
        dhq = _mm_nt(f"xattn{i}_q_dx", dq, Wf['c_w_q'][i], F32)
        dxi, dgain[i][2] = _rw_vjp(f"xattn{i}_pre_bwd", _f_pre(F32), [xin], [gains[i][2]], [dhq], 1,
                                   d_rows=[(0, F32)], d_params=[0], adds=[(0, dxo, 0)], seq=S)
        return dxi, G_q, G_kv, G_o, dmn

    G_up, G_down, G_q, G_kv, G_o = [None] * 2, [None] * 2, [None] * 2, [None] * 2, [None] * 2
    dx, G_up[1], G_down[1] = mlp_bwd(1, dx)
    dx, G_q[1], G_kv[1], G_o[1], dmn1 = xattn_bwd(1, dx)

    dt1, dgain[1][1] = _rw_vjp("rwkv_post_bwd", _f_post, [x3_, t1], [gains[1][1]], [dx], 1,
                               d_rows=[(1, BF16)], d_params=[0], seq=S)
    G['b_w_o'] = _mm_tn("rwkv_out_dw", og, dt1)[None]
    dog = _mm_nt("rwkv_out_dx", dt1, Wf['b_w_o'][0], F32)
    dy_r, dr_p, dk2_p, dv_p, dg_r, d_gn_g, d_gn_b, d_r_k = _rw_vjp(
        "rwkv_gn_bwd", _f_gn, [y_r, r_, k2, v_, g_r], gn_par, [dog], 1,
        d_rows=[(j, F32) for j in range(5)], d_params=[0, 1, 2], seq=S)
    dr_, ddecay, dk2, dv_, dra, drb = _rwkv_bwd("rwkv_scan_bwd", r_, decay, k2, v_, ra, rb, states, sa_r, last_r, dy_r,
                                                dr_p, dk2_p, dv_p, S)
    dk_, da_r, d_k_k, d_k_a = _rw_vjp("rwkv_kk_bwd", _f_kk, [k_, a_r], [k_k, k_a], [dk2, dra, drb], 3,
                                      d_rows=[(0, F32), (1, F32)], d_params=[0, 1], seq=S)
    G_rkv = [_mm_tn("rwkv_r_dw", xr, dr_), _mm_tn("rwkv_k_dw", xk, dk_), _mm_tn("rwkv_v_dw", xv, dv_)]
    G['b_w_rkv'] = jnp.stack(G_rkv)[None]
    dxr = _mm_nt("rwkv_r_dx", dr_, Wf['b_w_rkv'][0, 0], F32)
    dxk = _mm_nt("rwkv_k_dx", dk_, Wf['b_w_rkv'][0, 1], F32)
    dxv = _mm_nt("rwkv_v_dx", dv_, Wf['b_w_rkv'][0, 2], F32)
    mix_out = _rw_vjp("rwkv_mix_bwd", _f_mix, [hn1], mu + lora, [dxr, dxk, dxv, ddecay, da_r, dg_r], 6,
                      d_rows=[(0, F32)], d_shifts=[(0, F32)], d_params=list(range(14)), shifts=[(0, 1)], seq=S)
    dhn1, dhp1 = mix_out[0], mix_out[1]
    d_mu = mix_out[2:8]
    d_lora = mix_out[8:16]
    dx, dgain[1][0] = _rw_vjp("rwkv_pre_bwd", _f_pre(F32), [x3_], [gains[1][0]], [dhn1], 1, d_rows=[(0, F32)],
                              d_params=[0], adds=[(0, dx, 0)], ct_adds=[(0, dhp1, -1)], seq=S)
    G['b_mu'] = jnp.concatenate(d_mu, axis=0)[None]
    G['b_w0'], G['b_a0'] = d_lora[0], d_lora[3]
    G['b_w1'] = d_lora[1][:, :shp['b_w1'][2]][None]
    G['b_w2'] = d_lora[2][:shp['b_w2'][1]][None]
    G['b_a1'] = d_lora[4][:, :shp['b_a1'][2]][None]
    G['b_a2'] = d_lora[5][:shp['b_a2'][1]][None]
    G['b_g1'], G['b_g2'] = d_lora[6][None], d_lora[7][None]
    G['b_k_k'], G['b_k_a'], G['b_gn_g'], G['b_gn_b'] = d_k_k, d_k_a, d_gn_g, d_gn_b
    G['b_r_k'] = d_r_k.reshape(P['b_r_k'].shape)

    dx, G_up[0], G_down[0] = mlp_bwd(0, dx)
    dx, G_q[0], G_kv[0], G_o[0], dmn0 = xattn_bwd(0, dx)
    G['m_w_up'], G['m_w_down'] = jnp.stack(G_up), jnp.stack(G_down)
    G['c_w_q'], G['c_w_kv'], G['c_w_o'] = jnp.stack(G_q), jnp.stack(G_kv), jnp.stack(G_o)

    dt0, dgain[0][1], G['a_b_out'] = _rw_vjp("lru_post_bwd", _f_post_bias, [x, t0], [gains[0][1], b_out], [dx], 1,
                                             d_rows=[(1, BF16)], d_params=[0, 1], seq=S)
    G['a_w_out'] = _mm_tn("lru_out_dw", hy, dt0)[None]
    dhy = _mm_nt("lru_out_dx", dt0, Wf['a_w_out'][0], F32)
    dh_l, dyb = _rw_vjp("lru_hy_bwd", _f_hy, [h_l, yb], [], [dhy], 1, d_rows=[(0, F32), (1, F32)], seq=S)
    da_l, db_l = _lru_scan_bwd("lru_scan_bwd", a_l, h_l, dh_l, S)
    dconv, dproj_y, d_gate_w, d_gb_r, d_gb_i, G['a_lambda'] = _rw_vjp(
        "lru_gates_bwd", _f_gates, [conv, proj], gate_par, [da_l, db_l, dyb], 3,
        d_rows=[(0, F32), (1, F32)], d_params=[0, 1, 2, 3], seq=S)
    next3 = [(0, -1), (0, -2), (0, -3)]
    dproj, G['a_b_in'], dw0, dw1, dw2, dw3, G['a_conv_b'] = _rw(
        "lru_conv_bwd", _f_conv_bwd, [dconv, proj, dproj_y], [b_u] + cw, outs=[(2 * D, BF16)],
        accs=[(1, 2 * D)] + [(1, D)] * 5, shifts=next3 + [(1, 1), (1, 2), (1, 3)], seq=S)
    G['a_conv_w'] = jnp.concatenate([dw0, dw1, dw2, dw3], axis=0)[None]
    G['a_gate_w'] = d_gate_w.reshape(P['a_gate_w'].shape[:3] + (D // LRU_HEADS, D // LRU_HEADS))
    G['a_gate_b'] = jnp.concatenate([d_gb_r, d_gb_i], axis=0).reshape(1, 2, LRU_HEADS, D // LRU_HEADS)
    G['a_w_in'] = _mm_tn("lru_in_dw", hn0, dproj)[None]
    dhn0 = _mm_nt("lru_in_dx", dproj, Wf['a_w_in'][0], F32)
    grad_x, dgain[0][0] = _rw_vjp("lru_pre_bwd", _f_pre(F32), [x], [gains[0][0]], [dhn0], 1, d_rows=[(0, F32)],
                                  d_params=[0], adds=[(0, dx, 0)], seq=S)
    d_mem_norm, = _rw_vjp("mem_norm_bwd", _f_pre(F32), [mem], [row(Wf['mem_norm'])], [dmn0], 1, d_params=[0],
                          ct_adds=[(0, dmn1, 0)])
    G['mem_norm'] = d_mem_norm.reshape(-1)
    G['ln_gains'] = jnp.stack([jnp.concatenate(dgain[i], axis=0) for i in range(2)])

    sharded = BIG + SMALL
    rows_s = big_rows + small_rows
    tr = max(t for t in range(SUB, 161, SUB) if rows_s % t == 0)

    def by_core_blocks(n):
        return jnp.swapaxes(_blocks_from_full(G[n], shp[n], SHARD_AXIS[n]).reshape(N_CHIP, 2, -1), 0, 1)

    pieces = [by_core_blocks(n) for n in sharded]
    pieces.append(jnp.zeros((2, N_CHIP, n_small_pad - n_small), F32))
    by_core = jnp.concatenate(pieces, axis=2).reshape(2, N_CHIP, rows_s, PACK_W)
    n_split = max(q for q in range(1, 10) if rows_s % (q * SUB) == 0)
    got = _swap_with_sibling("exchange_grads_sibling", by_core, n_split)
    parts_s = _exchange_chips("exchange_grads_chips", _add_own_blocks("sum_with_sibling", by_core, got, tr))
    n_repl = sum(_numel(shp[n]) for n in REPL)
    n_repl_pad = -(-n_repl // (8 * PACK_W)) * (8 * PACK_W)
    rows_r = n_repl_pad // PACK_W
    parts_r = _all_gather("gather_replicated_grads", _pack_flat([G[n] for n in REPL], n_repl_pad).reshape(rows_r, PACK_W))

    def pack(src, names, total, rows_):
        return _pack_flat([src[n] for n in names], total).reshape(rows_, PACK_W)

    out_s = _adamw("adamw_sharded", parts_s, pack(P, sharded, n_big + n_small_pad, rows_s),
                   pack(M_, sharded, n_big + n_small_pad, rows_s), pack(V_, sharded, n_big + n_small_pad, rows_s), tr)
    out_r = _adamw("adamw_replicated", parts_r, pack(P, REPL, n_repl_pad, rows_r), pack(M_, REPL, n_repl_pad, rows_r),
                   pack(V_, REPL, n_repl_pad, rows_r), rows_r)

    def unpack(packed, names):
        flat = packed.reshape(-1)
        res, o = {}, 0
        for n in names:
            ne = _numel(shp[n])
            res[n] = flat[o:o + ne].reshape(shp[n])
            o += ne
        return res

    results = []
    for j in range(4):
        both = {**unpack(out_s[j], sharded), **unpack(out_r[j], REPL)}
        results += [both[n] for n in WEIGHTS]
    return (loss, grad_x.reshape(Bl, S, D), *results)


def kernel(x, mem, ln_gains, mem_norm, a_conv_w, a_conv_b, a_w_in, a_b_in, a_gate_w, a_gate_b, a_lambda, a_w_out, a_b_out, b_mu, b_w_rkv, b_w0, b_w1, b_w2, b_a0, b_a1, b_a2, b_g1, b_g2, b_k_k, b_k_a, b_r_k, b_gn_g, b_gn_b, b_w_o, c_w_q, c_w_kv, c_w_o, m_w_up, m_w_down, loss_target, m_ln_gains, m_mem_norm, m_a_conv_w, m_a_conv_b, m_a_w_in, m_a_b_in, m_a_gate_w, m_a_gate_b, m_a_lambda, m_a_w_out, m_a_b_out, m_b_mu, m_b_w_rkv, m_b_w0, m_b_w1, m_b_w2, m_b_a0, m_b_a1, m_b_a2, m_b_g1, m_b_g2, m_b_k_k, m_b_k_a, m_b_r_k, m_b_gn_g, m_b_gn_b, m_b_w_o, m_c_w_q, m_c_w_kv, m_c_w_o, m_m_w_up, m_m_w_down, v_ln_gains, v_mem_norm, v_a_conv_w, v_a_conv_b, v_a_w_in, v_a_b_in, v_a_gate_w, v_a_gate_b, v_a_lambda, v_a_w_out, v_a_b_out, v_b_mu, v_b_w_rkv, v_b_w0, v_b_w1, v_b_w2, v_b_a0, v_b_a1, v_b_a2, v_b_g1, v_b_g2, v_b_k_k, v_b_k_a, v_b_r_k, v_b_gn_g, v_b_gn_b, v_b_w_o, v_c_w_q, v_c_w_kv, v_c_w_o, v_m_w_up, v_m_w_down):
    weights = (ln_gains, mem_norm, a_conv_w, a_conv_b, a_w_in, a_b_in, a_gate_w, a_gate_b, a_lambda, a_w_out, a_b_out, b_mu, b_w_rkv, b_w0, b_w1, b_w2, b_a0, b_a1, b_a2, b_g1, b_g2, b_k_k, b_k_a, b_r_k, b_gn_g, b_gn_b, b_w_o, c_w_q, c_w_kv, c_w_o, m_w_up, m_w_down)
    moments1 = (m_ln_gains, m_mem_norm, m_a_conv_w, m_a_conv_b, m_a_w_in, m_a_b_in, m_a_gate_w, m_a_gate_b, m_a_lambda, m_a_w_out, m_a_b_out, m_b_mu, m_b_w_rkv, m_b_w0, m_b_w1, m_b_w2, m_b_a0, m_b_a1, m_b_a2, m_b_g1, m_b_g2, m_b_k_k, m_b_k_a, m_b_r_k, m_b_gn_g, m_b_gn_b, m_b_w_o, m_c_w_q, m_c_w_kv, m_c_w_o, m_m_w_up, m_m_w_down)
    moments2 = (v_ln_gains, v_mem_norm, v_a_conv_w, v_a_conv_b, v_a_w_in, v_a_b_in, v_a_gate_w, v_a_gate_b, v_a_lambda, v_a_w_out, v_a_b_out, v_b_mu, v_b_w_rkv, v_b_w0, v_b_w1, v_b_w2, v_b_a0, v_b_a1, v_b_a2, v_b_g1, v_b_g2, v_b_k_k, v_b_k_a, v_b_r_k, v_b_gn_g, v_b_gn_b, v_b_w_o, v_c_w_q, v_c_w_kv, v_c_w_o, v_m_w_up, v_m_w_down)
    return _train_step(x, mem, loss_target, dict(zip(WEIGHTS, weights)), dict(zip(WEIGHTS, moments1)),
                       dict(zip(WEIGHTS, moments2)))
```

```python
import jax
import jax.numpy as jnp
from jax import lax
from jax.experimental import pallas as pl
from jax.experimental.pallas import tpu as pltpu

F32 = jnp.float32
BF16 = jnp.bfloat16
N_DEV = 8
MESH = pl.DeviceIdType.MESH

V7X_VMEM_LIMIT_BYTES = 56 * 1024 * 1024
SUB = 8
HALO_ROWS = SUB
LANES = 128
RWKV_N = 64
RMS_EPS = 1e-6
GN_EPS = 64e-5
LRU_C = 8.0
XATTN_HEADS = 4
LRU_HEADS = 4

ADAM_LR, ADAM_B1, ADAM_B2, ADAM_EPS, ADAM_WD, ADAM_STEP = 0.001, 0.9, 0.999, 1e-8, 0.01, 10

WEIGHTS = ['ln_gains', 'mem_norm', 'a_conv_w', 'a_conv_b', 'a_w_in', 'a_b_in', 'a_gate_w', 'a_gate_b',
           'a_lambda', 'a_w_out', 'a_b_out', 'b_mu', 'b_w_rkv', 'b_w0', 'b_w1', 'b_w2', 'b_a0', 'b_a1',
           'b_a2', 'b_g1', 'b_g2', 'b_k_k', 'b_k_a', 'b_r_k', 'b_gn_g', 'b_gn_b', 'b_w_o', 'c_w_q',
           'c_w_kv', 'c_w_o', 'm_w_up', 'm_w_down']
SHARD_AXIS = {'ln_gains': 2, 'mem_norm': None, 'a_conv_w': 2, 'a_conv_b': None, 'a_w_in': 2, 'a_b_in': None,
              'a_gate_w': 3, 'a_gate_b': 3, 'a_lambda': None, 'a_w_out': 1, 'a_b_out': None, 'b_mu': 2,
              'b_w_rkv': 2, 'b_w0': 1, 'b_w1': 1, 'b_w2': 2, 'b_a0': 1, 'b_a1': 1, 'b_a2': 2, 'b_g1': 1,
              'b_g2': 2, 'b_k_k': 1, 'b_k_a': 1, 'b_r_k': None, 'b_gn_g': 1, 'b_gn_b': 1, 'b_w_o': 1,
              'c_w_q': 1, 'c_w_kv': 2, 'c_w_o': 1, 'm_w_up': 2, 'm_w_down': 1}
BIG = ['a_w_in', 'a_gate_w', 'a_w_out', 'b_w_rkv', 'b_w1', 'b_w2', 'b_a1', 'b_a2', 'b_g1', 'b_g2', 'b_w_o',
       'c_w_q', 'c_w_kv', 'c_w_o', 'm_w_up', 'm_w_down']
SMALL = ['ln_gains', 'a_conv_w', 'a_gate_b', 'b_mu', 'b_w0', 'b_a0', 'b_k_k', 'b_k_a', 'b_gn_g', 'b_gn_b']
REPL = ['mem_norm', 'a_conv_b', 'a_b_in', 'a_lambda', 'a_b_out', 'b_r_k']
PACK_W = 1024


def _cparams(*sem):
    return pltpu.CompilerParams(dimension_semantics=sem, vmem_limit_bytes=V7X_VMEM_LIMIT_BYTES)


def _shift_tile(tile, halo, k, pos, seq, tm):
    if k > 0:
        ext = jnp.concatenate([halo, tile], axis=0)
        r = pltpu.roll(ext, k, 0)[HALO_ROWS:HALO_ROWS + tm]
        return jnp.where(pos >= k, r, 0.0)
    kk = -k
    ext = jnp.concatenate([tile, halo], axis=0)
    r = pltpu.roll(ext, tm + HALO_ROWS - kk, 0)[0:tm]
    return jnp.where(pos + kk < seq, r, 0.0)


def _row_specs(rows, shifts, params, per_b, seq, tm):
    T = rows[0].shape[0]
    ns = seq // tm
    specs = [pl.BlockSpec((tm, r.shape[1]), lambda b, i: (b * ns + i, 0)) for r in rows]
    halo_keys = []
    for idx, k in shifts:
        key = (idx, k > 0)
        if key not in halo_keys:
            halo_keys.append(key)
    per8 = tm // HALO_ROWS
    last8 = T // HALO_ROWS - 1
    for idx, prev in halo_keys:
        c = rows[idx].shape[1]
        if prev:
            specs.append(pl.BlockSpec((HALO_ROWS, c), lambda b, i: (jnp.maximum((b * ns + i) * per8 - 1, 0), 0)))
        else:
            specs.append(pl.BlockSpec((HALO_ROWS, c), lambda b, i: (jnp.minimum((b * ns + i + 1) * per8, last8), 0)))
    for j, p in enumerate(params):
        if j in per_b:
            specs.append(pl.BlockSpec((None,) + p.shape[1:], lambda b, i: (b, 0, 0)))
        else:
            specs.append(pl.BlockSpec(p.shape, lambda b, i, nd=p.ndim: (0,) * nd))
    return specs, halo_keys


def _load_tiles(refs, rows, shifts, halo_keys, params, seq, tm):
    nr, nh = len(rows), len(halo_keys)
    i = pl.program_id(1)
    pos = i * tm + lax.broadcasted_iota(jnp.int32, (tm, 1), 0)
    row_t = [r[...] for r in refs[:nr]]
    halo_t = {key: refs[nr + j][...] for j, key in enumerate(halo_keys)}
    sh_t = [_shift_tile(row_t[idx], halo_t[(idx, k > 0)], k, pos, seq, tm) for idx, k in shifts]
    par_t = [p[...] for p in refs[nr + nh:nr + nh + len(params)]]
    return pos, row_t, sh_t, par_t


ROW_TILES = (512, 256)
ROW_TILE_BUDGET_BYTES = 16 * 1024 * 1024


def _row_tile(seq, row_bytes):
    for tm in ROW_TILES:
        if 2 * tm * row_bytes <= ROW_TILE_BUDGET_BYTES:
            return min(tm, seq)
    return min(ROW_TILES[-1], seq)


def _rw(name, fn, rows, params=(), outs=(), accs=(), shifts=(), seq=None, per_b=()):
    rows, params = list(rows), list(params)
    T = rows[0].shape[0]
    seq = seq or T
    tm = _row_tile(seq, sum(r.shape[1] * r.dtype.itemsize for r in rows) + sum(c * jnp.dtype(dt).itemsize for c, dt in outs))
    nb, ns = T // seq, seq // tm
    in_specs, halo_keys = _row_specs(rows, shifts, params, per_b, seq, tm)
    n_in, no = len(in_specs), len(outs)

    def body(*refs):
        pos, row_t, sh_t, par_t = _load_tiles(refs, rows, shifts, halo_keys, params, seq, tm)
        res = fn(pos, *row_t, *sh_t, *par_t)
        out_refs = refs[n_in:n_in + no]
        acc_refs = refs[n_in + no:]
        for o_ref, v in zip(out_refs, res[:no]):
            o_ref[...] = v.astype(o_ref.dtype)
        if acc_refs:
            @pl.when((pl.program_id(0) == 0) & (pl.program_id(1) == 0))
            def _():
                for a_ref in acc_refs:
                    a_ref[...] = jnp.zeros_like(a_ref)
            for a_ref, v in zip(acc_refs, res[no:]):
                a_ref[...] += v

    out_shape = [jax.ShapeDtypeStruct((T, c), dt) for c, dt in outs]
    out_shape += [jax.ShapeDtypeStruct(s, F32) for s in accs]
    out_specs = [pl.BlockSpec((tm, c), lambda b, i: (b * ns + i, 0)) for c, _ in outs]
    out_specs += [pl.BlockSpec(s, lambda b, i: (0, 0)) for s in accs]
    return pl.pallas_call(body, name=name, grid=(nb, ns), in_specs=in_specs, out_specs=out_specs,
                          out_shape=out_shape, compiler_params=_cparams("arbitrary", "arbitrary"))(*rows, *[
                              rows[idx] for idx, _ in halo_keys], *params)


def _rw_vjp(name, fn, rows, params, cts, n_out, d_rows=(), d_shifts=(), d_params=(), adds=(), ct_adds=(),
            shifts=(), seq=None, per_b=()):
    rows, params, cts = list(rows), list(params), list(cts)
    T = rows[0].shape[0]
    seq = seq or T
    row_bytes = sum(r.shape[1] * r.dtype.itemsize for r in rows + cts + [a for _, a, _ in list(adds) + list(ct_adds)])
    row_bytes += sum(rows[j].shape[1] * jnp.dtype(dt).itemsize for j, dt in d_rows)
    row_bytes += sum(rows[shifts[j][0]].shape[1] * jnp.dtype(dt).itemsize for j, dt in d_shifts)
    tm = _row_tile(seq, row_bytes)
    nb, ns = T // seq, seq // tm
    n_ga = len(adds)
    adds = list(adds) + list(ct_adds)
    add_rows = [a for _, a, _ in adds]
    add_shifts = [(j, k) for j, (_, _, k) in enumerate(adds) if k != 0]
    all_rows = rows + cts + add_rows
    off_ct, off_add = len(rows), len(rows) + len(cts)
    all_shifts = list(shifts) + [(off_add + j, k) for j, k in add_shifts]
    in_specs, halo_keys = _row_specs(all_rows, all_shifts, params, per_b, seq, tm)
    n_in = len(in_specs)
    n_dr, n_ds = len(d_rows), len(d_shifts)

    def body(*refs):
        pos, row_t, sh_t, par_t = _load_tiles(refs, all_rows, all_shifts, halo_keys, params, seq, tm)
        fn_rows, ct_t, add_t = row_t[:off_ct], row_t[off_ct:off_add], row_t[off_add:]
        fn_sh, add_sh = sh_t[:len(shifts)], sh_t[len(shifts):]
        diff = [fn_rows[j] for j, _ in d_rows] + [fn_sh[j] for j, _ in d_shifts] + [par_t[j] for j in d_params]

        def f(*d):
            r2, s2, p2 = list(fn_rows), list(fn_sh), list(par_t)
            for (j, _), v in zip(d_rows, d[:n_dr]):
                r2[j] = v
            for (j, _), v in zip(d_shifts, d[n_dr:n_dr + n_ds]):
                s2[j] = v
            for j, v in zip(d_params, d[n_dr + n_ds:]):
                p2[j] = v
            return tuple(fn(pos, *r2, *s2, *p2)[:n_out])

        sh_iter = iter(add_sh)
        add_v = [add_t[j] if k == 0 else next(sh_iter) for j, (_, _, k) in enumerate(adds)]
        ct_t = list(ct_t)
        for (tgt, _, _), v in zip(adds[n_ga:], add_v[n_ga:]):
            ct_t[tgt] = ct_t[tgt] + v
        outs_v, vjp_fn = jax.vjp(f, *diff)
        grads = list(vjp_fn(tuple(c.astype(o.dtype) for c, o in zip(ct_t, outs_v))))
        for (tgt, _, _), v in zip(adds[:n_ga], add_v[:n_ga]):
            grads[tgt] = grads[tgt] + v
        out_refs = refs[n_in:]
        for o_ref, g in zip(out_refs[:n_dr + n_ds], grads[:n_dr + n_ds]):
            o_ref[...] = g.astype(o_ref.dtype)
        for o_ref, g, j in zip(out_refs[n_dr + n_ds:], grads[n_dr + n_ds:], d_params):
            first = (pl.program_id(1) == 0) if j in per_b else ((pl.program_id(0) == 0) & (pl.program_id(1) == 0))

            @pl.when(first)
            def _(o_ref=o_ref):
                o_ref[...] = jnp.zeros_like(o_ref)
            o_ref[...] += g.astype(F32)

    out_shape, out_specs = [], []
    for j, dt in d_rows:
        out_shape.append(jax.ShapeDtypeStruct(rows[j].shape, dt))
        out_specs.append(pl.BlockSpec((tm, rows[j].shape[1]), lambda b, i: (b * ns + i, 0)))
    for j, dt in d_shifts:
        src = rows[shifts[j][0]]
        out_shape.append(jax.ShapeDtypeStruct(src.shape, dt))
        out_specs.append(pl.BlockSpec((tm, src.shape[1]), lambda b, i: (b * ns + i, 0)))
    for j in d_params:
        p = params[j]
        out_shape.append(jax.ShapeDtypeStruct(p.shape, F32))
        if j in per_b:
            out_specs.append(pl.BlockSpec((None,) + p.shape[1:], lambda b, i: (b, 0, 0)))
        else:
            out_specs.append(pl.BlockSpec(p.shape, lambda b, i, nd=p.ndim: (0,) * nd))
    return pl.pallas_call(body, name=name, grid=(nb, ns), in_specs=in_specs, out_specs=out_specs,
                          out_shape=out_shape, compiler_params=_cparams("arbitrary", "arbitrary"))(*all_rows, *[
                              all_rows[idx] for idx, _ in halo_keys], *params)


MM_TILE = 1024
MM_DEEP = 2048


def _mm_nn(name, x, w, out_dtype):
    M, K = x.shape
    N = w.shape[1]
    tm, tn = min(MM_TILE if K <= MM_DEEP else MM_TILE // 2, M), min(MM_TILE, N)

    def body(x_ref, w_ref, o_ref):
        o_ref[...] = jnp.dot(x_ref[...].astype(BF16), w_ref[...].astype(BF16),
                             preferred_element_type=F32).astype(o_ref.dtype)

    return pl.pallas_call(body, name=name, grid=(M // tm, N // tn),
                          in_specs=[pl.BlockSpec((tm, K), lambda i, j: (i, 0)), pl.BlockSpec((K, tn), lambda i, j: (0, j))],
                          out_specs=pl.BlockSpec((tm, tn), lambda i, j: (i, j)),
                          out_shape=jax.ShapeDtypeStruct((M, N), out_dtype),
                          compiler_params=_cparams("parallel", "parallel"))(x, w)


def _mm_nt(name, dy, w, out_dtype):
    M, N = dy.shape
    K = w.shape[0]
    tm, tk = min(MM_TILE if N <= MM_DEEP else MM_TILE // 2, M), min(MM_TILE, K)

    def body(dy_ref, w_ref, o_ref):
        o_ref[...] = lax.dot_general(dy_ref[...].astype(BF16), w_ref[...].astype(BF16), (((1,), (1,)), ((), ())),
                                     preferred_element_type=F32).astype(o_ref.dtype)

    return pl.pallas_call(body, name=name, grid=(M // tm, K // tk),
                          in_specs=[pl.BlockSpec((tm, N), lambda i, j: (i, 0)), pl.BlockSpec((tk, N), lambda i, j: (j, 0))],
                          out_specs=pl.BlockSpec((tm, tk), lambda i, j: (i, j)),
                          out_shape=jax.ShapeDtypeStruct((M, K), out_dtype),
                          compiler_params=_cparams("parallel", "parallel"))(dy, w)


def _mm_tn(name, x, dy):
    M, K = x.shape
    N = dy.shape[1]
    tm, tk, tn = min(MM_TILE, M), min(MM_TILE, K), min(MM_TILE, N)

    def body(x_ref, dy_ref, o_ref):
        @pl.when(pl.program_id(2) == 0)
        def _():
            o_ref[...] = jnp.zeros_like(o_ref)
        o_ref[...] += lax.dot_general(x_ref[...].astype(BF16), dy_ref[...].astype(BF16), (((0,), (0,)), ((), ())),
                                      preferred_element_type=F32)

    return pl.pallas_call(body, name=name, grid=(K // tk, N // tn, M // tm),
                          in_specs=[pl.BlockSpec((tm, tk), lambda i, j, m: (m, i)), pl.BlockSpec((tm, tn), lambda i, j, m: (m, j))],
                          out_specs=pl.BlockSpec((tk, tn), lambda i, j, m: (i, j)),
                          out_shape=jax.ShapeDtypeStruct((K, N), F32),
                          compiler_params=_cparams("parallel", "parallel", "arbitrary"))(x, dy)


def _lru_scan(name, a, b, seq, ct=128):
    T, C = a.shape

    def body(a_ref, b_ref, h_ref):
        A, Bv = a_ref[...], b_ref[...]
        row = lax.broadcasted_iota(jnp.int32, (seq, 1), 0)
        k = 1
        while k < seq:
            keep = row >= k
            Bv = A * jnp.where(keep, pltpu.roll(Bv, k, 0), 0.0) + Bv
            A = A * jnp.where(keep, pltpu.roll(A, k, 0), 1.0)
            k *= 2
        h_ref[...] = Bv

    spec = pl.BlockSpec((seq, ct), lambda b_, j: (b_, j))
    return pl.pallas_call(body, name=name, grid=(T // seq, C // ct), in_specs=[spec, spec], out_specs=spec,
                          out_shape=jax.ShapeDtypeStruct((T, C), F32),
                          compiler_params=_cparams("parallel", "parallel"))(a, b)


def _lru_scan_bwd(name, a, h, dh, seq, ct=128):
    T, C = a.shape

    def body(a_ref, h_ref, dh_ref, da_ref, db_ref):
        row = lax.broadcasted_iota(jnp.int32, (seq, 1), 0)
        Cf = jnp.where(row < seq - 1, pltpu.roll(a_ref[...], seq - 1, 0), 0.0)
        G = dh_ref[...]
        k = 1
        while k < seq:
            keep = row + k < seq
            G = Cf * jnp.where(keep, pltpu.roll(G, seq - k, 0), 0.0) + G
            Cf = Cf * jnp.where(keep, pltpu.roll(Cf, seq - k, 0), 1.0)
            k *= 2
        db_ref[...] = G
        da_ref[...] = G * jnp.where(row >= 1, pltpu.roll(h_ref[...], 1, 0), 0.0)

    spec = pl.BlockSpec((seq, ct), lambda b_, j: (b_, j))
    return pl.pallas_call(body, name=name, grid=(T // seq, C // ct), in_specs=[spec] * 3, out_specs=[spec] * 2,
                          out_shape=[jax.ShapeDtypeStruct((T, C), F32)] * 2,
                          compiler_params=_cparams("parallel", "parallel"))(a, h, dh)


def _pair_consts():
    jj = lax.broadcasted_iota(jnp.int32, (RWKV_N, LANES), 0)
    ll = lax.broadcasted_iota(jnp.int32, (RWKV_N, LANES), 1)
    pick = ((ll & (RWKV_N - 1)) == jj).astype(BF16)
    l0 = lax.broadcasted_iota(jnp.int32, (LANES, LANES), 0)
    l1 = lax.broadcasted_iota(jnp.int32, (LANES, LANES), 1)
    same = ((l0 >> 6) == (l1 >> 6)).astype(BF16)
    return pick, same


N_COL = 5


def _cols_raw(rows, pick, same):
    lhs = [pick * jnp.broadcast_to(x, (RWKV_N, LANES)).astype(BF16) for x in rows]
    out = jnp.dot(jnp.concatenate(lhs, axis=0), same, preferred_element_type=F32)
    return tuple(out[i * RWKV_N:(i + 1) * RWKV_N] for i in range(len(rows)))


def _cols_adjoint(g):
    n = len(g)
    g = list(g) + [jnp.zeros_like(g[0])] * (n % 2)
    head = (lax.broadcasted_iota(jnp.int32, (SUB, LANES), 1) >> 6) == lax.broadcasted_iota(jnp.int32, (SUB, LANES), 0)
    sums = lax.dot_general(head.astype(BF16), jnp.concatenate([t.astype(BF16) for t in g], axis=0),
                           (((1,), (1,)), ((), ())), preferred_element_type=F32)
    low = lax.broadcasted_iota(jnp.int32, (1, LANES), 1) < RWKV_N
    rows = []
    for i in range(0, n, 2):
        blk = sums[:, i * RWKV_N:(i + 2) * RWKV_N]
        swapped = pltpu.roll(blk, RWKV_N, 1)
        rows.append(jnp.where(low, blk[0:1], swapped[1:2]))
        rows.append(jnp.where(low, swapped[0:1], blk[1:2]))
    return tuple(rows[:n])


def _rwkv_cols(rows, pick, same):
    flat = []
    for r, w, k, v, a, b in rows:
        flat += [a, 1.0 - w, b, k, r]
    tiles = _cols_raw(tuple(flat), pick, same)
    return [tiles[N_COL * p:N_COL * (p + 1)] for p in range(len(rows))]


def _rwkv_step(s0, tiles, v):
    ca, ce, cb, ck, cr = tiles
    sa = jnp.sum(s0 * ca, axis=0, keepdims=True)
    s1 = s0 - s0 * ce + cb * sa + ck * v
    return s1, jnp.sum(s1 * cr, axis=0, keepdims=True), sa


def _rwkv_step_bwd(s0, s1, sa, tiles, v, dy, ds1):
    ca, ce, cb, ck, cr = tiles
    ds1 = ds1 + cr * dy
    dsa = jnp.sum(ds1 * cb, axis=0, keepdims=True)
    dv = jnp.sum(ds1 * ck, axis=0, keepdims=True)
    ds0 = ds1 - ds1 * ce + ca * dsa
    return ds0, (s0 * dsa, ds1 * s0, ds1 * sa, ds1 * v, s1 * dy), dv


def _rwkv_specs(C, seq, tc, G, reverse):
    nc = seq // tc
    if reverse:
        row = pl.BlockSpec((tc, LANES * G), lambda b, g, c: (b * nc + nc - 1 - c, g))
        st = pl.BlockSpec((tc, G, RWKV_N, LANES), lambda b, g, c: (b * nc + nc - 1 - c, g, 0, 0))
    else:
        row = pl.BlockSpec((tc, LANES * G), lambda b, g, c: (b * nc + c, g))
        st = pl.BlockSpec((tc, G, RWKV_N, LANES), lambda b, g, c: (b * nc + c, g, 0, 0))
    return row, st


def _rwkv_fwd(name, r, w, k, v, a, b, seq, tc=32, G=8):
    T, C = r.shape
    tc = min(tc, seq)
    NP = C // LANES
    row, st = _rwkv_specs(C, seq, tc, G, False)
    last = pl.BlockSpec((None, G, RWKV_N, LANES), lambda b_, g, c: (b_, g, 0, 0))

    def body(r_ref, w_ref, k_ref, v_ref, a_ref, b_ref, y_ref, st_ref, sa_ref, last_ref, s_scr):
        @pl.when(pl.program_id(2) == 0)
        def _():
            s_scr[...] = jnp.zeros_like(s_scr)
        pick, same = _pair_consts()
        rid = lax.broadcasted_iota(jnp.int32, (SUB, LANES), 0)

        def group(t8, carry):
            base = pl.multiple_of(t8 * SUB, SUB)
            lanes = [pl.ds(LANES * p, LANES) for p in range(G)]
            blk = [[x[pl.ds(base, SUB), ln] for x in (r_ref, w_ref, k_ref, v_ref, a_ref, b_ref)] for ln in lanes]
            s = [s_scr[p] for p in range(G)]
            ys = [jnp.zeros((SUB, LANES), F32) for _ in range(G)]
            sas = [jnp.zeros((SUB, LANES), F32) for _ in range(G)]
            for i in range(SUB):
                tiles = _rwkv_cols([tuple(b_[i:i + 1] for b_ in blk[p]) for p in range(G)], pick, same)
                for p in range(G):
                    st_ref[base + i, p] = s[p]
                    s[p], y, sa = _rwkv_step(s[p], tiles[p], blk[p][3][i:i + 1])
                    ys[p] = jnp.where(rid == i, y, ys[p])
                    sas[p] = jnp.where(rid == i, sa, sas[p])
            for p in range(G):
                s_scr[p] = s[p]
                y_ref[pl.ds(base, SUB), lanes[p]] = ys[p]
                sa_ref[pl.ds(base, SUB), lanes[p]] = sas[p]
            return carry

        lax.fori_loop(0, tc // SUB, group, 0)

        @pl.when(pl.program_id(2) == seq // tc - 1)
        def _():
            last_ref[...] = s_scr[...]

    return pl.pallas_call(body, name=name, grid=(T // seq, NP // G, seq // tc), in_specs=[row] * 6,
                          out_specs=[row, st, row, last],
                          out_shape=[jax.ShapeDtypeStruct((T, C), F32), jax.ShapeDtypeStruct((T, NP, RWKV_N, LANES), F32),
                                     jax.ShapeDtypeStruct((T, C), F32),
                                     jax.ShapeDtypeStruct((T // seq, NP, RWKV_N, LANES), F32)],
                          scratch_shapes=[pltpu.VMEM((G, RWKV_N, LANES), F32)],
                          compiler_params=_cparams("parallel", "parallel", "arbitrary"))(r, w, k, v, a, b)


def _rwkv_bwd(name, r, w, k, v, a, b, states, sa, last, dy, dr_p, dk_p, dv_p, seq, tc=32, G=8):
    T, C = r.shape
    tc = min(tc, seq)
    NP = C // LANES
    row, st = _rwkv_specs(C, seq, tc, G, True)
    last_spec = pl.BlockSpec((None, G, RWKV_N, LANES), lambda b_, g, c: (b_, g, 0, 0))

    def body(r_ref, w_ref, k_ref, v_ref, a_ref, b_ref, st_ref, sa_ref, last_ref, dy_ref, drp_ref, dkp_ref, dvp_ref,
             dr_ref, dw_ref, dk_ref, dv_ref, da_ref, db_ref, ds_scr, s1_scr):
        @pl.when(pl.program_id(2) == 0)
        def _():
            ds_scr[...] = jnp.zeros_like(ds_scr)
            s1_scr[...] = last_ref[...]
        pick, same = _pair_consts()
        rid = lax.broadcasted_iota(jnp.int32, (SUB, LANES), 0)
        out_refs = (da_ref, dw_ref, db_ref, dk_ref, dr_ref)

        def group(n, carry):
            base = pl.multiple_of((tc // SUB - 1 - n) * SUB, SUB)
            lanes = [pl.ds(LANES * p, LANES) for p in range(G)]
            blk = [[x[pl.ds(base, SUB), ln] for x in (r_ref, w_ref, k_ref, v_ref, a_ref, b_ref)] for ln in lanes]
            dys = [dy_ref[pl.ds(base, SUB), ln] for ln in lanes]
            sas = [sa_ref[pl.ds(base, SUB), ln] for ln in lanes]
            ds = [ds_scr[p] for p in range(G)]
            s1 = [s1_scr[p] for p in range(G)]
            zero = jnp.zeros((SUB, LANES), F32)
            gs = [[zero, zero, zero, dkp_ref[pl.ds(base, SUB), ln], drp_ref[pl.ds(base, SUB), ln]] for ln in lanes]
            gv = [dvp_ref[pl.ds(base, SUB), ln] for ln in lanes]
            for i in reversed(range(SUB)):
                tiles = _rwkv_cols([tuple(b_[i:i + 1] for b_ in blk[p]) for p in range(G)], pick, same)
                adj = []
                for p in range(G):
                    s0 = st_ref[base + i, p]
                    ds[p], g_tiles, dv = _rwkv_step_bwd(s0, s1[p], sas[p][i:i + 1], tiles[p], blk[p][3][i:i + 1],
                                                        dys[p][i:i + 1], ds[p])
                    s1[p] = s0
                    adj += list(g_tiles)
                    gv[p] = gv[p] + jnp.where(rid == i, dv, 0.0)
                rows = _cols_adjoint(adj)
                for p in range(G):
                    for j in range(N_COL):
                        gs[p][j] = gs[p][j] + jnp.where(rid == i, rows[N_COL * p + j], 0.0)
            for p in range(G):
                ds_scr[p] = ds[p]
                s1_scr[p] = s1[p]
                dv_ref[pl.ds(base, SUB), lanes[p]] = gv[p]
                for j in range(N_COL):
                    out_refs[j][pl.ds(base, SUB), lanes[p]] = gs[p][j]
            return carry

        lax.fori_loop(0, tc // SUB, group, 0)

    return pl.pallas_call(body, name=name, grid=(T // seq, NP // G, seq // tc),
                          in_specs=[row] * 6 + [st, row, last_spec] + [row] * 4, out_specs=[row] * 6,
                          out_shape=[jax.ShapeDtypeStruct((T, C), F32)] * 6,
                          scratch_shapes=[pltpu.VMEM((G, RWKV_N, LANES), F32)] * 2,
                          compiler_params=_cparams("parallel", "parallel", "arbitrary"))(
                              r, w, k, v, a, b, states, sa, last, dy, dr_p, dk_p, dv_p)


def _all_gather(name, shard):
    def body(x_ref, o_ref, send_sems, recv_sems, local_sem):
        x, y, c = lax.axis_index("x"), lax.axis_index("y"), lax.axis_index("c")
        me, sibling = (x, y, c), (x, y, 1 - c)
        chips = [(1 - x, y), (x, 1 - y), (1 - x, 1 - y)]

        def copy(k, block, to, src=None):
            dst = o_ref.at[4 * block[0] + 2 * block[1] + block[2]]
            return pltpu.make_async_remote_copy(src_ref=dst if src is None else src, dst_ref=dst, send_sem=send_sems.at[k],
                                                recv_sem=recv_sems.at[k], device_id=to, device_id_type=MESH)

        mine = pltpu.make_async_copy(x_ref, o_ref.at[4 * x + 2 * y + c], local_sem)
        mine.start()
        first = [copy(0, me, sibling, src=x_ref)] + [copy(1 + j, me, (*chip, c), src=x_ref) for j, chip in enumerate(chips)]
        for cp in first:
            cp.start()
        passed = [copy(4 + j, (*chip, c), sibling) for j, chip in enumerate(chips)]
        for j, chip in enumerate(chips):
            copy(1 + j, (*chip, c), me).wait_recv()
            passed[j].start()
        copy(0, sibling, me).wait_recv()
        for j, chip in enumerate(chips):
            copy(4 + j, (*chip, 1 - c), me).wait_recv()
        for cp in first + passed:
            cp.wait_send()
        mine.wait()

    hbm = pl.BlockSpec(memory_space=pltpu.HBM)
    return pl.pallas_call(body, name=name, in_specs=[hbm], out_specs=hbm,
                          out_shape=jax.ShapeDtypeStruct((N_DEV,) + shard.shape, shard.dtype),
                          scratch_shapes=[pltpu.SemaphoreType.DMA((N_DEV - 1,)), pltpu.SemaphoreType.DMA((N_DEV - 1,)),
                                          pltpu.SemaphoreType.DMA])(shard)


N_CHIP = 4


def _swap_with_sibling(name, blocks, n_split):
    _, R, W = blocks.shape
    N = N_CHIP
    rq = R // n_split

    def body(x_ref, got_ref, send_sems, recv_sems):
        x, y, c = lax.axis_index("x"), lax.axis_index("y"), lax.axis_index("c")
        copies = []
        for j in range(N):
            for q in range(n_split):
                rows = pl.ds(q * rq, rq)
                cp = pltpu.make_async_remote_copy(src_ref=x_ref.at[2 * j + 1 - c, rows], dst_ref=got_ref.at[j, rows],
                                                  send_sem=send_sems.at[j * n_split + q], recv_sem=recv_sems.at[j * n_split + q],
                                                  device_id=(x, y, 1 - c), device_id_type=MESH)
                cp.start()
                copies.append(cp)
        for cp in copies:
            cp.wait()

    hbm = pl.BlockSpec(memory_space=pltpu.HBM)
    return pl.pallas_call(body, name=name, in_specs=[hbm], out_specs=hbm,
                          out_shape=jax.ShapeDtypeStruct((N, R, W), blocks.dtype),
                          scratch_shapes=[pltpu.SemaphoreType.DMA((N * n_split,)), pltpu.SemaphoreType.DMA((N * n_split,))])(blocks)


def _add_own_blocks(name, blocks, got, tr):
    N, R, W = got.shape

    def body(c_ref, a_ref, b_ref, o_ref):
        o_ref[...] = a_ref[...] + b_ref[...]

    grid_spec = pltpu.PrefetchScalarGridSpec(
        num_scalar_prefetch=1, grid=(N, R // tr),
        in_specs=[pl.BlockSpec((None, tr, W), lambda n, i, c: (2 * n + c[0], i, 0)),
                  pl.BlockSpec((None, tr, W), lambda n, i, c: (n, i, 0))],
        out_specs=pl.BlockSpec((None, tr, W), lambda n, i, c: (n, i, 0)))
    core = lax.axis_index("c").astype(jnp.int32).reshape(1)
    return pl.pallas_call(body, name=name, grid_spec=grid_spec, out_shape=jax.ShapeDtypeStruct(got.shape, got.dtype),
                          compiler_params=_cparams("parallel", "parallel"))(core, blocks, got)


def _exchange_chips(name, blocks):
    def body(z_ref, o_ref, send_sems, recv_sems, local_sem):
        x, y, c = lax.axis_index("x"), lax.axis_index("y"), lax.axis_index("c")
        chip = 2 * x + y
        mine = pltpu.make_async_copy(z_ref.at[chip], o_ref.at[chip], local_sem)
        mine.start()
        copies = []
        for k in range(1, N_CHIP):
            to = (x ^ (k >> 1), y ^ (k & 1), c)
            cp = pltpu.make_async_remote_copy(src_ref=z_ref.at[chip ^ k], dst_ref=o_ref.at[chip], send_sem=send_sems.at[k - 1],
                                              recv_sem=recv_sems.at[k - 1], device_id=to, device_id_type=MESH)
            cp.start()
            copies.append(cp)
        for k in range(1, N_CHIP):
            pltpu.make_async_remote_copy(src_ref=z_ref.at[chip], dst_ref=o_ref.at[chip ^ k], send_sem=send_sems.at[k - 1],
                                         recv_sem=recv_sems.at[k - 1], device_id=(x ^ (k >> 1), y ^ (k & 1), c),
                                         device_id_type=MESH).wait_recv()
        for cp in copies:
            cp.wait_send()
        mine.wait()

    hbm = pl.BlockSpec(memory_space=pltpu.HBM)
    return pl.pallas_call(body, name=name, in_specs=[hbm], out_specs=hbm,
                          out_shape=jax.ShapeDtypeStruct(blocks.shape, blocks.dtype),
                          scratch_shapes=[pltpu.SemaphoreType.DMA((N_CHIP - 1,)), pltpu.SemaphoreType.DMA((N_CHIP - 1,)),
                                          pltpu.SemaphoreType.DMA])(blocks)


def _adamw(name, parts, w, m, v, tr):
    R, W = w.shape
    n_parts = parts.shape[0]

    def body(p_ref, w_ref, m_ref, v_ref, g_ref, d_ref, nm_ref, nv_ref):
        g = p_ref[0]
        for s in range(1, n_parts):
            g = g + p_ref[s]
        nm = ADAM_B1 * m_ref[...] + (1.0 - ADAM_B1) * g
        nv = ADAM_B2 * v_ref[...] + (1.0 - ADAM_B2) * (g * g)
        m_hat = nm / (1.0 - ADAM_B1 ** ADAM_STEP)
        v_hat = nv / (1.0 - ADAM_B2 ** ADAM_STEP)
        g_ref[...] = g
        d_ref[...] = -ADAM_LR * (m_hat / (jnp.sqrt(v_hat) + ADAM_EPS) + ADAM_WD * w_ref[...])
        nm_ref[...] = nm
        nv_ref[...] = nv

    spec = pl.BlockSpec((tr, W), lambda i: (i, 0))
    return pl.pallas_call(body, name=name, grid=(R // tr,),
                          in_specs=[pl.BlockSpec((n_parts, tr, W), lambda i: (0, i, 0)), spec, spec, spec],
                          out_specs=[spec] * 4, out_shape=[jax.ShapeDtypeStruct((R, W), F32)] * 4,
                          compiler_params=_cparams("parallel"))(parts, w, m, v)


def _rms(x, g):
    return x * lax.rsqrt(jnp.mean(x * x, axis=-1, keepdims=True) + RMS_EPS) * g


def _softplus(z):
    return jnp.maximum(z, 0.0) + jnp.log(1.0 + jnp.exp(-jnp.abs(z)))


def _neg_expm1(z):
    series = -z * (1.0 + z * (0.5 + z * (1.0 / 6.0)))
    return jnp.where(z > -1e-3, series, 1.0 - jnp.exp(z))


def _gelu_tanh(x):
    return 0.5 * x * (1.0 + jnp.tanh(0.7978845608028654 * (x + 0.044715 * x * x * x)))


def _bdot(x, w):
    return jnp.dot(x.astype(BF16), w.astype(BF16), preferred_element_type=F32)


def _seg_consts(C):
    c0 = lax.broadcasted_iota(jnp.int32, (C, LANES), 0)
    h1 = lax.broadcasted_iota(jnp.int32, (C, LANES), 1)
    red = ((c0 >> 6) == h1).astype(BF16)
    h0 = lax.broadcasted_iota(jnp.int32, (LANES, C), 0)
    c1 = lax.broadcasted_iota(jnp.int32, (LANES, C), 1)
    exp = ((c1 >> 6) == h0).astype(BF16)
    return red, exp


def _split_dot(x, m):
    hi = x.astype(BF16)
    lo = (x - hi.astype(F32)).astype(BF16)
    return jnp.dot(hi, m, preferred_element_type=F32) + jnp.dot(lo, m, preferred_element_type=F32)


def _segsum_raw(x):
    red, exp = _seg_consts(x.shape[1])
    return _split_dot(_split_dot(x, red), exp)


@jax.custom_vjp
def _segsum(x):
    return _segsum_raw(x)


_segsum.defvjp(lambda x: (_segsum_raw(x), None), lambda _, g: (_segsum_raw(g),))


def _f_pre(out_dtype):
    def fn(pos, x, g):
        return (_rms(x, g).astype(out_dtype),)
    return fn


def _f_post(pos, x, t, g):
    return (x + _rms(t, g),)


def _f_post_bias(pos, x, t, g, bias):
    return (x + _rms(t + bias, g),)


def _f_conv(pos, proj, p1, p2, p3, b_u, w0, w1, w2, w3, cb):
    D = b_u.shape[1]
    u0 = proj[:, D:] + b_u
    u1 = jnp.where(pos >= 1, p1[:, D:] + b_u, 0.0)
    u2 = jnp.where(pos >= 2, p2[:, D:] + b_u, 0.0)
    u3 = jnp.where(pos >= 3, p3[:, D:] + b_u, 0.0)
    return (cb + u3 * w0 + u2 * w1 + u1 * w2 + u0 * w3,)


def _f_conv_bwd(pos, dconv, proj, dyb, n1, n2, n3, p1, p2, p3, b_u, w0, w1, w2, w3):
    D = b_u.shape[1]
    du = dconv * w3 + n1 * w2 + n2 * w1 + n3 * w0
    dproj = jnp.concatenate([dyb[:, :D], du], axis=1)
    u0 = proj[:, D:] + b_u
    u1 = jnp.where(pos >= 1, p1[:, D:] + b_u, 0.0)
    u2 = jnp.where(pos >= 2, p2[:, D:] + b_u, 0.0)
    u3 = jnp.where(pos >= 3, p3[:, D:] + b_u, 0.0)
    rs = lambda z: jnp.sum(z, axis=0, keepdims=True)
    return (dproj, rs(dproj), rs(dconv * u3), rs(dconv * u2), rs(dconv * u1), rs(dconv * u0), rs(dconv))


def _f_gates(pos, conv, proj, gate_w, gb_r, gb_i, lam, b_y):
    D = conv.shape[1]
    blk = D // LRU_HEADS
    cb = conv.astype(BF16)

    def gate(g, bias):
        z = [jnp.dot(cb[:, h * blk:(h + 1) * blk], gate_w[(g * LRU_HEADS + h) * blk:(g * LRU_HEADS + h + 1) * blk, :].astype(BF16),
                     preferred_element_type=F32) for h in range(LRU_HEADS)]
        return jax.nn.sigmoid(jnp.concatenate(z, axis=1) + bias)

    r_gate, i_gate = gate(0, gb_r), gate(1, gb_i)
    log_a = -LRU_C * r_gate * _softplus(-lam)
    a = jnp.exp(log_a)
    b = jnp.sqrt(_neg_expm1(2.0 * log_a)) * i_gate * conv
    yb = _gelu_tanh(proj[:, :D] + b_y)
    return a, b, yb


def _f_hy(pos, h, yb):
    return ((h * yb).astype(BF16),)


def _f_mix(pos, hn, hp, mu_r, mu_w, mu_k, mu_v, mu_a, mu_g, w0, w1, w2, a0, a1, a2, g1, g2):
    xx = hp - hn
    xr, xw, xk, xv, xa, xg = (hn + xx * m for m in (mu_r, mu_w, mu_k, mu_v, mu_a, mu_g))
    lw = w0 + _bdot(jnp.tanh(_bdot(xw, w1)), w2)
    decay = jnp.exp(-jnp.exp(-_softplus(-lw) - 0.5))
    a = jax.nn.sigmoid(a0 + _bdot(_bdot(xa, a1), a2))
    g = _bdot(jax.nn.sigmoid(_bdot(xg, g1)), g2)
    return xr.astype(BF16), xk.astype(BF16), xv.astype(BF16), decay, a, g


def _f_kk(pos, k, a, k_k, k_a):
    kk = k * k_k
    kk = kk / jnp.maximum(jnp.sqrt(_segsum(kk * kk)), 1e-12)
    return k * (1.0 + (a - 1.0) * k_a), -kk, kk * a


def _f_gn(pos, y, r, k2, v, g, gn_g, gn_b, r_k):
    inv_n = 1.0 / RWKV_N
    yc = y - _segsum(y) * inv_n
    var = _segsum(yc * yc) * inv_n
    yn = yc * lax.rsqrt(var + GN_EPS) * gn_g + gn_b
    bonus = _segsum(r * k2 * r_k) * v
    return (((yn + bonus) * g).astype(BF16),)


def _f_attn(pos, q, kv):
    D = q.shape[1]
    hd = D // XATTN_HEADS
    qb, kb, vb = q.astype(BF16), kv[:, :D].astype(BF16), kv[:, D:].astype(BF16)
    outs = []
    for h in range(XATTN_HEADS):
        sl = slice(h * hd, (h + 1) * hd)
        s = lax.dot_general(qb[:, sl], kb[:, sl], (((1,), (1,)), ((), ())), preferred_element_type=F32) * (hd ** -0.5)
        s = s - jnp.max(s, axis=-1, keepdims=True)
        e = jnp.exp(s)
        p = e / jnp.sum(e, axis=-1, keepdims=True)
        outs.append(jnp.dot(p.astype(BF16), vb[:, sl], preferred_element_type=F32))
    return (jnp.concatenate(outs, axis=1).astype(BF16),)


def _f_act(pos, up):
    return (jnp.square(jnp.maximum(up, 0.0)).astype(BF16),)


def _f_loss(pos, y, target):
    d = y - target
    inv = 1.0 / y.shape[1]
    part = 0.5 * inv * jnp.sum(jnp.sum(d * d, axis=1, keepdims=True), axis=0, keepdims=True)
    return d * inv, jnp.broadcast_to(part, (8, LANES))


def _full_from_blocks(blocks, shard_shape, ax):
    g = jnp.moveaxis(blocks.reshape((N_DEV,) + tuple(shard_shape)), 0, ax)
    return g.reshape(tuple(shard_shape[:ax]) + (N_DEV * shard_shape[ax],) + tuple(shard_shape[ax + 1:]))


def _blocks_from_full(full, shard_shape, ax):
    g = full.reshape(tuple(shard_shape[:ax]) + (N_DEV, shard_shape[ax]) + tuple(shard_shape[ax + 1:]))
    return jnp.moveaxis(g, ax, 0).reshape(N_DEV, -1)


def _numel(shape):
    n = 1
    for s in shape:
        n *= s
    return n


def _pack_flat(arrs, total):
    flat = jnp.concatenate([a.reshape(-1).astype(F32) for a in arrs])
    return jnp.pad(flat, (0, total - flat.shape[0]))


def _train_step(x3, mem3, target3, P, M_, V_):
    Bl, S, D = x3.shape
    T = Bl * S
    ML = mem3.shape[1]
    x = x3.reshape(T, D)
    mem = mem3.reshape(Bl * ML, D)
    target = target3.reshape(T, D)
    shp = {n: P[n].shape for n in WEIGHTS}

    n_big = sum(_numel(shp[n]) for n in BIG)
    n_small = sum(_numel(shp[n]) for n in SMALL)
    n_small_pad = -(-n_small // (8 * PACK_W)) * (8 * PACK_W)
    big_rows, small_rows = n_big // PACK_W, n_small_pad // PACK_W
    big_pack = _pack_flat([P[n] for n in BIG], n_big).reshape(big_rows, PACK_W).astype(BF16)
    small_pack = _pack_flat([P[n] for n in SMALL], n_small_pad).reshape(small_rows, PACK_W)
    big_all = _all_gather("gather_matrices", big_pack).reshape(N_DEV, n_big)
    small_all = _all_gather("gather_vectors", small_pack).reshape(N_DEV, n_small_pad)
    Wf = {}
    off = 0
    for n in BIG:
        ne = _numel(shp[n])
        Wf[n] = _full_from_blocks(big_all[:, off:off + ne], shp[n], SHARD_AXIS[n])
        off += ne
    off = 0
    for n in SMALL:
        ne = _numel(shp[n])
        Wf[n] = _full_from_blocks(small_all[:, off:off + ne], shp[n], SHARD_AXIS[n])
        off += ne
    for n in REPL:
        Wf[n] = P[n]

    row = lambda v: v.reshape(1, -1).astype(F32)
    gains = [[row(Wf['ln_gains'][i, j]) for j in range(6)] for i in range(2)]
    LW = 128

    def pad_cols(w):
        return jnp.pad(w, ((0, 0), (0, LW - w.shape[1]))).astype(F32)

    def pad_rows(w):
        return jnp.pad(w, ((0, LW - w.shape[0]), (0, 0))).astype(F32)

    G = {}
    saved = {}

    mem_n, = _rw("mem_norm_fwd", _f_pre(BF16), [mem], [row(Wf['mem_norm'])], outs=[(D, BF16)])

    def xattn_fwd(i, xin):
        hq, = _rw(f"xattn{i}_pre", _f_pre(BF16), [xin], [gains[i][2]], outs=[(D, BF16)], seq=S)
        q = _mm_nn(f"xattn{i}_q", hq, Wf['c_w_q'][i], BF16)
        kv = _mm_nn(f"xattn{i}_kv", mem_n, Wf['c_w_kv'][i], F32)
        kv = kv.reshape(Bl, ML, 2 * D)
        o, = _rw(f"xattn{i}_attn", _f_attn, [q], [kv], outs=[(D, BF16)], seq=S, per_b=(0,))
        c = _mm_nn(f"xattn{i}_o", o, Wf['c_w_o'][i], F32)
        xo, = _rw(f"xattn{i}_post", _f_post, [xin, c], [gains[i][3]], outs=[(D, F32)], seq=S)
        saved[f"xattn{i}"] = (xin, hq, q, kv, o, c)
        return xo

    def mlp_fwd(i, xin):
        hm, = _rw(f"mlp{i}_pre", _f_pre(BF16), [xin], [gains[i][4]], outs=[(D, BF16)], seq=S)
        up = _mm_nn(f"mlp{i}_up", hm, Wf['m_w_up'][i], F32)
        act, = _rw(f"mlp{i}_act", _f_act, [up], outs=[(up.shape[1], BF16)], seq=S)
        m = _mm_nn(f"mlp{i}_down", act, Wf['m_w_down'][i], F32)
        xo, = _rw(f"mlp{i}_post", _f_post, [xin, m], [gains[i][5]], outs=[(D, F32)], seq=S)
        saved[f"mlp{i}"] = (xin, hm, up, act, m)
        return xo

    b_in = row(Wf['a_b_in'][0])
    b_y, b_u = b_in[:, :D], b_in[:, D:]
    cw = [row(Wf['a_conv_w'][0, t]) for t in range(4)]
    cb = row(Wf['a_conv_b'][0])
    gate_w = Wf['a_gate_w'][0].reshape(2 * LRU_HEADS * (D // LRU_HEADS), D // LRU_HEADS).astype(F32)
    gate_b = Wf['a_gate_b'][0].reshape(2, D).astype(F32)
    gb_r, gb_i = gate_b[0:1], gate_b[1:2]
    lam = row(Wf['a_lambda'][0])
    b_out = row(Wf['a_b_out'][0])

    hn0, = _rw("lru_pre", _f_pre(BF16), [x], [gains[0][0]], outs=[(D, BF16)], seq=S)
    proj = _mm_nn("lru_in", hn0, Wf['a_w_in'][0], F32)
    prev3 = [(0, 1), (0, 2), (0, 3)]
    conv, = _rw("lru_conv", _f_conv, [proj], [b_u] + cw + [cb], outs=[(D, F32)], shifts=prev3, seq=S)
    gate_par = [gate_w, gb_r, gb_i, lam, b_y]
    a_l, b_l, yb = _rw("lru_gates", _f_gates, [conv, proj], gate_par, outs=[(D, F32)] * 3, seq=S)
    h_l = _lru_scan("lru_scan", a_l, b_l, S)
    hy, = _rw("lru_hy", _f_hy, [h_l, yb], outs=[(D, BF16)], seq=S)
    t0 = _mm_nn("lru_out", hy, Wf['a_w_out'][0], F32)
    x1, = _rw("lru_post", _f_post_bias, [x, t0], [gains[0][1], b_out], outs=[(D, F32)], seq=S)
    x2 = xattn_fwd(0, x1)
    x3_ = mlp_fwd(0, x2)

    mu = [row(Wf['b_mu'][0, j]) for j in range(6)]
    lora = [row(Wf['b_w0'][0]), pad_cols(Wf['b_w1'][0]), pad_rows(Wf['b_w2'][0]),
            row(Wf['b_a0'][0]), pad_cols(Wf['b_a1'][0]), pad_rows(Wf['b_a2'][0]),
            Wf['b_g1'][0].astype(F32), Wf['b_g2'][0].astype(F32)]
    k_k, k_a = row(Wf['b_k_k'][0]), row(Wf['b_k_a'][0])
    gn_g, gn_b, r_k = row(Wf['b_gn_g'][0]), row(Wf['b_gn_b'][0]), row(Wf['b_r_k'][0])

    hn1, = _rw("rwkv_pre", _f_pre(F32), [x3_], [gains[1][0]], outs=[(D, F32)], seq=S)
    xr, xk, xv, decay, a_r, g_r = _rw("rwkv_mix", _f_mix, [hn1], mu + lora,
                                      outs=[(D, BF16)] * 3 + [(D, F32)] * 3, shifts=[(0, 1)], seq=S)
    r_ = _mm_nn("rwkv_r", xr, Wf['b_w_rkv'][0, 0], F32)
    k_ = _mm_nn("rwkv_k", xk, Wf['b_w_rkv'][0, 1], F32)
    v_ = _mm_nn("rwkv_v", xv, Wf['b_w_rkv'][0, 2], F32)
    k2, ra, rb = _rw("rwkv_kk", _f_kk, [k_, a_r], [k_k, k_a], outs=[(D, F32)] * 3, seq=S)
    y_r, states, sa_r, last_r = _rwkv_fwd("rwkv_scan", r_, decay, k2, v_, ra, rb, S)
    gn_par = [gn_g, gn_b, r_k]
    og, = _rw("rwkv_gn", _f_gn, [y_r, r_, k2, v_, g_r], gn_par, outs=[(D, BF16)], seq=S)
    t1 = _mm_nn("rwkv_out", og, Wf['b_w_o'][0], F32)
    x4, = _rw("rwkv_post", _f_post, [x3_, t1], [gains[1][1]], outs=[(D, F32)], seq=S)
    x5 = xattn_fwd(1, x4)
    x6 = mlp_fwd(1, x5)

    dx, loss_acc = _rw("loss", _f_loss, [x6, target], outs=[(D, F32)], accs=[(8, LANES)], seq=S)
    loss = lax.psum(loss_acc[0, 0], ("x", "y", "c"))

    dgain = [[None] * 6 for _ in range(2)]
    dmem_n = None

    def mlp_bwd(i, dxo):
        xin, hm, up, act, m = saved[f"mlp{i}"]
        dm, dgain[i][5] = _rw_vjp(f"mlp{i}_post_bwd", _f_post, [xin, m], [gains[i][5]], [dxo], 1,
                                  d_rows=[(1, BF16)], d_params=[0], seq=S)
        G_down = _mm_tn(f"mlp{i}_down_dw", act, dm)
        dact = _mm_nt(f"mlp{i}_down_dx", dm, Wf['m_w_down'][i], F32)
        dup, = _rw_vjp(f"mlp{i}_act_bwd", _f_act, [up], [], [dact], 1, d_rows=[(0, BF16)], seq=S)
        G_up = _mm_tn(f"mlp{i}_up_dw", hm, dup)
        dhm = _mm_nt(f"mlp{i}_up_dx", dup, Wf['m_w_up'][i], F32)
        dxi, dgain[i][4] = _rw_vjp(f"mlp{i}_pre_bwd", _f_pre(F32), [xin], [gains[i][4]], [dhm], 1,
                                   d_rows=[(0, F32)], d_params=[0], adds=[(0, dxo, 0)], seq=S)
        return dxi, G_up, G_down

    def xattn_bwd(i, dxo):
        xin, hq, q, kv, o, c = saved[f"xattn{i}"]
        dc, dgain[i][3] = _rw_vjp(f"xattn{i}_post_bwd", _f_post, [xin, c], [gains[i][3]], [dxo], 1,
                                  d_rows=[(1, BF16)], d_params=[0], seq=S)
        G_o = _mm_tn(f"xattn{i}_o_dw", o, dc)
        do = _mm_nt(f"xattn{i}_o_dx", dc, Wf['c_w_o'][i], F32)
        dq, dkv = _rw_vjp(f"xattn{i}_attn_bwd", _f_attn, [q], [kv], [do], 1, d_rows=[(0, BF16)],
                          d_params=[0], seq=S, per_b=(0,))
        dkv = dkv.reshape(Bl * ML, 2 * D)
        G_kv = _mm_tn(f"xattn{i}_kv_dw", mem_n, dkv)
        dmn = _mm_nt(f"xattn{i}_kv_dx", dkv, Wf['c_w_kv'][i], F32)
        G_q = _mm_tn(f"xattn{i}_q_dw", hq, dq)
        dhq = _mm_nt(f"xattn{i}_q_dx", dq, Wf['c_w_q'][i], F32)
        dxi, dgain[i][2] = _rw_vjp(f"xattn{i}_pre_bwd", _f_pre(F32), [xin], [gains[i][2]], [dhq], 1,
                                   d_rows=[(0, F32)], d_params=[0], adds=[(0, dxo, 0)], seq=S)
        return dxi, G_q, G_kv, G_o, dmn

    G_up, G_down, G_q, G_kv, G_o = [None] * 2, [None] * 2, [None] * 2, [None] * 2, [None] * 2
    dx, G_up[1], G_down[1] = mlp_bwd(1, dx)
    dx, G_q[1], G_kv[1], G_o[1], dmn1 = xattn_bwd(1, dx)

    dt1, dgain[1][1] = _rw_vjp("rwkv_post_bwd", _f_post, [x3_, t1], [gains[1][1]], [dx], 1,
                               d_rows=[(1, BF16)], d_params=[0], seq=S)
    G['b_w_o'] = _mm_tn("rwkv_out_dw", og, dt1)[None]
    dog = _mm_nt("rwkv_out_dx", dt1, Wf['b_w_o'][0], F32)
    dy_r, dr_p, dk2_p, dv_p, dg_r, d_gn_g, d_gn_b, d_r_k = _rw_vjp(
        "rwkv_gn_bwd", _f_gn, [y_r, r_, k2, v_, g_r], gn_par, [dog], 1,
        d_rows=[(j, F32) for j in range(5)], d_params=[0, 1, 2], seq=S)
    dr_, ddecay, dk2, dv_, dra, drb = _rwkv_bwd("rwkv_scan_bwd", r_, decay, k2, v_, ra, rb, states, sa_r, last_r, dy_r,
                                                dr_p, dk2_p, dv_p, S)
    dk_, da_r, d_k_k, d_k_a = _rw_vjp("rwkv_kk_bwd", _f_kk, [k_, a_r], [k_k, k_a], [dk2, dra, drb], 3,
                                      d_rows=[(0, F32), (1, F32)], d_params=[0, 1], seq=S)
    G_rkv = [_mm_tn("rwkv_r_dw", xr, dr_), _mm_tn("rwkv_k_dw", xk, dk_), _mm_tn("rwkv_v_dw", xv, dv_)]
    G['b_w_rkv'] = jnp.stack(G_rkv)[None]
    dxr = _mm_nt("rwkv_r_dx", dr_, Wf['b_w_rkv'][0, 0], F32)
    dxk = _mm_nt("rwkv_k_dx", dk_, Wf['b_w_rkv'][0, 1], F32)
    dxv = _mm_nt("rwkv_v_dx", dv_, Wf['b_w_rkv'][0, 2], F32)
    mix_out = _rw_vjp("rwkv_mix_bwd", _f_mix, [hn1], mu + lora, [dxr, dxk, dxv, ddecay, da_r, dg_r], 6,
                      d_rows=[(0, F32)], d_shifts=[(0, F32)], d_params=list(range(14)), shifts=[(0, 1)], seq=S)
    dhn1, dhp1 = mix_out[0], mix_out[1]
    d_mu = mix_out[2:8]
    d_lora = mix_out[8:16]
    dx, dgain[1][0] = _rw_vjp("rwkv_pre_bwd", _f_pre(F32), [x3_], [gains[1][0]], [dhn1], 1, d_rows=[(0, F32)],
                              d_params=[0], adds=[(0, dx, 0)], ct_adds=[(0, dhp1, -1)], seq=S)
    G['b_mu'] = jnp.concatenate(d_mu, axis=0)[None]
    G['b_w0'], G['b_a0'] = d_lora[0], d_lora[3]
    G['b_w1'] = d_lora[1][:, :shp['b_w1'][2]][None]
    G['b_w2'] = d_lora[2][:shp['b_w2'][1]][None]
    G['b_a1'] = d_lora[4][:, :shp['b_a1'][2]][None]
    G['b_a2'] = d_lora[5][:shp['b_a2'][1]][None]
    G['b_g1'], G['b_g2'] = d_lora[6][None], d_lora[7][None]
    G['b_k_k'], G['b_k_a'], G['b_gn_g'], G['b_gn_b'] = d_k_k, d_k_a, d_gn_g, d_gn_b
    G['b_r_k'] = d_r_k.reshape(P['b_r_k'].shape)

    dx, G_up[0], G_down[0] = mlp_bwd(0, dx)
    dx, G_q[0], G_kv[0], G_o[0], dmn0 = xattn_bwd(0, dx)
    G['m_w_up'], G['m_w_down'] = jnp.stack(G_up), jnp.stack(G_down)
    G['c_w_q'], G['c_w_kv'], G['c_w_o'] = jnp.stack(G_q), jnp.stack(G_kv), jnp.stack(G_o)

    dt0, dgain[0][1], G['a_b_out'] = _rw_vjp("lru_post_bwd", _f_post_bias, [x, t0], [gains[0][1], b_out], [dx], 1,
                                             d_rows=[(1, BF16)], d_params=[0, 1], seq=S)
    G['a_w_out'] = _mm_tn("lru_out_dw", hy, dt0)[None]
    dhy = _mm_nt("lru_out_dx", dt0, Wf['a_w_out'][0], F32)
    dh_l, dyb = _rw_vjp("lru_hy_bwd", _f_hy, [h_l, yb], [], [dhy], 1, d_rows=[(0, F32), (1, F32)], seq=S)
    da_l, db_l = _lru_scan_bwd("lru_scan_bwd", a_l, h_l, dh_l, S)
    dconv, dproj_y, d_gate_w, d_gb_r, d_gb_i, G['a_lambda'] = _rw_vjp(
        "lru_gates_bwd", _f_gates, [conv, proj], gate_par, [da_l, db_l, dyb], 3,
        d_rows=[(0, F32), (1, F32)], d_params=[0, 1, 2, 3], seq=S)
    next3 = [(0, -1), (0, -2), (0, -3)]
    dproj, G['a_b_in'], dw0, dw1, dw2, dw3, G['a_conv_b'] = _rw(
        "lru_conv_bwd", _f_conv_bwd, [dconv, proj, dproj_y], [b_u] + cw, outs=[(2 * D, BF16)],
        accs=[(1, 2 * D)] + [(1, D)] * 5, shifts=next3 + [(1, 1), (1, 2), (1, 3)], seq=S)
    G['a_conv_w'] = jnp.concatenate([dw0, dw1, dw2, dw3], axis=0)[None]
    G['a_gate_w'] = d_gate_w.reshape(P['a_gate_w'].shape[:3] + (D // LRU_HEADS, D // LRU_HEADS))
    G['a_gate_b'] = jnp.concatenate([d_gb_r, d_gb_i], axis=0).reshape(1, 2, LRU_HEADS, D // LRU_HEADS)
    G['a_w_in'] = _mm_tn("lru_in_dw", hn0, dproj)[None]
    dhn0 = _mm_nt("lru_in_dx", dproj, Wf['a_w_in'][0], F32)
    grad_x, dgain[0][0] = _rw_vjp("lru_pre_bwd", _f_pre(F32), [x], [gains[0][0]], [dhn0], 1, d_rows=[(0, F32)],
                                  d_params=[0], adds=[(0, dx, 0)], seq=S)
    d_mem_norm, = _rw_vjp("mem_norm_bwd", _f_pre(F32), [mem], [row(Wf['mem_norm'])], [dmn0], 1, d_params=[0],
                          ct_adds=[(0, dmn1, 0)])
    G['mem_norm'] = d_mem_norm.reshape(-1)
    G['ln_gains'] = jnp.stack([jnp.concatenate(dgain[i], axis=0) for i in range(2)])

    sharded = BIG + SMALL
    rows_s = big_rows + small_rows
    tr = max(t for t in range(SUB, 161, SUB) if rows_s % t == 0)

    pieces = [_blocks_from_full(G[n], shp[n], SHARD_AXIS[n]) for n in sharded]
    pieces.append(jnp.zeros((N_DEV, n_small_pad - n_small), F32))
    g_blocks = jnp.concatenate(pieces, axis=1).reshape(N_DEV, rows_s, PACK_W)
    n_split = max(q for q in range(1, 10) if rows_s % (q * SUB) == 0)
    got = _swap_with_sibling("exchange_grads_sibling", g_blocks, n_split)
    parts_s = _exchange_chips("exchange_grads_chips", _add_own_blocks("sum_with_sibling", g_blocks, got, tr))
    n_repl = sum(_numel(shp[n]) for n in REPL)
    n_repl_pad = -(-n_repl // (8 * PACK_W)) * (8 * PACK_W)
    rows_r = n_repl_pad // PACK_W
    parts_r = _all_gather("gather_replicated_grads", _pack_flat([G[n] for n in REPL], n_repl_pad).reshape(rows_r, PACK_W))

    def pack(src, names, total, rows_):
        return _pack_flat([src[n] for n in names], total).reshape(rows_, PACK_W)

    out_s = _adamw("adamw_sharded", parts_s, pack(P, sharded, n_big + n_small_pad, rows_s),
                   pack(M_, sharded, n_big + n_small_pad, rows_s), pack(V_, sharded, n_big + n_small_pad, rows_s), tr)
    out_r = _adamw("adamw_replicated", parts_r, pack(P, REPL, n_repl_pad, rows_r), pack(M_, REPL, n_repl_pad, rows_r),
                   pack(V_, REPL, n_repl_pad, rows_r), rows_r)

    def unpack(packed, names):
        flat = packed.reshape(-1)
        res, o = {}, 0
        for n in names:
            ne = _numel(shp[n])
            res[n] = flat[o:o + ne].reshape(shp[n])
            o += ne
        return res

    results = []
    for j in range(4):
        both = {**unpack(out_s[j], sharded), **unpack(out_r[j], REPL)}
        results += [both[n] for n in WEIGHTS]
    return (loss, grad_x.reshape(Bl, S, D), *results)


def kernel(x, mem, ln_gains, mem_norm, a_conv_w, a_conv_b, a_w_in, a_b_in, a_gate_w, a_gate_b, a_lambda, a_w_out, a_b_out, b_mu, b_w_rkv, b_w0, b_w1, b_w2, b_a0, b_a1, b_a2, b_g1, b_g2, b_k_k, b_k_a, b_r_k, b_gn_g, b_gn_b, b_w_o, c_w_q, c_w_kv, c_w_o, m_w_up, m_w_down, loss_target, m_ln_gains, m_mem_norm, m_a_conv_w, m_a_conv_b, m_a_w_in, m_a_b_in, m_a_gate_w, m_a_gate_b, m_a_lambda, m_a_w_out, m_a_b_out, m_b_mu, m_b_w_rkv, m_b_w0, m_b_w1, m_b_w2, m_b_a0, m_b_a1, m_b_a2, m_b_g1, m_b_g2, m_b_k_k, m_b_k_a, m_b_r_k, m_b_gn_g, m_b_gn_b, m_b_w_o, m_c_w_q, m_c_w_kv, m_c_w_o, m_m_w_up, m_m_w_down, v_ln_gains, v_mem_norm, v_a_conv_w, v_a_conv_b, v_a_w_in, v_a_b_in, v_a_gate_w, v_a_gate_b, v_a_lambda, v_a_w_out, v_a_b_out, v_b_mu, v_b_w_rkv, v_b_w0, v_b_w1, v_b_w2, v_b_a0, v_b_a1, v_b_a2, v_b_g1, v_b_g2, v_b_k_k, v_b_k_a, v_b_r_k, v_b_gn_g, v_b_gn_b, v_b_w_o, v_c_w_q, v_c_w_kv, v_c_w_o, v_m_w_up, v_m_w_down):
    weights = (ln_gains, mem_norm, a_conv_w, a_conv_b, a_w_in, a_b_in, a_gate_w, a_gate_b, a_lambda, a_w_out, a_b_out, b_mu, b_w_rkv, b_w0, b_w1, b_w2, b_a0, b_a1, b_a2, b_g1, b_g2, b_k_k, b_k_a, b_r_k, b_gn_g, b_gn_b, b_w_o, c_w_q, c_w_kv, c_w_o, m_w_up, m_w_down)
    moments1 = (m_ln_gains, m_mem_norm, m_a_conv_w, m_a_conv_b, m_a_w_in, m_a_b_in, m_a_gate_w, m_a_gate_b, m_a_lambda, m_a_w_out, m_a_b_out, m_b_mu, m_b_w_rkv, m_b_w0, m_b_w1, m_b_w2, m_b_a0, m_b_a1, m_b_a2, m_b_g1, m_b_g2, m_b_k_k, m_b_k_a, m_b_r_k, m_b_gn_g, m_b_gn_b, m_b_w_o, m_c_w_q, m_c_w_kv, m_c_w_o, m_m_w_up, m_m_w_down)
    moments2 = (v_ln_gains, v_mem_norm, v_a_conv_w, v_a_conv_b, v_a_w_in, v_a_b_in, v_a_gate_w, v_a_gate_b, v_a_lambda, v_a_w_out, v_a_b_out, v_b_mu, v_b_w_rkv, v_b_w0, v_b_w1, v_b_w2, v_b_a0, v_b_a1, v_b_a2, v_b_g1, v_b_g2, v_b_k_k, v_b_k_a, v_b_r_k, v_b_gn_g, v_b_gn_b, v_b_w_o, v_c_w_q, v_c_w_kv, v_c_w_o, v_m_w_up, v_m_w_down)
    return _train_step(x, mem, loss_target, dict(zip(WEIGHTS, weights)), dict(zip(WEIGHTS, moments1)),
                       dict(zip(WEIGHTS, moments2)))
```

```python
import jax
import jax.numpy as jnp
from jax import lax
from jax.experimental import pallas as pl
from jax.experimental.pallas import tpu as pltpu

F32 = jnp.float32
BF16 = jnp.bfloat16
N_DEV = 8
MESH = pl.DeviceIdType.MESH

V7X_VMEM_LIMIT_BYTES = 56 * 1024 * 1024
SUB = 8
HALO_ROWS = SUB
LANES = 128
RWKV_N = 64
RMS_EPS = 1e-6
GN_EPS = 64e-5
LRU_C = 8.0
XATTN_HEADS = 4
LRU_HEADS = 4

ADAM_LR, ADAM_B1, ADAM_B2, ADAM_EPS, ADAM_WD, ADAM_STEP = 0.001, 0.9, 0.999, 1e-8, 0.01, 10

WEIGHTS = ['ln_gains', 'mem_norm', 'a_conv_w', 'a_conv_b', 'a_w_in', 'a_b_in', 'a_gate_w', 'a_gate_b',
           'a_lambda', 'a_w_out', 'a_b_out', 'b_mu', 'b_w_rkv', 'b_w0', 'b_w1', 'b_w2', 'b_a0', 'b_a1',
           'b_a2', 'b_g1', 'b_g2', 'b_k_k', 'b_k_a', 'b_r_k', 'b_gn_g', 'b_gn_b', 'b_w_o', 'c_w_q',
           'c_w_kv', 'c_w_o', 'm_w_up', 'm_w_down']
SHARD_AXIS = {'ln_gains': 2, 'mem_norm': None, 'a_conv_w': 2, 'a_conv_b': None, 'a_w_in': 2, 'a_b_in': None,
              'a_gate_w': 3, 'a_gate_b': 3, 'a_lambda': None, 'a_w_out': 1, 'a_b_out': None, 'b_mu': 2,
              'b_w_rkv': 2, 'b_w0': 1, 'b_w1': 1, 'b_w2': 2, 'b_a0': 1, 'b_a1': 1, 'b_a2': 2, 'b_g1': 1,
              'b_g2': 2, 'b_k_k': 1, 'b_k_a': 1, 'b_r_k': None, 'b_gn_g': 1, 'b_gn_b': 1, 'b_w_o': 1,
              'c_w_q': 1, 'c_w_kv': 2, 'c_w_o': 1, 'm_w_up': 2, 'm_w_down': 1}
BIG = ['a_w_in', 'a_gate_w', 'a_w_out', 'b_w_rkv', 'b_w1', 'b_w2', 'b_a1', 'b_a2', 'b_g1', 'b_g2', 'b_w_o',
       'c_w_q', 'c_w_kv', 'c_w_o', 'm_w_up', 'm_w_down']
SMALL = ['ln_gains', 'a_conv_w', 'a_gate_b', 'b_mu', 'b_w0', 'b_a0', 'b_k_k', 'b_k_a', 'b_gn_g', 'b_gn_b']
REPL = ['mem_norm', 'a_conv_b', 'a_b_in', 'a_lambda', 'a_b_out', 'b_r_k']
PACK_W = 1024
PACK_ROWS = 128
BF16_ROWS = 16


def _cparams(*sem):
    return pltpu.CompilerParams(dimension_semantics=sem, vmem_limit_bytes=V7X_VMEM_LIMIT_BYTES)


def _shift_tile(tile, halo, k, pos, seq, tm):
    if k > 0:
        ext = jnp.concatenate([halo, tile], axis=0)
        r = pltpu.roll(ext, k, 0)[HALO_ROWS:HALO_ROWS + tm]
        return jnp.where(pos >= k, r, 0.0)
    kk = -k
    ext = jnp.concatenate([tile, halo], axis=0)
    r = pltpu.roll(ext, tm + HALO_ROWS - kk, 0)[0:tm]
    return jnp.where(pos + kk < seq, r, 0.0)


def _row_specs(rows, shifts, params, per_b, seq, tm):
    T = rows[0].shape[0]
    ns = seq // tm
    specs = [pl.BlockSpec((tm, r.shape[1]), lambda b, i: (b * ns + i, 0)) for r in rows]
    halo_keys = []
    for idx, k in shifts:
        key = (idx, k > 0)
        if key not in halo_keys:
            halo_keys.append(key)
    per8 = tm // HALO_ROWS
    last8 = T // HALO_ROWS - 1
    for idx, prev in halo_keys:
        c = rows[idx].shape[1]
        if prev:
            specs.append(pl.BlockSpec((HALO_ROWS, c), lambda b, i: (jnp.maximum((b * ns + i) * per8 - 1, 0), 0)))
        else:
            specs.append(pl.BlockSpec((HALO_ROWS, c), lambda b, i: (jnp.minimum((b * ns + i + 1) * per8, last8), 0)))
    for j, p in enumerate(params):
        if j in per_b:
            specs.append(pl.BlockSpec((None,) + p.shape[1:], lambda b, i: (b, 0, 0)))
        else:
            specs.append(pl.BlockSpec(p.shape, lambda b, i, nd=p.ndim: (0,) * nd))
    return specs, halo_keys


def _load_tiles(refs, rows, shifts, halo_keys, params, seq, tm):
    nr, nh = len(rows), len(halo_keys)
    i = pl.program_id(1)
    pos = i * tm + lax.broadcasted_iota(jnp.int32, (tm, 1), 0)
    row_t = [r[...] for r in refs[:nr]]
    halo_t = {key: refs[nr + j][...] for j, key in enumerate(halo_keys)}
    sh_t = [_shift_tile(row_t[idx], halo_t[(idx, k > 0)], k, pos, seq, tm) for idx, k in shifts]
    par_t = [p[...] for p in refs[nr + nh:nr + nh + len(params)]]
    return pos, row_t, sh_t, par_t


ROW_TILES = (512, 256)
ROW_TILE_BUDGET_BYTES = 16 * 1024 * 1024


def _row_tile(seq, row_bytes):
    for tm in ROW_TILES:
        if 2 * tm * row_bytes <= ROW_TILE_BUDGET_BYTES:
            return min(tm, seq)
    return min(ROW_TILES[-1], seq)


def _rw(name, fn, rows, params=(), outs=(), accs=(), shifts=(), seq=None, per_b=()):
    rows, params = list(rows), list(params)
    T = rows[0].shape[0]
    seq = seq or T
    tm = _row_tile(seq, sum(r.shape[1] * r.dtype.itemsize for r in rows) + sum(c * jnp.dtype(dt).itemsize for c, dt in outs))
    nb, ns = T // seq, seq // tm
    in_specs, halo_keys = _row_specs(rows, shifts, params, per_b, seq, tm)
    n_in, no = len(in_specs), len(outs)

    def body(*refs):
        pos, row_t, sh_t, par_t = _load_tiles(refs, rows, shifts, halo_keys, params, seq, tm)
        res = fn(pos, *row_t, *sh_t, *par_t)
        out_refs = refs[n_in:n_in + no]
        acc_refs = refs[n_in + no:]
        for o_ref, v in zip(out_refs, res[:no]):
            o_ref[...] = v.astype(o_ref.dtype)
        if acc_refs:
            @pl.when((pl.program_id(0) == 0) & (pl.program_id(1) == 0))
            def _():
                for a_ref in acc_refs:
                    a_ref[...] = jnp.zeros_like(a_ref)
            for a_ref, v in zip(acc_refs, res[no:]):
                a_ref[...] += v

    out_shape = [jax.ShapeDtypeStruct((T, c), dt) for c, dt in outs]
    out_shape += [jax.ShapeDtypeStruct(s, F32) for s in accs]
    out_specs = [pl.BlockSpec((tm, c), lambda b, i: (b * ns + i, 0)) for c, _ in outs]
    out_specs += [pl.BlockSpec(s, lambda b, i: (0, 0)) for s in accs]
    return pl.pallas_call(body, name=name, grid=(nb, ns), in_specs=in_specs, out_specs=out_specs,
                          out_shape=out_shape, compiler_params=_cparams("arbitrary", "arbitrary"))(*rows, *[
                              rows[idx] for idx, _ in halo_keys], *params)


def _rw_vjp(name, fn, rows, params, cts, n_out, d_rows=(), d_shifts=(), d_params=(), adds=(), ct_adds=(),
            shifts=(), seq=None, per_b=()):
    rows, params, cts = list(rows), list(params), list(cts)
    T = rows[0].shape[0]
    seq = seq or T
    row_bytes = sum(r.shape[1] * r.dtype.itemsize for r in rows + cts + [a for _, a, _ in list(adds) + list(ct_adds)])
    row_bytes += sum(rows[j].shape[1] * jnp.dtype(dt).itemsize for j, dt in d_rows)
    row_bytes += sum(rows[shifts[j][0]].shape[1] * jnp.dtype(dt).itemsize for j, dt in d_shifts)
    tm = _row_tile(seq, row_bytes)
    nb, ns = T // seq, seq // tm
    n_ga = len(adds)
    adds = list(adds) + list(ct_adds)
    add_rows = [a for _, a, _ in adds]
    add_shifts = [(j, k) for j, (_, _, k) in enumerate(adds) if k != 0]
    all_rows = rows + cts + add_rows
    off_ct, off_add = len(rows), len(rows) + len(cts)
    all_shifts = list(shifts) + [(off_add + j, k) for j, k in add_shifts]
    in_specs, halo_keys = _row_specs(all_rows, all_shifts, params, per_b, seq, tm)
    n_in = len(in_specs)
    n_dr, n_ds = len(d_rows), len(d_shifts)

    def body(*refs):
        pos, row_t, sh_t, par_t = _load_tiles(refs, all_rows, all_shifts, halo_keys, params, seq, tm)
        fn_rows, ct_t, add_t = row_t[:off_ct], row_t[off_ct:off_add], row_t[off_add:]
        fn_sh, add_sh = sh_t[:len(shifts)], sh_t[len(shifts):]
        diff = [fn_rows[j] for j, _ in d_rows] + [fn_sh[j] for j, _ in d_shifts] + [par_t[j] for j in d_params]

        def f(*d):
            r2, s2, p2 = list(fn_rows), list(fn_sh), list(par_t)
            for (j, _), v in zip(d_rows, d[:n_dr]):
                r2[j] = v
            for (j, _), v in zip(d_shifts, d[n_dr:n_dr + n_ds]):
                s2[j] = v
            for j, v in zip(d_params, d[n_dr + n_ds:]):
                p2[j] = v
            return tuple(fn(pos, *r2, *s2, *p2)[:n_out])

        sh_iter = iter(add_sh)
        add_v = [add_t[j] if k == 0 else next(sh_iter) for j, (_, _, k) in enumerate(adds)]
        ct_t = list(ct_t)
        for (tgt, _, _), v in zip(adds[n_ga:], add_v[n_ga:]):
            ct_t[tgt] = ct_t[tgt] + v
        outs_v, vjp_fn = jax.vjp(f, *diff)
        grads = list(vjp_fn(tuple(c.astype(o.dtype) for c, o in zip(ct_t, outs_v))))
        for (tgt, _, _), v in zip(adds[:n_ga], add_v[:n_ga]):
            grads[tgt] = grads[tgt] + v
        out_refs = refs[n_in:]
        for o_ref, g in zip(out_refs[:n_dr + n_ds], grads[:n_dr + n_ds]):
            o_ref[...] = g.astype(o_ref.dtype)
        for o_ref, g, j in zip(out_refs[n_dr + n_ds:], grads[n_dr + n_ds:], d_params):
            first = (pl.program_id(1) == 0) if j in per_b else ((pl.program_id(0) == 0) & (pl.program_id(1) == 0))

            @pl.when(first)
            def _(o_ref=o_ref):
                o_ref[...] = jnp.zeros_like(o_ref)
            o_ref[...] += g.astype(F32)

    out_shape, out_specs = [], []
    for j, dt in d_rows:
        out_shape.append(jax.ShapeDtypeStruct(rows[j].shape, dt))
        out_specs.append(pl.BlockSpec((tm, rows[j].shape[1]), lambda b, i: (b * ns + i, 0)))
    for j, dt in d_shifts:
        src = rows[shifts[j][0]]
        out_shape.append(jax.ShapeDtypeStruct(src.shape, dt))
        out_specs.append(pl.BlockSpec((tm, src.shape[1]), lambda b, i: (b * ns + i, 0)))
    for j in d_params:
        p = params[j]
        out_shape.append(jax.ShapeDtypeStruct(p.shape, F32))
        if j in per_b:
            out_specs.append(pl.BlockSpec((None,) + p.shape[1:], lambda b, i: (b, 0, 0)))
        else:
            out_specs.append(pl.BlockSpec(p.shape, lambda b, i, nd=p.ndim: (0,) * nd))
    return pl.pallas_call(body, name=name, grid=(nb, ns), in_specs=in_specs, out_specs=out_specs,
                          out_shape=out_shape, compiler_params=_cparams("arbitrary", "arbitrary"))(*all_rows, *[
                              all_rows[idx] for idx, _ in halo_keys], *params)


MM_TILE = 1024
MM_DEEP = 2048


def _mm_nn(name, x, w, out_dtype):
    M, K = x.shape
    N = w.shape[1]
    tm, tn = min(MM_TILE if K <= MM_DEEP else MM_TILE // 2, M), min(MM_TILE, N)

    def body(x_ref, w_ref, o_ref):
        o_ref[...] = jnp.dot(x_ref[...].astype(BF16), w_ref[...].astype(BF16),
                             preferred_element_type=F32).astype(o_ref.dtype)

    return pl.pallas_call(body, name=name, grid=(M // tm, N // tn),
                          in_specs=[pl.BlockSpec((tm, K), lambda i, j: (i, 0)), pl.BlockSpec((K, tn), lambda i, j: (0, j))],
                          out_specs=pl.BlockSpec((tm, tn), lambda i, j: (i, j)),
                          out_shape=jax.ShapeDtypeStruct((M, N), out_dtype),
                          compiler_params=_cparams("parallel", "parallel"))(x, w)


def _mm_up_act(name, x, w):
    M, K = x.shape
    N = w.shape[1]
    tm, tn = min(MM_TILE if K <= MM_DEEP else MM_TILE // 2, M), min(MM_TILE, N)

    def body(x_ref, w_ref, up_ref, act_ref):
        up = jnp.dot(x_ref[...].astype(BF16), w_ref[...].astype(BF16), preferred_element_type=F32)
        up_ref[...] = up.astype(BF16)
        act_ref[...] = jnp.square(jnp.maximum(up, 0.0)).astype(BF16)

    out = pl.BlockSpec((tm, tn), lambda i, j: (i, j))
    return pl.pallas_call(body, name=name, grid=(M // tm, N // tn),
                          in_specs=[pl.BlockSpec((tm, K), lambda i, j: (i, 0)), pl.BlockSpec((K, tn), lambda i, j: (0, j))],
                          out_specs=[out, out], out_shape=[jax.ShapeDtypeStruct((M, N), BF16)] * 2,
                          compiler_params=_cparams("parallel", "parallel"))(x, w)


def _mm_nt_act_bwd(name, dy, w, up):
    M, N = dy.shape
    K = w.shape[0]
    tm, tk = min(MM_TILE if N <= MM_DEEP else MM_TILE // 2, M), min(MM_TILE, K)

    def body(dy_ref, w_ref, up_ref, o_ref):
        dact = lax.dot_general(dy_ref[...].astype(BF16), w_ref[...].astype(BF16), (((1,), (1,)), ((), ())),
                               preferred_element_type=F32)
        o_ref[...] = (dact * (2.0 * jnp.maximum(up_ref[...].astype(F32), 0.0))).astype(o_ref.dtype)

    out = pl.BlockSpec((tm, tk), lambda i, j: (i, j))
    return pl.pallas_call(body, name=name, grid=(M // tm, K // tk),
                          in_specs=[pl.BlockSpec((tm, N), lambda i, j: (i, 0)), pl.BlockSpec((tk, N), lambda i, j: (j, 0)), out],
                          out_specs=out, out_shape=jax.ShapeDtypeStruct((M, K), BF16),
                          compiler_params=_cparams("parallel", "parallel"))(dy, w, up)


def _mm_nt(name, dy, w, out_dtype):
    M, N = dy.shape
    K = w.shape[0]
    tm, tk = min(MM_TILE if N <= MM_DEEP else MM_TILE // 2, M), min(MM_TILE, K)

    def body(dy_ref, w_ref, o_ref):
        o_ref[...] = lax.dot_general(dy_ref[...].astype(BF16), w_ref[...].astype(BF16), (((1,), (1,)), ((), ())),
                                     preferred_element_type=F32).astype(o_ref.dtype)

    return pl.pallas_call(body, name=name, grid=(M // tm, K // tk),
                          in_specs=[pl.BlockSpec((tm, N), lambda i, j: (i, 0)), pl.BlockSpec((tk, N), lambda i, j: (j, 0))],
                          out_specs=pl.BlockSpec((tm, tk), lambda i, j: (i, j)),
                          out_shape=jax.ShapeDtypeStruct((M, K), out_dtype),
                          compiler_params=_cparams("parallel", "parallel"))(dy, w)


def _mm_tn(name, x, dy):
    M, K = x.shape
    N = dy.shape[1]
    tm, tk, tn = min(MM_TILE, M), min(MM_TILE, K), min(MM_TILE, N)

    def body(x_ref, dy_ref, o_ref):
        @pl.when(pl.program_id(2) == 0)
        def _():
            o_ref[...] = jnp.zeros_like(o_ref)
        o_ref[...] += lax.dot_general(x_ref[...].astype(BF16), dy_ref[...].astype(BF16), (((0,), (0,)), ((), ())),
                                      preferred_element_type=F32)

    return pl.pallas_call(body, name=name, grid=(K // tk, N // tn, M // tm),
                          in_specs=[pl.BlockSpec((tm, tk), lambda i, j, m: (m, i)), pl.BlockSpec((tm, tn), lambda i, j, m: (m, j))],
                          out_specs=pl.BlockSpec((tk, tn), lambda i, j, m: (i, j)),
                          out_shape=jax.ShapeDtypeStruct((K, N), F32),
                          compiler_params=_cparams("parallel", "parallel", "arbitrary"))(x, dy)


def _lru_scan(name, a, b, seq, ct=128):
    T, C = a.shape

    def body(a_ref, b_ref, h_ref):
        A, Bv = a_ref[...], b_ref[...]
        row = lax.broadcasted_iota(jnp.int32, (seq, 1), 0)
        k = 1
        while k < seq:
            keep = row >= k
            Bv = A * jnp.where(keep, pltpu.roll(Bv, k, 0), 0.0) + Bv
            A = A * jnp.where(keep, pltpu.roll(A, k, 0), 1.0)
            k *= 2
        h_ref[...] = Bv

    spec = pl.BlockSpec((seq, ct), lambda b_, j: (b_, j))
    return pl.pallas_call(body, name=name, grid=(T // seq, C // ct), in_specs=[spec, spec], out_specs=spec,
                          out_shape=jax.ShapeDtypeStruct((T, C), F32),
                          compiler_params=_cparams("parallel", "parallel"))(a, b)


def _lru_scan_bwd(name, a, h, dh, seq, ct=128):
    T, C = a.shape

    def body(a_ref, h_ref, dh_ref, da_ref, db_ref):
        row = lax.broadcasted_iota(jnp.int32, (seq, 1), 0)
        Cf = jnp.where(row < seq - 1, pltpu.roll(a_ref[...], seq - 1, 0), 0.0)
        G = dh_ref[...]
        k = 1
        while k < seq:
            keep = row + k < seq
            G = Cf * jnp.where(keep, pltpu.roll(G, seq - k, 0), 0.0) + G
            Cf = Cf * jnp.where(keep, pltpu.roll(Cf, seq - k, 0), 1.0)
            k *= 2
        db_ref[...] = G
        da_ref[...] = G * jnp.where(row >= 1, pltpu.roll(h_ref[...], 1, 0), 0.0)

    spec = pl.BlockSpec((seq, ct), lambda b_, j: (b_, j))
    return pl.pallas_call(body, name=name, grid=(T // seq, C // ct), in_specs=[spec] * 3, out_specs=[spec] * 2,
                          out_shape=[jax.ShapeDtypeStruct((T, C), F32)] * 2,
                          compiler_params=_cparams("parallel", "parallel"))(a, h, dh)


def _pair_consts():
    jj = lax.broadcasted_iota(jnp.int32, (RWKV_N, LANES), 0)
    ll = lax.broadcasted_iota(jnp.int32, (RWKV_N, LANES), 1)
    pick = ((ll & (RWKV_N - 1)) == jj).astype(BF16)
    l0 = lax.broadcasted_iota(jnp.int32, (LANES, LANES), 0)
    l1 = lax.broadcasted_iota(jnp.int32, (LANES, LANES), 1)
    same = ((l0 >> 6) == (l1 >> 6)).astype(BF16)
    return pick, same


N_COL = 5


def _cols_raw(rows, pick, same):
    lhs = [pick * jnp.broadcast_to(x, (RWKV_N, LANES)).astype(BF16) for x in rows]
    out = jnp.dot(jnp.concatenate(lhs, axis=0), same, preferred_element_type=F32)
    return tuple(out[i * RWKV_N:(i + 1) * RWKV_N] for i in range(len(rows)))


def _cols_adjoint(g):
    n = len(g)
    g = list(g) + [jnp.zeros_like(g[0])] * (n % 2)
    head = ((lax.broadcasted_iota(jnp.int32, (SUB, LANES), 1) >> 6)
            == lax.broadcasted_iota(jnp.int32, (SUB, LANES), 0)).astype(BF16)
    low = lax.broadcasted_iota(jnp.int32, (1, LANES), 1) < RWKV_N
    sums = lax.dot_general(head, jnp.concatenate([t.astype(BF16) for t in g], axis=0),
                           (((1,), (1,)), ((), ())), preferred_element_type=F32)
    rows = []
    for i in range(0, n, 2):
        blk = sums[:, i * RWKV_N:(i + 2) * RWKV_N]
        swapped = pltpu.roll(blk, RWKV_N, 1)
        rows.append(jnp.where(low, blk[0:1], swapped[1:2]))
        rows.append(jnp.where(low, swapped[0:1], blk[1:2]))
    return tuple(rows[:n])


def _rwkv_cols(rows, pick, same):
    flat = []
    for r, w, k, v, a, b in rows:
        flat += [a, 1.0 - w, b, k, r]
    tiles = _cols_raw(tuple(flat), pick, same)
    return [tiles[N_COL * p:N_COL * (p + 1)] for p in range(len(rows))]


def _rwkv_step(s0, tiles, v):
    ca, ce, cb, ck, cr = tiles
    sa = jnp.sum(s0 * ca, axis=0, keepdims=True)
    s1 = s0 - s0 * ce + cb * sa + ck * v
    return s1, jnp.sum(s1 * cr, axis=0, keepdims=True), sa


def _rwkv_step_bwd(s0, s1, sa, tiles, v, dy, ds1):
    ca, ce, cb, ck, cr = tiles
    ds1 = ds1 + cr * dy
    dsa = jnp.sum(ds1 * cb, axis=0, keepdims=True)
    dv = jnp.sum(ds1 * ck, axis=0, keepdims=True)
    ds0 = ds1 - ds1 * ce + ca * dsa
    return ds0, (s0 * dsa, ds1 * s0, ds1 * sa, ds1 * v, s1 * dy), dv


def _rwkv_specs(C, seq, tc, G, reverse):
    nc = seq // tc
    if reverse:
        row = pl.BlockSpec((tc, LANES * G), lambda b, g, c: (b * nc + nc - 1 - c, g))
        st = pl.BlockSpec((tc, G, RWKV_N, LANES), lambda b, g, c: (b * nc + nc - 1 - c, g, 0, 0))
    else:
        row = pl.BlockSpec((tc, LANES * G), lambda b, g, c: (b * nc + c, g))
        st = pl.BlockSpec((tc, G, RWKV_N, LANES), lambda b, g, c: (b * nc + c, g, 0, 0))
    return row, st


def _rwkv_fwd(name, r, w, k, v, a, b, seq, tc=32, G=8):
    T, C = r.shape
    tc = min(tc, seq)
    NP = C // LANES
    row, st = _rwkv_specs(C, seq, tc, G, False)
    last = pl.BlockSpec((None, G, RWKV_N, LANES), lambda b_, g, c: (b_, g, 0, 0))

    def body(r_ref, w_ref, k_ref, v_ref, a_ref, b_ref, y_ref, st_ref, sa_ref, last_ref, s_scr):
        @pl.when(pl.program_id(2) == 0)
        def _():
            s_scr[...] = jnp.zeros_like(s_scr)
        pick, same = _pair_consts()
        rid = lax.broadcasted_iota(jnp.int32, (SUB, LANES), 0)

        def group(t8, carry):
            base = pl.multiple_of(t8 * SUB, SUB)
            lanes = [pl.ds(LANES * p, LANES) for p in range(G)]
            blk = [[x[pl.ds(base, SUB), ln] for x in (r_ref, w_ref, k_ref, v_ref, a_ref, b_ref)] for ln in lanes]
            s = [s_scr[p] for p in range(G)]
            ys = [jnp.zeros((SUB, LANES), F32) for _ in range(G)]
            sas = [jnp.zeros((SUB, LANES), F32) for _ in range(G)]
            for i in range(SUB):
                tiles = _rwkv_cols([tuple(b_[i:i + 1] for b_ in blk[p]) for p in range(G)], pick, same)
                for p in range(G):
                    st_ref[base + i, p] = s[p]
                    s[p], y, sa = _rwkv_step(s[p], tiles[p], blk[p][3][i:i + 1])
                    ys[p] = jnp.where(rid == i, y, ys[p])
                    sas[p] = jnp.where(rid == i, sa, sas[p])
            for p in range(G):
                s_scr[p] = s[p]
                y_ref[pl.ds(base, SUB), lanes[p]] = ys[p]
                sa_ref[pl.ds(base, SUB), lanes[p]] = sas[p]
            return carry

        lax.fori_loop(0, tc // SUB, group, 0)

        @pl.when(pl.program_id(2) == seq // tc - 1)
        def _():
            last_ref[...] = s_scr[...]

    return pl.pallas_call(body, name=name, grid=(T // seq, NP // G, seq // tc), in_specs=[row] * 6,
                          out_specs=[row, st, row, last],
                          out_shape=[jax.ShapeDtypeStruct((T, C), F32), jax.ShapeDtypeStruct((T, NP, RWKV_N, LANES), F32),
                                     jax.ShapeDtypeStruct((T, C), F32),
                                     jax.ShapeDtypeStruct((T // seq, NP, RWKV_N, LANES), F32)],
                          scratch_shapes=[pltpu.VMEM((G, RWKV_N, LANES), F32)],
                          compiler_params=_cparams("parallel", "parallel", "arbitrary"))(r, w, k, v, a, b)


def _rwkv_bwd(name, r, w, k, v, a, b, states, sa, last, dy, dr_p, dk_p, dv_p, seq, tc=32, G=8):
    T, C = r.shape
    tc = min(tc, seq)
    NP = C // LANES
    row, st = _rwkv_specs(C, seq, tc, G, True)
    last_spec = pl.BlockSpec((None, G, RWKV_N, LANES), lambda b_, g, c: (b_, g, 0, 0))

    def body(r_ref, w_ref, k_ref, v_ref, a_ref, b_ref, st_ref, sa_ref, last_ref, dy_ref, drp_ref, dkp_ref, dvp_ref,
             dr_ref, dw_ref, dk_ref, dv_ref, da_ref, db_ref, ds_scr, s1_scr):
        @pl.when(pl.program_id(2) == 0)
        def _():
            ds_scr[...] = jnp.zeros_like(ds_scr)
            s1_scr[...] = last_ref[...]
        pick, same = _pair_consts()
        rid = lax.broadcasted_iota(jnp.int32, (SUB, LANES), 0)
        out_refs = (da_ref, dw_ref, db_ref, dk_ref, dr_ref)

        def group(n, carry):
            base = pl.multiple_of((tc // SUB - 1 - n) * SUB, SUB)
            lanes = [pl.ds(LANES * p, LANES) for p in range(G)]
            blk = [[x[pl.ds(base, SUB), ln] for x in (r_ref, w_ref, k_ref, v_ref, a_ref, b_ref)] for ln in lanes]
            dys = [dy_ref[pl.ds(base, SUB), ln] for ln in lanes]
            sas = [sa_ref[pl.ds(base, SUB), ln] for ln in lanes]
            ds = [ds_scr[p] for p in range(G)]
            s1 = [s1_scr[p] for p in range(G)]
            zero = jnp.zeros((SUB, LANES), F32)
            gs = [[zero, zero, zero, dkp_ref[pl.ds(base, SUB), ln], drp_ref[pl.ds(base, SUB), ln]] for ln in lanes]
            gv = [dvp_ref[pl.ds(base, SUB), ln] for ln in lanes]
            for i2 in reversed(range(0, SUB, 4)):
                steps = (i2 + 3, i2 + 2, i2 + 1, i2)
                tiles2 = [_rwkv_cols([tuple(b_[i:i + 1] for b_ in blk[p]) for p in range(G)], pick, same) for i in steps]
                adj2 = []
                for i, tiles in zip(steps, tiles2):
                    adj = []
                    for p in range(G):
                        s0 = st_ref[base + i, p]
                        ds[p], g_tiles, dv = _rwkv_step_bwd(s0, s1[p], sas[p][i:i + 1], tiles[p], blk[p][3][i:i + 1],
                                                            dys[p][i:i + 1], ds[p])
                        s1[p] = s0
                        adj += list(g_tiles)
                        gv[p] = gv[p] + jnp.where(rid == i, dv, 0.0)
                    adj2.append(adj)
                for i, adj in zip(steps, adj2):
                    rows = _cols_adjoint(adj)
                    for p in range(G):
                        for j in range(N_COL):
                            gs[p][j] = gs[p][j] + jnp.where(rid == i, rows[N_COL * p + j], 0.0)
            for p in range(G):
                ds_scr[p] = ds[p]
                s1_scr[p] = s1[p]
                dv_ref[pl.ds(base, SUB), lanes[p]] = gv[p]
                for j in range(N_COL):
                    out_refs[j][pl.ds(base, SUB), lanes[p]] = gs[p][j]
            return carry

        lax.fori_loop(0, tc // SUB, group, 0)

    return pl.pallas_call(body, name=name, grid=(T // seq, NP // G, seq // tc),
                          in_specs=[row] * 6 + [st, row, last_spec] + [row] * 4, out_specs=[row] * 6,
                          out_shape=[jax.ShapeDtypeStruct((T, C), F32)] * 6,
                          scratch_shapes=[pltpu.VMEM((G, RWKV_N, LANES), F32)] * 2,
                          compiler_params=_cparams("parallel", "parallel", "arbitrary"))(
                              r, w, k, v, a, b, states, sa, last, dy, dr_p, dk_p, dv_p)


def _all_gather(name, shard):
    def body(x_ref, o_ref, send_sems, recv_sems, local_sem):
        x, y, c = lax.axis_index("x"), lax.axis_index("y"), lax.axis_index("c")
        me, sibling = (x, y, c), (x, y, 1 - c)
        chips = [(1 - x, y), (x, 1 - y), (1 - x, 1 - y)]

        def copy(k, block, to, src=None):
            dst = o_ref.at[4 * block[0] + 2 * block[1] + block[2]]
            return pltpu.make_async_remote_copy(src_ref=dst if src is None else src, dst_ref=dst, send_sem=send_sems.at[k],
                                                recv_sem=recv_sems.at[k], device_id=to, device_id_type=MESH)

        mine = pltpu.make_async_copy(x_ref, o_ref.at[4 * x + 2 * y + c], local_sem)
        mine.start()
        first = [copy(0, me, sibling, src=x_ref)] + [copy(1 + j, me, (*chip, c), src=x_ref) for j, chip in enumerate(chips)]
        for cp in first:
            cp.start()
        passed = [copy(4 + j, (*chip, c), sibling) for j, chip in enumerate(chips)]
        for j, chip in enumerate(chips):
            copy(1 + j, (*chip, c), me).wait_recv()
            passed[j].start()
        copy(0, sibling, me).wait_recv()
        for j, chip in enumerate(chips):
            copy(4 + j, (*chip, 1 - c), me).wait_recv()
        for cp in first + passed:
            cp.wait_send()
        mine.wait()

    hbm = pl.BlockSpec(memory_space=pltpu.HBM)
    return pl.pallas_call(body, name=name, in_specs=[hbm], out_specs=hbm,
                          out_shape=jax.ShapeDtypeStruct((N_DEV,) + shard.shape, shard.dtype),
                          scratch_shapes=[pltpu.SemaphoreType.DMA((N_DEV - 1,)), pltpu.SemaphoreType.DMA((N_DEV - 1,)),
                                          pltpu.SemaphoreType.DMA])(shard)


N_CHIP = 4


def _swap_with_sibling(name, blocks, n_split):
    _, R, W = blocks.shape
    N = N_CHIP
    rq = R // n_split

    def body(x_ref, got_ref, send_sems, recv_sems):
        x, y, c = lax.axis_index("x"), lax.axis_index("y"), lax.axis_index("c")
        copies = []
        for j in range(N):
            for q in range(n_split):
                rows = pl.ds(q * rq, rq)
                cp = pltpu.make_async_remote_copy(src_ref=x_ref.at[2 * j + 1 - c, rows], dst_ref=got_ref.at[j, rows],
                                                  send_sem=send_sems.at[j * n_split + q], recv_sem=recv_sems.at[j * n_split + q],
                                                  device_id=(x, y, 1 - c), device_id_type=MESH)
                cp.start()
                copies.append(cp)
        for cp in copies:
            cp.wait()

    hbm = pl.BlockSpec(memory_space=pltpu.HBM)
    return pl.pallas_call(body, name=name, in_specs=[hbm], out_specs=hbm,
                          out_shape=jax.ShapeDtypeStruct((N, R, W), blocks.dtype),
                          scratch_shapes=[pltpu.SemaphoreType.DMA((N * n_split,)), pltpu.SemaphoreType.DMA((N * n_split,))])(blocks)


def _add_own_blocks(name, blocks, got, tr, out_dtype):
    N, R, W = got.shape

    def body(c_ref, a_ref, b_ref, o_ref):
        o_ref[...] = (a_ref[...] + b_ref[...]).astype(o_ref.dtype)

    grid_spec = pltpu.PrefetchScalarGridSpec(
        num_scalar_prefetch=1, grid=(N, R // tr),
        in_specs=[pl.BlockSpec((None, tr, W), lambda n, i, c: (2 * n + c[0], i, 0)),
                  pl.BlockSpec((None, tr, W), lambda n, i, c: (n, i, 0))],
        out_specs=pl.BlockSpec((None, tr, W), lambda n, i, c: (n, i, 0)))
    core = lax.axis_index("c").astype(jnp.int32).reshape(1)
    return pl.pallas_call(body, name=name, grid_spec=grid_spec, out_shape=jax.ShapeDtypeStruct(got.shape, out_dtype),
                          compiler_params=_cparams("parallel", "parallel"))(core, blocks, got)


def _exchange_chips(name, blocks):
    def body(z_ref, o_ref, send_sems, recv_sems, local_sem):
        x, y, c = lax.axis_index("x"), lax.axis_index("y"), lax.axis_index("c")
        chip = 2 * x + y
        mine = pltpu.make_async_copy(z_ref.at[chip], o_ref.at[chip], local_sem)
        mine.start()
        copies = []
        for k in range(1, N_CHIP):
            to = (x ^ (k >> 1), y ^ (k & 1), c)
            cp = pltpu.make_async_remote_copy(src_ref=z_ref.at[chip ^ k], dst_ref=o_ref.at[chip], send_sem=send_sems.at[k - 1],
                                              recv_sem=recv_sems.at[k - 1], device_id=to, device_id_type=MESH)
            cp.start()
            copies.append(cp)
        for k in range(1, N_CHIP):
            pltpu.make_async_remote_copy(src_ref=z_ref.at[chip], dst_ref=o_ref.at[chip ^ k], send_sem=send_sems.at[k - 1],
                                         recv_sem=recv_sems.at[k - 1], device_id=(x ^ (k >> 1), y ^ (k & 1), c),
                                         device_id_type=MESH).wait_recv()
        for cp in copies:
            cp.wait_send()
        mine.wait()

    hbm = pl.BlockSpec(memory_space=pltpu.HBM)
    return pl.pallas_call(body, name=name, in_specs=[hbm], out_specs=hbm,
                          out_shape=jax.ShapeDtypeStruct(blocks.shape, blocks.dtype),
                          scratch_shapes=[pltpu.SemaphoreType.DMA((N_CHIP - 1,)), pltpu.SemaphoreType.DMA((N_CHIP - 1,)),
                                          pltpu.SemaphoreType.DMA])(blocks)


def _adamw(name, parts, w, m, v, tr):
    R, W = w.shape
    n_parts = parts.shape[0]

    def body(p_ref, w_ref, m_ref, v_ref, g_ref, d_ref, nm_ref, nv_ref):
        g = p_ref[0].astype(F32)
        for s in range(1, n_parts):
            g = g + p_ref[s].astype(F32)
        nm = ADAM_B1 * m_ref[...] + (1.0 - ADAM_B1) * g
        nv = ADAM_B2 * v_ref[...] + (1.0 - ADAM_B2) * (g * g)
        m_hat = nm / (1.0 - ADAM_B1 ** ADAM_STEP)
        v_hat = nv / (1.0 - ADAM_B2 ** ADAM_STEP)
        g_ref[...] = g
        d_ref[...] = -ADAM_LR * (m_hat / (jnp.sqrt(v_hat) + ADAM_EPS) + ADAM_WD * w_ref[...])
        nm_ref[...] = nm
        nv_ref[...] = nv

    spec = pl.BlockSpec((tr, W), lambda i: (i, 0))
    return pl.pallas_call(body, name=name, grid=(R // tr,),
                          in_specs=[pl.BlockSpec((n_parts, tr, W), lambda i: (0, i, 0)), spec, spec, spec],
                          out_specs=[spec] * 4, out_shape=[jax.ShapeDtypeStruct((R, W), F32)] * 4,
                          compiler_params=_cparams("parallel"))(parts, w, m, v)


def _rms(x, g):
    return x * lax.rsqrt(jnp.mean(x * x, axis=-1, keepdims=True) + RMS_EPS) * g


def _softplus(z):
    return jnp.maximum(z, 0.0) + jnp.log(1.0 + jnp.exp(-jnp.abs(z)))


def _neg_expm1(z):
    series = -z * (1.0 + z * (0.5 + z * (1.0 / 6.0)))
    return jnp.where(z > -1e-3, series, 1.0 - jnp.exp(z))


def _gelu_tanh(x):
    return 0.5 * x * (1.0 + jnp.tanh(0.7978845608028654 * (x + 0.044715 * x * x * x)))


def _bdot(x, w):
    return jnp.dot(x.astype(BF16), w.astype(BF16), preferred_element_type=F32)


def _seg_consts(C):
    c0 = lax.broadcasted_iota(jnp.int32, (C, LANES), 0)
    h1 = lax.broadcasted_iota(jnp.int32, (C, LANES), 1)
    red = ((c0 >> 6) == h1).astype(BF16)
    h0 = lax.broadcasted_iota(jnp.int32, (LANES, C), 0)
    c1 = lax.broadcasted_iota(jnp.int32, (LANES, C), 1)
    exp = ((c1 >> 6) == h0).astype(BF16)
    return red, exp


def _split_dot(x, m):
    hi = x.astype(BF16)
    lo = (x - hi.astype(F32)).astype(BF16)
    return jnp.dot(hi, m, preferred_element_type=F32) + jnp.dot(lo, m, preferred_element_type=F32)


def _segsum_raw(x):
    red, exp = _seg_consts(x.shape[1])
    return _split_dot(_split_dot(x, red), exp)


@jax.custom_vjp
def _segsum(x):
    return _segsum_raw(x)


_segsum.defvjp(lambda x: (_segsum_raw(x), None), lambda _, g: (_segsum_raw(g),))


def _f_pre(out_dtype):
    def fn(pos, x, g):
        return (_rms(x, g).astype(out_dtype),)
    return fn


def _f_post(pos, x, t, g):
    return (x + _rms(t, g),)


def _f_post_bias(pos, x, t, g, bias):
    return (x + _rms(t + bias, g),)


def _f_conv(pos, proj, p1, p2, p3, b_u, w0, w1, w2, w3, cb):
    D = b_u.shape[1]
    u0 = proj[:, D:] + b_u
    u1 = jnp.where(pos >= 1, p1[:, D:] + b_u, 0.0)
    u2 = jnp.where(pos >= 2, p2[:, D:] + b_u, 0.0)
    u3 = jnp.where(pos >= 3, p3[:, D:] + b_u, 0.0)
    return (cb + u3 * w0 + u2 * w1 + u1 * w2 + u0 * w3,)


def _f_conv_bwd(pos, dconv, proj, dyb, n1, n2, n3, p1, p2, p3, b_u, w0, w1, w2, w3):
    D = b_u.shape[1]
    du = dconv * w3 + n1 * w2 + n2 * w1 + n3 * w0
    dproj = jnp.concatenate([dyb[:, :D], du], axis=1)
    u0 = proj[:, D:] + b_u
    u1 = jnp.where(pos >= 1, p1[:, D:] + b_u, 0.0)
    u2 = jnp.where(pos >= 2, p2[:, D:] + b_u, 0.0)
    u3 = jnp.where(pos >= 3, p3[:, D:] + b_u, 0.0)
    rs = lambda z: jnp.sum(z, axis=0, keepdims=True)
    return (dproj, rs(dproj), rs(dconv * u3), rs(dconv * u2), rs(dconv * u1), rs(dconv * u0), rs(dconv))


def _f_gates(pos, conv, proj, gate_w, gb_r, gb_i, lam, b_y):
    D = conv.shape[1]
    blk = D // LRU_HEADS
    cb = conv.astype(BF16)

    def gate(g, bias):
        z = [jnp.dot(cb[:, h * blk:(h + 1) * blk], gate_w[(g * LRU_HEADS + h) * blk:(g * LRU_HEADS + h + 1) * blk, :].astype(BF16),
                     preferred_element_type=F32) for h in range(LRU_HEADS)]
        return jax.nn.sigmoid(jnp.concatenate(z, axis=1) + bias)

    r_gate, i_gate = gate(0, gb_r), gate(1, gb_i)
    log_a = -LRU_C * r_gate * _softplus(-lam)
    a = jnp.exp(log_a)
    b = jnp.sqrt(_neg_expm1(2.0 * log_a)) * i_gate * conv
    yb = _gelu_tanh(proj[:, :D] + b_y)
    return a, b, yb


def _f_hy(pos, h, yb):
    return ((h * yb).astype(BF16),)


def _f_mix(pos, hn, hp, mu_r, mu_w, mu_k, mu_v, mu_a, mu_g, w0, w1, w2, a0, a1, a2, g1, g2):
    xx = hp - hn
    xr, xw, xk, xv, xa, xg = (hn + xx * m for m in (mu_r, mu_w, mu_k, mu_v, mu_a, mu_g))
    lw = w0 + _bdot(jnp.tanh(_bdot(xw, w1)), w2)
    decay = jnp.exp(-jnp.exp(-_softplus(-lw) - 0.5))
    a = jax.nn.sigmoid(a0 + _bdot(_bdot(xa, a1), a2))
    g = _bdot(jax.nn.sigmoid(_bdot(xg, g1)), g2)
    return xr.astype(BF16), xk.astype(BF16), xv.astype(BF16), decay, a, g


def _f_kk(pos, k, a, k_k, k_a):
    kk = k * k_k
    kk = kk / jnp.maximum(jnp.sqrt(_segsum(kk * kk)), 1e-12)
    return k * (1.0 + (a - 1.0) * k_a), -kk, kk * a


def _f_gn(pos, y, r, k2, v, g, gn_g, gn_b, r_k):
    inv_n = 1.0 / RWKV_N
    yc = y - _segsum(y) * inv_n
    var = _segsum(yc * yc) * inv_n
    yn = yc * lax.rsqrt(var + GN_EPS) * gn_g + gn_b
    bonus = _segsum(r * k2 * r_k) * v
    return (((yn + bonus) * g).astype(BF16),)


def _f_attn(pos, q, kv):
    D = q.shape[1]
    hd = D // XATTN_HEADS
    qb, kb, vb = q.astype(BF16), kv[:, :D].astype(BF16), kv[:, D:].astype(BF16)
    outs = []
    for h in range(XATTN_HEADS):
        sl = slice(h * hd, (h + 1) * hd)
        s = lax.dot_general(qb[:, sl], kb[:, sl], (((1,), (1,)), ((), ())), preferred_element_type=F32) * (hd ** -0.5)
        s = s - jnp.max(s, axis=-1, keepdims=True)
        e = jnp.exp(s)
        p = e / jnp.sum(e, axis=-1, keepdims=True)
        outs.append(jnp.dot(p.astype(BF16), vb[:, sl], preferred_element_type=F32))
    return (jnp.concatenate(outs, axis=1).astype(BF16),)


def _f_loss(pos, y, target):
    d = y - target
    inv = 1.0 / y.shape[1]
    part = 0.5 * inv * jnp.sum(jnp.sum(d * d, axis=1, keepdims=True), axis=0, keepdims=True)
    return d * inv, jnp.broadcast_to(part, (8, LANES))


def _full_from_blocks(blocks, shard_shape, ax):
    g = jnp.moveaxis(blocks.reshape((N_DEV,) + tuple(shard_shape)), 0, ax)
    return g.reshape(tuple(shard_shape[:ax]) + (N_DEV * shard_shape[ax],) + tuple(shard_shape[ax + 1:]))


def _blocks_from_full(full, shard_shape, ax):
    g = full.reshape(tuple(shard_shape[:ax]) + (N_DEV, shard_shape[ax]) + tuple(shard_shape[ax + 1:]))
    return jnp.moveaxis(g, ax, 0).reshape(N_DEV, -1)


def _numel(shape):
    n = 1
    for s in shape:
        n *= s
    return n


def _pack_flat(arrs, total):
    flat = jnp.concatenate([a.reshape(-1).astype(F32) for a in arrs])
    return jnp.pad(flat, (0, total - flat.shape[0]))


def _train_step(x3, mem3, target3, P, M_, V_):
    Bl, S, D = x3.shape
    T = Bl * S
    ML = mem3.shape[1]
    x = x3.reshape(T, D)
    mem = mem3.reshape(Bl * ML, D)
    target = target3.reshape(T, D)
    shp = {n: P[n].shape for n in WEIGHTS}

    n_big = sum(_numel(shp[n]) for n in BIG)
    n_small = sum(_numel(shp[n]) for n in SMALL)
    n_small_pad = -(-n_small // (PACK_ROWS * PACK_W)) * (PACK_ROWS * PACK_W)
    big_rows, small_rows = n_big // PACK_W, n_small_pad // PACK_W
    big_pack = _pack_flat([P[n] for n in BIG], n_big).reshape(big_rows, PACK_W).astype(BF16)
    small_pack = _pack_flat([P[n] for n in SMALL], n_small_pad).reshape(small_rows, PACK_W)
    big_all = _all_gather("gather_matrices", big_pack).reshape(N_DEV, n_big)
    small_all = _all_gather("gather_vectors", small_pack).reshape(N_DEV, n_small_pad)
    Wf = {}
    off = 0
    for n in BIG:
        ne = _numel(shp[n])
        Wf[n] = _full_from_blocks(big_all[:, off:off + ne], shp[n], SHARD_AXIS[n])
        off += ne
    off = 0
    for n in SMALL:
        ne = _numel(shp[n])
        Wf[n] = _full_from_blocks(small_all[:, off:off + ne], shp[n], SHARD_AXIS[n])
        off += ne
    for n in REPL:
        Wf[n] = P[n]

    row = lambda v: v.reshape(1, -1).astype(F32)
    gains = [[row(Wf['ln_gains'][i, j]) for j in range(6)] for i in range(2)]
    LW = 128

    def pad_cols(w):
        return jnp.pad(w, ((0, 0), (0, LW - w.shape[1]))).astype(F32)

    def pad_rows(w):
        return jnp.pad(w, ((0, LW - w.shape[0]), (0, 0))).astype(F32)

    G = {}
    saved = {}

    mem_n, = _rw("mem_norm_fwd", _f_pre(BF16), [mem], [row(Wf['mem_norm'])], outs=[(D, BF16)])

    def xattn_fwd(i, xin):
        hq, = _rw(f"xattn{i}_pre", _f_pre(BF16), [xin], [gains[i][2]], outs=[(D, BF16)], seq=S)
        q = _mm_nn(f"xattn{i}_q", hq, Wf['c_w_q'][i], BF16)
        kv = _mm_nn(f"xattn{i}_kv", mem_n, Wf['c_w_kv'][i], F32)
        kv = kv.reshape(Bl, ML, 2 * D)
        o, = _rw(f"xattn{i}_attn", _f_attn, [q], [kv], outs=[(D, BF16)], seq=S, per_b=(0,))
        c = _mm_nn(f"xattn{i}_o", o, Wf['c_w_o'][i], F32)
        xo, = _rw(f"xattn{i}_post", _f_post, [xin, c], [gains[i][3]], outs=[(D, F32)], seq=S)
        saved[f"xattn{i}"] = (xin, hq, q, kv, o, c)
        return xo

    def mlp_fwd(i, xin):
        hm, = _rw(f"mlp{i}_pre", _f_pre(BF16), [xin], [gains[i][4]], outs=[(D, BF16)], seq=S)
        up, act = _mm_up_act(f"mlp{i}_up", hm, Wf['m_w_up'][i])
        m = _mm_nn(f"mlp{i}_down", act, Wf['m_w_down'][i], F32)
        xo, = _rw(f"mlp{i}_post", _f_post, [xin, m], [gains[i][5]], outs=[(D, F32)], seq=S)
        saved[f"mlp{i}"] = (xin, hm, up, act, m)
        return xo

    b_in = row(Wf['a_b_in'][0])
    b_y, b_u = b_in[:, :D], b_in[:, D:]
    cw = [row(Wf['a_conv_w'][0, t]) for t in range(4)]
    cb = row(Wf['a_conv_b'][0])
    gate_w = Wf['a_gate_w'][0].reshape(2 * LRU_HEADS * (D // LRU_HEADS), D // LRU_HEADS).astype(F32)
    gate_b = Wf['a_gate_b'][0].reshape(2, D).astype(F32)
    gb_r, gb_i = gate_b[0:1], gate_b[1:2]
    lam = row(Wf['a_lambda'][0])
    b_out = row(Wf['a_b_out'][0])

    hn0, = _rw("lru_pre", _f_pre(BF16), [x], [gains[0][0]], outs=[(D, BF16)], seq=S)
    proj = _mm_nn("lru_in", hn0, Wf['a_w_in'][0], F32)
    prev3 = [(0, 1), (0, 2), (0, 3)]
    conv, = _rw("lru_conv", _f_conv, [proj], [b_u] + cw + [cb], outs=[(D, F32)], shifts=prev3, seq=S)
    gate_par = [gate_w, gb_r, gb_i, lam, b_y]
    a_l, b_l, yb = _rw("lru_gates", _f_gates, [conv, proj], gate_par, outs=[(D, F32)] * 3, seq=S)
    h_l = _lru_scan("lru_scan", a_l, b_l, S)
    hy, = _rw("lru_hy", _f_hy, [h_l, yb], outs=[(D, BF16)], seq=S)
    t0 = _mm_nn("lru_out", hy, Wf['a_w_out'][0], F32)
    x1, = _rw("lru_post", _f_post_bias, [x, t0], [gains[0][1], b_out], outs=[(D, F32)], seq=S)
    x2 = xattn_fwd(0, x1)
    x3_ = mlp_fwd(0, x2)

    mu = [row(Wf['b_mu'][0, j]) for j in range(6)]
    lora = [row(Wf['b_w0'][0]), pad_cols(Wf['b_w1'][0]), pad_rows(Wf['b_w2'][0]),
            row(Wf['b_a0'][0]), pad_cols(Wf['b_a1'][0]), pad_rows(Wf['b_a2'][0]),
            Wf['b_g1'][0].astype(F32), Wf['b_g2'][0].astype(F32)]
    k_k, k_a = row(Wf['b_k_k'][0]), row(Wf['b_k_a'][0])
    gn_g, gn_b, r_k = row(Wf['b_gn_g'][0]), row(Wf['b_gn_b'][0]), row(Wf['b_r_k'][0])

    hn1, = _rw("rwkv_pre", _f_pre(F32), [x3_], [gains[1][0]], outs=[(D, F32)], seq=S)
    xr, xk, xv, decay, a_r, g_r = _rw("rwkv_mix", _f_mix, [hn1], mu + lora,
                                      outs=[(D, BF16)] * 3 + [(D, F32)] * 3, shifts=[(0, 1)], seq=S)
    r_ = _mm_nn("rwkv_r", xr, Wf['b_w_rkv'][0, 0], F32)
    k_ = _mm_nn("rwkv_k", xk, Wf['b_w_rkv'][0, 1], F32)
    v_ = _mm_nn("rwkv_v", xv, Wf['b_w_rkv'][0, 2], F32)
    k2, ra, rb = _rw("rwkv_kk", _f_kk, [k_, a_r], [k_k, k_a], outs=[(D, F32)] * 3, seq=S)
    y_r, states, sa_r, last_r = _rwkv_fwd("rwkv_scan", r_, decay, k2, v_, ra, rb, S)
    gn_par = [gn_g, gn_b, r_k]
    og, = _rw("rwkv_gn", _f_gn, [y_r, r_, k2, v_, g_r], gn_par, outs=[(D, BF16)], seq=S)
    t1 = _mm_nn("rwkv_out", og, Wf['b_w_o'][0], F32)
    x4, = _rw("rwkv_post", _f_post, [x3_, t1], [gains[1][1]], outs=[(D, F32)], seq=S)
    x5 = xattn_fwd(1, x4)
    x6 = mlp_fwd(1, x5)

    dx, loss_acc = _rw("loss", _f_loss, [x6, target], outs=[(D, F32)], accs=[(8, LANES)], seq=S)
    loss = lax.psum(loss_acc[0, 0], ("x", "y", "c"))

    dgain = [[None] * 6 for _ in range(2)]
    dmem_n = None

    def mlp_bwd(i, dxo):
        xin, hm, up, act, m = saved[f"mlp{i}"]
        dm, dgain[i][5] = _rw_vjp(f"mlp{i}_post_bwd", _f_post, [xin, m], [gains[i][5]], [dxo], 1,
                                  d_rows=[(1, BF16)], d_params=[0], seq=S)
        G_down = _mm_tn(f"mlp{i}_down_dw", act, dm)
        dup = _mm_nt_act_bwd(f"mlp{i}_down_dx", dm, Wf['m_w_down'][i], up)
        G_up = _mm_tn(f"mlp{i}_up_dw", hm, dup)
        dhm = _mm_nt(f"mlp{i}_up_dx", dup, Wf['m_w_up'][i], F32)
        dxi, dgain[i][4] = _rw_vjp(f"mlp{i}_pre_bwd", _f_pre(F32), [xin], [gains[i][4]], [dhm], 1,
                                   d_rows=[(0, F32)], d_params=[0], adds=[(0, dxo, 0)], seq=S)
        return dxi, G_up, G_down

    def xattn_bwd(i, dxo):
        xin, hq, q, kv, o, c = saved[f"xattn{i}"]
        dc, dgain[i][3] = _rw_vjp(f"xattn{i}_post_bwd", _f_post, [xin, c], [gains[i][3]], [dxo], 1,
                                  d_rows=[(1, BF16)], d_params=[0], seq=S)
        G_o = _mm_tn(f"xattn{i}_o_dw", o, dc)
        do = _mm_nt(f"xattn{i}_o_dx", dc, Wf['c_w_o'][i], F32)
        dq, dkv = _rw_vjp(f"xattn{i}_attn_bwd", _f_attn, [q], [kv], [do], 1, d_rows=[(0, BF16)],
                          d_params=[0], seq=S, per_b=(0,))
        dkv = dkv.reshape(Bl * ML, 2 * D)
        G_kv = _mm_tn(f"xattn{i}_kv_dw", mem_n, dkv)
        dmn = _mm_nt(f"xattn{i}_kv_dx", dkv, Wf['c_w_kv'][i], F32)
        G_q = _mm_tn(f"xattn{i}_q_dw", hq, dq)
        dhq = _mm_nt(f"xattn{i}_q_dx", dq, Wf['c_w_q'][i], F32)
        dxi, dgain[i][2] = _rw_vjp(f"xattn{i}_pre_bwd", _f_pre(F32), [xin], [gains[i][2]], [dhq], 1,
                                   d_rows=[(0, F32)], d_params=[0], adds=[(0, dxo, 0)], seq=S)
        return dxi, G_q, G_kv, G_o, dmn

    G_up, G_down, G_q, G_kv, G_o = [None] * 2, [None] * 2, [None] * 2, [None] * 2, [None] * 2
    dx, G_up[1], G_down[1] = mlp_bwd(1, dx)
    dx, G_q[1], G_kv[1], G_o[1], dmn1 = xattn_bwd(1, dx)

    dt1, dgain[1][1] = _rw_vjp("rwkv_post_bwd", _f_post, [x3_, t1], [gains[1][1]], [dx], 1,
                               d_rows=[(1, BF16)], d_params=[0], seq=S)
    G['b_w_o'] = _mm_tn("rwkv_out_dw", og, dt1)[None]
    dog = _mm_nt("rwkv_out_dx", dt1, Wf['b_w_o'][0], F32)
    dy_r, dr_p, dk2_p, dv_p, dg_r, d_gn_g, d_gn_b, d_r_k = _rw_vjp(
        "rwkv_gn_bwd", _f_gn, [y_r, r_, k2, v_, g_r], gn_par, [dog], 1,
        d_rows=[(j, F32) for j in range(5)], d_params=[0, 1, 2], seq=S)
    dr_, ddecay, dk2, dv_, dra, drb = _rwkv_bwd("rwkv_scan_bwd", r_, decay, k2, v_, ra, rb, states, sa_r, last_r, dy_r,
                                                dr_p, dk2_p, dv_p, S)
    dk_, da_r, d_k_k, d_k_a = _rw_vjp("rwkv_kk_bwd", _f_kk, [k_, a_r], [k_k, k_a], [dk2, dra, drb], 3,
                                      d_rows=[(0, F32), (1, F32)], d_params=[0, 1], seq=S)
    G_rkv = [_mm_tn("rwkv_r_dw", xr, dr_), _mm_tn("rwkv_k_dw", xk, dk_), _mm_tn("rwkv_v_dw", xv, dv_)]
    G['b_w_rkv'] = jnp.stack(G_rkv)[None]
    dxr = _mm_nt("rwkv_r_dx", dr_, Wf['b_w_rkv'][0, 0], F32)
    dxk = _mm_nt("rwkv_k_dx", dk_, Wf['b_w_rkv'][0, 1], F32)
    dxv = _mm_nt("rwkv_v_dx", dv_, Wf['b_w_rkv'][0, 2], F32)
    mix_out = _rw_vjp("rwkv_mix_bwd", _f_mix, [hn1], mu + lora, [dxr, dxk, dxv, ddecay, da_r, dg_r], 6,
                      d_rows=[(0, F32)], d_shifts=[(0, F32)], d_params=list(range(14)), shifts=[(0, 1)], seq=S)
    dhn1, dhp1 = mix_out[0], mix_out[1]
    d_mu = mix_out[2:8]
    d_lora = mix_out[8:16]
    dx, dgain[1][0] = _rw_vjp("rwkv_pre_bwd", _f_pre(F32), [x3_], [gains[1][0]], [dhn1], 1, d_rows=[(0, F32)],
                              d_params=[0], adds=[(0, dx, 0)], ct_adds=[(0, dhp1, -1)], seq=S)
    G['b_mu'] = jnp.concatenate(d_mu, axis=0)[None]
    G['b_w0'], G['b_a0'] = d_lora[0], d_lora[3]
    G['b_w1'] = d_lora[1][:, :shp['b_w1'][2]][None]
    G['b_w2'] = d_lora[2][:shp['b_w2'][1]][None]
    G['b_a1'] = d_lora[4][:, :shp['b_a1'][2]][None]
    G['b_a2'] = d_lora[5][:shp['b_a2'][1]][None]
    G['b_g1'], G['b_g2'] = d_lora[6][None], d_lora[7][None]
    G['b_k_k'], G['b_k_a'], G['b_gn_g'], G['b_gn_b'] = d_k_k, d_k_a, d_gn_g, d_gn_b
    G['b_r_k'] = d_r_k.reshape(P['b_r_k'].shape)

    dx, G_up[0], G_down[0] = mlp_bwd(0, dx)
    dx, G_q[0], G_kv[0], G_o[0], dmn0 = xattn_bwd(0, dx)
    G['m_w_up'], G['m_w_down'] = jnp.stack(G_up), jnp.stack(G_down)
    G['c_w_q'], G['c_w_kv'], G['c_w_o'] = jnp.stack(G_q), jnp.stack(G_kv), jnp.stack(G_o)

    dt0, dgain[0][1], G['a_b_out'] = _rw_vjp("lru_post_bwd", _f_post_bias, [x, t0], [gains[0][1], b_out], [dx], 1,
                                             d_rows=[(1, BF16)], d_params=[0, 1], seq=S)
    G['a_w_out'] = _mm_tn("lru_out_dw", hy, dt0)[None]
    dhy = _mm_nt("lru_out_dx", dt0, Wf['a_w_out'][0], F32)
    dh_l, dyb = _rw_vjp("lru_hy_bwd", _f_hy, [h_l, yb], [], [dhy], 1, d_rows=[(0, F32), (1, F32)], seq=S)
    da_l, db_l = _lru_scan_bwd("lru_scan_bwd", a_l, h_l, dh_l, S)
    dconv, dproj_y, d_gate_w, d_gb_r, d_gb_i, G['a_lambda'] = _rw_vjp(
        "lru_gates_bwd", _f_gates, [conv, proj], gate_par, [da_l, db_l, dyb], 3,
        d_rows=[(0, F32), (1, F32)], d_params=[0, 1, 2, 3], seq=S)
    next3 = [(0, -1), (0, -2), (0, -3)]
    dproj, G['a_b_in'], dw0, dw1, dw2, dw3, G['a_conv_b'] = _rw(
        "lru_conv_bwd", _f_conv_bwd, [dconv, proj, dproj_y], [b_u] + cw, outs=[(2 * D, BF16)],
        accs=[(1, 2 * D)] + [(1, D)] * 5, shifts=next3 + [(1, 1), (1, 2), (1, 3)], seq=S)
    G['a_conv_w'] = jnp.concatenate([dw0, dw1, dw2, dw3], axis=0)[None]
    G['a_gate_w'] = d_gate_w.reshape(P['a_gate_w'].shape[:3] + (D // LRU_HEADS, D // LRU_HEADS))
    G['a_gate_b'] = jnp.concatenate([d_gb_r, d_gb_i], axis=0).reshape(1, 2, LRU_HEADS, D // LRU_HEADS)
    G['a_w_in'] = _mm_tn("lru_in_dw", hn0, dproj)[None]
    dhn0 = _mm_nt("lru_in_dx", dproj, Wf['a_w_in'][0], F32)
    grad_x, dgain[0][0] = _rw_vjp("lru_pre_bwd", _f_pre(F32), [x], [gains[0][0]], [dhn0], 1, d_rows=[(0, F32)],
                                  d_params=[0], adds=[(0, dx, 0)], seq=S)
    d_mem_norm, = _rw_vjp("mem_norm_bwd", _f_pre(F32), [mem], [row(Wf['mem_norm'])], [dmn0], 1, d_params=[0],
                          ct_adds=[(0, dmn1, 0)])
    G['mem_norm'] = d_mem_norm.reshape(-1)
    G['ln_gains'] = jnp.stack([jnp.concatenate(dgain[i], axis=0) for i in range(2)])

    sharded = BIG + SMALL
    rows_s = big_rows + small_rows
    tr = max(t for t in range(BF16_ROWS, 161, BF16_ROWS) if rows_s % t == 0)

    pieces = [_blocks_from_full(G[n], shp[n], SHARD_AXIS[n]) for n in sharded]
    pieces.append(jnp.zeros((N_DEV, n_small_pad - n_small), F32))
    g_blocks = jnp.concatenate(pieces, axis=1).reshape(N_DEV, rows_s, PACK_W)
    n_split = max(q for q in range(1, 10) if rows_s % (q * BF16_ROWS) == 0)
    got = _swap_with_sibling("exchange_grads_sibling", g_blocks, n_split)
    parts_s = _exchange_chips("exchange_grads_chips", _add_own_blocks("sum_with_sibling", g_blocks, got, tr, BF16))
    n_repl = sum(_numel(shp[n]) for n in REPL)
    n_repl_pad = -(-n_repl // (8 * PACK_W)) * (8 * PACK_W)
    rows_r = n_repl_pad // PACK_W
    parts_r = _all_gather("gather_replicated_grads", _pack_flat([G[n] for n in REPL], n_repl_pad).reshape(rows_r, PACK_W))

    def pack(src, names, total, rows_):
        return _pack_flat([src[n] for n in names], total).reshape(rows_, PACK_W)

    out_s = _adamw("adamw_sharded", parts_s, pack(P, sharded, n_big + n_small_pad, rows_s),
                   pack(M_, sharded, n_big + n_small_pad, rows_s), pack(V_, sharded, n_big + n_small_pad, rows_s), tr)
    out_r = _adamw("adamw_replicated", parts_r, pack(P, REPL, n_repl_pad, rows_r), pack(M_, REPL, n_repl_pad, rows_r),
                   pack(V_, REPL, n_repl_pad, rows_r), rows_r)

    def unpack(packed, names):
        flat = packed.reshape(-1)
        res, o = {}, 0
        for n in names:
            ne = _numel(shp[n])
            res[n] = flat[o:o + ne].reshape(shp[n])
            o += ne
        return res

    results = []
    for j in range(4):
        both = {**unpack(out_s[j], sharded), **unpack(out_r[j], REPL)}
        results += [both[n] for n in WEIGHTS]
    return (loss, grad_x.reshape(Bl, S, D), *results)


def kernel(x, mem, ln_gains, mem_norm, a_conv_w, a_conv_b, a_w_in, a_b_in, a_gate_w, a_gate_b, a_lambda, a_w_out, a_b_out, b_mu, b_w_rkv, b_w0, b_w1, b_w2, b_a0, b_a1, b_a2, b_g1, b_g2, b_k_k, b_k_a, b_r_k, b_gn_g, b_gn_b, b_w_o, c_w_q, c_w_kv, c_w_o, m_w_up, m_w_down, loss_target, m_ln_gains, m_mem_norm, m_a_conv_w, m_a_conv_b, m_a_w_in, m_a_b_in, m_a_gate_w, m_a_gate_b, m_a_lambda, m_a_w_out, m_a_b_out, m_b_mu, m_b_w_rkv, m_b_w0, m_b_w1, m_b_w2, m_b_a0, m_b_a1, m_b_a2, m_b_g1, m_b_g2, m_b_k_k, m_b_k_a, m_b_r_k, m_b_gn_g, m_b_gn_b, m_b_w_o, m_c_w_q, m_c_w_kv, m_c_w_o, m_m_w_up, m_m_w_down, v_ln_gains, v_mem_norm, v_a_conv_w, v_a_conv_b, v_a_w_in, v_a_b_in, v_a_gate_w, v_a_gate_b, v_a_lambda, v_a_w_out, v_a_b_out, v_b_mu, v_b_w_rkv, v_b_w0, v_b_w1, v_b_w2, v_b_a0, v_b_a1, v_b_a2, v_b_g1, v_b_g2, v_b_k_k, v_b_k_a, v_b_r_k, v_b_gn_g, v_b_gn_b, v_b_w_o, v_c_w_q, v_c_w_kv, v_c_w_o, v_m_w_up, v_m_w_down):
    weights = (ln_gains, mem_norm, a_conv_w, a_conv_b, a_w_in, a_b_in, a_gate_w, a_gate_b, a_lambda, a_w_out, a_b_out, b_mu, b_w_rkv, b_w0, b_w1, b_w2, b_a0, b_a1, b_a2, b_g1, b_g2, b_k_k, b_k_a, b_r_k, b_gn_g, b_gn_b, b_w_o, c_w_q, c_w_kv, c_w_o, m_w_up, m_w_down)
    moments1 = (m_ln_gains, m_mem_norm, m_a_conv_w, m_a_conv_b, m_a_w_in, m_a_b_in, m_a_gate_w, m_a_gate_b, m_a_lambda, m_a_w_out, m_a_b_out, m_b_mu, m_b_w_rkv, m_b_w0, m_b_w1, m_b_w2, m_b_a0, m_b_a1, m_b_a2, m_b_g1, m_b_g2, m_b_k_k, m_b_k_a, m_b_r_k, m_b_gn_g, m_b_gn_b, m_b_w_o, m_c_w_q, m_c_w_kv, m_c_w_o, m_m_w_up, m_m_w_down)
    moments2 = (v_ln_gains, v_mem_norm, v_a_conv_w, v_a_conv_b, v_a_w_in, v_a_b_in, v_a_gate_w, v_a_gate_b, v_a_lambda, v_a_w_out, v_a_b_out, v_b_mu, v_b_w_rkv, v_b_w0, v_b_w1, v_b_w2, v_b_a0, v_b_a1, v_b_a2, v_b_g1, v_b_g2, v_b_k_k, v_b_k_a, v_b_r_k, v_b_gn_g, v_b_gn_b, v_b_w_o, v_c_w_q, v_c_w_kv, v_c_w_o, v_m_w_up, v_m_w_down)
    return _train_step(x, mem, loss_target, dict(zip(WEIGHTS, weights)), dict(zip(WEIGHTS, moments1)),
                       dict(zip(WEIGHTS, moments2)))
```

```python
import jax
import jax.numpy as jnp
from jax import lax
from jax.experimental import pallas as pl
from jax.experimental.pallas import tpu as pltpu

F32 = jnp.float32
BF16 = jnp.bfloat16
N_DEV = 8
MESH = pl.DeviceIdType.MESH

V7X_VMEM_LIMIT_BYTES = 56 * 1024 * 1024
SUB = 8
HALO_ROWS = SUB
LANES = 128
RWKV_N = 64
RMS_EPS = 1e-6
GN_EPS = 64e-5
LRU_C = 8.0
XATTN_HEADS = 4
LRU_HEADS = 4

ADAM_LR, ADAM_B1, ADAM_B2, ADAM_EPS, ADAM_WD, ADAM_STEP = 0.001, 0.9, 0.999, 1e-8, 0.01, 10

WEIGHTS = ['ln_gains', 'mem_norm', 'a_conv_w', 'a_conv_b', 'a_w_in', 'a_b_in', 'a_gate_w', 'a_gate_b',
           'a_lambda', 'a_w_out', 'a_b_out', 'b_mu', 'b_w_rkv', 'b_w0', 'b_w1', 'b_w2', 'b_a0', 'b_a1',
           'b_a2', 'b_g1', 'b_g2', 'b_k_k', 'b_k_a', 'b_r_k', 'b_gn_g', 'b_gn_b', 'b_w_o', 'c_w_q',
           'c_w_kv', 'c_w_o', 'm_w_up', 'm_w_down']
SHARD_AXIS = {'ln_gains': 2, 'mem_norm': None, 'a_conv_w': 2, 'a_conv_b': None, 'a_w_in': 2, 'a_b_in': None,
              'a_gate_w': 3, 'a_gate_b': 3, 'a_lambda': None, 'a_w_out': 1, 'a_b_out': None, 'b_mu': 2,
              'b_w_rkv': 2, 'b_w0': 1, 'b_w1': 1, 'b_w2': 2, 'b_a0': 1, 'b_a1': 1, 'b_a2': 2, 'b_g1': 1,
              'b_g2': 2, 'b_k_k': 1, 'b_k_a': 1, 'b_r_k': None, 'b_gn_g': 1, 'b_gn_b': 1, 'b_w_o': 1,
              'c_w_q': 1, 'c_w_kv': 2, 'c_w_o': 1, 'm_w_up': 2, 'm_w_down': 1}
BIG = ['a_w_in', 'a_gate_w', 'a_w_out', 'b_w_rkv', 'b_w1', 'b_w2', 'b_a1', 'b_a2', 'b_g1', 'b_g2', 'b_w_o',
       'c_w_q', 'c_w_kv', 'c_w_o', 'm_w_up', 'm_w_down']
SMALL = ['ln_gains', 'a_conv_w', 'a_gate_b', 'b_mu', 'b_w0', 'b_a0', 'b_k_k', 'b_k_a', 'b_gn_g', 'b_gn_b']
REPL = ['mem_norm', 'a_conv_b', 'a_b_in', 'a_lambda', 'a_b_out', 'b_r_k']
PACK_W = 1024
PACK_ROWS = 128
BF16_ROWS = 16


def _cparams(*sem):
    return pltpu.CompilerParams(dimension_semantics=sem, vmem_limit_bytes=V7X_VMEM_LIMIT_BYTES)


def _shift_tile(tile, halo, k, pos, seq, tm):
    if k > 0:
        ext = jnp.concatenate([halo, tile], axis=0)
        r = pltpu.roll(ext, k, 0)[HALO_ROWS:HALO_ROWS + tm]
        return jnp.where(pos >= k, r, 0.0)
    kk = -k
    ext = jnp.concatenate([tile, halo], axis=0)
    r = pltpu.roll(ext, tm + HALO_ROWS - kk, 0)[0:tm]
    return jnp.where(pos + kk < seq, r, 0.0)


def _row_specs(rows, shifts, params, per_b, seq, tm):
    T = rows[0].shape[0]
    ns = seq // tm
    specs = [pl.BlockSpec((tm, r.shape[1]), lambda b, i: (b * ns + i, 0)) for r in rows]
    halo_keys = []
    for idx, k in shifts:
        key = (idx, k > 0)
        if key not in halo_keys:
            halo_keys.append(key)
    per8 = tm // HALO_ROWS
    last8 = T // HALO_ROWS - 1
    for idx, prev in halo_keys:
        c = rows[idx].shape[1]
        if prev:
            specs.append(pl.BlockSpec((HALO_ROWS, c), lambda b, i: (jnp.maximum((b * ns + i) * per8 - 1, 0), 0)))
        else:
            specs.append(pl.BlockSpec((HALO_ROWS, c), lambda b, i: (jnp.minimum((b * ns + i + 1) * per8, last8), 0)))
    for j, p in enumerate(params):
        if j in per_b:
            specs.append(pl.BlockSpec((None,) + p.shape[1:], lambda b, i: (b, 0, 0)))
        else:
            specs.append(pl.BlockSpec(p.shape, lambda b, i, nd=p.ndim: (0,) * nd))
    return specs, halo_keys


def _load_tiles(refs, rows, shifts, halo_keys, params, seq, tm):
    nr, nh = len(rows), len(halo_keys)
    i = pl.program_id(1)
    pos = i * tm + lax.broadcasted_iota(jnp.int32, (tm, 1), 0)
    row_t = [r[...] for r in refs[:nr]]
    halo_t = {key: refs[nr + j][...] for j, key in enumerate(halo_keys)}
    sh_t = [_shift_tile(row_t[idx], halo_t[(idx, k > 0)], k, pos, seq, tm) for idx, k in shifts]
    par_t = [p[...] for p in refs[nr + nh:nr + nh + len(params)]]
    return pos, row_t, sh_t, par_t


ROW_TILES = (512, 256)
ROW_TILE_BUDGET_BYTES = 16 * 1024 * 1024


def _row_tile(seq, row_bytes):
    for tm in ROW_TILES:
        if 2 * tm * row_bytes <= ROW_TILE_BUDGET_BYTES:
            return min(tm, seq)
    return min(ROW_TILES[-1], seq)


def _rw(name, fn, rows, params=(), outs=(), accs=(), shifts=(), seq=None, per_b=()):
    rows, params = list(rows), list(params)
    T = rows[0].shape[0]
    seq = seq or T
    tm = _row_tile(seq, sum(r.shape[1] * r.dtype.itemsize for r in rows) + sum(c * jnp.dtype(dt).itemsize for c, dt in outs))
    nb, ns = T // seq, seq // tm
    in_specs, halo_keys = _row_specs(rows, shifts, params, per_b, seq, tm)
    n_in, no = len(in_specs), len(outs)

    def body(*refs):
        pos, row_t, sh_t, par_t = _load_tiles(refs, rows, shifts, halo_keys, params, seq, tm)
        res = fn(pos, *row_t, *sh_t, *par_t)
        out_refs = refs[n_in:n_in + no]
        acc_refs = refs[n_in + no:]
        for o_ref, v in zip(out_refs, res[:no]):
            o_ref[...] = v.astype(o_ref.dtype)
        if acc_refs:
            @pl.when((pl.program_id(0) == 0) & (pl.program_id(1) == 0))
            def _():
                for a_ref in acc_refs:
                    a_ref[...] = jnp.zeros_like(a_ref)
            for a_ref, v in zip(acc_refs, res[no:]):
                a_ref[...] += v

    out_shape = [jax.ShapeDtypeStruct((T, c), dt) for c, dt in outs]
    out_shape += [jax.ShapeDtypeStruct(s, F32) for s in accs]
    out_specs = [pl.BlockSpec((tm, c), lambda b, i: (b * ns + i, 0)) for c, _ in outs]
    out_specs += [pl.BlockSpec(s, lambda b, i: (0, 0)) for s in accs]
    return pl.pallas_call(body, name=name, grid=(nb, ns), in_specs=in_specs, out_specs=out_specs,
                          out_shape=out_shape, compiler_params=_cparams("arbitrary", "arbitrary"))(*rows, *[
                              rows[idx] for idx, _ in halo_keys], *params)


def _rw_vjp(name, fn, rows, params, cts, n_out, d_rows=(), d_shifts=(), d_params=(), adds=(), ct_adds=(),
            shifts=(), seq=None, per_b=()):
    rows, params, cts = list(rows), list(params), list(cts)
    T = rows[0].shape[0]
    seq = seq or T
    row_bytes = sum(r.shape[1] * r.dtype.itemsize for r in rows + cts + [a for _, a, _ in list(adds) + list(ct_adds)])
    row_bytes += sum(rows[j].shape[1] * jnp.dtype(dt).itemsize for j, dt in d_rows)
    row_bytes += sum(rows[shifts[j][0]].shape[1] * jnp.dtype(dt).itemsize for j, dt in d_shifts)
    tm = _row_tile(seq, row_bytes)
    nb, ns = T // seq, seq // tm
    n_ga = len(adds)
    adds = list(adds) + list(ct_adds)
    add_rows = [a for _, a, _ in adds]
    add_shifts = [(j, k) for j, (_, _, k) in enumerate(adds) if k != 0]
    all_rows = rows + cts + add_rows
    off_ct, off_add = len(rows), len(rows) + len(cts)
    all_shifts = list(shifts) + [(off_add + j, k) for j, k in add_shifts]
    in_specs, halo_keys = _row_specs(all_rows, all_shifts, params, per_b, seq, tm)
    n_in = len(in_specs)
    n_dr, n_ds = len(d_rows), len(d_shifts)

    def body(*refs):
        pos, row_t, sh_t, par_t = _load_tiles(refs, all_rows, all_shifts, halo_keys, params, seq, tm)
        fn_rows, ct_t, add_t = row_t[:off_ct], row_t[off_ct:off_add], row_t[off_add:]
        fn_sh, add_sh = sh_t[:len(shifts)], sh_t[len(shifts):]
        diff = [fn_rows[j] for j, _ in d_rows] + [fn_sh[j] for j, _ in d_shifts] + [par_t[j] for j in d_params]

        def f(*d):
            r2, s2, p2 = list(fn_rows), list(fn_sh), list(par_t)
            for (j, _), v in zip(d_rows, d[:n_dr]):
                r2[j] = v
            for (j, _), v in zip(d_shifts, d[n_dr:n_dr + n_ds]):
                s2[j] = v
            for j, v in zip(d_params, d[n_dr + n_ds:]):
                p2[j] = v
            return tuple(fn(pos, *r2, *s2, *p2)[:n_out])

        sh_iter = iter(add_sh)
        add_v = [add_t[j] if k == 0 else next(sh_iter) for j, (_, _, k) in enumerate(adds)]
        ct_t = list(ct_t)
        for (tgt, _, _), v in zip(adds[n_ga:], add_v[n_ga:]):
            ct_t[tgt] = ct_t[tgt] + v
        outs_v, vjp_fn = jax.vjp(f, *diff)
        grads = list(vjp_fn(tuple(c.astype(o.dtype) for c, o in zip(ct_t, outs_v))))
        for (tgt, _, _), v in zip(adds[:n_ga], add_v[:n_ga]):
            grads[tgt] = grads[tgt] + v
        out_refs = refs[n_in:]
        for o_ref, g in zip(out_refs[:n_dr + n_ds], grads[:n_dr + n_ds]):
            o_ref[...] = g.astype(o_ref.dtype)
        for o_ref, g, j in zip(out_refs[n_dr + n_ds:], grads[n_dr + n_ds:], d_params):
            first = (pl.program_id(1) == 0) if j in per_b else ((pl.program_id(0) == 0) & (pl.program_id(1) == 0))

            @pl.when(first)
            def _(o_ref=o_ref):
                o_ref[...] = jnp.zeros_like(o_ref)
            o_ref[...] += g.astype(F32)

    out_shape, out_specs = [], []
    for j, dt in d_rows:
        out_shape.append(jax.ShapeDtypeStruct(rows[j].shape, dt))
        out_specs.append(pl.BlockSpec((tm, rows[j].shape[1]), lambda b, i: (b * ns + i, 0)))
    for j, dt in d_shifts:
        src = rows[shifts[j][0]]
        out_shape.append(jax.ShapeDtypeStruct(src.shape, dt))
        out_specs.append(pl.BlockSpec((tm, src.shape[1]), lambda b, i: (b * ns + i, 0)))
    for j in d_params:
        p = params[j]
        out_shape.append(jax.ShapeDtypeStruct(p.shape, F32))
        if j in per_b:
            out_specs.append(pl.BlockSpec((None,) + p.shape[1:], lambda b, i: (b, 0, 0)))
        else:
            out_specs.append(pl.BlockSpec(p.shape, lambda b, i, nd=p.ndim: (0,) * nd))
    return pl.pallas_call(body, name=name, grid=(nb, ns), in_specs=in_specs, out_specs=out_specs,
                          out_shape=out_shape, compiler_params=_cparams("arbitrary", "arbitrary"))(*all_rows, *[
                              all_rows[idx] for idx, _ in halo_keys], *params)


MM_TILE = 1024
MM_DEEP = 2048


def _mm_nn(name, x, w, out_dtype):
    M, K = x.shape
    N = w.shape[1]
    tm, tn = min(MM_TILE if K <= MM_DEEP else MM_TILE // 2, M), min(MM_TILE, N)

    def body(x_ref, w_ref, o_ref):
        o_ref[...] = jnp.dot(x_ref[...].astype(BF16), w_ref[...].astype(BF16),
                             preferred_element_type=F32).astype(o_ref.dtype)

    return pl.pallas_call(body, name=name, grid=(M // tm, N // tn),
                          in_specs=[pl.BlockSpec((tm, K), lambda i, j: (i, 0)), pl.BlockSpec((K, tn), lambda i, j: (0, j))],
                          out_specs=pl.BlockSpec((tm, tn), lambda i, j: (i, j)),
                          out_shape=jax.ShapeDtypeStruct((M, N), out_dtype),
                          compiler_params=_cparams("parallel", "parallel"))(x, w)


def _mm_up_act(name, x, w):
    M, K = x.shape
    N = w.shape[1]
    tm, tn = min(MM_TILE if K <= MM_DEEP else MM_TILE // 2, M), min(MM_TILE, N)

    def body(x_ref, w_ref, up_ref, act_ref):
        up = jnp.dot(x_ref[...].astype(BF16), w_ref[...].astype(BF16), preferred_element_type=F32)
        up_ref[...] = up.astype(BF16)
        act_ref[...] = jnp.square(jnp.maximum(up, 0.0)).astype(BF16)

    out = pl.BlockSpec((tm, tn), lambda i, j: (i, j))
    return pl.pallas_call(body, name=name, grid=(M // tm, N // tn),
                          in_specs=[pl.BlockSpec((tm, K), lambda i, j: (i, 0)), pl.BlockSpec((K, tn), lambda i, j: (0, j))],
                          out_specs=[out, out], out_shape=[jax.ShapeDtypeStruct((M, N), BF16)] * 2,
                          compiler_params=_cparams("parallel", "parallel"))(x, w)


def _mm_nt_act_bwd(name, dy, w, up):
    M, N = dy.shape
    K = w.shape[0]
    tm, tk = min(MM_TILE if N <= MM_DEEP else MM_TILE // 2, M), min(MM_TILE, K)

    def body(dy_ref, w_ref, up_ref, o_ref):
        dact = lax.dot_general(dy_ref[...].astype(BF16), w_ref[...].astype(BF16), (((1,), (1,)), ((), ())),
                               preferred_element_type=F32)
        o_ref[...] = (dact * (2.0 * jnp.maximum(up_ref[...].astype(F32), 0.0))).astype(o_ref.dtype)

    out = pl.BlockSpec((tm, tk), lambda i, j: (i, j))
    return pl.pallas_call(body, name=name, grid=(M // tm, K // tk),
                          in_specs=[pl.BlockSpec((tm, N), lambda i, j: (i, 0)), pl.BlockSpec((tk, N), lambda i, j: (j, 0)), out],
                          out_specs=out, out_shape=jax.ShapeDtypeStruct((M, K), BF16),
                          compiler_params=_cparams("parallel", "parallel"))(dy, w, up)


def _mm_nt(name, dy, w, out_dtype):
    M, N = dy.shape
    K = w.shape[0]
    tm, tk = min(MM_TILE if N <= MM_DEEP else MM_TILE // 2, M), min(MM_TILE, K)

    def body(dy_ref, w_ref, o_ref):
        o_ref[...] = lax.dot_general(dy_ref[...].astype(BF16), w_ref[...].astype(BF16), (((1,), (1,)), ((), ())),
                                     preferred_element_type=F32).astype(o_ref.dtype)

    return pl.pallas_call(body, name=name, grid=(M // tm, K // tk),
                          in_specs=[pl.BlockSpec((tm, N), lambda i, j: (i, 0)), pl.BlockSpec((tk, N), lambda i, j: (j, 0))],
                          out_specs=pl.BlockSpec((tm, tk), lambda i, j: (i, j)),
                          out_shape=jax.ShapeDtypeStruct((M, K), out_dtype),
                          compiler_params=_cparams("parallel", "parallel"))(dy, w)


def _mm_tn(name, x, dy):
    M, K = x.shape
    N = dy.shape[1]
    tm, tk, tn = min(MM_TILE, M), min(MM_TILE, K), min(MM_TILE, N)

    def body(x_ref, dy_ref, o_ref):
        @pl.when(pl.program_id(2) == 0)
        def _():
            o_ref[...] = jnp.zeros_like(o_ref)
        o_ref[...] += lax.dot_general(x_ref[...].astype(BF16), dy_ref[...].astype(BF16), (((0,), (0,)), ((), ())),
                                      preferred_element_type=F32)

    return pl.pallas_call(body, name=name, grid=(K // tk, N // tn, M // tm),
                          in_specs=[pl.BlockSpec((tm, tk), lambda i, j, m: (m, i)), pl.BlockSpec((tm, tn), lambda i, j, m: (m, j))],
                          out_specs=pl.BlockSpec((tk, tn), lambda i, j, m: (i, j)),
                          out_shape=jax.ShapeDtypeStruct((K, N), F32),
                          compiler_params=_cparams("parallel", "parallel", "arbitrary"))(x, dy)


def _lru_scan(name, a, b, seq, ct=128):
    T, C = a.shape

    def body(a_ref, b_ref, h_ref):
        A, Bv = a_ref[...], b_ref[...]
        row = lax.broadcasted_iota(jnp.int32, (seq, 1), 0)
        k = 1
        while k < seq:
            keep = row >= k
            Bv = A * jnp.where(keep, pltpu.roll(Bv, k, 0), 0.0) + Bv
            A = A * jnp.where(keep, pltpu.roll(A, k, 0), 1.0)
            k *= 2
        h_ref[...] = Bv

    spec = pl.BlockSpec((seq, ct), lambda b_, j: (b_, j))
    return pl.pallas_call(body, name=name, grid=(T // seq, C // ct), in_specs=[spec, spec], out_specs=spec,
                          out_shape=jax.ShapeDtypeStruct((T, C), F32),
                          compiler_params=_cparams("parallel", "parallel"))(a, b)


def _lru_scan_bwd(name, a, h, dh, seq, ct=128):
    T, C = a.shape

    def body(a_ref, h_ref, dh_ref, da_ref, db_ref):
        row = lax.broadcasted_iota(jnp.int32, (seq, 1), 0)
        Cf = jnp.where(row < seq - 1, pltpu.roll(a_ref[...], seq - 1, 0), 0.0)
        G = dh_ref[...]
        k = 1
        while k < seq:
            keep = row + k < seq
            G = Cf * jnp.where(keep, pltpu.roll(G, seq - k, 0), 0.0) + G
            Cf = Cf * jnp.where(keep, pltpu.roll(Cf, seq - k, 0), 1.0)
            k *= 2
        db_ref[...] = G
        da_ref[...] = G * jnp.where(row >= 1, pltpu.roll(h_ref[...], 1, 0), 0.0)

    spec = pl.BlockSpec((seq, ct), lambda b_, j: (b_, j))
    return pl.pallas_call(body, name=name, grid=(T // seq, C // ct), in_specs=[spec] * 3, out_specs=[spec] * 2,
                          out_shape=[jax.ShapeDtypeStruct((T, C), F32)] * 2,
                          compiler_params=_cparams("parallel", "parallel"))(a, h, dh)


def _pair_consts():
    jj = lax.broadcasted_iota(jnp.int32, (RWKV_N, LANES), 0)
    ll = lax.broadcasted_iota(jnp.int32, (RWKV_N, LANES), 1)
    pick = ((ll & (RWKV_N - 1)) == jj).astype(BF16)
    l0 = lax.broadcasted_iota(jnp.int32, (LANES, LANES), 0)
    l1 = lax.broadcasted_iota(jnp.int32, (LANES, LANES), 1)
    same = ((l0 >> 6) == (l1 >> 6)).astype(BF16)
    return pick, same


N_COL = 5


def _cols_raw(rows, pick, same):
    lhs = [pick * jnp.broadcast_to(x, (RWKV_N, LANES)).astype(BF16) for x in rows]
    out = jnp.dot(jnp.concatenate(lhs, axis=0), same, preferred_element_type=F32)
    return tuple(out[i * RWKV_N:(i + 1) * RWKV_N] for i in range(len(rows)))


def _cols_adjoint(g):
    n = len(g)
    g = list(g) + [jnp.zeros_like(g[0])] * (n % 2)
    head = ((lax.broadcasted_iota(jnp.int32, (SUB, LANES), 1) >> 6)
            == lax.broadcasted_iota(jnp.int32, (SUB, LANES), 0)).astype(BF16)
    low = lax.broadcasted_iota(jnp.int32, (1, LANES), 1) < RWKV_N
    sums = lax.dot_general(head, jnp.concatenate([t.astype(BF16) for t in g], axis=0),
                           (((1,), (1,)), ((), ())), preferred_element_type=F32)
    rows = []
    for i in range(0, n, 2):
        blk = sums[:, i * RWKV_N:(i + 2) * RWKV_N]
        swapped = pltpu.roll(blk, RWKV_N, 1)
        rows.append(jnp.where(low, blk[0:1], swapped[1:2]))
        rows.append(jnp.where(low, swapped[0:1], blk[1:2]))
    return tuple(rows[:n])


def _rwkv_cols(rows, pick, same):
    flat = []
    for r, w, k, v, a, b in rows:
        flat += [a, 1.0 - w, b, k, r]
    tiles = _cols_raw(tuple(flat), pick, same)
    return [tiles[N_COL * p:N_COL * (p + 1)] for p in range(len(rows))]


def _rwkv_step(s0, tiles, v):
    ca, ce, cb, ck, cr = tiles
    sa = jnp.sum(s0 * ca, axis=0, keepdims=True)
    s1 = s0 - s0 * ce + cb * sa + ck * v
    return s1, jnp.sum(s1 * cr, axis=0, keepdims=True), sa


def _rwkv_step_bwd(s0, s1, sa, tiles, v, dy, ds1):
    ca, ce, cb, ck, cr = tiles
    ds1 = ds1 + cr * dy
    dsa = jnp.sum(ds1 * cb, axis=0, keepdims=True)
    dv = jnp.sum(ds1 * ck, axis=0, keepdims=True)
    ds0 = ds1 - ds1 * ce + ca * dsa
    return ds0, (s0 * dsa, ds1 * s0, ds1 * sa, ds1 * v, s1 * dy), dv


def _rwkv_specs(C, seq, tc, G, reverse):
    nc = seq // tc
    if reverse:
        row = pl.BlockSpec((tc, LANES * G), lambda b, g, c: (b * nc + nc - 1 - c, g))
        st = pl.BlockSpec((tc, G, RWKV_N, LANES), lambda b, g, c: (b * nc + nc - 1 - c, g, 0, 0))
    else:
        row = pl.BlockSpec((tc, LANES * G), lambda b, g, c: (b * nc + c, g))
        st = pl.BlockSpec((tc, G, RWKV_N, LANES), lambda b, g, c: (b * nc + c, g, 0, 0))
    return row, st


def _rwkv_fwd(name, r, w, k, v, a, b, seq, tc=32, G=8):
    T, C = r.shape
    tc = min(tc, seq)
    NP = C // LANES
    row, st = _rwkv_specs(C, seq, tc, G, False)
    last = pl.BlockSpec((None, G, RWKV_N, LANES), lambda b_, g, c: (b_, g, 0, 0))

    def body(r_ref, w_ref, k_ref, v_ref, a_ref, b_ref, y_ref, st_ref, sa_ref, last_ref, s_scr):
        @pl.when(pl.program_id(2) == 0)
        def _():
            s_scr[...] = jnp.zeros_like(s_scr)
        pick, same = _pair_consts()
        rid = lax.broadcasted_iota(jnp.int32, (SUB, LANES), 0)

        def group(t8, carry):
            base = pl.multiple_of(t8 * SUB, SUB)
            lanes = [pl.ds(LANES * p, LANES) for p in range(G)]
            blk = [[x[pl.ds(base, SUB), ln] for x in (r_ref, w_ref, k_ref, v_ref, a_ref, b_ref)] for ln in lanes]
            s = [s_scr[p] for p in range(G)]
            ys = [jnp.zeros((SUB, LANES), F32) for _ in range(G)]
            sas = [jnp.zeros((SUB, LANES), F32) for _ in range(G)]
            for i in range(SUB):
                tiles = _rwkv_cols([tuple(b_[i:i + 1] for b_ in blk[p]) for p in range(G)], pick, same)
                for p in range(G):
                    st_ref[base + i, p] = s[p]
                    s[p], y, sa = _rwkv_step(s[p], tiles[p], blk[p][3][i:i + 1])
                    ys[p] = jnp.where(rid == i, y, ys[p])
                    sas[p] = jnp.where(rid == i, sa, sas[p])
            for p in range(G):
                s_scr[p] = s[p]
                y_ref[pl.ds(base, SUB), lanes[p]] = ys[p]
                sa_ref[pl.ds(base, SUB), lanes[p]] = sas[p]
            return carry

        lax.fori_loop(0, tc // SUB, group, 0)

        @pl.when(pl.program_id(2) == seq // tc - 1)
        def _():
            last_ref[...] = s_scr[...]

    return pl.pallas_call(body, name=name, grid=(T // seq, NP // G, seq // tc), in_specs=[row] * 6,
                          out_specs=[row, st, row, last],
                          out_shape=[jax.ShapeDtypeStruct((T, C), F32), jax.ShapeDtypeStruct((T, NP, RWKV_N, LANES), F32),
                                     jax.ShapeDtypeStruct((T, C), F32),
                                     jax.ShapeDtypeStruct((T // seq, NP, RWKV_N, LANES), F32)],
                          scratch_shapes=[pltpu.VMEM((G, RWKV_N, LANES), F32)],
                          compiler_params=_cparams("parallel", "parallel", "arbitrary"))(r, w, k, v, a, b)


def _rwkv_bwd(name, r, w, k, v, a, b, states, sa, last, dy, dr_p, dk_p, dv_p, seq, tc=32, G=8):
    T, C = r.shape
    tc = min(tc, seq)
    NP = C // LANES
    row, st = _rwkv_specs(C, seq, tc, G, True)
    last_spec = pl.BlockSpec((None, G, RWKV_N, LANES), lambda b_, g, c: (b_, g, 0, 0))

    def body(r_ref, w_ref, k_ref, v_ref, a_ref, b_ref, st_ref, sa_ref, last_ref, dy_ref, drp_ref, dkp_ref, dvp_ref,
             dr_ref, dw_ref, dk_ref, dv_ref, da_ref, db_ref, ds_scr, s1_scr):
        @pl.when(pl.program_id(2) == 0)
        def _():
            ds_scr[...] = jnp.zeros_like(ds_scr)
            s1_scr[...] = last_ref[...]
        pick, same = _pair_consts()
        rid = lax.broadcasted_iota(jnp.int32, (SUB, LANES), 0)
        out_refs = (da_ref, dw_ref, db_ref, dk_ref, dr_ref)

        def group(n, carry):
            base = pl.multiple_of((tc // SUB - 1 - n) * SUB, SUB)
            lanes = [pl.ds(LANES * p, LANES) for p in range(G)]
            blk = [[x[pl.ds(base, SUB), ln] for x in (r_ref, w_ref, k_ref, v_ref, a_ref, b_ref)] for ln in lanes]
            dys = [dy_ref[pl.ds(base, SUB), ln] for ln in lanes]
            sas = [sa_ref[pl.ds(base, SUB), ln] for ln in lanes]
            ds = [ds_scr[p] for p in range(G)]
            s1 = [s1_scr[p] for p in range(G)]
            zero = jnp.zeros((SUB, LANES), F32)
            gs = [[zero, zero, zero, dkp_ref[pl.ds(base, SUB), ln], drp_ref[pl.ds(base, SUB), ln]] for ln in lanes]
            gv = [dvp_ref[pl.ds(base, SUB), ln] for ln in lanes]
            for i2 in reversed(range(0, SUB, 8)):
                steps = tuple(reversed(range(i2, i2 + 8)))
                tiles2 = [_rwkv_cols([tuple(b_[i:i + 1] for b_ in blk[p]) for p in range(G)], pick, same) for i in steps]
                adj2 = []
                for i, tiles in zip(steps, tiles2):
                    adj = []
                    for p in range(G):
                        s0 = st_ref[base + i, p]
                        ds[p], g_tiles, dv = _rwkv_step_bwd(s0, s1[p], sas[p][i:i + 1], tiles[p], blk[p][3][i:i + 1],
                                                            dys[p][i:i + 1], ds[p])
                        s1[p] = s0
                        adj += list(g_tiles)
                        gv[p] = gv[p] + jnp.where(rid == i, dv, 0.0)
                    adj2.append(adj)
                for i, adj in zip(steps, adj2):
                    rows = _cols_adjoint(adj)
                    for p in range(G):
                        for j in range(N_COL):
                            gs[p][j] = gs[p][j] + jnp.where(rid == i, rows[N_COL * p + j], 0.0)
            for p in range(G):
                ds_scr[p] = ds[p]
                s1_scr[p] = s1[p]
                dv_ref[pl.ds(base, SUB), lanes[p]] = gv[p]
                for j in range(N_COL):
                    out_refs[j][pl.ds(base, SUB), lanes[p]] = gs[p][j]
            return carry

        lax.fori_loop(0, tc // SUB, group, 0)

    return pl.pallas_call(body, name=name, grid=(T // seq, NP // G, seq // tc),
                          in_specs=[row] * 6 + [st, row, last_spec] + [row] * 4, out_specs=[row] * 6,
                          out_shape=[jax.ShapeDtypeStruct((T, C), F32)] * 6,
                          scratch_shapes=[pltpu.VMEM((G, RWKV_N, LANES), F32)] * 2,
                          compiler_params=_cparams("parallel", "parallel", "arbitrary"))(
                              r, w, k, v, a, b, states, sa, last, dy, dr_p, dk_p, dv_p)


def _all_gather(name, shard):
    def body(x_ref, o_ref, send_sems, recv_sems, local_sem):
        x, y, c = lax.axis_index("x"), lax.axis_index("y"), lax.axis_index("c")
        me, sibling = (x, y, c), (x, y, 1 - c)
        chips = [(1 - x, y), (x, 1 - y), (1 - x, 1 - y)]

        def copy(k, block, to, src=None):
            dst = o_ref.at[4 * block[0] + 2 * block[1] + block[2]]
            return pltpu.make_async_remote_copy(src_ref=dst if src is None else src, dst_ref=dst, send_sem=send_sems.at[k],
                                                recv_sem=recv_sems.at[k], device_id=to, device_id_type=MESH)

        mine = pltpu.make_async_copy(x_ref, o_ref.at[4 * x + 2 * y + c], local_sem)
        mine.start()
        first = [copy(0, me, sibling, src=x_ref)] + [copy(1 + j, me, (*chip, c), src=x_ref) for j, chip in enumerate(chips)]
        for cp in first:
            cp.start()
        passed = [copy(4 + j, (*chip, c), sibling) for j, chip in enumerate(chips)]
        for j, chip in enumerate(chips):
            copy(1 + j, (*chip, c), me).wait_recv()
            passed[j].start()
        copy(0, sibling, me).wait_recv()
        for j, chip in enumerate(chips):
            copy(4 + j, (*chip, 1 - c), me).wait_recv()
        for cp in first + passed:
            cp.wait_send()
        mine.wait()

    hbm = pl.BlockSpec(memory_space=pltpu.HBM)
    return pl.pallas_call(body, name=name, in_specs=[hbm], out_specs=hbm,
                          out_shape=jax.ShapeDtypeStruct((N_DEV,) + shard.shape, shard.dtype),
                          scratch_shapes=[pltpu.SemaphoreType.DMA((N_DEV - 1,)), pltpu.SemaphoreType.DMA((N_DEV - 1,)),
                                          pltpu.SemaphoreType.DMA])(shard)


N_CHIP = 4


def _swap_with_sibling(name, blocks, n_split):
    _, R, W = blocks.shape
    N = N_CHIP
    rq = R // n_split

    def body(x_ref, got_ref, send_sems, recv_sems):
        x, y, c = lax.axis_index("x"), lax.axis_index("y"), lax.axis_index("c")
        copies = []
        for j in range(N):
            for q in range(n_split):
                rows = pl.ds(q * rq, rq)
                cp = pltpu.make_async_remote_copy(src_ref=x_ref.at[2 * j + 1 - c, rows], dst_ref=got_ref.at[j, rows],
                                                  send_sem=send_sems.at[j * n_split + q], recv_sem=recv_sems.at[j * n_split + q],
                                                  device_id=(x, y, 1 - c), device_id_type=MESH)
                cp.start()
                copies.append(cp)
        for cp in copies:
            cp.wait()

    hbm = pl.BlockSpec(memory_space=pltpu.HBM)
    return pl.pallas_call(body, name=name, in_specs=[hbm], out_specs=hbm,
                          out_shape=jax.ShapeDtypeStruct((N, R, W), blocks.dtype),
                          scratch_shapes=[pltpu.SemaphoreType.DMA((N * n_split,)), pltpu.SemaphoreType.DMA((N * n_split,))])(blocks)


def _add_own_blocks(name, blocks, got, tr, out_dtype):
    N, R, W = got.shape

    def body(c_ref, a_ref, b_ref, o_ref):
        o_ref[...] = (a_ref[...] + b_ref[...]).astype(o_ref.dtype)

    grid_spec = pltpu.PrefetchScalarGridSpec(
        num_scalar_prefetch=1, grid=(N, R // tr),
        in_specs=[pl.BlockSpec((None, tr, W), lambda n, i, c: (2 * n + c[0], i, 0)),
                  pl.BlockSpec((None, tr, W), lambda n, i, c: (n, i, 0))],
        out_specs=pl.BlockSpec((None, tr, W), lambda n, i, c: (n, i, 0)))
    core = lax.axis_index("c").astype(jnp.int32).reshape(1)
    return pl.pallas_call(body, name=name, grid_spec=grid_spec, out_shape=jax.ShapeDtypeStruct(got.shape, out_dtype),
                          compiler_params=_cparams("parallel", "parallel"))(core, blocks, got)


def _exchange_chips(name, blocks):
    def body(z_ref, o_ref, send_sems, recv_sems, local_sem):
        x, y, c = lax.axis_index("x"), lax.axis_index("y"), lax.axis_index("c")
        chip = 2 * x + y
        mine = pltpu.make_async_copy(z_ref.at[chip], o_ref.at[chip], local_sem)
        mine.start()
        copies = []
        for k in range(1, N_CHIP):
            to = (x ^ (k >> 1), y ^ (k & 1), c)
            cp = pltpu.make_async_remote_copy(src_ref=z_ref.at[chip ^ k], dst_ref=o_ref.at[chip], send_sem=send_sems.at[k - 1],
                                              recv_sem=recv_sems.at[k - 1], device_id=to, device_id_type=MESH)
            cp.start()
            copies.append(cp)
        for k in range(1, N_CHIP):
            pltpu.make_async_remote_copy(src_ref=z_ref.at[chip], dst_ref=o_ref.at[chip ^ k], send_sem=send_sems.at[k - 1],
                                         recv_sem=recv_sems.at[k - 1], device_id=(x ^ (k >> 1), y ^ (k & 1), c),
                                         device_id_type=MESH).wait_recv()
        for cp in copies:
            cp.wait_send()
        mine.wait()

    hbm = pl.BlockSpec(memory_space=pltpu.HBM)
    return pl.pallas_call(body, name=name, in_specs=[hbm], out_specs=hbm,
                          out_shape=jax.ShapeDtypeStruct(blocks.shape, blocks.dtype),
                          scratch_shapes=[pltpu.SemaphoreType.DMA((N_CHIP - 1,)), pltpu.SemaphoreType.DMA((N_CHIP - 1,)),
                                          pltpu.SemaphoreType.DMA])(blocks)


def _adamw(name, parts, w, m, v, tr):
    R, W = w.shape
    n_parts = parts.shape[0]

    def body(p_ref, w_ref, m_ref, v_ref, g_ref, d_ref, nm_ref, nv_ref):
        g = p_ref[0].astype(F32)
        for s in range(1, n_parts):
            g = g + p_ref[s].astype(F32)
        nm = ADAM_B1 * m_ref[...] + (1.0 - ADAM_B1) * g
        nv = ADAM_B2 * v_ref[...] + (1.0 - ADAM_B2) * (g * g)
        m_hat = nm / (1.0 - ADAM_B1 ** ADAM_STEP)
        v_hat = nv / (1.0 - ADAM_B2 ** ADAM_STEP)
        g_ref[...] = g
        d_ref[...] = -ADAM_LR * (m_hat / (jnp.sqrt(v_hat) + ADAM_EPS) + ADAM_WD * w_ref[...])
        nm_ref[...] = nm
        nv_ref[...] = nv

    spec = pl.BlockSpec((tr, W), lambda i: (i, 0))
    return pl.pallas_call(body, name=name, grid=(R // tr,),
                          in_specs=[pl.BlockSpec((n_parts, tr, W), lambda i: (0, i, 0)), spec, spec, spec],
                          out_specs=[spec] * 4, out_shape=[jax.ShapeDtypeStruct((R, W), F32)] * 4,
                          compiler_params=_cparams("parallel"))(parts, w, m, v)


def _rms(x, g):
    return x * lax.rsqrt(jnp.mean(x * x, axis=-1, keepdims=True) + RMS_EPS) * g


def _softplus(z):
    return jnp.maximum(z, 0.0) + jnp.log(1.0 + jnp.exp(-jnp.abs(z)))


def _neg_expm1(z):
    series = -z * (1.0 + z * (0.5 + z * (1.0 / 6.0)))
    return jnp.where(z > -1e-3, series, 1.0 - jnp.exp(z))


def _gelu_tanh(x):
    return 0.5 * x * (1.0 + jnp.tanh(0.7978845608028654 * (x + 0.044715 * x * x * x)))


def _bdot(x, w):
    return jnp.dot(x.astype(BF16), w.astype(BF16), preferred_element_type=F32)


def _seg_consts(C):
    c0 = lax.broadcasted_iota(jnp.int32, (C, LANES), 0)
    h1 = lax.broadcasted_iota(jnp.int32, (C, LANES), 1)
    red = ((c0 >> 6) == h1).astype(BF16)
    h0 = lax.broadcasted_iota(jnp.int32, (LANES, C), 0)
    c1 = lax.broadcasted_iota(jnp.int32, (LANES, C), 1)
    exp = ((c1 >> 6) == h0).astype(BF16)
    return red, exp


def _split_dot(x, m):
    hi = x.astype(BF16)
    lo = (x - hi.astype(F32)).astype(BF16)
    return jnp.dot(hi, m, preferred_element_type=F32) + jnp.dot(lo, m, preferred_element_type=F32)


def _segsum_raw(x):
    red, exp = _seg_consts(x.shape[1])
    return _split_dot(_split_dot(x, red), exp)


@jax.custom_vjp
def _segsum(x):
    return _segsum_raw(x)


_segsum.defvjp(lambda x: (_segsum_raw(x), None), lambda _, g: (_segsum_raw(g),))


def _f_pre(out_dtype):
    def fn(pos, x, g):
        return (_rms(x, g).astype(out_dtype),)
    return fn


def _f_post(pos, x, t, g):
    return (x + _rms(t, g),)


def _f_post_pre(out_dtype, has_bias):
    def fn(pos, x, t, g_post, g_pre, *bias):
        x1 = x + _rms(t + bias[0] if has_bias else t, g_post)
        return x1, _rms(x1, g_pre).astype(out_dtype)
    return fn


def _f_conv(pos, proj, p1, p2, p3, b_u, w0, w1, w2, w3, cb):
    D = b_u.shape[1]
    u0 = proj[:, D:] + b_u
    u1 = jnp.where(pos >= 1, p1[:, D:] + b_u, 0.0)
    u2 = jnp.where(pos >= 2, p2[:, D:] + b_u, 0.0)
    u3 = jnp.where(pos >= 3, p3[:, D:] + b_u, 0.0)
    return (cb + u3 * w0 + u2 * w1 + u1 * w2 + u0 * w3,)


def _f_conv_bwd(pos, dconv, proj, dyb, n1, n2, n3, p1, p2, p3, b_u, w0, w1, w2, w3):
    D = b_u.shape[1]
    du = dconv * w3 + n1 * w2 + n2 * w1 + n3 * w0
    dproj = jnp.concatenate([dyb[:, :D], du], axis=1)
    u0 = proj[:, D:] + b_u
    u1 = jnp.where(pos >= 1, p1[:, D:] + b_u, 0.0)
    u2 = jnp.where(pos >= 2, p2[:, D:] + b_u, 0.0)
    u3 = jnp.where(pos >= 3, p3[:, D:] + b_u, 0.0)
    rs = lambda z: jnp.sum(z, axis=0, keepdims=True)
    return (dproj, rs(dproj), rs(dconv * u3), rs(dconv * u2), rs(dconv * u1), rs(dconv * u0), rs(dconv))


def _f_gates(pos, conv, proj, gate_w, gb_r, gb_i, lam, b_y):
    D = conv.shape[1]
    blk = D // LRU_HEADS
    cb = conv.astype(BF16)

    def gate(g, bias):
        z = [jnp.dot(cb[:, h * blk:(h + 1) * blk], gate_w[(g * LRU_HEADS + h) * blk:(g * LRU_HEADS + h + 1) * blk, :].astype(BF16),
                     preferred_element_type=F32) for h in range(LRU_HEADS)]
        return jax.nn.sigmoid(jnp.concatenate(z, axis=1) + bias)

    r_gate, i_gate = gate(0, gb_r), gate(1, gb_i)
    log_a = -LRU_C * r_gate * _softplus(-lam)
    a = jnp.exp(log_a)
    b = jnp.sqrt(_neg_expm1(2.0 * log_a)) * i_gate * conv
    yb = _gelu_tanh(proj[:, :D] + b_y)
    return a, b, yb


def _f_hy(pos, h, yb):
    return ((h * yb).astype(BF16),)


def _f_mix(pos, hn, hp, mu_r, mu_w, mu_k, mu_v, mu_a, mu_g, w0, w1, w2, a0, a1, a2, g1, g2):
    xx = hp - hn
    xr, xw, xk, xv, xa, xg = (hn + xx * m for m in (mu_r, mu_w, mu_k, mu_v, mu_a, mu_g))
    lw = w0 + _bdot(jnp.tanh(_bdot(xw, w1)), w2)
    decay = jnp.exp(-jnp.exp(-_softplus(-lw) - 0.5))
    a = jax.nn.sigmoid(a0 + _bdot(_bdot(xa, a1), a2))
    g = _bdot(jax.nn.sigmoid(_bdot(xg, g1)), g2)
    return xr.astype(BF16), xk.astype(BF16), xv.astype(BF16), decay, a, g


def _f_kk(pos, k, a, k_k, k_a):
    kk = k * k_k
    kk = kk / jnp.maximum(jnp.sqrt(_segsum(kk * kk)), 1e-12)
    return k * (1.0 + (a - 1.0) * k_a), -kk, kk * a


def _f_gn(pos, y, r, k2, v, g, gn_g, gn_b, r_k):
    inv_n = 1.0 / RWKV_N
    yc = y - _segsum(y) * inv_n
    var = _segsum(yc * yc) * inv_n
    yn = yc * lax.rsqrt(var + GN_EPS) * gn_g + gn_b
    bonus = _segsum(r * k2 * r_k) * v
    return (((yn + bonus) * g).astype(BF16),)


def _f_attn(pos, q, kv):
    D = q.shape[1]
    hd = D // XATTN_HEADS
    qb, kb, vb = q.astype(BF16), kv[:, :D].astype(BF16), kv[:, D:].astype(BF16)
    outs = []
    for h in range(XATTN_HEADS):
        sl = slice(h * hd, (h + 1) * hd)
        s = lax.dot_general(qb[:, sl], kb[:, sl], (((1,), (1,)), ((), ())), preferred_element_type=F32) * (hd ** -0.5)
        s = s - jnp.max(s, axis=-1, keepdims=True)
        e = jnp.exp(s)
        p = e / jnp.sum(e, axis=-1, keepdims=True)
        outs.append(jnp.dot(p.astype(BF16), vb[:, sl], preferred_element_type=F32))
    return (jnp.concatenate(outs, axis=1).astype(BF16),)


def _f_loss(pos, y, target):
    d = y - target
    inv = 1.0 / y.shape[1]
    part = 0.5 * inv * jnp.sum(jnp.sum(d * d, axis=1, keepdims=True), axis=0, keepdims=True)
    return d * inv, jnp.broadcast_to(part, (8, LANES))


def _full_from_blocks(blocks, shard_shape, ax):
    g = jnp.moveaxis(blocks.reshape((N_DEV,) + tuple(shard_shape)), 0, ax)
    return g.reshape(tuple(shard_shape[:ax]) + (N_DEV * shard_shape[ax],) + tuple(shard_shape[ax + 1:]))


def _blocks_from_full(full, shard_shape, ax):
    g = full.reshape(tuple(shard_shape[:ax]) + (N_DEV, shard_shape[ax]) + tuple(shard_shape[ax + 1:]))
    return jnp.moveaxis(g, ax, 0).reshape(N_DEV, -1)


def _numel(shape):
    n = 1
    for s in shape:
        n *= s
    return n


def _pack_flat(arrs, total):
    flat = jnp.concatenate([a.reshape(-1).astype(F32) for a in arrs])
    return jnp.pad(flat, (0, total - flat.shape[0]))


def _train_step(x3, mem3, target3, P, M_, V_):
    Bl, S, D = x3.shape
    T = Bl * S
    ML = mem3.shape[1]
    x = x3.reshape(T, D)
    mem = mem3.reshape(Bl * ML, D)
    target = target3.reshape(T, D)
    shp = {n: P[n].shape for n in WEIGHTS}

    n_big = sum(_numel(shp[n]) for n in BIG)
    n_small = sum(_numel(shp[n]) for n in SMALL)
    n_small_pad = -(-n_small // (PACK_ROWS * PACK_W)) * (PACK_ROWS * PACK_W)
    big_rows, small_rows = n_big // PACK_W, n_small_pad // PACK_W
    def pack_rows(src):
        big = [src[n].reshape(-1, PACK_W) for n in BIG]
        return jnp.concatenate(big + [_pack_flat([src[n] for n in SMALL], n_small_pad).reshape(small_rows, PACK_W)], axis=0)

    big_pack = jnp.concatenate([P[n].reshape(-1, PACK_W) for n in BIG], axis=0).astype(BF16)
    small_pack = _pack_flat([P[n] for n in SMALL], n_small_pad).reshape(small_rows, PACK_W)
    big_all = _all_gather("gather_matrices", big_pack)
    small_all = _all_gather("gather_vectors", small_pack).reshape(N_DEV, n_small_pad)
    Wf = {}
    big_row0 = {}
    off = 0
    for n in BIG:
        nr = _numel(shp[n]) // PACK_W
        big_row0[n] = off
        Wf[n] = _full_from_blocks(big_all[:, off:off + nr], shp[n], SHARD_AXIS[n])
        off += nr
    off = 0
    for n in SMALL:
        ne = _numel(shp[n])
        Wf[n] = _full_from_blocks(small_all[:, off:off + ne], shp[n], SHARD_AXIS[n])
        off += ne
    for n in REPL:
        Wf[n] = P[n]

    row = lambda v: v.reshape(1, -1).astype(F32)
    gains = [[row(Wf['ln_gains'][i, j]) for j in range(6)] for i in range(2)]
    LW = 128

    def pad_cols(w):
        return jnp.pad(w, ((0, 0), (0, LW - w.shape[1]))).astype(F32)

    def pad_rows(w):
        return jnp.pad(w, ((0, LW - w.shape[0]), (0, 0))).astype(F32)

    G = {}
    saved = {}

    mem_n, = _rw("mem_norm_fwd", _f_pre(BF16), [mem], [row(Wf['mem_norm'])], outs=[(D, BF16)])

    def post_pre(name, xin, t, g_post, g_pre, out_dtype, bias=()):
        return _rw(name, _f_post_pre(out_dtype, bool(bias)), [xin, t], [g_post, g_pre, *bias],
                   outs=[(D, F32), (D, out_dtype)], seq=S)

    def post_pre_bwd(name, xin, t, g_post, g_pre, dx_next, dhn, bias=(), ct_adds=()):
        return _rw_vjp(name, _f_post_pre(F32, bool(bias)), [xin, t], [g_post, g_pre, *bias], [dx_next, dhn], 2,
                       d_rows=[(0, F32), (1, BF16)], d_params=list(range(2 + len(bias))), ct_adds=ct_adds, seq=S)

    def xattn_fwd(i, hq):
        q = _mm_nn(f"xattn{i}_q", hq, Wf['c_w_q'][i], BF16)
        kv = _mm_nn(f"xattn{i}_kv", mem_n, Wf['c_w_kv'][i], F32)
        kv = kv.reshape(Bl, ML, 2 * D)
        o, = _rw(f"xattn{i}_attn", _f_attn, [q], [kv], outs=[(D, BF16)], seq=S, per_b=(0,))
        saved[f"xattn{i}"] = (hq, q, kv, o)
        return _mm_nn(f"xattn{i}_o", o, Wf['c_w_o'][i], F32)

    def mlp_fwd(i, hm):
        up, act = _mm_up_act(f"mlp{i}_up", hm, Wf['m_w_up'][i])
        saved[f"mlp{i}"] = (hm, up, act)
        return _mm_nn(f"mlp{i}_down", act, Wf['m_w_down'][i], F32)

    b_in = row(Wf['a_b_in'][0])
    b_y, b_u = b_in[:, :D], b_in[:, D:]
    cw = [row(Wf['a_conv_w'][0, t]) for t in range(4)]
    cb = row(Wf['a_conv_b'][0])
    gate_w = Wf['a_gate_w'][0].reshape(2 * LRU_HEADS * (D // LRU_HEADS), D // LRU_HEADS).astype(F32)
    gate_b = Wf['a_gate_b'][0].reshape(2, D).astype(F32)
    gb_r, gb_i = gate_b[0:1], gate_b[1:2]
    lam = row(Wf['a_lambda'][0])
    b_out = row(Wf['a_b_out'][0])

    hn0, = _rw("lru_pre", _f_pre(BF16), [x], [gains[0][0]], outs=[(D, BF16)], seq=S)
    proj = _mm_nn("lru_in", hn0, Wf['a_w_in'][0], F32)
    prev3 = [(0, 1), (0, 2), (0, 3)]
    conv, = _rw("lru_conv", _f_conv, [proj], [b_u] + cw + [cb], outs=[(D, F32)], shifts=prev3, seq=S)
    gate_par = [gate_w, gb_r, gb_i, lam, b_y]
    a_l, b_l, yb = _rw("lru_gates", _f_gates, [conv, proj], gate_par, outs=[(D, F32)] * 3, seq=S)
    h_l = _lru_scan("lru_scan", a_l, b_l, S)
    hy, = _rw("lru_hy", _f_hy, [h_l, yb], outs=[(D, BF16)], seq=S)
    t0 = _mm_nn("lru_out", hy, Wf['a_w_out'][0], F32)
    x1, hq0 = post_pre("lru_post", x, t0, gains[0][1], gains[0][2], BF16, bias=(b_out,))
    c0 = xattn_fwd(0, hq0)
    x2, hm0 = post_pre("xattn0_post", x1, c0, gains[0][3], gains[0][4], BF16)
    m0 = mlp_fwd(0, hm0)
    x3_, hn1 = post_pre("mlp0_post", x2, m0, gains[0][5], gains[1][0], F32)

    mu = [row(Wf['b_mu'][0, j]) for j in range(6)]
    lora = [row(Wf['b_w0'][0]), pad_cols(Wf['b_w1'][0]), pad_rows(Wf['b_w2'][0]),
            row(Wf['b_a0'][0]), pad_cols(Wf['b_a1'][0]), pad_rows(Wf['b_a2'][0]),
            Wf['b_g1'][0].astype(F32), Wf['b_g2'][0].astype(F32)]
    k_k, k_a = row(Wf['b_k_k'][0]), row(Wf['b_k_a'][0])
    gn_g, gn_b, r_k = row(Wf['b_gn_g'][0]), row(Wf['b_gn_b'][0]), row(Wf['b_r_k'][0])

    xr, xk, xv, decay, a_r, g_r = _rw("rwkv_mix", _f_mix, [hn1], mu + lora,
                                      outs=[(D, BF16)] * 3 + [(D, F32)] * 3, shifts=[(0, 1)], seq=S)
    r_ = _mm_nn("rwkv_r", xr, Wf['b_w_rkv'][0, 0], F32)
    k_ = _mm_nn("rwkv_k", xk, Wf['b_w_rkv'][0, 1], F32)
    v_ = _mm_nn("rwkv_v", xv, Wf['b_w_rkv'][0, 2], F32)
    k2, ra, rb = _rw("rwkv_kk", _f_kk, [k_, a_r], [k_k, k_a], outs=[(D, F32)] * 3, seq=S)
    y_r, states, sa_r, last_r = _rwkv_fwd("rwkv_scan", r_, decay, k2, v_, ra, rb, S)
    gn_par = [gn_g, gn_b, r_k]
    og, = _rw("rwkv_gn", _f_gn, [y_r, r_, k2, v_, g_r], gn_par, outs=[(D, BF16)], seq=S)
    t1 = _mm_nn("rwkv_out", og, Wf['b_w_o'][0], F32)
    x4, hq1 = post_pre("rwkv_post", x3_, t1, gains[1][1], gains[1][2], BF16)
    c1 = xattn_fwd(1, hq1)
    x5, hm1 = post_pre("xattn1_post", x4, c1, gains[1][3], gains[1][4], BF16)
    m1 = mlp_fwd(1, hm1)
    x6, = _rw("mlp1_post", _f_post, [x5, m1], [gains[1][5]], outs=[(D, F32)], seq=S)

    dx, loss_acc = _rw("loss", _f_loss, [x6, target], outs=[(D, F32)], accs=[(8, LANES)], seq=S)
    loss = lax.psum(loss_acc[0, 0], ("x", "y", "c"))

    dgain = [[None] * 6 for _ in range(2)]
    dmem_n = None

    def mlp_bwd(i, dm):
        hm, up, act = saved[f"mlp{i}"]
        G_down = _mm_tn(f"mlp{i}_down_dw", act, dm)
        dup = _mm_nt_act_bwd(f"mlp{i}_down_dx", dm, Wf['m_w_down'][i], up)
        G_up = _mm_tn(f"mlp{i}_up_dw", hm, dup)
        return _mm_nt(f"mlp{i}_up_dx", dup, Wf['m_w_up'][i], F32), G_up, G_down

    def xattn_bwd(i, dc):
        hq, q, kv, o = saved[f"xattn{i}"]
        G_o = _mm_tn(f"xattn{i}_o_dw", o, dc)
        do = _mm_nt(f"xattn{i}_o_dx", dc, Wf['c_w_o'][i], F32)
        dq, dkv = _rw_vjp(f"xattn{i}_attn_bwd", _f_attn, [q], [kv], [do], 1, d_rows=[(0, BF16)],
                          d_params=[0], seq=S, per_b=(0,))
        dkv = dkv.reshape(Bl * ML, 2 * D)
        G_kv = _mm_tn(f"xattn{i}_kv_dw", mem_n, dkv)
        dmn = _mm_nt(f"xattn{i}_kv_dx", dkv, Wf['c_w_kv'][i], F32)
        G_q = _mm_tn(f"xattn{i}_q_dw", hq, dq)
        dhq = _mm_nt(f"xattn{i}_q_dx", dq, Wf['c_w_q'][i], F32)
        return dhq, G_q, G_kv, G_o, dmn

    G_up, G_down, G_q, G_kv, G_o = [None] * 2, [None] * 2, [None] * 2, [None] * 2, [None] * 2
    dm1, dgain[1][5] = _rw_vjp("mlp1_post_bwd", _f_post, [x5, m1], [gains[1][5]], [dx], 1,
                               d_rows=[(1, BF16)], d_params=[0], seq=S)
    dhm1, G_up[1], G_down[1] = mlp_bwd(1, dm1)
    dx, dc1, dgain[1][3], dgain[1][4] = post_pre_bwd("xattn1_post_bwd", x4, c1, gains[1][3], gains[1][4], dx, dhm1)
    dhq1, G_q[1], G_kv[1], G_o[1], dmn1 = xattn_bwd(1, dc1)

    dx, dt1, dgain[1][1], dgain[1][2] = post_pre_bwd("rwkv_post_bwd", x3_, t1, gains[1][1], gains[1][2], dx, dhq1)
    G['b_w_o'] = _mm_tn("rwkv_out_dw", og, dt1)[None]
    dog = _mm_nt("rwkv_out_dx", dt1, Wf['b_w_o'][0], F32)
    dy_r, dr_p, dk2_p, dv_p, dg_r, d_gn_g, d_gn_b, d_r_k = _rw_vjp(
        "rwkv_gn_bwd", _f_gn, [y_r, r_, k2, v_, g_r], gn_par, [dog], 1,
        d_rows=[(j, F32) for j in range(5)], d_params=[0, 1, 2], seq=S)
    dr_, ddecay, dk2, dv_, dra, drb = _rwkv_bwd("rwkv_scan_bwd", r_, decay, k2, v_, ra, rb, states, sa_r, last_r, dy_r,
                                                dr_p, dk2_p, dv_p, S)
    dk_, da_r, d_k_k, d_k_a = _rw_vjp("rwkv_kk_bwd", _f_kk, [k_, a_r], [k_k, k_a], [dk2, dra, drb], 3,
                                      d_rows=[(0, F32), (1, F32)], d_params=[0, 1], seq=S)
    G_rkv = [_mm_tn("rwkv_r_dw", xr, dr_), _mm_tn("rwkv_k_dw", xk, dk_), _mm_tn("rwkv_v_dw", xv, dv_)]
    G['b_w_rkv'] = jnp.stack(G_rkv)[None]
    dxr = _mm_nt("rwkv_r_dx", dr_, Wf['b_w_rkv'][0, 0], F32)
    dxk = _mm_nt("rwkv_k_dx", dk_, Wf['b_w_rkv'][0, 1], F32)
    dxv = _mm_nt("rwkv_v_dx", dv_, Wf['b_w_rkv'][0, 2], F32)
    mix_out = _rw_vjp("rwkv_mix_bwd", _f_mix, [hn1], mu + lora, [dxr, dxk, dxv, ddecay, da_r, dg_r], 6,
                      d_rows=[(0, F32)], d_shifts=[(0, F32)], d_params=list(range(14)), shifts=[(0, 1)], seq=S)
    dhn1, dhp1 = mix_out[0], mix_out[1]
    d_mu = mix_out[2:8]
    d_lora = mix_out[8:16]
    dx, dm0, dgain[0][5], dgain[1][0] = post_pre_bwd("mlp0_post_bwd", x2, m0, gains[0][5], gains[1][0], dx, dhn1,
                                                     ct_adds=[(1, dhp1, -1)])
    G['b_mu'] = jnp.concatenate(d_mu, axis=0)[None]
    G['b_w0'], G['b_a0'] = d_lora[0], d_lora[3]
    G['b_w1'] = d_lora[1][:, :shp['b_w1'][2]][None]
    G['b_w2'] = d_lora[2][:shp['b_w2'][1]][None]
    G['b_a1'] = d_lora[4][:, :shp['b_a1'][2]][None]
    G['b_a2'] = d_lora[5][:shp['b_a2'][1]][None]
    G['b_g1'], G['b_g2'] = d_lora[6][None], d_lora[7][None]
    G['b_k_k'], G['b_k_a'], G['b_gn_g'], G['b_gn_b'] = d_k_k, d_k_a, d_gn_g, d_gn_b
    G['b_r_k'] = d_r_k.reshape(P['b_r_k'].shape)

    dhm0, G_up[0], G_down[0] = mlp_bwd(0, dm0)
    dx, dc0, dgain[0][3], dgain[0][4] = post_pre_bwd("xattn0_post_bwd", x1, c0, gains[0][3], gains[0][4], dx, dhm0)
    dhq0, G_q[0], G_kv[0], G_o[0], dmn0 = xattn_bwd(0, dc0)
    G['m_w_up'], G['m_w_down'] = jnp.stack(G_up), jnp.stack(G_down)
    G['c_w_q'], G['c_w_kv'], G['c_w_o'] = jnp.stack(G_q), jnp.stack(G_kv), jnp.stack(G_o)

    dx, dt0, dgain[0][1], dgain[0][2], G['a_b_out'] = post_pre_bwd("lru_post_bwd", x, t0, gains[0][1], gains[0][2],
                                                                  dx, dhq0, bias=(b_out,))
    G['a_w_out'] = _mm_tn("lru_out_dw", hy, dt0)[None]
    dhy = _mm_nt("lru_out_dx", dt0, Wf['a_w_out'][0], F32)
    dh_l, dyb = _rw_vjp("lru_hy_bwd", _f_hy, [h_l, yb], [], [dhy], 1, d_rows=[(0, F32), (1, F32)], seq=S)
    da_l, db_l = _lru_scan_bwd("lru_scan_bwd", a_l, h_l, dh_l, S)
    dconv, dproj_y, d_gate_w, d_gb_r, d_gb_i, G['a_lambda'] = _rw_vjp(
        "lru_gates_bwd", _f_gates, [conv, proj], gate_par, [da_l, db_l, dyb], 3,
        d_rows=[(0, F32), (1, F32)], d_params=[0, 1, 2, 3], seq=S)
    next3 = [(0, -1), (0, -2), (0, -3)]
    dproj, G['a_b_in'], dw0, dw1, dw2, dw3, G['a_conv_b'] = _rw(
        "lru_conv_bwd", _f_conv_bwd, [dconv, proj, dproj_y], [b_u] + cw, outs=[(2 * D, BF16)],
        accs=[(1, 2 * D)] + [(1, D)] * 5, shifts=next3 + [(1, 1), (1, 2), (1, 3)], seq=S)
    G['a_conv_w'] = jnp.concatenate([dw0, dw1, dw2, dw3], axis=0)[None]
    G['a_gate_w'] = d_gate_w.reshape(P['a_gate_w'].shape[:3] + (D // LRU_HEADS, D // LRU_HEADS))
    G['a_gate_b'] = jnp.concatenate([d_gb_r, d_gb_i], axis=0).reshape(1, 2, LRU_HEADS, D // LRU_HEADS)
    G['a_w_in'] = _mm_tn("lru_in_dw", hn0, dproj)[None]
    dhn0 = _mm_nt("lru_in_dx", dproj, Wf['a_w_in'][0], F32)
    grad_x, dgain[0][0] = _rw_vjp("lru_pre_bwd", _f_pre(F32), [x], [gains[0][0]], [dhn0], 1, d_rows=[(0, F32)],
                                  d_params=[0], adds=[(0, dx, 0)], seq=S)
    d_mem_norm, = _rw_vjp("mem_norm_bwd", _f_pre(F32), [mem], [row(Wf['mem_norm'])], [dmn0], 1, d_params=[0],
                          ct_adds=[(0, dmn1, 0)])
    G['mem_norm'] = d_mem_norm.reshape(-1)
    G['ln_gains'] = jnp.stack([jnp.concatenate(dgain[i], axis=0) for i in range(2)])

    sharded = BIG + SMALL
    rows_s = big_rows + small_rows
    tr = max(t for t in range(BF16_ROWS, 161, BF16_ROWS) if rows_s % t == 0)

    pieces = [_blocks_from_full(G[n], shp[n], SHARD_AXIS[n]).reshape(N_DEV, -1, PACK_W) for n in BIG]
    small = [_blocks_from_full(G[n], shp[n], SHARD_AXIS[n]) for n in SMALL] + [jnp.zeros((N_DEV, n_small_pad - n_small), F32)]
    pieces.append(jnp.concatenate(small, axis=1).reshape(N_DEV, small_rows, PACK_W))
    g_blocks = jnp.concatenate(pieces, axis=1)
    n_split = max(q for q in range(1, 10) if rows_s % (q * BF16_ROWS) == 0)
    got = _swap_with_sibling("exchange_grads_sibling", g_blocks, n_split)
    parts_s = _exchange_chips("exchange_grads_chips", _add_own_blocks("sum_with_sibling", g_blocks, got, tr, BF16))
    n_repl = sum(_numel(shp[n]) for n in REPL)
    n_repl_pad = -(-n_repl // (8 * PACK_W)) * (8 * PACK_W)
    rows_r = n_repl_pad // PACK_W
    parts_r = _all_gather("gather_replicated_grads", _pack_flat([G[n] for n in REPL], n_repl_pad).reshape(rows_r, PACK_W))

    def pack(src, names, total, rows_):
        return _pack_flat([src[n] for n in names], total).reshape(rows_, PACK_W)

    out_s = _adamw("adamw_sharded", parts_s, pack_rows(P), pack_rows(M_), pack_rows(V_), tr)
    out_r = _adamw("adamw_replicated", parts_r, pack(P, REPL, n_repl_pad, rows_r), pack(M_, REPL, n_repl_pad, rows_r),
                   pack(V_, REPL, n_repl_pad, rows_r), rows_r)

    def unpack(packed, names):
        flat = packed.reshape(-1)
        res, o = {}, 0
        for n in names:
            ne = _numel(shp[n])
            res[n] = flat[o:o + ne].reshape(shp[n])
            o += ne
        return res

    results = []
    for j in range(4):
        both = {n: out_s[j][big_row0[n]:big_row0[n] + _numel(shp[n]) // PACK_W].reshape(shp[n]) for n in BIG}
        both.update(unpack(out_s[j][big_rows:], SMALL))
        both.update(unpack(out_r[j], REPL))
        results += [both[n] for n in WEIGHTS]
    return (loss, grad_x.reshape(Bl, S, D), *results)


def kernel(x, mem, ln_gains, mem_norm, a_conv_w, a_conv_b, a_w_in, a_b_in, a_gate_w, a_gate_b, a_lambda, a_w_out, a_b_out, b_mu, b_w_rkv, b_w0, b_w1, b_w2, b_a0, b_a1, b_a2, b_g1, b_g2, b_k_k, b_k_a, b_r_k, b_gn_g, b_gn_b, b_w_o, c_w_q, c_w_kv, c_w_o, m_w_up, m_w_down, loss_target, m_ln_gains, m_mem_norm, m_a_conv_w, m_a_conv_b, m_a_w_in, m_a_b_in, m_a_gate_w, m_a_gate_b, m_a_lambda, m_a_w_out, m_a_b_out, m_b_mu, m_b_w_rkv, m_b_w0, m_b_w1, m_b_w2, m_b_a0, m_b_a1, m_b_a2, m_b_g1, m_b_g2, m_b_k_k, m_b_k_a, m_b_r_k, m_b_gn_g, m_b_gn_b, m_b_w_o, m_c_w_q, m_c_w_kv, m_c_w_o, m_m_w_up, m_m_w_down, v_ln_gains, v_mem_norm, v_a_conv_w, v_a_conv_b, v_a_w_in, v_a_b_in, v_a_gate_w, v_a_gate_b, v_a_lambda, v_a_w_out, v_a_b_out, v_b_mu, v_b_w_rkv, v_b_w0, v_b_w1, v_b_w2, v_b_a0, v_b_a1, v_b_a2, v_b_g1, v_b_g2, v_b_k_k, v_b_k_a, v_b_r_k, v_b_gn_g, v_b_gn_b, v_b_w_o, v_c_w_q, v_c_w_kv, v_c_w_o, v_m_w_up, v_m_w_down):
    weights = (ln_gains, mem_norm, a_conv_w, a_conv_b, a_w_in, a_b_in, a_gate_w, a_gate_b, a_lambda, a_w_out, a_b_out, b_mu, b_w_rkv, b_w0, b_w1, b_w2, b_a0, b_a1, b_a2, b_g1, b_g2, b_k_k, b_k_a, b_r_k, b_gn_g, b_gn_b, b_w_o, c_w_q, c_w_kv, c_w_o, m_w_up, m_w_down)
    moments1 = (m_ln_gains, m_mem_norm, m_a_conv_w, m_a_conv_b, m_a_w_in, m_a_b_in, m_a_gate_w, m_a_gate_b, m_a_lambda, m_a_w_out, m_a_b_out, m_b_mu, m_b_w_rkv, m_b_w0, m_b_w1, m_b_w2, m_b_a0, m_b_a1, m_b_a2, m_b_g1, m_b_g2, m_b_k_k, m_b_k_a, m_b_r_k, m_b_gn_g, m_b_gn_b, m_b_w_o, m_c_w_q, m_c_w_kv, m_c_w_o, m_m_w_up, m_m_w_down)
    moments2 = (v_ln_gains, v_mem_norm, v_a_conv_w, v_a_conv_b, v_a_w_in, v_a_b_in, v_a_gate_w, v_a_gate_b, v_a_lambda, v_a_w_out, v_a_b_out, v_b_mu, v_b_w_rkv, v_b_w0, v_b_w1, v_b_w2, v_b_a0, v_b_a1, v_b_a2, v_b_g1, v_b_g2, v_b_k_k, v_b_k_a, v_b_r_k, v_b_gn_g, v_b_gn_b, v_b_w_o, v_c_w_q, v_c_w_kv, v_c_w_o, v_m_w_up, v_m_w_down)
    return _train_step(x, mem, loss_target, dict(zip(WEIGHTS, weights)), dict(zip(WEIGHTS, moments1)),
                       dict(zip(WEIGHTS, moments2)))
```

```python
import jax
import jax.numpy as jnp
from jax import lax
from jax.experimental import pallas as pl
from jax.experimental.pallas import tpu as pltpu

F32 = jnp.float32
BF16 = jnp.bfloat16
N_DEV = 8
MESH = pl.DeviceIdType.MESH

V7X_VMEM_LIMIT_BYTES = 56 * 1024 * 1024
SUB = 8
HALO_ROWS = SUB
LANES = 128
RWKV_N = 64
RMS_EPS = 1e-6
GN_EPS = 64e-5
LRU_C = 8.0
XATTN_HEADS = 4
LRU_HEADS = 4

ADAM_LR, ADAM_B1, ADAM_B2, ADAM_EPS, ADAM_WD, ADAM_STEP = 0.001, 0.9, 0.999, 1e-8, 0.01, 10

WEIGHTS = ['ln_gains', 'mem_norm', 'a_conv_w', 'a_conv_b', 'a_w_in', 'a_b_in', 'a_gate_w', 'a_gate_b',
           'a_lambda', 'a_w_out', 'a_b_out', 'b_mu', 'b_w_rkv', 'b_w0', 'b_w1', 'b_w2', 'b_a0', 'b_a1',
           'b_a2', 'b_g1', 'b_g2', 'b_k_k', 'b_k_a', 'b_r_k', 'b_gn_g', 'b_gn_b', 'b_w_o', 'c_w_q',
           'c_w_kv', 'c_w_o', 'm_w_up', 'm_w_down']
SHARD_AXIS = {'ln_gains': 2, 'mem_norm': None, 'a_conv_w': 2, 'a_conv_b': None, 'a_w_in': 2, 'a_b_in': None,
              'a_gate_w': 3, 'a_gate_b': 3, 'a_lambda': None, 'a_w_out': 1, 'a_b_out': None, 'b_mu': 2,
              'b_w_rkv': 2, 'b_w0': 1, 'b_w1': 1, 'b_w2': 2, 'b_a0': 1, 'b_a1': 1, 'b_a2': 2, 'b_g1': 1,
              'b_g2': 2, 'b_k_k': 1, 'b_k_a': 1, 'b_r_k': None, 'b_gn_g': 1, 'b_gn_b': 1, 'b_w_o': 1,
              'c_w_q': 1, 'c_w_kv': 2, 'c_w_o': 1, 'm_w_up': 2, 'm_w_down': 1}
BIG = ['a_w_in', 'a_gate_w', 'a_w_out', 'b_w_rkv', 'b_w1', 'b_w2', 'b_a1', 'b_a2', 'b_g1', 'b_g2', 'b_w_o',
       'c_w_q', 'c_w_kv', 'c_w_o', 'm_w_up', 'm_w_down']
SMALL = ['ln_gains', 'a_conv_w', 'a_gate_b', 'b_mu', 'b_w0', 'b_a0', 'b_k_k', 'b_k_a', 'b_gn_g', 'b_gn_b']
REPL = ['mem_norm', 'a_conv_b', 'a_b_in', 'a_lambda', 'a_b_out', 'b_r_k']
PACK_W = 1024
PACK_ROWS = 128
BF16_ROWS = 16


def _cparams(*sem):
    return pltpu.CompilerParams(dimension_semantics=sem, vmem_limit_bytes=V7X_VMEM_LIMIT_BYTES)


def _shift_tile(tile, halo, k, pos, seq, tm):
    if k > 0:
        ext = jnp.concatenate([halo, tile], axis=0)
        r = pltpu.roll(ext, k, 0)[HALO_ROWS:HALO_ROWS + tm]
        return jnp.where(pos >= k, r, 0.0)
    kk = -k
    ext = jnp.concatenate([tile, halo], axis=0)
    r = pltpu.roll(ext, tm + HALO_ROWS - kk, 0)[0:tm]
    return jnp.where(pos + kk < seq, r, 0.0)


def _row_specs(rows, shifts, params, per_b, seq, tm):
    T = rows[0].shape[0]
    ns = seq // tm
    specs = [pl.BlockSpec((tm, r.shape[1]), lambda b, i: (b * ns + i, 0)) for r in rows]
    halo_keys = []
    for idx, k in shifts:
        key = (idx, k > 0)
        if key not in halo_keys:
            halo_keys.append(key)
    per8 = tm // HALO_ROWS
    last8 = T // HALO_ROWS - 1
    for idx, prev in halo_keys:
        c = rows[idx].shape[1]
        if prev:
            specs.append(pl.BlockSpec((HALO_ROWS, c), lambda b, i: (jnp.maximum((b * ns + i) * per8 - 1, 0), 0)))
        else:
            specs.append(pl.BlockSpec((HALO_ROWS, c), lambda b, i: (jnp.minimum((b * ns + i + 1) * per8, last8), 0)))
    for j, p in enumerate(params):
        if j in per_b:
            specs.append(pl.BlockSpec((None,) + p.shape[1:], lambda b, i: (b, 0, 0)))
        else:
            specs.append(pl.BlockSpec(p.shape, lambda b, i, nd=p.ndim: (0,) * nd))
    return specs, halo_keys


def _load_tiles(refs, rows, shifts, halo_keys, params, seq, tm):
    nr, nh = len(rows), len(halo_keys)
    i = pl.program_id(1)
    pos = i * tm + lax.broadcasted_iota(jnp.int32, (tm, 1), 0)
    row_t = [r[...] for r in refs[:nr]]
    halo_t = {key: refs[nr + j][...] for j, key in enumerate(halo_keys)}
    sh_t = [_shift_tile(row_t[idx], halo_t[(idx, k > 0)], k, pos, seq, tm) for idx, k in shifts]
    par_t = [p[...] for p in refs[nr + nh:nr + nh + len(params)]]
    return pos, row_t, sh_t, par_t


ROW_TILES = (512, 256)
ROW_TILE_BUDGET_BYTES = 16 * 1024 * 1024


def _row_tile(seq, row_bytes):
    for tm in ROW_TILES:
        if 2 * tm * row_bytes <= ROW_TILE_BUDGET_BYTES:
            return min(tm, seq)
    return min(ROW_TILES[-1], seq)


def _rw(name, fn, rows, params=(), outs=(), accs=(), shifts=(), seq=None, per_b=()):
    rows, params = list(rows), list(params)
    T = rows[0].shape[0]
    seq = seq or T
    tm = _row_tile(seq, sum(r.shape[1] * r.dtype.itemsize for r in rows) + sum(c * jnp.dtype(dt).itemsize for c, dt in outs))
    nb, ns = T // seq, seq // tm
    in_specs, halo_keys = _row_specs(rows, shifts, params, per_b, seq, tm)
    n_in, no = len(in_specs), len(outs)

    def body(*refs):
        pos, row_t, sh_t, par_t = _load_tiles(refs, rows, shifts, halo_keys, params, seq, tm)
        res = fn(pos, *row_t, *sh_t, *par_t)
        out_refs = refs[n_in:n_in + no]
        acc_refs = refs[n_in + no:]
        for o_ref, v in zip(out_refs, res[:no]):
            o_ref[...] = v.astype(o_ref.dtype)
        if acc_refs:
            @pl.when((pl.program_id(0) == 0) & (pl.program_id(1) == 0))
            def _():
                for a_ref in acc_refs:
                    a_ref[...] = jnp.zeros_like(a_ref)
            for a_ref, v in zip(acc_refs, res[no:]):
                a_ref[...] += v

    out_shape = [jax.ShapeDtypeStruct((T, c), dt) for c, dt in outs]
    out_shape += [jax.ShapeDtypeStruct(s, F32) for s in accs]
    out_specs = [pl.BlockSpec((tm, c), lambda b, i: (b * ns + i, 0)) for c, _ in outs]
    out_specs += [pl.BlockSpec(s, lambda b, i: (0, 0)) for s in accs]
    return pl.pallas_call(body, name=name, grid=(nb, ns), in_specs=in_specs, out_specs=out_specs,
                          out_shape=out_shape, compiler_params=_cparams("arbitrary", "arbitrary"))(*rows, *[
                              rows[idx] for idx, _ in halo_keys], *params)


def _rw_vjp(name, fn, rows, params, cts, n_out, d_rows=(), d_shifts=(), d_params=(), adds=(), ct_adds=(),
            shifts=(), seq=None, per_b=()):
    rows, params, cts = list(rows), list(params), list(cts)
    T = rows[0].shape[0]
    seq = seq or T
    row_bytes = sum(r.shape[1] * r.dtype.itemsize for r in rows + cts + [a for _, a, _ in list(adds) + list(ct_adds)])
    row_bytes += sum(rows[j].shape[1] * jnp.dtype(dt).itemsize for j, dt in d_rows)
    row_bytes += sum(rows[shifts[j][0]].shape[1] * jnp.dtype(dt).itemsize for j, dt in d_shifts)
    tm = _row_tile(seq, row_bytes)
    nb, ns = T // seq, seq // tm
    n_ga = len(adds)
    adds = list(adds) + list(ct_adds)
    add_rows = [a for _, a, _ in adds]
    add_shifts = [(j, k) for j, (_, _, k) in enumerate(adds) if k != 0]
    all_rows = rows + cts + add_rows
    off_ct, off_add = len(rows), len(rows) + len(cts)
    all_shifts = list(shifts) + [(off_add + j, k) for j, k in add_shifts]
    in_specs, halo_keys = _row_specs(all_rows, all_shifts, params, per_b, seq, tm)
    n_in = len(in_specs)
    n_dr, n_ds = len(d_rows), len(d_shifts)

    def body(*refs):
        pos, row_t, sh_t, par_t = _load_tiles(refs, all_rows, all_shifts, halo_keys, params, seq, tm)
        fn_rows, ct_t, add_t = row_t[:off_ct], row_t[off_ct:off_add], row_t[off_add:]
        fn_sh, add_sh = sh_t[:len(shifts)], sh_t[len(shifts):]
        diff = [fn_rows[j] for j, _ in d_rows] + [fn_sh[j] for j, _ in d_shifts] + [par_t[j] for j in d_params]

        def f(*d):
            r2, s2, p2 = list(fn_rows), list(fn_sh), list(par_t)
            for (j, _), v in zip(d_rows, d[:n_dr]):
                r2[j] = v
            for (j, _), v in zip(d_shifts, d[n_dr:n_dr + n_ds]):
                s2[j] = v
            for j, v in zip(d_params, d[n_dr + n_ds:]):
                p2[j] = v
            return tuple(fn(pos, *r2, *s2, *p2)[:n_out])

        sh_iter = iter(add_sh)
        add_v = [add_t[j] if k == 0 else next(sh_iter) for j, (_, _, k) in enumerate(adds)]
        ct_t = list(ct_t)
        for (tgt, _, _), v in zip(adds[n_ga:], add_v[n_ga:]):
            ct_t[tgt] = ct_t[tgt] + v
        outs_v, vjp_fn = jax.vjp(f, *diff)
        grads = list(vjp_fn(tuple(c.astype(o.dtype) for c, o in zip(ct_t, outs_v))))
        for (tgt, _, _), v in zip(adds[:n_ga], add_v[:n_ga]):
            grads[tgt] = grads[tgt] + v
        out_refs = refs[n_in:]
        for o_ref, g in zip(out_refs[:n_dr + n_ds], grads[:n_dr + n_ds]):
            o_ref[...] = g.astype(o_ref.dtype)
        for o_ref, g, j in zip(out_refs[n_dr + n_ds:], grads[n_dr + n_ds:], d_params):
            first = (pl.program_id(1) == 0) if j in per_b else ((pl.program_id(0) == 0) & (pl.program_id(1) == 0))

            @pl.when(first)
            def _(o_ref=o_ref):
                o_ref[...] = jnp.zeros_like(o_ref)
            o_ref[...] += g.astype(F32)

    out_shape, out_specs = [], []
    for j, dt in d_rows:
        out_shape.append(jax.ShapeDtypeStruct(rows[j].shape, dt))
        out_specs.append(pl.BlockSpec((tm, rows[j].shape[1]), lambda b, i: (b * ns + i, 0)))
    for j, dt in d_shifts:
        src = rows[shifts[j][0]]
        out_shape.append(jax.ShapeDtypeStruct(src.shape, dt))
        out_specs.append(pl.BlockSpec((tm, src.shape[1]), lambda b, i: (b * ns + i, 0)))
    for j in d_params:
        p = params[j]
        out_shape.append(jax.ShapeDtypeStruct(p.shape, F32))
        if j in per_b:
            out_specs.append(pl.BlockSpec((None,) + p.shape[1:], lambda b, i: (b, 0, 0)))
        else:
            out_specs.append(pl.BlockSpec(p.shape, lambda b, i, nd=p.ndim: (0,) * nd))
    return pl.pallas_call(body, name=name, grid=(nb, ns), in_specs=in_specs, out_specs=out_specs,
                          out_shape=out_shape, compiler_params=_cparams("arbitrary", "arbitrary"))(*all_rows, *[
                              all_rows[idx] for idx, _ in halo_keys], *params)


MM_TILE = 1024
MM_DEEP = 2048


def _mm_nn(name, x, w, out_dtype):
    M, K = x.shape
    N = w.shape[1]
    tm, tn = min(MM_TILE if K <= MM_DEEP else MM_TILE // 2, M), min(MM_TILE, N)

    def body(x_ref, w_ref, o_ref):
        o_ref[...] = jnp.dot(x_ref[...].astype(BF16), w_ref[...].astype(BF16),
                             preferred_element_type=F32).astype(o_ref.dtype)

    return pl.pallas_call(body, name=name, grid=(M // tm, N // tn),
                          in_specs=[pl.BlockSpec((tm, K), lambda i, j: (i, 0)), pl.BlockSpec((K, tn), lambda i, j: (0, j))],
                          out_specs=pl.BlockSpec((tm, tn), lambda i, j: (i, j)),
                          out_shape=jax.ShapeDtypeStruct((M, N), out_dtype),
                          compiler_params=_cparams("parallel", "parallel"))(x, w)


def _mm_up_act(name, x, w):
    M, K = x.shape
    N = w.shape[1]
    tm, tn = min(MM_TILE if K <= MM_DEEP else MM_TILE // 2, M), min(MM_TILE, N)

    def body(x_ref, w_ref, up_ref, act_ref):
        up = jnp.dot(x_ref[...].astype(BF16), w_ref[...].astype(BF16), preferred_element_type=F32)
        up_ref[...] = up.astype(BF16)
        act_ref[...] = jnp.square(jnp.maximum(up, 0.0)).astype(BF16)

    out = pl.BlockSpec((tm, tn), lambda i, j: (i, j))
    return pl.pallas_call(body, name=name, grid=(M // tm, N // tn),
                          in_specs=[pl.BlockSpec((tm, K), lambda i, j: (i, 0)), pl.BlockSpec((K, tn), lambda i, j: (0, j))],
                          out_specs=[out, out], out_shape=[jax.ShapeDtypeStruct((M, N), BF16)] * 2,
                          compiler_params=_cparams("parallel", "parallel"))(x, w)


def _mm_nt_act_bwd(name, dy, w, up):
    M, N = dy.shape
    K = w.shape[0]
    tm, tk = min(MM_TILE if N <= MM_DEEP else MM_TILE // 2, M), min(MM_TILE, K)

    def body(dy_ref, w_ref, up_ref, o_ref):
        dact = lax.dot_general(dy_ref[...].astype(BF16), w_ref[...].astype(BF16), (((1,), (1,)), ((), ())),
                               preferred_element_type=F32)
        o_ref[...] = (dact * (2.0 * jnp.maximum(up_ref[...].astype(F32), 0.0))).astype(o_ref.dtype)

    out = pl.BlockSpec((tm, tk), lambda i, j: (i, j))
    return pl.pallas_call(body, name=name, grid=(M // tm, K // tk),
                          in_specs=[pl.BlockSpec((tm, N), lambda i, j: (i, 0)), pl.BlockSpec((tk, N), lambda i, j: (j, 0)), out],
                          out_specs=out, out_shape=jax.ShapeDtypeStruct((M, K), BF16),
                          compiler_params=_cparams("parallel", "parallel"))(dy, w, up)


def _mm_nt(name, dy, w, out_dtype):
    M, N = dy.shape
    K = w.shape[0]
    tm, tk = min(MM_TILE if N <= MM_DEEP else MM_TILE // 2, M), min(MM_TILE, K)

    def body(dy_ref, w_ref, o_ref):
        o_ref[...] = lax.dot_general(dy_ref[...].astype(BF16), w_ref[...].astype(BF16), (((1,), (1,)), ((), ())),
                                     preferred_element_type=F32).astype(o_ref.dtype)

    return pl.pallas_call(body, name=name, grid=(M // tm, K // tk),
                          in_specs=[pl.BlockSpec((tm, N), lambda i, j: (i, 0)), pl.BlockSpec((tk, N), lambda i, j: (j, 0))],
                          out_specs=pl.BlockSpec((tm, tk), lambda i, j: (i, j)),
                          out_shape=jax.ShapeDtypeStruct((M, K), out_dtype),
                          compiler_params=_cparams("parallel", "parallel"))(dy, w)


def _mm_tn(name, x, dy):
    M, K = x.shape
    N = dy.shape[1]
    tm, tk, tn = min(MM_TILE, M), min(MM_TILE, K), min(MM_TILE, N)

    def body(x_ref, dy_ref, o_ref):
        @pl.when(pl.program_id(2) == 0)
        def _():
            o_ref[...] = jnp.zeros_like(o_ref)
        o_ref[...] += lax.dot_general(x_ref[...].astype(BF16), dy_ref[...].astype(BF16), (((0,), (0,)), ((), ())),
                                      preferred_element_type=F32)

    return pl.pallas_call(body, name=name, grid=(K // tk, N // tn, M // tm),
                          in_specs=[pl.BlockSpec((tm, tk), lambda i, j, m: (m, i)), pl.BlockSpec((tm, tn), lambda i, j, m: (m, j))],
                          out_specs=pl.BlockSpec((tk, tn), lambda i, j, m: (i, j)),
                          out_shape=jax.ShapeDtypeStruct((K, N), F32),
                          compiler_params=_cparams("parallel", "parallel", "arbitrary"))(x, dy)


def _lru_scan(name, a, b, seq, ct=128):
    T, C = a.shape

    def body(a_ref, b_ref, h_ref):
        A, Bv = a_ref[...], b_ref[...]
        row = lax.broadcasted_iota(jnp.int32, (seq, 1), 0)
        k = 1
        while k < seq:
            keep = row >= k
            Bv = A * jnp.where(keep, pltpu.roll(Bv, k, 0), 0.0) + Bv
            A = A * jnp.where(keep, pltpu.roll(A, k, 0), 1.0)
            k *= 2
        h_ref[...] = Bv

    spec = pl.BlockSpec((seq, ct), lambda b_, j: (b_, j))
    return pl.pallas_call(body, name=name, grid=(T // seq, C // ct), in_specs=[spec, spec], out_specs=spec,
                          out_shape=jax.ShapeDtypeStruct((T, C), F32),
                          compiler_params=_cparams("parallel", "parallel"))(a, b)


def _lru_scan_bwd(name, a, h, dh, seq, ct=128):
    T, C = a.shape

    def body(a_ref, h_ref, dh_ref, da_ref, db_ref):
        row = lax.broadcasted_iota(jnp.int32, (seq, 1), 0)
        Cf = jnp.where(row < seq - 1, pltpu.roll(a_ref[...], seq - 1, 0), 0.0)
        G = dh_ref[...]
        k = 1
        while k < seq:
            keep = row + k < seq
            G = Cf * jnp.where(keep, pltpu.roll(G, seq - k, 0), 0.0) + G
            Cf = Cf * jnp.where(keep, pltpu.roll(Cf, seq - k, 0), 1.0)
            k *= 2
        db_ref[...] = G
        da_ref[...] = G * jnp.where(row >= 1, pltpu.roll(h_ref[...], 1, 0), 0.0)

    spec = pl.BlockSpec((seq, ct), lambda b_, j: (b_, j))
    return pl.pallas_call(body, name=name, grid=(T // seq, C // ct), in_specs=[spec] * 3, out_specs=[spec] * 2,
                          out_shape=[jax.ShapeDtypeStruct((T, C), F32)] * 2,
                          compiler_params=_cparams("parallel", "parallel"))(a, h, dh)


def _pair_consts():
    jj = lax.broadcasted_iota(jnp.int32, (RWKV_N, LANES), 0)
    ll = lax.broadcasted_iota(jnp.int32, (RWKV_N, LANES), 1)
    pick = ((ll & (RWKV_N - 1)) == jj).astype(BF16)
    l0 = lax.broadcasted_iota(jnp.int32, (LANES, LANES), 0)
    l1 = lax.broadcasted_iota(jnp.int32, (LANES, LANES), 1)
    same = ((l0 >> 6) == (l1 >> 6)).astype(BF16)
    return pick, same


N_COL = 5


def _cols_raw(rows, pick, same):
    lhs = [pick * jnp.broadcast_to(x, (RWKV_N, LANES)).astype(BF16) for x in rows]
    out = jnp.dot(jnp.concatenate(lhs, axis=0), same, preferred_element_type=F32)
    return tuple(out[i * RWKV_N:(i + 1) * RWKV_N] for i in range(len(rows)))


def _cols_adjoint(g):
    n = len(g)
    g = list(g) + [jnp.zeros_like(g[0])] * (n % 2)
    head = ((lax.broadcasted_iota(jnp.int32, (SUB, LANES), 1) >> 6)
            == lax.broadcasted_iota(jnp.int32, (SUB, LANES), 0)).astype(BF16)
    low = lax.broadcasted_iota(jnp.int32, (1, LANES), 1) < RWKV_N
    sums = lax.dot_general(head, jnp.concatenate([t.astype(BF16) for t in g], axis=0),
                           (((1,), (1,)), ((), ())), preferred_element_type=F32)
    rows = []
    for i in range(0, n, 2):
        blk = sums[:, i * RWKV_N:(i + 2) * RWKV_N]
        swapped = pltpu.roll(blk, RWKV_N, 1)
        rows.append(jnp.where(low, blk[0:1], swapped[1:2]))
        rows.append(jnp.where(low, swapped[0:1], blk[1:2]))
    return tuple(rows[:n])


def _rwkv_cols(rows, pick, same):
    flat = []
    for r, w, k, v, a, b in rows:
        flat += [a, 1.0 - w, b, k, r]
    tiles = _cols_raw(tuple(flat), pick, same)
    return [tiles[N_COL * p:N_COL * (p + 1)] for p in range(len(rows))]


def _rwkv_step(s0, tiles, v):
    ca, ce, cb, ck, cr = tiles
    sa = jnp.sum(s0 * ca, axis=0, keepdims=True)
    s1 = s0 - s0 * ce + cb * sa + ck * v
    return s1, jnp.sum(s1 * cr, axis=0, keepdims=True), sa


def _rwkv_step_bwd(s0, s1, sa, tiles, v, dy, ds1):
    ca, ce, cb, ck, cr = tiles
    ds1 = ds1 + cr * dy
    dsa = jnp.sum(ds1 * cb, axis=0, keepdims=True)
    dv = jnp.sum(ds1 * ck, axis=0, keepdims=True)
    ds0 = ds1 - ds1 * ce + ca * dsa
    return ds0, (s0 * dsa, ds1 * s0, ds1 * sa, ds1 * v, s1 * dy), dv


def _rwkv_specs(C, seq, tc, G, reverse):
    nc = seq // tc
    if reverse:
        row = pl.BlockSpec((tc, LANES * G), lambda b, g, c: (b * nc + nc - 1 - c, g))
        st = pl.BlockSpec((tc, G, RWKV_N, LANES), lambda b, g, c: (b * nc + nc - 1 - c, g, 0, 0))
    else:
        row = pl.BlockSpec((tc, LANES * G), lambda b, g, c: (b * nc + c, g))
        st = pl.BlockSpec((tc, G, RWKV_N, LANES), lambda b, g, c: (b * nc + c, g, 0, 0))
    return row, st


def _rwkv_fwd(name, r, w, k, v, a, b, seq, tc=32, G=8):
    T, C = r.shape
    tc = min(tc, seq)
    NP = C // LANES
    row, st = _rwkv_specs(C, seq, tc, G, False)
    last = pl.BlockSpec((None, G, RWKV_N, LANES), lambda b_, g, c: (b_, g, 0, 0))

    def body(r_ref, w_ref, k_ref, v_ref, a_ref, b_ref, y_ref, st_ref, sa_ref, last_ref, s_scr):
        @pl.when(pl.program_id(2) == 0)
        def _():
            s_scr[...] = jnp.zeros_like(s_scr)
        pick, same = _pair_consts()
        rid = lax.broadcasted_iota(jnp.int32, (SUB, LANES), 0)

        def group(t8, carry):
            base = pl.multiple_of(t8 * SUB, SUB)
            lanes = [pl.ds(LANES * p, LANES) for p in range(G)]
            blk = [[x[pl.ds(base, SUB), ln] for x in (r_ref, w_ref, k_ref, v_ref, a_ref, b_ref)] for ln in lanes]
            s = [s_scr[p] for p in range(G)]
            ys = [jnp.zeros((SUB, LANES), F32) for _ in range(G)]
            sas = [jnp.zeros((SUB, LANES), F32) for _ in range(G)]
            for i in range(SUB):
                tiles = _rwkv_cols([tuple(b_[i:i + 1] for b_ in blk[p]) for p in range(G)], pick, same)
                for p in range(G):
                    st_ref[base + i, p] = s[p]
                    s[p], y, sa = _rwkv_step(s[p], tiles[p], blk[p][3][i:i + 1])
                    ys[p] = jnp.where(rid == i, y, ys[p])
                    sas[p] = jnp.where(rid == i, sa, sas[p])
            for p in range(G):
                s_scr[p] = s[p]
                y_ref[pl.ds(base, SUB), lanes[p]] = ys[p]
                sa_ref[pl.ds(base, SUB), lanes[p]] = sas[p]
            return carry

        lax.fori_loop(0, tc // SUB, group, 0)

        @pl.when(pl.program_id(2) == seq // tc - 1)
        def _():
            last_ref[...] = s_scr[...]

    return pl.pallas_call(body, name=name, grid=(T // seq, NP // G, seq // tc), in_specs=[row] * 6,
                          out_specs=[row, st, row, last],
                          out_shape=[jax.ShapeDtypeStruct((T, C), F32), jax.ShapeDtypeStruct((T, NP, RWKV_N, LANES), F32),
                                     jax.ShapeDtypeStruct((T, C), F32),
                                     jax.ShapeDtypeStruct((T // seq, NP, RWKV_N, LANES), F32)],
                          scratch_shapes=[pltpu.VMEM((G, RWKV_N, LANES), F32)],
                          compiler_params=_cparams("parallel", "parallel", "arbitrary"))(r, w, k, v, a, b)


def _rwkv_bwd(name, r, w, k, v, a, b, states, sa, last, dy, dr_p, dk_p, dv_p, seq, tc=16, G=8, NB=2):
    T, C = r.shape
    B = T // seq
    tc = min(tc, seq)
    NB = NB if B % NB == 0 else 1
    NP = C // LANES
    nc = seq // tc
    row = pl.BlockSpec((NB, tc, LANES * G), lambda b_, g, c: (b_, nc - 1 - c, g))
    st = pl.BlockSpec((NB, tc, G, RWKV_N, LANES), lambda b_, g, c: (b_, nc - 1 - c, g, 0, 0))
    last_spec = pl.BlockSpec((NB, G, RWKV_N, LANES), lambda b_, g, c: (b_, g, 0, 0))
    chains = [(q, p) for q in range(NB) for p in range(G)]

    def body(r_ref, w_ref, k_ref, v_ref, a_ref, b_ref, st_ref, sa_ref, last_ref, dy_ref, drp_ref, dkp_ref, dvp_ref,
             dr_ref, dw_ref, dk_ref, dv_ref, da_ref, db_ref, ds_scr, s1_scr):
        @pl.when(pl.program_id(2) == 0)
        def _():
            ds_scr[...] = jnp.zeros_like(ds_scr)
            s1_scr[...] = last_ref[...]
        pick, same = _pair_consts()
        rid = lax.broadcasted_iota(jnp.int32, (SUB, LANES), 0)
        out_refs = (da_ref, dw_ref, db_ref, dk_ref, dr_ref)

        def group(n, carry):
            base = pl.multiple_of((tc // SUB - 1 - n) * SUB, SUB)
            rows8 = pl.ds(base, SUB)
            lanes = [pl.ds(LANES * p, LANES) for p in range(G)]
            blk = [[x[q, rows8, lanes[p]] for x in (r_ref, w_ref, k_ref, v_ref, a_ref, b_ref)] for q, p in chains]
            dys = [dy_ref[q, rows8, lanes[p]] for q, p in chains]
            sas = [sa_ref[q, rows8, lanes[p]] for q, p in chains]
            ds = [ds_scr[q, p] for q, p in chains]
            s1 = [s1_scr[q, p] for q, p in chains]
            zero = jnp.zeros((SUB, LANES), F32)
            gs = [[zero, zero, zero, dkp_ref[q, rows8, lanes[p]], drp_ref[q, rows8, lanes[p]]] for q, p in chains]
            gv = [dvp_ref[q, rows8, lanes[p]] for q, p in chains]
            steps = tuple(reversed(range(SUB)))
            tiles_of = [_rwkv_cols([tuple(b_[i:i + 1] for b_ in blk[n_]) for n_ in range(len(chains))], pick, same)
                        for i in steps]
            adj_of = []
            for i, tiles in zip(steps, tiles_of):
                adj = []
                for n_, (q, p) in enumerate(chains):
                    s0 = st_ref[q, base + i, p]
                    ds[n_], g_tiles, dv = _rwkv_step_bwd(s0, s1[n_], sas[n_][i:i + 1], tiles[n_], blk[n_][3][i:i + 1],
                                                         dys[n_][i:i + 1], ds[n_])
                    s1[n_] = s0
                    adj += list(g_tiles)
                    gv[n_] = gv[n_] + jnp.where(rid == i, dv, 0.0)
                adj_of.append(adj)
            for i, adj in zip(steps, adj_of):
                rows = _cols_adjoint(adj)
                for n_ in range(len(chains)):
                    for j in range(N_COL):
                        gs[n_][j] = gs[n_][j] + jnp.where(rid == i, rows[N_COL * n_ + j], 0.0)
            for n_, (q, p) in enumerate(chains):
                ds_scr[q, p] = ds[n_]
                s1_scr[q, p] = s1[n_]
                dv_ref[q, rows8, lanes[p]] = gv[n_]
                for j in range(N_COL):
                    out_refs[j][q, rows8, lanes[p]] = gs[n_][j]
            return carry

        lax.fori_loop(0, tc // SUB, group, 0)

    seq3 = lambda x: x.reshape(B, seq, C)
    outs = pl.pallas_call(body, name=name, grid=(B // NB, NP // G, nc),
                          in_specs=[row] * 6 + [st, row, last_spec] + [row] * 4, out_specs=[row] * 6,
                          out_shape=[jax.ShapeDtypeStruct((B, seq, C), F32)] * 6,
                          scratch_shapes=[pltpu.VMEM((NB, G, RWKV_N, LANES), F32)] * 2,
                          compiler_params=_cparams("parallel", "parallel", "arbitrary"))(
                              *[seq3(x) for x in (r, w, k, v, a, b)], states.reshape(B, seq, NP, RWKV_N, LANES), seq3(sa),
                              last, *[seq3(x) for x in (dy, dr_p, dk_p, dv_p)])
    return [o.reshape(T, C) for o in outs]


def _all_gather(name, shard):
    def body(x_ref, o_ref, send_sems, recv_sems, local_sem):
        x, y, c = lax.axis_index("x"), lax.axis_index("y"), lax.axis_index("c")
        me, sibling = (x, y, c), (x, y, 1 - c)
        chips = [(1 - x, y), (x, 1 - y), (1 - x, 1 - y)]

        def copy(k, block, to, src=None):
            dst = o_ref.at[4 * block[0] + 2 * block[1] + block[2]]
            return pltpu.make_async_remote_copy(src_ref=dst if src is None else src, dst_ref=dst, send_sem=send_sems.at[k],
                                                recv_sem=recv_sems.at[k], device_id=to, device_id_type=MESH)

        mine = pltpu.make_async_copy(x_ref, o_ref.at[4 * x + 2 * y + c], local_sem)
        mine.start()
        first = [copy(0, me, sibling, src=x_ref)] + [copy(1 + j, me, (*chip, c), src=x_ref) for j, chip in enumerate(chips)]
        for cp in first:
            cp.start()
        passed = [copy(4 + j, (*chip, c), sibling) for j, chip in enumerate(chips)]
        for j, chip in enumerate(chips):
            copy(1 + j, (*chip, c), me).wait_recv()
            passed[j].start()
        copy(0, sibling, me).wait_recv()
        for j, chip in enumerate(chips):
            copy(4 + j, (*chip, 1 - c), me).wait_recv()
        for cp in first + passed:
            cp.wait_send()
        mine.wait()

    hbm = pl.BlockSpec(memory_space=pltpu.HBM)
    return pl.pallas_call(body, name=name, in_specs=[hbm], out_specs=hbm,
                          out_shape=jax.ShapeDtypeStruct((N_DEV,) + shard.shape, shard.dtype),
                          scratch_shapes=[pltpu.SemaphoreType.DMA((N_DEV - 1,)), pltpu.SemaphoreType.DMA((N_DEV - 1,)),
                                          pltpu.SemaphoreType.DMA])(shard)


N_CHIP = 4


def _swap_with_sibling(name, blocks, n_split):
    _, R, W = blocks.shape
    N = N_CHIP
    rq = R // n_split

    def body(x_ref, got_ref, send_sems, recv_sems):
        x, y, c = lax.axis_index("x"), lax.axis_index("y"), lax.axis_index("c")
        copies = []
        for j in range(N):
            for q in range(n_split):
                rows = pl.ds(q * rq, rq)
                cp = pltpu.make_async_remote_copy(src_ref=x_ref.at[2 * j + 1 - c, rows], dst_ref=got_ref.at[j, rows],
                                                  send_sem=send_sems.at[j * n_split + q], recv_sem=recv_sems.at[j * n_split + q],
                                                  device_id=(x, y, 1 - c), device_id_type=MESH)
                cp.start()
                copies.append(cp)
        for cp in copies:
            cp.wait()

    hbm = pl.BlockSpec(memory_space=pltpu.HBM)
    return pl.pallas_call(body, name=name, in_specs=[hbm], out_specs=hbm,
                          out_shape=jax.ShapeDtypeStruct((N, R, W), blocks.dtype),
                          scratch_shapes=[pltpu.SemaphoreType.DMA((N * n_split,)), pltpu.SemaphoreType.DMA((N * n_split,))])(blocks)


def _add_own_blocks(name, blocks, got, tr, out_dtype):
    N, R, W = got.shape

    def body(c_ref, a_ref, b_ref, o_ref):
        o_ref[...] = (a_ref[...] + b_ref[...]).astype(o_ref.dtype)

    grid_spec = pltpu.PrefetchScalarGridSpec(
        num_scalar_prefetch=1, grid=(N, R // tr),
        in_specs=[pl.BlockSpec((None, tr, W), lambda n, i, c: (2 * n + c[0], i, 0)),
                  pl.BlockSpec((None, tr, W), lambda n, i, c: (n, i, 0))],
        out_specs=pl.BlockSpec((None, tr, W), lambda n, i, c: (n, i, 0)))
    core = lax.axis_index("c").astype(jnp.int32).reshape(1)
    return pl.pallas_call(body, name=name, grid_spec=grid_spec, out_shape=jax.ShapeDtypeStruct(got.shape, out_dtype),
                          compiler_params=_cparams("parallel", "parallel"))(core, blocks, got)


def _exchange_chips(name, blocks):
    def body(z_ref, o_ref, send_sems, recv_sems, local_sem):
        x, y, c = lax.axis_index("x"), lax.axis_index("y"), lax.axis_index("c")
        chip = 2 * x + y
        mine = pltpu.make_async_copy(z_ref.at[chip], o_ref.at[chip], local_sem)
        mine.start()
        copies = []
        for k in range(1, N_CHIP):
            to = (x ^ (k >> 1), y ^ (k & 1), c)
            cp = pltpu.make_async_remote_copy(src_ref=z_ref.at[chip ^ k], dst_ref=o_ref.at[chip], send_sem=send_sems.at[k - 1],
                                              recv_sem=recv_sems.at[k - 1], device_id=to, device_id_type=MESH)
            cp.start()
            copies.append(cp)
        for k in range(1, N_CHIP):
            pltpu.make_async_remote_copy(src_ref=z_ref.at[chip], dst_ref=o_ref.at[chip ^ k], send_sem=send_sems.at[k - 1],
                                         recv_sem=recv_sems.at[k - 1], device_id=(x ^ (k >> 1), y ^ (k & 1), c),
                                         device_id_type=MESH).wait_recv()
        for cp in copies:
            cp.wait_send()
        mine.wait()

    hbm = pl.BlockSpec(memory_space=pltpu.HBM)
    return pl.pallas_call(body, name=name, in_specs=[hbm], out_specs=hbm,
                          out_shape=jax.ShapeDtypeStruct(blocks.shape, blocks.dtype),
                          scratch_shapes=[pltpu.SemaphoreType.DMA((N_CHIP - 1,)), pltpu.SemaphoreType.DMA((N_CHIP - 1,)),
                                          pltpu.SemaphoreType.DMA])(blocks)


def _adamw(name, parts, w, m, v, tr):
    R, W = w.shape
    n_parts = parts.shape[0]

    def body(p_ref, w_ref, m_ref, v_ref, g_ref, d_ref, nm_ref, nv_ref):
        g = p_ref[0].astype(F32)
        for s in range(1, n_parts):
            g = g + p_ref[s].astype(F32)
        nm = ADAM_B1 * m_ref[...] + (1.0 - ADAM_B1) * g
        nv = ADAM_B2 * v_ref[...] + (1.0 - ADAM_B2) * (g * g)
        m_hat = nm / (1.0 - ADAM_B1 ** ADAM_STEP)
        v_hat = nv / (1.0 - ADAM_B2 ** ADAM_STEP)
        g_ref[...] = g
        d_ref[...] = -ADAM_LR * (m_hat / (jnp.sqrt(v_hat) + ADAM_EPS) + ADAM_WD * w_ref[...])
        nm_ref[...] = nm
        nv_ref[...] = nv

    spec = pl.BlockSpec((tr, W), lambda i: (i, 0))
    return pl.pallas_call(body, name=name, grid=(R // tr,),
                          in_specs=[pl.BlockSpec((n_parts, tr, W), lambda i: (0, i, 0)), spec, spec, spec],
                          out_specs=[spec] * 4, out_shape=[jax.ShapeDtypeStruct((R, W), F32)] * 4,
                          compiler_params=_cparams("parallel"))(parts, w, m, v)


def _rms(x, g):
    return x * lax.rsqrt(jnp.mean(x * x, axis=-1, keepdims=True) + RMS_EPS) * g


def _softplus(z):
    return jnp.maximum(z, 0.0) + jnp.log(1.0 + jnp.exp(-jnp.abs(z)))


def _neg_expm1(z):
    series = -z * (1.0 + z * (0.5 + z * (1.0 / 6.0)))
    return jnp.where(z > -1e-3, series, 1.0 - jnp.exp(z))


def _gelu_tanh(x):
    return 0.5 * x * (1.0 + jnp.tanh(0.7978845608028654 * (x + 0.044715 * x * x * x)))


def _bdot(x, w):
    return jnp.dot(x.astype(BF16), w.astype(BF16), preferred_element_type=F32)


def _seg_consts(C):
    c0 = lax.broadcasted_iota(jnp.int32, (C, LANES), 0)
    h1 = lax.broadcasted_iota(jnp.int32, (C, LANES), 1)
    red = ((c0 >> 6) == h1).astype(BF16)
    h0 = lax.broadcasted_iota(jnp.int32, (LANES, C), 0)
    c1 = lax.broadcasted_iota(jnp.int32, (LANES, C), 1)
    exp = ((c1 >> 6) == h0).astype(BF16)
    return red, exp


def _split_dot(x, m):
    hi = x.astype(BF16)
    lo = (x - hi.astype(F32)).astype(BF16)
    return jnp.dot(hi, m, preferred_element_type=F32) + jnp.dot(lo, m, preferred_element_type=F32)


def _segsum_raw(x):
    red, exp = _seg_consts(x.shape[1])
    return _split_dot(_split_dot(x, red), exp)


@jax.custom_vjp
def _segsum(x):
    return _segsum_raw(x)


_segsum.defvjp(lambda x: (_segsum_raw(x), None), lambda _, g: (_segsum_raw(g),))


def _f_pre(out_dtype):
    def fn(pos, x, g):
        return (_rms(x, g).astype(out_dtype),)
    return fn


def _f_post(pos, x, t, g):
    return (x + _rms(t, g),)


def _f_post_pre(out_dtype, has_bias):
    def fn(pos, x, t, g_post, g_pre, *bias):
        x1 = x + _rms(t + bias[0] if has_bias else t, g_post)
        return x1, _rms(x1, g_pre).astype(out_dtype)
    return fn


def _f_conv(pos, proj, p1, p2, p3, b_u, w0, w1, w2, w3, cb):
    D = b_u.shape[1]
    u0 = proj[:, D:] + b_u
    u1 = jnp.where(pos >= 1, p1[:, D:] + b_u, 0.0)
    u2 = jnp.where(pos >= 2, p2[:, D:] + b_u, 0.0)
    u3 = jnp.where(pos >= 3, p3[:, D:] + b_u, 0.0)
    return (cb + u3 * w0 + u2 * w1 + u1 * w2 + u0 * w3,)


def _f_conv_bwd(pos, dconv, proj, dyb, n1, n2, n3, p1, p2, p3, b_u, w0, w1, w2, w3):
    D = b_u.shape[1]
    du = dconv * w3 + n1 * w2 + n2 * w1 + n3 * w0
    dproj = jnp.concatenate([dyb[:, :D], du], axis=1)
    u0 = proj[:, D:] + b_u
    u1 = jnp.where(pos >= 1, p1[:, D:] + b_u, 0.0)
    u2 = jnp.where(pos >= 2, p2[:, D:] + b_u, 0.0)
    u3 = jnp.where(pos >= 3, p3[:, D:] + b_u, 0.0)
    rs = lambda z: jnp.sum(z, axis=0, keepdims=True)
    return (dproj, rs(dproj), rs(dconv * u3), rs(dconv * u2), rs(dconv * u1), rs(dconv * u0), rs(dconv))


def _f_gates(pos, conv, proj, gate_w, gb_r, gb_i, lam, b_y):
    D = conv.shape[1]
    blk = D // LRU_HEADS
    cb = conv.astype(BF16)

    def gate(g, bias):
        z = [jnp.dot(cb[:, h * blk:(h + 1) * blk], gate_w[(g * LRU_HEADS + h) * blk:(g * LRU_HEADS + h + 1) * blk, :].astype(BF16),
                     preferred_element_type=F32) for h in range(LRU_HEADS)]
        return jax.nn.sigmoid(jnp.concatenate(z, axis=1) + bias)

    r_gate, i_gate = gate(0, gb_r), gate(1, gb_i)
    log_a = -LRU_C * r_gate * _softplus(-lam)
    a = jnp.exp(log_a)
    b = jnp.sqrt(_neg_expm1(2.0 * log_a)) * i_gate * conv
    yb = _gelu_tanh(proj[:, :D] + b_y)
    return a, b, yb


def _f_hy(pos, h, yb):
    return ((h * yb).astype(BF16),)


def _f_mix(pos, hn, hp, mu_r, mu_w, mu_k, mu_v, mu_a, mu_g, w0, w1, w2, a0, a1, a2, g1, g2):
    xx = hp - hn
    xr, xw, xk, xv, xa, xg = (hn + xx * m for m in (mu_r, mu_w, mu_k, mu_v, mu_a, mu_g))
    lw = w0 + _bdot(jnp.tanh(_bdot(xw, w1)), w2)
    decay = jnp.exp(-jnp.exp(-_softplus(-lw) - 0.5))
    a = jax.nn.sigmoid(a0 + _bdot(_bdot(xa, a1), a2))
    g = _bdot(jax.nn.sigmoid(_bdot(xg, g1)), g2)
    return xr.astype(BF16), xk.astype(BF16), xv.astype(BF16), decay, a, g


def _f_kk(pos, k, a, k_k, k_a):
    kk = k * k_k
    kk = kk / jnp.maximum(jnp.sqrt(_segsum(kk * kk)), 1e-12)
    return k * (1.0 + (a - 1.0) * k_a), -kk, kk * a


def _f_gn(pos, y, r, k2, v, g, gn_g, gn_b, r_k):
    inv_n = 1.0 / RWKV_N
    yc = y - _segsum(y) * inv_n
    var = _segsum(yc * yc) * inv_n
    yn = yc * lax.rsqrt(var + GN_EPS) * gn_g + gn_b
    bonus = _segsum(r * k2 * r_k) * v
    return (((yn + bonus) * g).astype(BF16),)


def _f_attn(pos, q, kv):
    D = q.shape[1]
    hd = D // XATTN_HEADS
    qb, kb, vb = q.astype(BF16), kv[:, :D].astype(BF16), kv[:, D:].astype(BF16)
    outs = []
    for h in range(XATTN_HEADS):
        sl = slice(h * hd, (h + 1) * hd)
        s = lax.dot_general(qb[:, sl], kb[:, sl], (((1,), (1,)), ((), ())), preferred_element_type=F32) * (hd ** -0.5)
        s = s - jnp.max(s, axis=-1, keepdims=True)
        e = jnp.exp(s)
        p = e / jnp.sum(e, axis=-1, keepdims=True)
        outs.append(jnp.dot(p.astype(BF16), vb[:, sl], preferred_element_type=F32))
    return (jnp.concatenate(outs, axis=1).astype(BF16),)


def _f_loss(pos, y, target):
    d = y - target
    inv = 1.0 / y.shape[1]
    part = 0.5 * inv * jnp.sum(jnp.sum(d * d, axis=1, keepdims=True), axis=0, keepdims=True)
    return d * inv, jnp.broadcast_to(part, (8, LANES))


def _full_from_blocks(blocks, shard_shape, ax):
    g = jnp.moveaxis(blocks.reshape((N_DEV,) + tuple(shard_shape)), 0, ax)
    return g.reshape(tuple(shard_shape[:ax]) + (N_DEV * shard_shape[ax],) + tuple(shard_shape[ax + 1:]))


def _blocks_from_full(full, shard_shape, ax):
    g = full.reshape(tuple(shard_shape[:ax]) + (N_DEV, shard_shape[ax]) + tuple(shard_shape[ax + 1:]))
    return jnp.moveaxis(g, ax, 0).reshape(N_DEV, -1)


def _numel(shape):
    n = 1
    for s in shape:
        n *= s
    return n


def _pack_flat(arrs, total):
    flat = jnp.concatenate([a.reshape(-1).astype(F32) for a in arrs])
    return jnp.pad(flat, (0, total - flat.shape[0]))


def _train_step(x3, mem3, target3, P, M_, V_):
    Bl, S, D = x3.shape
    T = Bl * S
    ML = mem3.shape[1]
    x = x3.reshape(T, D)
    mem = mem3.reshape(Bl * ML, D)
    target = target3.reshape(T, D)
    shp = {n: P[n].shape for n in WEIGHTS}

    n_big = sum(_numel(shp[n]) for n in BIG)
    n_small = sum(_numel(shp[n]) for n in SMALL)
    n_small_pad = -(-n_small // (PACK_ROWS * PACK_W)) * (PACK_ROWS * PACK_W)
    big_rows, small_rows = n_big // PACK_W, n_small_pad // PACK_W
    def pack_rows(src):
        big = [src[n].reshape(-1, PACK_W) for n in BIG]
        return jnp.concatenate(big + [_pack_flat([src[n] for n in SMALL], n_small_pad).reshape(small_rows, PACK_W)], axis=0)

    big_pack = jnp.concatenate([P[n].reshape(-1, PACK_W) for n in BIG], axis=0).astype(BF16)
    small_pack = _pack_flat([P[n] for n in SMALL], n_small_pad).reshape(small_rows, PACK_W)
    big_all = _all_gather("gather_matrices", big_pack)
    small_all = _all_gather("gather_vectors", small_pack).reshape(N_DEV, n_small_pad)
    Wf = {}
    big_row0 = {}
    off = 0
    for n in BIG:
        nr = _numel(shp[n]) // PACK_W
        big_row0[n] = off
        Wf[n] = _full_from_blocks(big_all[:, off:off + nr], shp[n], SHARD_AXIS[n])
        off += nr
    off = 0
    for n in SMALL:
        ne = _numel(shp[n])
        Wf[n] = _full_from_blocks(small_all[:, off:off + ne], shp[n], SHARD_AXIS[n])
        off += ne
    for n in REPL:
        Wf[n] = P[n]

    row = lambda v: v.reshape(1, -1).astype(F32)
    gains = [[row(Wf['ln_gains'][i, j]) for j in range(6)] for i in range(2)]
    LW = 128

    def pad_cols(w):
        return jnp.pad(w, ((0, 0), (0, LW - w.shape[1]))).astype(F32)

    def pad_rows(w):
        return jnp.pad(w, ((0, LW - w.shape[0]), (0, 0))).astype(F32)

    G = {}
    saved = {}

    mem_n, = _rw("mem_norm_fwd", _f_pre(BF16), [mem], [row(Wf['mem_norm'])], outs=[(D, BF16)])

    def post_pre(name, xin, t, g_post, g_pre, out_dtype, bias=()):
        return _rw(name, _f_post_pre(out_dtype, bool(bias)), [xin, t], [g_post, g_pre, *bias],
                   outs=[(D, F32), (D, out_dtype)], seq=S)

    def post_pre_bwd(name, xin, t, g_post, g_pre, dx_next, dhn, bias=(), ct_adds=()):
        return _rw_vjp(name, _f_post_pre(F32, bool(bias)), [xin, t], [g_post, g_pre, *bias], [dx_next, dhn], 2,
                       d_rows=[(0, F32), (1, BF16)], d_params=list(range(2 + len(bias))), ct_adds=ct_adds, seq=S)

    def xattn_fwd(i, hq):
        q = _mm_nn(f"xattn{i}_q", hq, Wf['c_w_q'][i], BF16)
        kv = _mm_nn(f"xattn{i}_kv", mem_n, Wf['c_w_kv'][i], F32)
        kv = kv.reshape(Bl, ML, 2 * D)
        o, = _rw(f"xattn{i}_attn", _f_attn, [q], [kv], outs=[(D, BF16)], seq=S, per_b=(0,))
        saved[f"xattn{i}"] = (hq, q, kv, o)
        return _mm_nn(f"xattn{i}_o", o, Wf['c_w_o'][i], F32)

    def mlp_fwd(i, hm):
        up, act = _mm_up_act(f"mlp{i}_up", hm, Wf['m_w_up'][i])
        saved[f"mlp{i}"] = (hm, up, act)
        return _mm_nn(f"mlp{i}_down", act, Wf['m_w_down'][i], F32)

    b_in = row(Wf['a_b_in'][0])
    b_y, b_u = b_in[:, :D], b_in[:, D:]
    cw = [row(Wf['a_conv_w'][0, t]) for t in range(4)]
    cb = row(Wf['a_conv_b'][0])
    gate_w = Wf['a_gate_w'][0].reshape(2 * LRU_HEADS * (D // LRU_HEADS), D // LRU_HEADS).astype(F32)
    gate_b = Wf['a_gate_b'][0].reshape(2, D).astype(F32)
    gb_r, gb_i = gate_b[0:1], gate_b[1:2]
    lam = row(Wf['a_lambda'][0])
    b_out = row(Wf['a_b_out'][0])

    hn0, = _rw("lru_pre", _f_pre(BF16), [x], [gains[0][0]], outs=[(D, BF16)], seq=S)
    proj = _mm_nn("lru_in", hn0, Wf['a_w_in'][0], F32)
    prev3 = [(0, 1), (0, 2), (0, 3)]
    conv, = _rw("lru_conv", _f_conv, [proj], [b_u] + cw + [cb], outs=[(D, F32)], shifts=prev3, seq=S)
    gate_par = [gate_w, gb_r, gb_i, lam, b_y]
    a_l, b_l, yb = _rw("lru_gates", _f_gates, [conv, proj], gate_par, outs=[(D, F32)] * 3, seq=S)
    h_l = _lru_scan("lru_scan", a_l, b_l, S)
    hy, = _rw("lru_hy", _f_hy, [h_l, yb], outs=[(D, BF16)], seq=S)
    t0 = _mm_nn("lru_out", hy, Wf['a_w_out'][0], F32)
    x1, hq0 = post_pre("lru_post", x, t0, gains[0][1], gains[0][2], BF16, bias=(b_out,))
    c0 = xattn_fwd(0, hq0)
    x2, hm0 = post_pre("xattn0_post", x1, c0, gains[0][3], gains[0][4], BF16)
    m0 = mlp_fwd(0, hm0)
    x3_, hn1 = post_pre("mlp0_post", x2, m0, gains[0][5], gains[1][0], F32)

    mu = [row(Wf['b_mu'][0, j]) for j in range(6)]
    lora = [row(Wf['b_w0'][0]), pad_cols(Wf['b_w1'][0]), pad_rows(Wf['b_w2'][0]),
            row(Wf['b_a0'][0]), pad_cols(Wf['b_a1'][0]), pad_rows(Wf['b_a2'][0]),
            Wf['b_g1'][0].astype(F32), Wf['b_g2'][0].astype(F32)]
    k_k, k_a = row(Wf['b_k_k'][0]), row(Wf['b_k_a'][0])
    gn_g, gn_b, r_k = row(Wf['b_gn_g'][0]), row(Wf['b_gn_b'][0]), row(Wf['b_r_k'][0])

    xr, xk, xv, decay, a_r, g_r = _rw("rwkv_mix", _f_mix, [hn1], mu + lora,
                                      outs=[(D, BF16)] * 3 + [(D, F32)] * 3, shifts=[(0, 1)], seq=S)
    r_ = _mm_nn("rwkv_r", xr, Wf['b_w_rkv'][0, 0], F32)
    k_ = _mm_nn("rwkv_k", xk, Wf['b_w_rkv'][0, 1], F32)
    v_ = _mm_nn("rwkv_v", xv, Wf['b_w_rkv'][0, 2], F32)
    k2, ra, rb = _rw("rwkv_kk", _f_kk, [k_, a_r], [k_k, k_a], outs=[(D, F32)] * 3, seq=S)
    y_r, states, sa_r, last_r = _rwkv_fwd("rwkv_scan", r_, decay, k2, v_, ra, rb, S)
    gn_par = [gn_g, gn_b, r_k]
    og, = _rw("rwkv_gn", _f_gn, [y_r, r_, k2, v_, g_r], gn_par, outs=[(D, BF16)], seq=S)
    t1 = _mm_nn("rwkv_out", og, Wf['b_w_o'][0], F32)
    x4, hq1 = post_pre("rwkv_post", x3_, t1, gains[1][1], gains[1][2], BF16)
    c1 = xattn_fwd(1, hq1)
    x5, hm1 = post_pre("xattn1_post", x4, c1, gains[1][3], gains[1][4], BF16)
    m1 = mlp_fwd(1, hm1)
    x6, = _rw("mlp1_post", _f_post, [x5, m1], [gains[1][5]], outs=[(D, F32)], seq=S)

    dx, loss_acc = _rw("loss", _f_loss, [x6, target], outs=[(D, F32)], accs=[(8, LANES)], seq=S)
    loss = lax.psum(loss_acc[0, 0], ("x", "y", "c"))

    dgain = [[None] * 6 for _ in range(2)]
    dmem_n = None

    def mlp_bwd(i, dm):
        hm, up, act = saved[f"mlp{i}"]
        G_down = _mm_tn(f"mlp{i}_down_dw", act, dm)
        dup = _mm_nt_act_bwd(f"mlp{i}_down_dx", dm, Wf['m_w_down'][i], up)
        G_up = _mm_tn(f"mlp{i}_up_dw", hm, dup)
        return _mm_nt(f"mlp{i}_up_dx", dup, Wf['m_w_up'][i], F32), G_up, G_down

    def xattn_bwd(i, dc):
        hq, q, kv, o = saved[f"xattn{i}"]
        G_o = _mm_tn(f"xattn{i}_o_dw", o, dc)
        do = _mm_nt(f"xattn{i}_o_dx", dc, Wf['c_w_o'][i], F32)
        dq, dkv = _rw_vjp(f"xattn{i}_attn_bwd", _f_attn, [q], [kv], [do], 1, d_rows=[(0, BF16)],
                          d_params=[0], seq=S, per_b=(0,))
        dkv = dkv.reshape(Bl * ML, 2 * D)
        G_kv = _mm_tn(f"xattn{i}_kv_dw", mem_n, dkv)
        dmn = _mm_nt(f"xattn{i}_kv_dx", dkv, Wf['c_w_kv'][i], F32)
        G_q = _mm_tn(f"xattn{i}_q_dw", hq, dq)
        dhq = _mm_nt(f"xattn{i}_q_dx", dq, Wf['c_w_q'][i], F32)
        return dhq, G_q, G_kv, G_o, dmn

    G_up, G_down, G_q, G_kv, G_o = [None] * 2, [None] * 2, [None] * 2, [None] * 2, [None] * 2
    dm1, dgain[1][5] = _rw_vjp("mlp1_post_bwd", _f_post, [x5, m1], [gains[1][5]], [dx], 1,
                               d_rows=[(1, BF16)], d_params=[0], seq=S)
    dhm1, G_up[1], G_down[1] = mlp_bwd(1, dm1)
    dx, dc1, dgain[1][3], dgain[1][4] = post_pre_bwd("xattn1_post_bwd", x4, c1, gains[1][3], gains[1][4], dx, dhm1)
    dhq1, G_q[1], G_kv[1], G_o[1], dmn1 = xattn_bwd(1, dc1)

    dx, dt1, dgain[1][1], dgain[1][2] = post_pre_bwd("rwkv_post_bwd", x3_, t1, gains[1][1], gains[1][2], dx, dhq1)
    G['b_w_o'] = _mm_tn("rwkv_out_dw", og, dt1)[None]
    dog = _mm_nt("rwkv_out_dx", dt1, Wf['b_w_o'][0], F32)
    dy_r, dr_p, dk2_p, dv_p, dg_r, d_gn_g, d_gn_b, d_r_k = _rw_vjp(
        "rwkv_gn_bwd", _f_gn, [y_r, r_, k2, v_, g_r], gn_par, [dog], 1,
        d_rows=[(j, F32) for j in range(5)], d_params=[0, 1, 2], seq=S)
    dr_, ddecay, dk2, dv_, dra, drb = _rwkv_bwd("rwkv_scan_bwd", r_, decay, k2, v_, ra, rb, states, sa_r, last_r, dy_r,
                                                dr_p, dk2_p, dv_p, S)
    dk_, da_r, d_k_k, d_k_a = _rw_vjp("rwkv_kk_bwd", _f_kk, [k_, a_r], [k_k, k_a], [dk2, dra, drb], 3,
                                      d_rows=[(0, F32), (1, F32)], d_params=[0, 1], seq=S)
    G_rkv = [_mm_tn("rwkv_r_dw", xr, dr_), _mm_tn("rwkv_k_dw", xk, dk_), _mm_tn("rwkv_v_dw", xv, dv_)]
    G['b_w_rkv'] = jnp.stack(G_rkv)[None]
    dxr = _mm_nt("rwkv_r_dx", dr_, Wf['b_w_rkv'][0, 0], F32)
    dxk = _mm_nt("rwkv_k_dx", dk_, Wf['b_w_rkv'][0, 1], F32)
    dxv = _mm_nt("rwkv_v_dx", dv_, Wf['b_w_rkv'][0, 2], F32)
    mix_out = _rw_vjp("rwkv_mix_bwd", _f_mix, [hn1], mu + lora, [dxr, dxk, dxv, ddecay, da_r, dg_r], 6,
                      d_rows=[(0, F32)], d_shifts=[(0, F32)], d_params=list(range(14)), shifts=[(0, 1)], seq=S)
    dhn1, dhp1 = mix_out[0], mix_out[1]
    d_mu = mix_out[2:8]
    d_lora = mix_out[8:16]
    dx, dm0, dgain[0][5], dgain[1][0] = post_pre_bwd("mlp0_post_bwd", x2, m0, gains[0][5], gains[1][0], dx, dhn1,
                                                     ct_adds=[(1, dhp1, -1)])
    G['b_mu'] = jnp.concatenate(d_mu, axis=0)[None]
    G['b_w0'], G['b_a0'] = d_lora[0], d_lora[3]
    G['b_w1'] = d_lora[1][:, :shp['b_w1'][2]][None]
    G['b_w2'] = d_lora[2][:shp['b_w2'][1]][None]
    G['b_a1'] = d_lora[4][:, :shp['b_a1'][2]][None]
    G['b_a2'] = d_lora[5][:shp['b_a2'][1]][None]
    G['b_g1'], G['b_g2'] = d_lora[6][None], d_lora[7][None]
    G['b_k_k'], G['b_k_a'], G['b_gn_g'], G['b_gn_b'] = d_k_k, d_k_a, d_gn_g, d_gn_b
    G['b_r_k'] = d_r_k.reshape(P['b_r_k'].shape)

    dhm0, G_up[0], G_down[0] = mlp_bwd(0, dm0)
    dx, dc0, dgain[0][3], dgain[0][4] = post_pre_bwd("xattn0_post_bwd", x1, c0, gains[0][3], gains[0][4], dx, dhm0)
    dhq0, G_q[0], G_kv[0], G_o[0], dmn0 = xattn_bwd(0, dc0)
    G['m_w_up'], G['m_w_down'] = jnp.stack(G_up), jnp.stack(G_down)
    G['c_w_q'], G['c_w_kv'], G['c_w_o'] = jnp.stack(G_q), jnp.stack(G_kv), jnp.stack(G_o)

    dx, dt0, dgain[0][1], dgain[0][2], G['a_b_out'] = post_pre_bwd("lru_post_bwd", x, t0, gains[0][1], gains[0][2],
                                                                  dx, dhq0, bias=(b_out,))
    G['a_w_out'] = _mm_tn("lru_out_dw", hy, dt0)[None]
    dhy = _mm_nt("lru_out_dx", dt0, Wf['a_w_out'][0], F32)
    dh_l, dyb = _rw_vjp("lru_hy_bwd", _f_hy, [h_l, yb], [], [dhy], 1, d_rows=[(0, F32), (1, F32)], seq=S)
    da_l, db_l = _lru_scan_bwd("lru_scan_bwd", a_l, h_l, dh_l, S)
    dconv, dproj_y, d_gate_w, d_gb_r, d_gb_i, G['a_lambda'] = _rw_vjp(
        "lru_gates_bwd", _f_gates, [conv, proj], gate_par, [da_l, db_l, dyb], 3,
        d_rows=[(0, F32), (1, F32)], d_params=[0, 1, 2, 3], seq=S)
    next3 = [(0, -1), (0, -2), (0, -3)]
    dproj, G['a_b_in'], dw0, dw1, dw2, dw3, G['a_conv_b'] = _rw(
        "lru_conv_bwd", _f_conv_bwd, [dconv, proj, dproj_y], [b_u] + cw, outs=[(2 * D, BF16)],
        accs=[(1, 2 * D)] + [(1, D)] * 5, shifts=next3 + [(1, 1), (1, 2), (1, 3)], seq=S)
    G['a_conv_w'] = jnp.concatenate([dw0, dw1, dw2, dw3], axis=0)[None]
    G['a_gate_w'] = d_gate_w.reshape(P['a_gate_w'].shape[:3] + (D // LRU_HEADS, D // LRU_HEADS))
    G['a_gate_b'] = jnp.concatenate([d_gb_r, d_gb_i], axis=0).reshape(1, 2, LRU_HEADS, D // LRU_HEADS)
    G['a_w_in'] = _mm_tn("lru_in_dw", hn0, dproj)[None]
    dhn0 = _mm_nt("lru_in_dx", dproj, Wf['a_w_in'][0], F32)
    grad_x, dgain[0][0] = _rw_vjp("lru_pre_bwd", _f_pre(F32), [x], [gains[0][0]], [dhn0], 1, d_rows=[(0, F32)],
                                  d_params=[0], adds=[(0, dx, 0)], seq=S)
    d_mem_norm, = _rw_vjp("mem_norm_bwd", _f_pre(F32), [mem], [row(Wf['mem_norm'])], [dmn0], 1, d_params=[0],
                          ct_adds=[(0, dmn1, 0)])
    G['mem_norm'] = d_mem_norm.reshape(-1)
    G['ln_gains'] = jnp.stack([jnp.concatenate(dgain[i], axis=0) for i in range(2)])

    sharded = BIG + SMALL
    rows_s = big_rows + small_rows
    tr = max(t for t in range(BF16_ROWS, 161, BF16_ROWS) if rows_s % t == 0)

    pieces = [_blocks_from_full(G[n], shp[n], SHARD_AXIS[n]).reshape(N_DEV, -1, PACK_W) for n in BIG]
    small = [_blocks_from_full(G[n], shp[n], SHARD_AXIS[n]) for n in SMALL] + [jnp.zeros((N_DEV, n_small_pad - n_small), F32)]
    pieces.append(jnp.concatenate(small, axis=1).reshape(N_DEV, small_rows, PACK_W))
    g_blocks = jnp.concatenate(pieces, axis=1)
    n_split = max(q for q in range(1, 10) if rows_s % (q * BF16_ROWS) == 0)
    got = _swap_with_sibling("exchange_grads_sibling", g_blocks, n_split)
    parts_s = _exchange_chips("exchange_grads_chips", _add_own_blocks("sum_with_sibling", g_blocks, got, tr, BF16))
    n_repl = sum(_numel(shp[n]) for n in REPL)
    n_repl_pad = -(-n_repl // (8 * PACK_W)) * (8 * PACK_W)
    rows_r = n_repl_pad // PACK_W
    parts_r = _all_gather("gather_replicated_grads", _pack_flat([G[n] for n in REPL], n_repl_pad).reshape(rows_r, PACK_W))

    def pack(src, names, total, rows_):
        return _pack_flat([src[n] for n in names], total).reshape(rows_, PACK_W)

    out_s = _adamw("adamw_sharded", parts_s, pack_rows(P), pack_rows(M_), pack_rows(V_), tr)
    out_r = _adamw("adamw_replicated", parts_r, pack(P, REPL, n_repl_pad, rows_r), pack(M_, REPL, n_repl_pad, rows_r),
                   pack(V_, REPL, n_repl_pad, rows_r), rows_r)

    def unpack(packed, names):
        flat = packed.reshape(-1)
        res, o = {}, 0
        for n in names:
            ne = _numel(shp[n])
            res[n] = flat[o:o + ne].reshape(shp[n])
            o += ne
        return res

    results = []
    for j in range(4):
        both = {n: out_s[j][big_row0[n]:big_row0[n] + _numel(shp[n]) // PACK_W].reshape(shp[n]) for n in BIG}
        both.update(unpack(out_s[j][big_rows:], SMALL))
        both.update(unpack(out_r[j], REPL))
        results += [both[n] for n in WEIGHTS]
    return (loss, grad_x.reshape(Bl, S, D), *results)


def kernel(x, mem, ln_gains, mem_norm, a_conv_w, a_conv_b, a_w_in, a_b_in, a_gate_w, a_gate_b, a_lambda, a_w_out, a_b_out, b_mu, b_w_rkv, b_w0, b_w1, b_w2, b_a0, b_a1, b_a2, b_g1, b_g2, b_k_k, b_k_a, b_r_k, b_gn_g, b_gn_b, b_w_o, c_w_q, c_w_kv, c_w_o, m_w_up, m_w_down, loss_target, m_ln_gains, m_mem_norm, m_a_conv_w, m_a_conv_b, m_a_w_in, m_a_b_in, m_a_gate_w, m_a_gate_b, m_a_lambda, m_a_w_out, m_a_b_out, m_b_mu, m_b_w_rkv, m_b_w0, m_b_w1, m_b_w2, m_b_a0, m_b_a1, m_b_a2, m_b_g1, m_b_g2, m_b_k_k, m_b_k_a, m_b_r_k, m_b_gn_g, m_b_gn_b, m_b_w_o, m_c_w_q, m_c_w_kv, m_c_w_o, m_m_w_up, m_m_w_down, v_ln_gains, v_mem_norm, v_a_conv_w, v_a_conv_b, v_a_w_in, v_a_b_in, v_a_gate_w, v_a_gate_b, v_a_lambda, v_a_w_out, v_a_b_out, v_b_mu, v_b_w_rkv, v_b_w0, v_b_w1, v_b_w2, v_b_a0, v_b_a1, v_b_a2, v_b_g1, v_b_g2, v_b_k_k, v_b_k_a, v_b_r_k, v_b_gn_g, v_b_gn_b, v_b_w_o, v_c_w_q, v_c_w_kv, v_c_w_o, v_m_w_up, v_m_w_down):
    weights = (ln_gains, mem_norm, a_conv_w, a_conv_b, a_w_in, a_b_in, a_gate_w, a_gate_b, a_lambda, a_w_out, a_b_out, b_mu, b_w_rkv, b_w0, b_w1, b_w2, b_a0, b_a1, b_a2, b_g1, b_g2, b_k_k, b_k_a, b_r_k, b_gn_g, b_gn_b, b_w_o, c_w_q, c_w_kv, c_w_o, m_w_up, m_w_down)
    moments1 = (m_ln_gains, m_mem_norm, m_a_conv_w, m_a_conv_b, m_a_w_in, m_a_b_in, m_a_gate_w, m_a_gate_b, m_a_lambda, m_a_w_out, m_a_b_out, m_b_mu, m_b_w_rkv, m_b_w0, m_b_w1, m_b_w2, m_b_a0, m_b_a1, m_b_a2, m_b_g1, m_b_g2, m_b_k_k, m_b_k_a, m_b_r_k, m_b_gn_g, m_b_gn_b, m_b_w_o, m_c_w_q, m_c_w_kv, m_c_w_o, m_m_w_up, m_m_w_down)
    moments2 = (v_ln_gains, v_mem_norm, v_a_conv_w, v_a_conv_b, v_a_w_in, v_a_b_in, v_a_gate_w, v_a_gate_b, v_a_lambda, v_a_w_out, v_a_b_out, v_b_mu, v_b_w_rkv, v_b_w0, v_b_w1, v_b_w2, v_b_a0, v_b_a1, v_b_a2, v_b_g1, v_b_g2, v_b_k_k, v_b_k_a, v_b_r_k, v_b_gn_g, v_b_gn_b, v_b_w_o, v_c_w_q, v_c_w_kv, v_c_w_o, v_m_w_up, v_m_w_down)
    return _train_step(x, mem, loss_target, dict(zip(WEIGHTS, weights)), dict(zip(WEIGHTS, moments1)),
                       dict(zip(WEIGHTS, moments2)))
```

```python
import jax
import jax.numpy as jnp
from jax import lax
from jax.experimental import pallas as pl
from jax.experimental.pallas import tpu as pltpu

F32 = jnp.float32
BF16 = jnp.bfloat16
N_DEV = 8
MESH = pl.DeviceIdType.MESH

V7X_VMEM_LIMIT_BYTES = 56 * 1024 * 1024
SUB = 8
HALO_ROWS = SUB
LANES = 128
RWKV_N = 64
RMS_EPS = 1e-6
GN_EPS = 64e-5
LRU_C = 8.0
XATTN_HEADS = 4
LRU_HEADS = 4

ADAM_LR, ADAM_B1, ADAM_B2, ADAM_EPS, ADAM_WD, ADAM_STEP = 0.001, 0.9, 0.999, 1e-8, 0.01, 10

WEIGHTS = ['ln_gains', 'mem_norm', 'a_conv_w', 'a_conv_b', 'a_w_in', 'a_b_in', 'a_gate_w', 'a_gate_b',
           'a_lambda', 'a_w_out', 'a_b_out', 'b_mu', 'b_w_rkv', 'b_w0', 'b_w1', 'b_w2', 'b_a0', 'b_a1',
           'b_a2', 'b_g1', 'b_g2', 'b_k_k', 'b_k_a', 'b_r_k', 'b_gn_g', 'b_gn_b', 'b_w_o', 'c_w_q',
           'c_w_kv', 'c_w_o', 'm_w_up', 'm_w_down']
SHARD_AXIS = {'ln_gains': 2, 'mem_norm': None, 'a_conv_w': 2, 'a_conv_b': None, 'a_w_in': 2, 'a_b_in': None,
              'a_gate_w': 3, 'a_gate_b': 3, 'a_lambda': None, 'a_w_out': 1, 'a_b_out': None, 'b_mu': 2,
              'b_w_rkv': 2, 'b_w0': 1, 'b_w1': 1, 'b_w2': 2, 'b_a0': 1, 'b_a1': 1, 'b_a2': 2, 'b_g1': 1,
              'b_g2': 2, 'b_k_k': 1, 'b_k_a': 1, 'b_r_k': None, 'b_gn_g': 1, 'b_gn_b': 1, 'b_w_o': 1,
              'c_w_q': 1, 'c_w_kv': 2, 'c_w_o': 1, 'm_w_up': 2, 'm_w_down': 1}
BIG = ['a_w_in', 'a_gate_w', 'a_w_out', 'b_w_rkv', 'b_w1', 'b_w2', 'b_a1', 'b_a2', 'b_g1', 'b_g2', 'b_w_o',
       'c_w_q', 'c_w_kv', 'c_w_o', 'm_w_up', 'm_w_down']
SMALL = ['ln_gains', 'a_conv_w', 'a_gate_b', 'b_mu', 'b_w0', 'b_a0', 'b_k_k', 'b_k_a', 'b_gn_g', 'b_gn_b']
REPL = ['mem_norm', 'a_conv_b', 'a_b_in', 'a_lambda', 'a_b_out', 'b_r_k']
PACK_W = 1024
PACK_ROWS = 128
BF16_ROWS = 16


def _cparams(*sem):
    return pltpu.CompilerParams(dimension_semantics=sem, vmem_limit_bytes=V7X_VMEM_LIMIT_BYTES)


def _shift_tile(tile, halo, k, pos, seq, tm):
    if k > 0:
        ext = jnp.concatenate([halo, tile], axis=0)
        r = pltpu.roll(ext, k, 0)[HALO_ROWS:HALO_ROWS + tm]
        return jnp.where(pos >= k, r, 0.0)
    kk = -k
    ext = jnp.concatenate([tile, halo], axis=0)
    r = pltpu.roll(ext, tm + HALO_ROWS - kk, 0)[0:tm]
    return jnp.where(pos + kk < seq, r, 0.0)


def _row_specs(rows, shifts, params, per_b, seq, tm):
    T = rows[0].shape[0]
    ns = seq // tm
    specs = [pl.BlockSpec((tm, r.shape[1]), lambda b, i: (b * ns + i, 0)) for r in rows]
    halo_keys = []
    for idx, k in shifts:
        key = (idx, k > 0)
        if key not in halo_keys:
            halo_keys.append(key)
    per8 = tm // HALO_ROWS
    last8 = T // HALO_ROWS - 1
    for idx, prev in halo_keys:
        c = rows[idx].shape[1]
        if prev:
            specs.append(pl.BlockSpec((HALO_ROWS, c), lambda b, i: (jnp.maximum((b * ns + i) * per8 - 1, 0), 0)))
        else:
            specs.append(pl.BlockSpec((HALO_ROWS, c), lambda b, i: (jnp.minimum((b * ns + i + 1) * per8, last8), 0)))
    for j, p in enumerate(params):
        if j in per_b:
            specs.append(pl.BlockSpec((None,) + p.shape[1:], lambda b, i: (b, 0, 0)))
        else:
            specs.append(pl.BlockSpec(p.shape, lambda b, i, nd=p.ndim: (0,) * nd))
    return specs, halo_keys


def _load_tiles(refs, rows, shifts, halo_keys, params, seq, tm):
    nr, nh = len(rows), len(halo_keys)
    i = pl.program_id(1)
    pos = i * tm + lax.broadcasted_iota(jnp.int32, (tm, 1), 0)
    row_t = [r[...] for r in refs[:nr]]
    halo_t = {key: refs[nr + j][...] for j, key in enumerate(halo_keys)}
    sh_t = [_shift_tile(row_t[idx], halo_t[(idx, k > 0)], k, pos, seq, tm) for idx, k in shifts]
    par_t = [p[...] for p in refs[nr + nh:nr + nh + len(params)]]
    return pos, row_t, sh_t, par_t


ROW_TILES = (512, 256)
ROW_TILE_BUDGET_BYTES = 16 * 1024 * 1024


def _row_tile(seq, row_bytes):
    for tm in ROW_TILES:
        if 2 * tm * row_bytes <= ROW_TILE_BUDGET_BYTES:
            return min(tm, seq)
    return min(ROW_TILES[-1], seq)


def _rw(name, fn, rows, params=(), outs=(), accs=(), shifts=(), seq=None, per_b=()):
    rows, params = list(rows), list(params)
    T = rows[0].shape[0]
    seq = seq or T
    tm = _row_tile(seq, sum(r.shape[1] * r.dtype.itemsize for r in rows) + sum(c * jnp.dtype(dt).itemsize for c, dt in outs))
    nb, ns = T // seq, seq // tm
    in_specs, halo_keys = _row_specs(rows, shifts, params, per_b, seq, tm)
    n_in, no = len(in_specs), len(outs)

    def body(*refs):
        pos, row_t, sh_t, par_t = _load_tiles(refs, rows, shifts, halo_keys, params, seq, tm)
        res = fn(pos, *row_t, *sh_t, *par_t)
        out_refs = refs[n_in:n_in + no]
        acc_refs = refs[n_in + no:]
        for o_ref, v in zip(out_refs, res[:no]):
            o_ref[...] = v.astype(o_ref.dtype)
        if acc_refs:
            @pl.when((pl.program_id(0) == 0) & (pl.program_id(1) == 0))
            def _():
                for a_ref in acc_refs:
                    a_ref[...] = jnp.zeros_like(a_ref)
            for a_ref, v in zip(acc_refs, res[no:]):
                a_ref[...] += v

    out_shape = [jax.ShapeDtypeStruct((T, c), dt) for c, dt in outs]
    out_shape += [jax.ShapeDtypeStruct(s, F32) for s in accs]
    out_specs = [pl.BlockSpec((tm, c), lambda b, i: (b * ns + i, 0)) for c, _ in outs]
    out_specs += [pl.BlockSpec(s, lambda b, i: (0, 0)) for s in accs]
    return pl.pallas_call(body, name=name, grid=(nb, ns), in_specs=in_specs, out_specs=out_specs,
                          out_shape=out_shape, compiler_params=_cparams("arbitrary", "arbitrary"))(*rows, *[
                              rows[idx] for idx, _ in halo_keys], *params)


def _rw_vjp(name, fn, rows, params, cts, n_out, d_rows=(), d_shifts=(), d_params=(), adds=(), ct_adds=(),
            shifts=(), seq=None, per_b=()):
    rows, params, cts = list(rows), list(params), list(cts)
    T = rows[0].shape[0]
    seq = seq or T
    row_bytes = sum(r.shape[1] * r.dtype.itemsize for r in rows + cts + [a for _, a, _ in list(adds) + list(ct_adds)])
    row_bytes += sum(rows[j].shape[1] * jnp.dtype(dt).itemsize for j, dt in d_rows)
    row_bytes += sum(rows[shifts[j][0]].shape[1] * jnp.dtype(dt).itemsize for j, dt in d_shifts)
    tm = _row_tile(seq, row_bytes)
    nb, ns = T // seq, seq // tm
    n_ga = len(adds)
    adds = list(adds) + list(ct_adds)
    add_rows = [a for _, a, _ in adds]
    add_shifts = [(j, k) for j, (_, _, k) in enumerate(adds) if k != 0]
    all_rows = rows + cts + add_rows
    off_ct, off_add = len(rows), len(rows) + len(cts)
    all_shifts = list(shifts) + [(off_add + j, k) for j, k in add_shifts]
    in_specs, halo_keys = _row_specs(all_rows, all_shifts, params, per_b, seq, tm)
    n_in = len(in_specs)
    n_dr, n_ds = len(d_rows), len(d_shifts)

    def body(*refs):
        pos, row_t, sh_t, par_t = _load_tiles(refs, all_rows, all_shifts, halo_keys, params, seq, tm)
        fn_rows, ct_t, add_t = row_t[:off_ct], row_t[off_ct:off_add], row_t[off_add:]
        fn_sh, add_sh = sh_t[:len(shifts)], sh_t[len(shifts):]
        diff = [fn_rows[j] for j, _ in d_rows] + [fn_sh[j] for j, _ in d_shifts] + [par_t[j] for j in d_params]

        def f(*d):
            r2, s2, p2 = list(fn_rows), list(fn_sh), list(par_t)
            for (j, _), v in zip(d_rows, d[:n_dr]):
                r2[j] = v
            for (j, _), v in zip(d_shifts, d[n_dr:n_dr + n_ds]):
                s2[j] = v
            for j, v in zip(d_params, d[n_dr + n_ds:]):
                p2[j] = v
            return tuple(fn(pos, *r2, *s2, *p2)[:n_out])

        sh_iter = iter(add_sh)
        add_v = [add_t[j] if k == 0 else next(sh_iter) for j, (_, _, k) in enumerate(adds)]
        ct_t = list(ct_t)
        for (tgt, _, _), v in zip(adds[n_ga:], add_v[n_ga:]):
            ct_t[tgt] = ct_t[tgt] + v
        outs_v, vjp_fn = jax.vjp(f, *diff)
        grads = list(vjp_fn(tuple(c.astype(o.dtype) for c, o in zip(ct_t, outs_v))))
        for (tgt, _, _), v in zip(adds[:n_ga], add_v[:n_ga]):
            grads[tgt] = grads[tgt] + v
        out_refs = refs[n_in:]
        for o_ref, g in zip(out_refs[:n_dr + n_ds], grads[:n_dr + n_ds]):
            o_ref[...] = g.astype(o_ref.dtype)
        for o_ref, g, j in zip(out_refs[n_dr + n_ds:], grads[n_dr + n_ds:], d_params):
            first = (pl.program_id(1) == 0) if j in per_b else ((pl.program_id(0) == 0) & (pl.program_id(1) == 0))

            @pl.when(first)
            def _(o_ref=o_ref):
                o_ref[...] = jnp.zeros_like(o_ref)
            o_ref[...] += g.astype(F32)

    out_shape, out_specs = [], []
    for j, dt in d_rows:
        out_shape.append(jax.ShapeDtypeStruct(rows[j].shape, dt))
        out_specs.append(pl.BlockSpec((tm, rows[j].shape[1]), lambda b, i: (b * ns + i, 0)))
    for j, dt in d_shifts:
        src = rows[shifts[j][0]]
        out_shape.append(jax.ShapeDtypeStruct(src.shape, dt))
        out_specs.append(pl.BlockSpec((tm, src.shape[1]), lambda b, i: (b * ns + i, 0)))
    for j in d_params:
        p = params[j]
        out_shape.append(jax.ShapeDtypeStruct(p.shape, F32))
        if j in per_b:
            out_specs.append(pl.BlockSpec((None,) + p.shape[1:], lambda b, i: (b, 0, 0)))
        else:
            out_specs.append(pl.BlockSpec(p.shape, lambda b, i, nd=p.ndim: (0,) * nd))
    return pl.pallas_call(body, name=name, grid=(nb, ns), in_specs=in_specs, out_specs=out_specs,
                          out_shape=out_shape, compiler_params=_cparams("arbitrary", "arbitrary"))(*all_rows, *[
                              all_rows[idx] for idx, _ in halo_keys], *params)


MM_TILE = 1024
MM_DEEP = 2048


def _mm_nn(name, x, w, out_dtype):
    M, K = x.shape
    N = w.shape[1]
    tm, tn = min(MM_TILE if K <= MM_DEEP else MM_TILE // 2, M), min(MM_TILE, N)

    def body(x_ref, w_ref, o_ref):
        o_ref[...] = jnp.dot(x_ref[...].astype(BF16), w_ref[...].astype(BF16),
                             preferred_element_type=F32).astype(o_ref.dtype)

    return pl.pallas_call(body, name=name, grid=(M // tm, N // tn),
                          in_specs=[pl.BlockSpec((tm, K), lambda i, j: (i, 0)), pl.BlockSpec((K, tn), lambda i, j: (0, j))],
                          out_specs=pl.BlockSpec((tm, tn), lambda i, j: (i, j)),
                          out_shape=jax.ShapeDtypeStruct((M, N), out_dtype),
                          compiler_params=_cparams("parallel", "parallel"))(x, w)


def _mm_up_act(name, x, w):
    M, K = x.shape
    N = w.shape[1]
    tm, tn = min(MM_TILE if K <= MM_DEEP else MM_TILE // 2, M), min(MM_TILE, N)

    def body(x_ref, w_ref, up_ref, act_ref):
        up = jnp.dot(x_ref[...].astype(BF16), w_ref[...].astype(BF16), preferred_element_type=F32)
        up_ref[...] = up.astype(BF16)
        act_ref[...] = jnp.square(jnp.maximum(up, 0.0)).astype(BF16)

    out = pl.BlockSpec((tm, tn), lambda i, j: (i, j))
    return pl.pallas_call(body, name=name, grid=(M // tm, N // tn),
                          in_specs=[pl.BlockSpec((tm, K), lambda i, j: (i, 0)), pl.BlockSpec((K, tn), lambda i, j: (0, j))],
                          out_specs=[out, out], out_shape=[jax.ShapeDtypeStruct((M, N), BF16)] * 2,
                          compiler_params=_cparams("parallel", "parallel"))(x, w)


def _mm_nt_act_bwd(name, dy, w, up):
    M, N = dy.shape
    K = w.shape[0]
    tm, tk = min(MM_TILE if N <= MM_DEEP else MM_TILE // 2, M), min(MM_TILE, K)

    def body(dy_ref, w_ref, up_ref, o_ref):
        dact = lax.dot_general(dy_ref[...].astype(BF16), w_ref[...].astype(BF16), (((1,), (1,)), ((), ())),
                               preferred_element_type=F32)
        o_ref[...] = (dact * (2.0 * jnp.maximum(up_ref[...].astype(F32), 0.0))).astype(o_ref.dtype)

    out = pl.BlockSpec((tm, tk), lambda i, j: (i, j))
    return pl.pallas_call(body, name=name, grid=(M // tm, K // tk),
                          in_specs=[pl.BlockSpec((tm, N), lambda i, j: (i, 0)), pl.BlockSpec((tk, N), lambda i, j: (j, 0)), out],
                          out_specs=out, out_shape=jax.ShapeDtypeStruct((M, K), BF16),
                          compiler_params=_cparams("parallel", "parallel"))(dy, w, up)


def _mm_nt(name, dy, w, out_dtype):
    M, N = dy.shape
    K = w.shape[0]
    tm, tk = min(MM_TILE if N <= MM_DEEP else MM_TILE // 2, M), min(MM_TILE, K)

    def body(dy_ref, w_ref, o_ref):
        o_ref[...] = lax.dot_general(dy_ref[...].astype(BF16), w_ref[...].astype(BF16), (((1,), (1,)), ((), ())),
                                     preferred_element_type=F32).astype(o_ref.dtype)

    return pl.pallas_call(body, name=name, grid=(M // tm, K // tk),
                          in_specs=[pl.BlockSpec((tm, N), lambda i, j: (i, 0)), pl.BlockSpec((tk, N), lambda i, j: (j, 0))],
                          out_specs=pl.BlockSpec((tm, tk), lambda i, j: (i, j)),
                          out_shape=jax.ShapeDtypeStruct((M, K), out_dtype),
                          compiler_params=_cparams("parallel", "parallel"))(dy, w)


def _mm_tn(name, x, dy):
    M, K = x.shape
    N = dy.shape[1]
    tm, tk, tn = min(MM_TILE, M), min(MM_TILE, K), min(MM_TILE, N)

    def body(x_ref, dy_ref, o_ref):
        @pl.when(pl.program_id(2) == 0)
        def _():
            o_ref[...] = jnp.zeros_like(o_ref)
        o_ref[...] += lax.dot_general(x_ref[...].astype(BF16), dy_ref[...].astype(BF16), (((0,), (0,)), ((), ())),
                                      preferred_element_type=F32)

    return pl.pallas_call(body, name=name, grid=(K // tk, N // tn, M // tm),
                          in_specs=[pl.BlockSpec((tm, tk), lambda i, j, m: (m, i)), pl.BlockSpec((tm, tn), lambda i, j, m: (m, j))],
                          out_specs=pl.BlockSpec((tk, tn), lambda i, j, m: (i, j)),
                          out_shape=jax.ShapeDtypeStruct((K, N), F32),
                          compiler_params=_cparams("parallel", "parallel", "arbitrary"))(x, dy)


def _lru_scan(name, a, b, seq, ct=128):
    T, C = a.shape

    def body(a_ref, b_ref, h_ref):
        A, Bv = a_ref[...], b_ref[...]
        row = lax.broadcasted_iota(jnp.int32, (seq, 1), 0)
        k = 1
        while k < seq:
            keep = row >= k
            Bv = A * jnp.where(keep, pltpu.roll(Bv, k, 0), 0.0) + Bv
            A = A * jnp.where(keep, pltpu.roll(A, k, 0), 1.0)
            k *= 2
        h_ref[...] = Bv

    spec = pl.BlockSpec((seq, ct), lambda b_, j: (b_, j))
    return pl.pallas_call(body, name=name, grid=(T // seq, C // ct), in_specs=[spec, spec], out_specs=spec,
                          out_shape=jax.ShapeDtypeStruct((T, C), F32),
                          compiler_params=_cparams("parallel", "parallel"))(a, b)


def _lru_scan_bwd(name, a, h, dh, seq, ct=128):
    T, C = a.shape

    def body(a_ref, h_ref, dh_ref, da_ref, db_ref):
        row = lax.broadcasted_iota(jnp.int32, (seq, 1), 0)
        Cf = jnp.where(row < seq - 1, pltpu.roll(a_ref[...], seq - 1, 0), 0.0)
        G = dh_ref[...]
        k = 1
        while k < seq:
            keep = row + k < seq
            G = Cf * jnp.where(keep, pltpu.roll(G, seq - k, 0), 0.0) + G
            Cf = Cf * jnp.where(keep, pltpu.roll(Cf, seq - k, 0), 1.0)
            k *= 2
        db_ref[...] = G
        da_ref[...] = G * jnp.where(row >= 1, pltpu.roll(h_ref[...], 1, 0), 0.0)

    spec = pl.BlockSpec((seq, ct), lambda b_, j: (b_, j))
    return pl.pallas_call(body, name=name, grid=(T // seq, C // ct), in_specs=[spec] * 3, out_specs=[spec] * 2,
                          out_shape=[jax.ShapeDtypeStruct((T, C), F32)] * 2,
                          compiler_params=_cparams("parallel", "parallel"))(a, h, dh)


def _pair_consts():
    jj = lax.broadcasted_iota(jnp.int32, (RWKV_N, LANES), 0)
    ll = lax.broadcasted_iota(jnp.int32, (RWKV_N, LANES), 1)
    pick = ((ll & (RWKV_N - 1)) == jj).astype(BF16)
    l0 = lax.broadcasted_iota(jnp.int32, (LANES, LANES), 0)
    l1 = lax.broadcasted_iota(jnp.int32, (LANES, LANES), 1)
    same = ((l0 >> 6) == (l1 >> 6)).astype(BF16)
    return pick, same


N_COL = 5


def _cols_raw(rows, pick, same):
    lhs = [pick * jnp.broadcast_to(x, (RWKV_N, LANES)).astype(BF16) for x in rows]
    out = jnp.dot(jnp.concatenate(lhs, axis=0), same, preferred_element_type=F32)
    return tuple(out[i * RWKV_N:(i + 1) * RWKV_N] for i in range(len(rows)))


def _cols_adjoint(g):
    n = len(g)
    g = list(g) + [jnp.zeros_like(g[0])] * (n % 2)
    head = ((lax.broadcasted_iota(jnp.int32, (SUB, LANES), 1) >> 6)
            == lax.broadcasted_iota(jnp.int32, (SUB, LANES), 0)).astype(BF16)
    low = lax.broadcasted_iota(jnp.int32, (1, LANES), 1) < RWKV_N
    sums = lax.dot_general(head, jnp.concatenate([t.astype(BF16) for t in g], axis=0),
                           (((1,), (1,)), ((), ())), preferred_element_type=F32)
    rows = []
    for i in range(0, n, 2):
        blk = sums[:, i * RWKV_N:(i + 2) * RWKV_N]
        swapped = pltpu.roll(blk, RWKV_N, 1)
        rows.append(jnp.where(low, blk[0:1], swapped[1:2]))
        rows.append(jnp.where(low, swapped[0:1], blk[1:2]))
    return tuple(rows[:n])


def _rwkv_cols(rows, pick, same):
    flat = []
    for r, w, k, v, a, b in rows:
        flat += [a, 1.0 - w, b, k, r]
    tiles = _cols_raw(tuple(flat), pick, same)
    return [tiles[N_COL * p:N_COL * (p + 1)] for p in range(len(rows))]


def _rwkv_step(s0, tiles, v):
    ca, ce, cb, ck, cr = tiles
    sa = jnp.sum(s0 * ca, axis=0, keepdims=True)
    s1 = s0 - s0 * ce + cb * sa + ck * v
    return s1, jnp.sum(s1 * cr, axis=0, keepdims=True), sa


def _rwkv_step_bwd(s0, s1, sa, tiles, v, dy, ds1):
    ca, ce, cb, ck, cr = tiles
    ds1 = ds1 + cr * dy
    dsa = jnp.sum(ds1 * cb, axis=0, keepdims=True)
    dv = jnp.sum(ds1 * ck, axis=0, keepdims=True)
    ds0 = ds1 - ds1 * ce + ca * dsa
    return ds0, (s0 * dsa, ds1 * s0, ds1 * sa, ds1 * v, s1 * dy), dv


def _rwkv_fwd(name, r, w, k, v, a, b, seq, tc=16, G=8, NB=2):
    T, C = r.shape
    B = T // seq
    tc = min(tc, seq)
    NB = NB if B % NB == 0 else 1
    NP = C // LANES
    nc = seq // tc
    row = pl.BlockSpec((NB, tc, LANES * G), lambda b_, g, c: (b_, c, g))
    st = pl.BlockSpec((NB, tc, G, RWKV_N, LANES), lambda b_, g, c: (b_, c, g, 0, 0))
    last = pl.BlockSpec((NB, G, RWKV_N, LANES), lambda b_, g, c: (b_, g, 0, 0))
    chains = [(q, p) for q in range(NB) for p in range(G)]

    def body(r_ref, w_ref, k_ref, v_ref, a_ref, b_ref, y_ref, st_ref, sa_ref, last_ref, s_scr):
        @pl.when(pl.program_id(2) == 0)
        def _():
            s_scr[...] = jnp.zeros_like(s_scr)
        pick, same = _pair_consts()
        rid = lax.broadcasted_iota(jnp.int32, (SUB, LANES), 0)

        def group(t8, carry):
            base = pl.multiple_of(t8 * SUB, SUB)
            rows8 = pl.ds(base, SUB)
            lanes = [pl.ds(LANES * p, LANES) for p in range(G)]
            blk = [[x[q, rows8, lanes[p]] for x in (r_ref, w_ref, k_ref, v_ref, a_ref, b_ref)] for q, p in chains]
            s = [s_scr[q, p] for q, p in chains]
            ys = [jnp.zeros((SUB, LANES), F32) for _ in chains]
            sas = [jnp.zeros((SUB, LANES), F32) for _ in chains]
            for i in range(SUB):
                tiles = _rwkv_cols([tuple(b_[i:i + 1] for b_ in blk[n_]) for n_ in range(len(chains))], pick, same)
                for n_, (q, p) in enumerate(chains):
                    st_ref[q, base + i, p] = s[n_]
                    s[n_], y, sa = _rwkv_step(s[n_], tiles[n_], blk[n_][3][i:i + 1])
                    ys[n_] = jnp.where(rid == i, y, ys[n_])
                    sas[n_] = jnp.where(rid == i, sa, sas[n_])
            for n_, (q, p) in enumerate(chains):
                s_scr[q, p] = s[n_]
                y_ref[q, rows8, lanes[p]] = ys[n_]
                sa_ref[q, rows8, lanes[p]] = sas[n_]
            return carry

        lax.fori_loop(0, tc // SUB, group, 0)

        @pl.when(pl.program_id(2) == nc - 1)
        def _():
            last_ref[...] = s_scr[...]

    seq3 = lambda x: x.reshape(B, seq, C)
    y, states, sa, final = pl.pallas_call(body,
        name=name, grid=(B // NB, NP // G, nc), in_specs=[row] * 6, out_specs=[row, st, row, last],
        out_shape=[jax.ShapeDtypeStruct((B, seq, C), F32), jax.ShapeDtypeStruct((B, seq, NP, RWKV_N, LANES), F32),
                   jax.ShapeDtypeStruct((B, seq, C), F32), jax.ShapeDtypeStruct((B, NP, RWKV_N, LANES), F32)],
        scratch_shapes=[pltpu.VMEM((NB, G, RWKV_N, LANES), F32)],
        compiler_params=_cparams("parallel", "parallel", "arbitrary"))(*[seq3(x) for x in (r, w, k, v, a, b)])
    return y.reshape(T, C), states.reshape(T, NP, RWKV_N, LANES), sa.reshape(T, C), final


def _rwkv_bwd(name, r, w, k, v, a, b, states, sa, last, dy, dr_p, dk_p, dv_p, seq, tc=16, G=8, NB=2):
    T, C = r.shape
    B = T // seq
    tc = min(tc, seq)
    NB = NB if B % NB == 0 else 1
    NP = C // LANES
    nc = seq // tc
    row = pl.BlockSpec((NB, tc, LANES * G), lambda b_, g, c: (b_, nc - 1 - c, g))
    st = pl.BlockSpec((NB, tc, G, RWKV_N, LANES), lambda b_, g, c: (b_, nc - 1 - c, g, 0, 0))
    last_spec = pl.BlockSpec((NB, G, RWKV_N, LANES), lambda b_, g, c: (b_, g, 0, 0))
    chains = [(q, p) for q in range(NB) for p in range(G)]

    def body(r_ref, w_ref, k_ref, v_ref, a_ref, b_ref, st_ref, sa_ref, last_ref, dy_ref, drp_ref, dkp_ref, dvp_ref,
             dr_ref, dw_ref, dk_ref, dv_ref, da_ref, db_ref, ds_scr, s1_scr):
        @pl.when(pl.program_id(2) == 0)
        def _():
            ds_scr[...] = jnp.zeros_like(ds_scr)
            s1_scr[...] = last_ref[...]
        pick, same = _pair_consts()
        rid = lax.broadcasted_iota(jnp.int32, (SUB, LANES), 0)
        out_refs = (da_ref, dw_ref, db_ref, dk_ref, dr_ref)

        def group(n, carry):
            base = pl.multiple_of((tc // SUB - 1 - n) * SUB, SUB)
            rows8 = pl.ds(base, SUB)
            lanes = [pl.ds(LANES * p, LANES) for p in range(G)]
            blk = [[x[q, rows8, lanes[p]] for x in (r_ref, w_ref, k_ref, v_ref, a_ref, b_ref)] for q, p in chains]
            dys = [dy_ref[q, rows8, lanes[p]] for q, p in chains]
            sas = [sa_ref[q, rows8, lanes[p]] for q, p in chains]
            ds = [ds_scr[q, p] for q, p in chains]
            s1 = [s1_scr[q, p] for q, p in chains]
            zero = jnp.zeros((SUB, LANES), F32)
            gs = [[zero, zero, zero, dkp_ref[q, rows8, lanes[p]], drp_ref[q, rows8, lanes[p]]] for q, p in chains]
            gv = [dvp_ref[q, rows8, lanes[p]] for q, p in chains]
            steps = tuple(reversed(range(SUB)))
            tiles_of = [_rwkv_cols([tuple(b_[i:i + 1] for b_ in blk[n_]) for n_ in range(len(chains))], pick, same)
                        for i in steps]
            adj_of = []
            for i, tiles in zip(steps, tiles_of):
                adj = []
                for n_, (q, p) in enumerate(chains):
                    s0 = st_ref[q, base + i, p]
                    ds[n_], g_tiles, dv = _rwkv_step_bwd(s0, s1[n_], sas[n_][i:i + 1], tiles[n_], blk[n_][3][i:i + 1],
                                                         dys[n_][i:i + 1], ds[n_])
                    s1[n_] = s0
                    adj += list(g_tiles)
                    gv[n_] = gv[n_] + jnp.where(rid == i, dv, 0.0)
                adj_of.append(adj)
            for i, adj in zip(steps, adj_of):
                rows = _cols_adjoint(adj)
                for n_ in range(len(chains)):
                    for j in range(N_COL):
                        gs[n_][j] = gs[n_][j] + jnp.where(rid == i, rows[N_COL * n_ + j], 0.0)
            for n_, (q, p) in enumerate(chains):
                ds_scr[q, p] = ds[n_]
                s1_scr[q, p] = s1[n_]
                dv_ref[q, rows8, lanes[p]] = gv[n_]
                for j in range(N_COL):
                    out_refs[j][q, rows8, lanes[p]] = gs[n_][j]
            return carry

        lax.fori_loop(0, tc // SUB, group, 0)

    seq3 = lambda x: x.reshape(B, seq, C)
    outs = pl.pallas_call(body, name=name, grid=(B // NB, NP // G, nc),
                          in_specs=[row] * 6 + [st, row, last_spec] + [row] * 4, out_specs=[row] * 6,
                          out_shape=[jax.ShapeDtypeStruct((B, seq, C), F32)] * 6,
                          scratch_shapes=[pltpu.VMEM((NB, G, RWKV_N, LANES), F32)] * 2,
                          compiler_params=_cparams("parallel", "parallel", "arbitrary"))(
                              *[seq3(x) for x in (r, w, k, v, a, b)], states.reshape(B, seq, NP, RWKV_N, LANES), seq3(sa),
                              last, *[seq3(x) for x in (dy, dr_p, dk_p, dv_p)])
    return [o.reshape(T, C) for o in outs]


def _all_gather(name, shard):
    def body(x_ref, o_ref, send_sems, recv_sems, local_sem):
        x, y, c = lax.axis_index("x"), lax.axis_index("y"), lax.axis_index("c")
        me, sibling = (x, y, c), (x, y, 1 - c)
        chips = [(1 - x, y), (x, 1 - y), (1 - x, 1 - y)]

        def copy(k, block, to, src=None):
            dst = o_ref.at[4 * block[0] + 2 * block[1] + block[2]]
            return pltpu.make_async_remote_copy(src_ref=dst if src is None else src, dst_ref=dst, send_sem=send_sems.at[k],
                                                recv_sem=recv_sems.at[k], device_id=to, device_id_type=MESH)

        mine = pltpu.make_async_copy(x_ref, o_ref.at[4 * x + 2 * y + c], local_sem)
        mine.start()
        first = [copy(0, me, sibling, src=x_ref)] + [copy(1 + j, me, (*chip, c), src=x_ref) for j, chip in enumerate(chips)]
        for cp in first:
            cp.start()
        passed = [copy(4 + j, (*chip, c), sibling) for j, chip in enumerate(chips)]
        for j, chip in enumerate(chips):
            copy(1 + j, (*chip, c), me).wait_recv()
            passed[j].start()
        copy(0, sibling, me).wait_recv()
        for j, chip in enumerate(chips):
            copy(4 + j, (*chip, 1 - c), me).wait_recv()
        for cp in first + passed:
            cp.wait_send()
        mine.wait()

    hbm = pl.BlockSpec(memory_space=pltpu.HBM)
    return pl.pallas_call(body, name=name, in_specs=[hbm], out_specs=hbm,
                          out_shape=jax.ShapeDtypeStruct((N_DEV,) + shard.shape, shard.dtype),
                          scratch_shapes=[pltpu.SemaphoreType.DMA((N_DEV - 1,)), pltpu.SemaphoreType.DMA((N_DEV - 1,)),
                                          pltpu.SemaphoreType.DMA])(shard)


N_CHIP = 4


def _swap_with_sibling(name, blocks, n_split):
    _, R, W = blocks.shape
    N = N_CHIP
    rq = R // n_split

    def body(x_ref, got_ref, send_sems, recv_sems):
        x, y, c = lax.axis_index("x"), lax.axis_index("y"), lax.axis_index("c")
        copies = []
        for j in range(N):
            for q in range(n_split):
                rows = pl.ds(q * rq, rq)
                cp = pltpu.make_async_remote_copy(src_ref=x_ref.at[2 * j + 1 - c, rows], dst_ref=got_ref.at[j, rows],
                                                  send_sem=send_sems.at[j * n_split + q], recv_sem=recv_sems.at[j * n_split + q],
                                                  device_id=(x, y, 1 - c), device_id_type=MESH)
                cp.start()
                copies.append(cp)
        for cp in copies:
            cp.wait()

    hbm = pl.BlockSpec(memory_space=pltpu.HBM)
    return pl.pallas_call(body, name=name, in_specs=[hbm], out_specs=hbm,
                          out_shape=jax.ShapeDtypeStruct((N, R, W), blocks.dtype),
                          scratch_shapes=[pltpu.SemaphoreType.DMA((N * n_split,)), pltpu.SemaphoreType.DMA((N * n_split,))])(blocks)


def _add_own_blocks(name, blocks, got, tr, out_dtype):
    N, R, W = got.shape

    def body(c_ref, a_ref, b_ref, o_ref):
        o_ref[...] = (a_ref[...] + b_ref[...]).astype(o_ref.dtype)

    grid_spec = pltpu.PrefetchScalarGridSpec(
        num_scalar_prefetch=1, grid=(N, R // tr),
        in_specs=[pl.BlockSpec((None, tr, W), lambda n, i, c: (2 * n + c[0], i, 0)),
                  pl.BlockSpec((None, tr, W), lambda n, i, c: (n, i, 0))],
        out_specs=pl.BlockSpec((None, tr, W), lambda n, i, c: (n, i, 0)))
    core = lax.axis_index("c").astype(jnp.int32).reshape(1)
    return pl.pallas_call(body, name=name, grid_spec=grid_spec, out_shape=jax.ShapeDtypeStruct(got.shape, out_dtype),
                          compiler_params=_cparams("parallel", "parallel"))(core, blocks, got)


def _exchange_chips(name, blocks):
    def body(z_ref, o_ref, send_sems, recv_sems, local_sem):
        x, y, c = lax.axis_index("x"), lax.axis_index("y"), lax.axis_index("c")
        chip = 2 * x + y
        mine = pltpu.make_async_copy(z_ref.at[chip], o_ref.at[chip], local_sem)
        mine.start()
        copies = []
        for k in range(1, N_CHIP):
            to = (x ^ (k >> 1), y ^ (k & 1), c)
            cp = pltpu.make_async_remote_copy(src_ref=z_ref.at[chip ^ k], dst_ref=o_ref.at[chip], send_sem=send_sems.at[k - 1],
                                              recv_sem=recv_sems.at[k - 1], device_id=to, device_id_type=MESH)
            cp.start()
            copies.append(cp)
        for k in range(1, N_CHIP):
            pltpu.make_async_remote_copy(src_ref=z_ref.at[chip], dst_ref=o_ref.at[chip ^ k], send_sem=send_sems.at[k - 1],
                                         recv_sem=recv_sems.at[k - 1], device_id=(x ^ (k >> 1), y ^ (k & 1), c),
                                         device_id_type=MESH).wait_recv()
        for cp in copies:
            cp.wait_send()
        mine.wait()

    hbm = pl.BlockSpec(memory_space=pltpu.HBM)
    return pl.pallas_call(body, name=name, in_specs=[hbm], out_specs=hbm,
                          out_shape=jax.ShapeDtypeStruct(blocks.shape, blocks.dtype),
                          scratch_shapes=[pltpu.SemaphoreType.DMA((N_CHIP - 1,)), pltpu.SemaphoreType.DMA((N_CHIP - 1,)),
                                          pltpu.SemaphoreType.DMA])(blocks)


def _adamw(name, parts, w, m, v, tr):
    R, W = w.shape
    n_parts = parts.shape[0]

    def body(p_ref, w_ref, m_ref, v_ref, g_ref, d_ref, nm_ref, nv_ref):
        g = p_ref[0].astype(F32)
        for s in range(1, n_parts):
            g = g + p_ref[s].astype(F32)
        nm = ADAM_B1 * m_ref[...] + (1.0 - ADAM_B1) * g
        nv = ADAM_B2 * v_ref[...] + (1.0 - ADAM_B2) * (g * g)
        m_hat = nm / (1.0 - ADAM_B1 ** ADAM_STEP)
        v_hat = nv / (1.0 - ADAM_B2 ** ADAM_STEP)
        g_ref[...] = g
        d_ref[...] = -ADAM_LR * (m_hat / (jnp.sqrt(v_hat) + ADAM_EPS) + ADAM_WD * w_ref[...])
        nm_ref[...] = nm
        nv_ref[...] = nv

    spec = pl.BlockSpec((tr, W), lambda i: (i, 0))
    return pl.pallas_call(body, name=name, grid=(R // tr,),
                          in_specs=[pl.BlockSpec((n_parts, tr, W), lambda i: (0, i, 0)), spec, spec, spec],
                          out_specs=[spec] * 4, out_shape=[jax.ShapeDtypeStruct((R, W), F32)] * 4,
                          compiler_params=_cparams("parallel"))(parts, w, m, v)


def _rms(x, g):
    return x * lax.rsqrt(jnp.mean(x * x, axis=-1, keepdims=True) + RMS_EPS) * g


def _softplus(z):
    return jnp.maximum(z, 0.0) + jnp.log(1.0 + jnp.exp(-jnp.abs(z)))


def _neg_expm1(z):
    series = -z * (1.0 + z * (0.5 + z * (1.0 / 6.0)))
    return jnp.where(z > -1e-3, series, 1.0 - jnp.exp(z))


def _gelu_tanh(x):
    return 0.5 * x * (1.0 + jnp.tanh(0.7978845608028654 * (x + 0.044715 * x * x * x)))


def _bdot(x, w):
    return jnp.dot(x.astype(BF16), w.astype(BF16), preferred_element_type=F32)


def _seg_consts(C):
    c0 = lax.broadcasted_iota(jnp.int32, (C, LANES), 0)
    h1 = lax.broadcasted_iota(jnp.int32, (C, LANES), 1)
    red = ((c0 >> 6) == h1).astype(BF16)
    h0 = lax.broadcasted_iota(jnp.int32, (LANES, C), 0)
    c1 = lax.broadcasted_iota(jnp.int32, (LANES, C), 1)
    exp = ((c1 >> 6) == h0).astype(BF16)
    return red, exp


def _split_dot(x, m):
    hi = x.astype(BF16)
    lo = (x - hi.astype(F32)).astype(BF16)
    return jnp.dot(hi, m, preferred_element_type=F32) + jnp.dot(lo, m, preferred_element_type=F32)


def _segsum_raw(x):
    red, exp = _seg_consts(x.shape[1])
    return _split_dot(_split_dot(x, red), exp)


@jax.custom_vjp
def _segsum(x):
    return _segsum_raw(x)


_segsum.defvjp(lambda x: (_segsum_raw(x), None), lambda _, g: (_segsum_raw(g),))


def _f_pre(out_dtype):
    def fn(pos, x, g):
        return (_rms(x, g).astype(out_dtype),)
    return fn


def _f_post(pos, x, t, g):
    return (x + _rms(t, g),)


def _f_post_pre(out_dtype, has_bias):
    def fn(pos, x, t, g_post, g_pre, *bias):
        x1 = x + _rms(t + bias[0] if has_bias else t, g_post)
        return x1, _rms(x1, g_pre).astype(out_dtype)
    return fn


def _f_conv(pos, proj, p1, p2, p3, b_u, w0, w1, w2, w3, cb):
    D = b_u.shape[1]
    u0 = proj[:, D:] + b_u
    u1 = jnp.where(pos >= 1, p1[:, D:] + b_u, 0.0)
    u2 = jnp.where(pos >= 2, p2[:, D:] + b_u, 0.0)
    u3 = jnp.where(pos >= 3, p3[:, D:] + b_u, 0.0)
    return (cb + u3 * w0 + u2 * w1 + u1 * w2 + u0 * w3,)


def _f_conv_bwd(pos, dconv, proj, dyb, n1, n2, n3, p1, p2, p3, b_u, w0, w1, w2, w3):
    D = b_u.shape[1]
    du = dconv * w3 + n1 * w2 + n2 * w1 + n3 * w0
    dproj = jnp.concatenate([dyb[:, :D], du], axis=1)
    u0 = proj[:, D:] + b_u
    u1 = jnp.where(pos >= 1, p1[:, D:] + b_u, 0.0)
    u2 = jnp.where(pos >= 2, p2[:, D:] + b_u, 0.0)
    u3 = jnp.where(pos >= 3, p3[:, D:] + b_u, 0.0)
    rs = lambda z: jnp.sum(z, axis=0, keepdims=True)
    return (dproj, rs(dproj), rs(dconv * u3), rs(dconv * u2), rs(dconv * u1), rs(dconv * u0), rs(dconv))


def _f_gates(pos, conv, proj, gate_w, gb_r, gb_i, lam, b_y):
    D = conv.shape[1]
    blk = D // LRU_HEADS
    cb = conv.astype(BF16)

    def gate(g, bias):
        z = [jnp.dot(cb[:, h * blk:(h + 1) * blk], gate_w[(g * LRU_HEADS + h) * blk:(g * LRU_HEADS + h + 1) * blk, :].astype(BF16),
                     preferred_element_type=F32) for h in range(LRU_HEADS)]
        return jax.nn.sigmoid(jnp.concatenate(z, axis=1) + bias)

    r_gate, i_gate = gate(0, gb_r), gate(1, gb_i)
    log_a = -LRU_C * r_gate * _softplus(-lam)
    a = jnp.exp(log_a)
    b = jnp.sqrt(_neg_expm1(2.0 * log_a)) * i_gate * conv
    yb = _gelu_tanh(proj[:, :D] + b_y)
    return a, b, yb


def _f_hy(pos, h, yb):
    return ((h * yb).astype(BF16),)


def _f_mix(pos, hn, hp, mu_r, mu_w, mu_k, mu_v, mu_a, mu_g, w0, w1, w2, a0, a1, a2, g1, g2):
    xx = hp - hn
    xr, xw, xk, xv, xa, xg = (hn + xx * m for m in (mu_r, mu_w, mu_k, mu_v, mu_a, mu_g))
    lw = w0 + _bdot(jnp.tanh(_bdot(xw, w1)), w2)
    decay = jnp.exp(-jnp.exp(-_softplus(-lw) - 0.5))
    a = jax.nn.sigmoid(a0 + _bdot(_bdot(xa, a1), a2))
    g = _bdot(jax.nn.sigmoid(_bdot(xg, g1)), g2)
    return xr.astype(BF16), xk.astype(BF16), xv.astype(BF16), decay, a, g


def _f_kk(pos, k, a, k_k, k_a):
    kk = k * k_k
    kk = kk / jnp.maximum(jnp.sqrt(_segsum(kk * kk)), 1e-12)
    return k * (1.0 + (a - 1.0) * k_a), -kk, kk * a


def _f_gn(pos, y, r, k2, v, g, gn_g, gn_b, r_k):
    inv_n = 1.0 / RWKV_N
    yc = y - _segsum(y) * inv_n
    var = _segsum(yc * yc) * inv_n
    yn = yc * lax.rsqrt(var + GN_EPS) * gn_g + gn_b
    bonus = _segsum(r * k2 * r_k) * v
    return (((yn + bonus) * g).astype(BF16),)


def _f_attn(pos, q, kv):
    D = q.shape[1]
    hd = D // XATTN_HEADS
    qb, kb, vb = q.astype(BF16), kv[:, :D].astype(BF16), kv[:, D:].astype(BF16)
    outs = []
    for h in range(XATTN_HEADS):
        sl = slice(h * hd, (h + 1) * hd)
        s = lax.dot_general(qb[:, sl], kb[:, sl], (((1,), (1,)), ((), ())), preferred_element_type=F32) * (hd ** -0.5)
        s = s - jnp.max(s, axis=-1, keepdims=True)
        e = jnp.exp(s)
        p = e / jnp.sum(e, axis=-1, keepdims=True)
        outs.append(jnp.dot(p.astype(BF16), vb[:, sl], preferred_element_type=F32))
    return (jnp.concatenate(outs, axis=1).astype(BF16),)


def _f_loss(pos, y, target):
    d = y - target
    inv = 1.0 / y.shape[1]
    part = 0.5 * inv * jnp.sum(jnp.sum(d * d, axis=1, keepdims=True), axis=0, keepdims=True)
    return d * inv, jnp.broadcast_to(part, (8, LANES))


def _full_from_blocks(blocks, shard_shape, ax):
    g = jnp.moveaxis(blocks.reshape((N_DEV,) + tuple(shard_shape)), 0, ax)
    return g.reshape(tuple(shard_shape[:ax]) + (N_DEV * shard_shape[ax],) + tuple(shard_shape[ax + 1:]))


def _blocks_from_full(full, shard_shape, ax):
    g = full.reshape(tuple(shard_shape[:ax]) + (N_DEV, shard_shape[ax]) + tuple(shard_shape[ax + 1:]))
    return jnp.moveaxis(g, ax, 0).reshape(N_DEV, -1)


def _numel(shape):
    n = 1
    for s in shape:
        n *= s
    return n


def _pack_flat(arrs, total):
    flat = jnp.concatenate([a.reshape(-1).astype(F32) for a in arrs])
    return jnp.pad(flat, (0, total - flat.shape[0]))


def _train_step(x3, mem3, target3, P, M_, V_):
    Bl, S, D = x3.shape
    T = Bl * S
    ML = mem3.shape[1]
    x = x3.reshape(T, D)
    mem = mem3.reshape(Bl * ML, D)
    target = target3.reshape(T, D)
    shp = {n: P[n].shape for n in WEIGHTS}

    n_big = sum(_numel(shp[n]) for n in BIG)
    n_small = sum(_numel(shp[n]) for n in SMALL)
    n_small_pad = -(-n_small // (PACK_ROWS * PACK_W)) * (PACK_ROWS * PACK_W)
    big_rows, small_rows = n_big // PACK_W, n_small_pad // PACK_W
    def pack_rows(src):
        big = [src[n].reshape(-1, PACK_W) for n in BIG]
        return jnp.concatenate(big + [_pack_flat([src[n] for n in SMALL], n_small_pad).reshape(small_rows, PACK_W)], axis=0)

    big_pack = jnp.concatenate([P[n].reshape(-1, PACK_W) for n in BIG], axis=0).astype(BF16)
    small_pack = _pack_flat([P[n] for n in SMALL], n_small_pad).reshape(small_rows, PACK_W)
    big_all = _all_gather("gather_matrices", big_pack)
    small_all = _all_gather("gather_vectors", small_pack).reshape(N_DEV, n_small_pad)
    Wf = {}
    big_row0 = {}
    off = 0
    for n in BIG:
        nr = _numel(shp[n]) // PACK_W
        big_row0[n] = off
        Wf[n] = _full_from_blocks(big_all[:, off:off + nr], shp[n], SHARD_AXIS[n])
        off += nr
    off = 0
    for n in SMALL:
        ne = _numel(shp[n])
        Wf[n] = _full_from_blocks(small_all[:, off:off + ne], shp[n], SHARD_AXIS[n])
        off += ne
    for n in REPL:
        Wf[n] = P[n]

    row = lambda v: v.reshape(1, -1).astype(F32)
    gains = [[row(Wf['ln_gains'][i, j]) for j in range(6)] for i in range(2)]
    LW = 128

    def pad_cols(w):
        return jnp.pad(w, ((0, 0), (0, LW - w.shape[1]))).astype(F32)

    def pad_rows(w):
        return jnp.pad(w, ((0, LW - w.shape[0]), (0, 0))).astype(F32)

    G = {}
    saved = {}

    mem_n, = _rw("mem_norm_fwd", _f_pre(BF16), [mem], [row(Wf['mem_norm'])], outs=[(D, BF16)])

    def post_pre(name, xin, t, g_post, g_pre, out_dtype, bias=()):
        return _rw(name, _f_post_pre(out_dtype, bool(bias)), [xin, t], [g_post, g_pre, *bias],
                   outs=[(D, F32), (D, out_dtype)], seq=S)

    def post_pre_bwd(name, xin, t, g_post, g_pre, dx_next, dhn, bias=(), ct_adds=()):
        return _rw_vjp(name, _f_post_pre(F32, bool(bias)), [xin, t], [g_post, g_pre, *bias], [dx_next, dhn], 2,
                       d_rows=[(0, F32), (1, BF16)], d_params=list(range(2 + len(bias))), ct_adds=ct_adds, seq=S)

    def xattn_fwd(i, hq):
        q = _mm_nn(f"xattn{i}_q", hq, Wf['c_w_q'][i], BF16)
        kv = _mm_nn(f"xattn{i}_kv", mem_n, Wf['c_w_kv'][i], F32)
        kv = kv.reshape(Bl, ML, 2 * D)
        o, = _rw(f"xattn{i}_attn", _f_attn, [q], [kv], outs=[(D, BF16)], seq=S, per_b=(0,))
        saved[f"xattn{i}"] = (hq, q, kv, o)
        return _mm_nn(f"xattn{i}_o", o, Wf['c_w_o'][i], F32)

    def mlp_fwd(i, hm):
        up, act = _mm_up_act(f"mlp{i}_up", hm, Wf['m_w_up'][i])
        saved[f"mlp{i}"] = (hm, up, act)
        return _mm_nn(f"mlp{i}_down", act, Wf['m_w_down'][i], F32)

    b_in = row(Wf['a_b_in'][0])
    b_y, b_u = b_in[:, :D], b_in[:, D:]
    cw = [row(Wf['a_conv_w'][0, t]) for t in range(4)]
    cb = row(Wf['a_conv_b'][0])
    gate_w = Wf['a_gate_w'][0].reshape(2 * LRU_HEADS * (D // LRU_HEADS), D // LRU_HEADS).astype(F32)
    gate_b = Wf['a_gate_b'][0].reshape(2, D).astype(F32)
    gb_r, gb_i = gate_b[0:1], gate_b[1:2]
    lam = row(Wf['a_lambda'][0])
    b_out = row(Wf['a_b_out'][0])

    hn0, = _rw("lru_pre", _f_pre(BF16), [x], [gains[0][0]], outs=[(D, BF16)], seq=S)
    proj = _mm_nn("lru_in", hn0, Wf['a_w_in'][0], F32)
    prev3 = [(0, 1), (0, 2), (0, 3)]
    conv, = _rw("lru_conv", _f_conv, [proj], [b_u] + cw + [cb], outs=[(D, F32)], shifts=prev3, seq=S)
    gate_par = [gate_w, gb_r, gb_i, lam, b_y]
    a_l, b_l, yb = _rw("lru_gates", _f_gates, [conv, proj], gate_par, outs=[(D, F32)] * 3, seq=S)
    h_l = _lru_scan("lru_scan", a_l, b_l, S)
    hy, = _rw("lru_hy", _f_hy, [h_l, yb], outs=[(D, BF16)], seq=S)
    t0 = _mm_nn("lru_out", hy, Wf['a_w_out'][0], F32)
    x1, hq0 = post_pre("lru_post", x, t0, gains[0][1], gains[0][2], BF16, bias=(b_out,))
    c0 = xattn_fwd(0, hq0)
    x2, hm0 = post_pre("xattn0_post", x1, c0, gains[0][3], gains[0][4], BF16)
    m0 = mlp_fwd(0, hm0)
    x3_, hn1 = post_pre("mlp0_post", x2, m0, gains[0][5], gains[1][0], F32)

    mu = [row(Wf['b_mu'][0, j]) for j in range(6)]
    lora = [row(Wf['b_w0'][0]), pad_cols(Wf['b_w1'][0]), pad_rows(Wf['b_w2'][0]),
            row(Wf['b_a0'][0]), pad_cols(Wf['b_a1'][0]), pad_rows(Wf['b_a2'][0]),
            Wf['b_g1'][0].astype(F32), Wf['b_g2'][0].astype(F32)]
    k_k, k_a = row(Wf['b_k_k'][0]), row(Wf['b_k_a'][0])
    gn_g, gn_b, r_k = row(Wf['b_gn_g'][0]), row(Wf['b_gn_b'][0]), row(Wf['b_r_k'][0])

    xr, xk, xv, decay, a_r, g_r = _rw("rwkv_mix", _f_mix, [hn1], mu + lora,
                                      outs=[(D, BF16)] * 3 + [(D, F32)] * 3, shifts=[(0, 1)], seq=S)
    r_ = _mm_nn("rwkv_r", xr, Wf['b_w_rkv'][0, 0], F32)
    k_ = _mm_nn("rwkv_k", xk, Wf['b_w_rkv'][0, 1], F32)
    v_ = _mm_nn("rwkv_v", xv, Wf['b_w_rkv'][0, 2], F32)
    k2, ra, rb = _rw("rwkv_kk", _f_kk, [k_, a_r], [k_k, k_a], outs=[(D, F32)] * 3, seq=S)
    y_r, states, sa_r, last_r = _rwkv_fwd("rwkv_scan", r_, decay, k2, v_, ra, rb, S)
    gn_par = [gn_g, gn_b, r_k]
    og, = _rw("rwkv_gn", _f_gn, [y_r, r_, k2, v_, g_r], gn_par, outs=[(D, BF16)], seq=S)
    t1 = _mm_nn("rwkv_out", og, Wf['b_w_o'][0], F32)
    x4, hq1 = post_pre("rwkv_post", x3_, t1, gains[1][1], gains[1][2], BF16)
    c1 = xattn_fwd(1, hq1)
    x5, hm1 = post_pre("xattn1_post", x4, c1, gains[1][3], gains[1][4], BF16)
    m1 = mlp_fwd(1, hm1)
    x6, = _rw("mlp1_post", _f_post, [x5, m1], [gains[1][5]], outs=[(D, F32)], seq=S)

    dx, loss_acc = _rw("loss", _f_loss, [x6, target], outs=[(D, F32)], accs=[(8, LANES)], seq=S)
    loss = lax.psum(loss_acc[0, 0], ("x", "y", "c"))

    dgain = [[None] * 6 for _ in range(2)]
    dmem_n = None

    def mlp_bwd(i, dm):
        hm, up, act = saved[f"mlp{i}"]
        G_down = _mm_tn(f"mlp{i}_down_dw", act, dm)
        dup = _mm_nt_act_bwd(f"mlp{i}_down_dx", dm, Wf['m_w_down'][i], up)
        G_up = _mm_tn(f"mlp{i}_up_dw", hm, dup)
        return _mm_nt(f"mlp{i}_up_dx", dup, Wf['m_w_up'][i], F32), G_up, G_down

    def xattn_bwd(i, dc):
        hq, q, kv, o = saved[f"xattn{i}"]
        G_o = _mm_tn(f"xattn{i}_o_dw", o, dc)
        do = _mm_nt(f"xattn{i}_o_dx", dc, Wf['c_w_o'][i], F32)
        dq, dkv = _rw_vjp(f"xattn{i}_attn_bwd", _f_attn, [q], [kv], [do], 1, d_rows=[(0, BF16)],
                          d_params=[0], seq=S, per_b=(0,))
        dkv = dkv.reshape(Bl * ML, 2 * D)
        G_kv = _mm_tn(f"xattn{i}_kv_dw", mem_n, dkv)
        dmn = _mm_nt(f"xattn{i}_kv_dx", dkv, Wf['c_w_kv'][i], F32)
        G_q = _mm_tn(f"xattn{i}_q_dw", hq, dq)
        dhq = _mm_nt(f"xattn{i}_q_dx", dq, Wf['c_w_q'][i], F32)
        return dhq, G_q, G_kv, G_o, dmn

    G_up, G_down, G_q, G_kv, G_o = [None] * 2, [None] * 2, [None] * 2, [None] * 2, [None] * 2
    dm1, dgain[1][5] = _rw_vjp("mlp1_post_bwd", _f_post, [x5, m1], [gains[1][5]], [dx], 1,
                               d_rows=[(1, BF16)], d_params=[0], seq=S)
    dhm1, G_up[1], G_down[1] = mlp_bwd(1, dm1)
    dx, dc1, dgain[1][3], dgain[1][4] = post_pre_bwd("xattn1_post_bwd", x4, c1, gains[1][3], gains[1][4], dx, dhm1)
    dhq1, G_q[1], G_kv[1], G_o[1], dmn1 = xattn_bwd(1, dc1)

    dx, dt1, dgain[1][1], dgain[1][2] = post_pre_bwd("rwkv_post_bwd", x3_, t1, gains[1][1], gains[1][2], dx, dhq1)
    G['b_w_o'] = _mm_tn("rwkv_out_dw", og, dt1)[None]
    dog = _mm_nt("rwkv_out_dx", dt1, Wf['b_w_o'][0], F32)
    dy_r, dr_p, dk2_p, dv_p, dg_r, d_gn_g, d_gn_b, d_r_k = _rw_vjp(
        "rwkv_gn_bwd", _f_gn, [y_r, r_, k2, v_, g_r], gn_par, [dog], 1,
        d_rows=[(j, F32) for j in range(5)], d_params=[0, 1, 2], seq=S)
    dr_, ddecay, dk2, dv_, dra, drb = _rwkv_bwd("rwkv_scan_bwd", r_, decay, k2, v_, ra, rb, states, sa_r, last_r, dy_r,
                                                dr_p, dk2_p, dv_p, S)
    dk_, da_r, d_k_k, d_k_a = _rw_vjp("rwkv_kk_bwd", _f_kk, [k_, a_r], [k_k, k_a], [dk2, dra, drb], 3,
                                      d_rows=[(0, F32), (1, F32)], d_params=[0, 1], seq=S)
    G_rkv = [_mm_tn("rwkv_r_dw", xr, dr_), _mm_tn("rwkv_k_dw", xk, dk_), _mm_tn("rwkv_v_dw", xv, dv_)]
    G['b_w_rkv'] = jnp.stack(G_rkv)[None]
    dxr = _mm_nt("rwkv_r_dx", dr_, Wf['b_w_rkv'][0, 0], F32)
    dxk = _mm_nt("rwkv_k_dx", dk_, Wf['b_w_rkv'][0, 1], F32)
    dxv = _mm_nt("rwkv_v_dx", dv_, Wf['b_w_rkv'][0, 2], F32)
    mix_out = _rw_vjp("rwkv_mix_bwd", _f_mix, [hn1], mu + lora, [dxr, dxk, dxv, ddecay, da_r, dg_r], 6,
                      d_rows=[(0, F32)], d_shifts=[(0, F32)], d_params=list(range(14)), shifts=[(0, 1)], seq=S)
    dhn1, dhp1 = mix_out[0], mix_out[1]
    d_mu = mix_out[2:8]
    d_lora = mix_out[8:16]
    dx, dm0, dgain[0][5], dgain[1][0] = post_pre_bwd("mlp0_post_bwd", x2, m0, gains[0][5], gains[1][0], dx, dhn1,
                                                     ct_adds=[(1, dhp1, -1)])
    G['b_mu'] = jnp.concatenate(d_mu, axis=0)[None]
    G['b_w0'], G['b_a0'] = d_lora[0], d_lora[3]
    G['b_w1'] = d_lora[1][:, :shp['b_w1'][2]][None]
    G['b_w2'] = d_lora[2][:shp['b_w2'][1]][None]
    G['b_a1'] = d_lora[4][:, :shp['b_a1'][2]][None]
    G['b_a2'] = d_lora[5][:shp['b_a2'][1]][None]
    G['b_g1'], G['b_g2'] = d_lora[6][None], d_lora[7][None]
    G['b_k_k'], G['b_k_a'], G['b_gn_g'], G['b_gn_b'] = d_k_k, d_k_a, d_gn_g, d_gn_b
    G['b_r_k'] = d_r_k.reshape(P['b_r_k'].shape)

    dhm0, G_up[0], G_down[0] = mlp_bwd(0, dm0)
    dx, dc0, dgain[0][3], dgain[0][4] = post_pre_bwd("xattn0_post_bwd", x1, c0, gains[0][3], gains[0][4], dx, dhm0)
    dhq0, G_q[0], G_kv[0], G_o[0], dmn0 = xattn_bwd(0, dc0)
    G['m_w_up'], G['m_w_down'] = jnp.stack(G_up), jnp.stack(G_down)
    G['c_w_q'], G['c_w_kv'], G['c_w_o'] = jnp.stack(G_q), jnp.stack(G_kv), jnp.stack(G_o)

    dx, dt0, dgain[0][1], dgain[0][2], G['a_b_out'] = post_pre_bwd("lru_post_bwd", x, t0, gains[0][1], gains[0][2],
                                                                  dx, dhq0, bias=(b_out,))
    G['a_w_out'] = _mm_tn("lru_out_dw", hy, dt0)[None]
    dhy = _mm_nt("lru_out_dx", dt0, Wf['a_w_out'][0], F32)
    dh_l, dyb = _rw_vjp("lru_hy_bwd", _f_hy, [h_l, yb], [], [dhy], 1, d_rows=[(0, F32), (1, F32)], seq=S)
    da_l, db_l = _lru_scan_bwd("lru_scan_bwd", a_l, h_l, dh_l, S)
    dconv, dproj_y, d_gate_w, d_gb_r, d_gb_i, G['a_lambda'] = _rw_vjp(
        "lru_gates_bwd", _f_gates, [conv, proj], gate_par, [da_l, db_l, dyb], 3,
        d_rows=[(0, F32), (1, F32)], d_params=[0, 1, 2, 3], seq=S)
    next3 = [(0, -1), (0, -2), (0, -3)]
    dproj, G['a_b_in'], dw0, dw1, dw2, dw3, G['a_conv_b'] = _rw(
        "lru_conv_bwd", _f_conv_bwd, [dconv, proj, dproj_y], [b_u] + cw, outs=[(2 * D, BF16)],
        accs=[(1, 2 * D)] + [(1, D)] * 5, shifts=next3 + [(1, 1), (1, 2), (1, 3)], seq=S)
    G['a_conv_w'] = jnp.concatenate([dw0, dw1, dw2, dw3], axis=0)[None]
    G['a_gate_w'] = d_gate_w.reshape(P['a_gate_w'].shape[:3] + (D // LRU_HEADS, D // LRU_HEADS))
    G['a_gate_b'] = jnp.concatenate([d_gb_r, d_gb_i], axis=0).reshape(1, 2, LRU_HEADS, D // LRU_HEADS)
    G['a_w_in'] = _mm_tn("lru_in_dw", hn0, dproj)[None]
    dhn0 = _mm_nt("lru_in_dx", dproj, Wf['a_w_in'][0], F32)
    grad_x, dgain[0][0] = _rw_vjp("lru_pre_bwd", _f_pre(F32), [x], [gains[0][0]], [dhn0], 1, d_rows=[(0, F32)],
                                  d_params=[0], adds=[(0, dx, 0)], seq=S)
    d_mem_norm, = _rw_vjp("mem_norm_bwd", _f_pre(F32), [mem], [row(Wf['mem_norm'])], [dmn0], 1, d_params=[0],
                          ct_adds=[(0, dmn1, 0)])
    G['mem_norm'] = d_mem_norm.reshape(-1)
    G['ln_gains'] = jnp.stack([jnp.concatenate(dgain[i], axis=0) for i in range(2)])

    sharded = BIG + SMALL
    rows_s = big_rows + small_rows
    tr = max(t for t in range(BF16_ROWS, 161, BF16_ROWS) if rows_s % t == 0)

    pieces = [_blocks_from_full(G[n], shp[n], SHARD_AXIS[n]).reshape(N_DEV, -1, PACK_W) for n in BIG]
    small = [_blocks_from_full(G[n], shp[n], SHARD_AXIS[n]) for n in SMALL] + [jnp.zeros((N_DEV, n_small_pad - n_small), F32)]
    pieces.append(jnp.concatenate(small, axis=1).reshape(N_DEV, small_rows, PACK_W))
    g_blocks = jnp.concatenate(pieces, axis=1)
    n_split = max(q for q in range(1, 10) if rows_s % (q * BF16_ROWS) == 0)
    got = _swap_with_sibling("exchange_grads_sibling", g_blocks, n_split)
    parts_s = _exchange_chips("exchange_grads_chips", _add_own_blocks("sum_with_sibling", g_blocks, got, tr, BF16))
    n_repl = sum(_numel(shp[n]) for n in REPL)
    n_repl_pad = -(-n_repl // (8 * PACK_W)) * (8 * PACK_W)
    rows_r = n_repl_pad // PACK_W
    parts_r = _all_gather("gather_replicated_grads", _pack_flat([G[n] for n in REPL], n_repl_pad).reshape(rows_r, PACK_W))

    def pack(src, names, total, rows_):
        return _pack_flat([src[n] for n in names], total).reshape(rows_, PACK_W)

    out_s = _adamw("adamw_sharded", parts_s, pack_rows(P), pack_rows(M_), pack_rows(V_), tr)
    out_r = _adamw("adamw_replicated", parts_r, pack(P, REPL, n_repl_pad, rows_r), pack(M_, REPL, n_repl_pad, rows_r),
                   pack(V_, REPL, n_repl_pad, rows_r), rows_r)

    def unpack(packed, names):
        flat = packed.reshape(-1)
        res, o = {}, 0
        for n in names:
            ne = _numel(shp[n])
            res[n] = flat[o:o + ne].reshape(shp[n])
            o += ne
        return res

    results = []
    for j in range(4):
        both = {n: out_s[j][big_row0[n]:big_row0[n] + _numel(shp[n]) // PACK_W].reshape(shp[n]) for n in BIG}
        both.update(unpack(out_s[j][big_rows:], SMALL))
        both.update(unpack(out_r[j], REPL))
        results += [both[n] for n in WEIGHTS]
    return (loss, grad_x.reshape(Bl, S, D), *results)


def kernel(x, mem, ln_gains, mem_norm, a_conv_w, a_conv_b, a_w_in, a_b_in, a_gate_w, a_gate_b, a_lambda, a_w_out, a_b_out, b_mu, b_w_rkv, b_w0, b_w1, b_w2, b_a0, b_a1, b_a2, b_g1, b_g2, b_k_k, b_k_a, b_r_k, b_gn_g, b_gn_b, b_w_o, c_w_q, c_w_kv, c_w_o, m_w_up, m_w_down, loss_target, m_ln_gains, m_mem_norm, m_a_conv_w, m_a_conv_b, m_a_w_in, m_a_b_in, m_a_gate_w, m_a_gate_b, m_a_lambda, m_a_w_out, m_a_b_out, m_b_mu, m_b_w_rkv, m_b_w0, m_b_w1, m_b_w2, m_b_a0, m_b_a1, m_b_a2, m_b_g1, m_b_g2, m_b_k_k, m_b_k_a, m_b_r_k, m_b_gn_g, m_b_gn_b, m_b_w_o, m_c_w_q, m_c_w_kv, m_c_w_o, m_m_w_up, m_m_w_down, v_ln_gains, v_mem_norm, v_a_conv_w, v_a_conv_b, v_a_w_in, v_a_b_in, v_a_gate_w, v_a_gate_b, v_a_lambda, v_a_w_out, v_a_b_out, v_b_mu, v_b_w_rkv, v_b_w0, v_b_w1, v_b_w2, v_b_a0, v_b_a1, v_b_a2, v_b_g1, v_b_g2, v_b_k_k, v_b_k_a, v_b_r_k, v_b_gn_g, v_b_gn_b, v_b_w_o, v_c_w_q, v_c_w_kv, v_c_w_o, v_m_w_up, v_m_w_down):
    weights = (ln_gains, mem_norm, a_conv_w, a_conv_b, a_w_in, a_b_in, a_gate_w, a_gate_b, a_lambda, a_w_out, a_b_out, b_mu, b_w_rkv, b_w0, b_w1, b_w2, b_a0, b_a1, b_a2, b_g1, b_g2, b_k_k, b_k_a, b_r_k, b_gn_g, b_gn_b, b_w_o, c_w_q, c_w_kv, c_w_o, m_w_up, m_w_down)
    moments1 = (m_ln_gains, m_mem_norm, m_a_conv_w, m_a_conv_b, m_a_w_in, m_a_b_in, m_a_gate_w, m_a_gate_b, m_a_lambda, m_a_w_out, m_a_b_out, m_b_mu, m_b_w_rkv, m_b_w0, m_b_w1, m_b_w2, m_b_a0, m_b_a1, m_b_a2, m_b_g1, m_b_g2, m_b_k_k, m_b_k_a, m_b_r_k, m_b_gn_g, m_b_gn_b, m_b_w_o, m_c_w_q, m_c_w_kv, m_c_w_o, m_m_w_up, m_m_w_down)
    moments2 = (v_ln_gains, v_mem_norm, v_a_conv_w, v_a_conv_b, v_a_w_in, v_a_b_in, v_a_gate_w, v_a_gate_b, v_a_lambda, v_a_w_out, v_a_b_out, v_b_mu, v_b_w_rkv, v_b_w0, v_b_w1, v_b_w2, v_b_a0, v_b_a1, v_b_a2, v_b_g1, v_b_g2, v_b_k_k, v_b_k_a, v_b_r_k, v_b_gn_g, v_b_gn_b, v_b_w_o, v_c_w_q, v_c_w_kv, v_c_w_o, v_m_w_up, v_m_w_down)
    return _train_step(x, mem, loss_target, dict(zip(WEIGHTS, weights)), dict(zip(WEIGHTS, moments1)),
                       dict(zip(WEIGHTS, moments2)))
```

```python
import jax
import jax.numpy as jnp
from jax import lax
from jax.experimental import pallas as pl
from jax.experimental.pallas import tpu as pltpu

F32 = jnp.float32
BF16 = jnp.bfloat16
N_DEV = 8
MESH = pl.DeviceIdType.MESH

V7X_VMEM_LIMIT_BYTES = 56 * 1024 * 1024
SUB = 8
HALO_ROWS = SUB
LANES = 128
RWKV_N = 64
RMS_EPS = 1e-6
GN_EPS = 64e-5
LRU_C = 8.0
XATTN_HEADS = 4
LRU_HEADS = 4

ADAM_LR, ADAM_B1, ADAM_B2, ADAM_EPS, ADAM_WD, ADAM_STEP = 0.001, 0.9, 0.999, 1e-8, 0.01, 10

WEIGHTS = ['ln_gains', 'mem_norm', 'a_conv_w', 'a_conv_b', 'a_w_in', 'a_b_in', 'a_gate_w', 'a_gate_b',
           'a_lambda', 'a_w_out', 'a_b_out', 'b_mu', 'b_w_rkv', 'b_w0', 'b_w1', 'b_w2', 'b_a0', 'b_a1',
           'b_a2', 'b_g1', 'b_g2', 'b_k_k', 'b_k_a', 'b_r_k', 'b_gn_g', 'b_gn_b', 'b_w_o', 'c_w_q',
           'c_w_kv', 'c_w_o', 'm_w_up', 'm_w_down']
SHARD_AXIS = {'ln_gains': 2, 'mem_norm': None, 'a_conv_w': 2, 'a_conv_b': None, 'a_w_in': 2, 'a_b_in': None,
              'a_gate_w': 3, 'a_gate_b': 3, 'a_lambda': None, 'a_w_out': 1, 'a_b_out': None, 'b_mu': 2,
              'b_w_rkv': 2, 'b_w0': 1, 'b_w1': 1, 'b_w2': 2, 'b_a0': 1, 'b_a1': 1, 'b_a2': 2, 'b_g1': 1,
              'b_g2': 2, 'b_k_k': 1, 'b_k_a': 1, 'b_r_k': None, 'b_gn_g': 1, 'b_gn_b': 1, 'b_w_o': 1,
              'c_w_q': 1, 'c_w_kv': 2, 'c_w_o': 1, 'm_w_up': 2, 'm_w_down': 1}
BIG = ['a_w_in', 'a_gate_w', 'a_w_out', 'b_w_rkv', 'b_w1', 'b_w2', 'b_a1', 'b_a2', 'b_g1', 'b_g2', 'b_w_o',
       'c_w_q', 'c_w_kv', 'c_w_o', 'm_w_up', 'm_w_down']
SMALL = ['ln_gains', 'a_conv_w', 'a_gate_b', 'b_mu', 'b_w0', 'b_a0', 'b_k_k', 'b_k_a', 'b_gn_g', 'b_gn_b']
REPL = ['mem_norm', 'a_conv_b', 'a_b_in', 'a_lambda', 'a_b_out', 'b_r_k']
PACK_W = 1024
PACK_ROWS = 128
BF16_ROWS = 16


def _cparams(*sem):
    return pltpu.CompilerParams(dimension_semantics=sem, vmem_limit_bytes=V7X_VMEM_LIMIT_BYTES)


def _shift_tile(tile, halo, k, pos, seq, tm):
    if k > 0:
        ext = jnp.concatenate([halo, tile], axis=0)
        r = pltpu.roll(ext, k, 0)[HALO_ROWS:HALO_ROWS + tm]
        return jnp.where(pos >= k, r, 0.0)
    kk = -k
    ext = jnp.concatenate([tile, halo], axis=0)
    r = pltpu.roll(ext, tm + HALO_ROWS - kk, 0)[0:tm]
    return jnp.where(pos + kk < seq, r, 0.0)


def _row_specs(rows, shifts, params, per_b, seq, tm):
    T = rows[0].shape[0]
    ns = seq // tm
    specs = [pl.BlockSpec((tm, r.shape[1]), lambda b, i: (b * ns + i, 0)) for r in rows]
    halo_keys = []
    for idx, k in shifts:
        key = (idx, k > 0)
        if key not in halo_keys:
            halo_keys.append(key)
    per8 = tm // HALO_ROWS
    last8 = T // HALO_ROWS - 1
    for idx, prev in halo_keys:
        c = rows[idx].shape[1]
        if prev:
            specs.append(pl.BlockSpec((HALO_ROWS, c), lambda b, i: (jnp.maximum((b * ns + i) * per8 - 1, 0), 0)))
        else:
            specs.append(pl.BlockSpec((HALO_ROWS, c), lambda b, i: (jnp.minimum((b * ns + i + 1) * per8, last8), 0)))
    for j, p in enumerate(params):
        if j in per_b:
            specs.append(pl.BlockSpec((None,) + p.shape[1:], lambda b, i: (b, 0, 0)))
        else:
            specs.append(pl.BlockSpec(p.shape, lambda b, i, nd=p.ndim: (0,) * nd))
    return specs, halo_keys


def _load_tiles(refs, rows, shifts, halo_keys, params, seq, tm):
    nr, nh = len(rows), len(halo_keys)
    i = pl.program_id(1)
    pos = i * tm + lax.broadcasted_iota(jnp.int32, (tm, 1), 0)
    row_t = [r[...] for r in refs[:nr]]
    halo_t = {key: refs[nr + j][...] for j, key in enumerate(halo_keys)}
    sh_t = [_shift_tile(row_t[idx], halo_t[(idx, k > 0)], k, pos, seq, tm) for idx, k in shifts]
    par_t = [p[...] for p in refs[nr + nh:nr + nh + len(params)]]
    return pos, row_t, sh_t, par_t


ROW_TILES = (512, 256)
ROW_TILE_BUDGET_BYTES = 16 * 1024 * 1024


def _row_tile(seq, row_bytes):
    for tm in ROW_TILES:
        if 2 * tm * row_bytes <= ROW_TILE_BUDGET_BYTES:
            return min(tm, seq)
    return min(ROW_TILES[-1], seq)


def _rw(name, fn, rows, params=(), outs=(), accs=(), shifts=(), seq=None, per_b=()):
    rows, params = list(rows), list(params)
    T = rows[0].shape[0]
    seq = seq or T
    tm = _row_tile(seq, sum(r.shape[1] * r.dtype.itemsize for r in rows) + sum(c * jnp.dtype(dt).itemsize for c, dt in outs))
    nb, ns = T // seq, seq // tm
    in_specs, halo_keys = _row_specs(rows, shifts, params, per_b, seq, tm)
    n_in, no = len(in_specs), len(outs)

    def body(*refs):
        pos, row_t, sh_t, par_t = _load_tiles(refs, rows, shifts, halo_keys, params, seq, tm)
        res = fn(pos, *row_t, *sh_t, *par_t)
        out_refs = refs[n_in:n_in + no]
        acc_refs = refs[n_in + no:]
        for o_ref, v in zip(out_refs, res[:no]):
            o_ref[...] = v.astype(o_ref.dtype)
        if acc_refs:
            @pl.when((pl.program_id(0) == 0) & (pl.program_id(1) == 0))
            def _():
                for a_ref in acc_refs:
                    a_ref[...] = jnp.zeros_like(a_ref)
            for a_ref, v in zip(acc_refs, res[no:]):
                a_ref[...] += v

    out_shape = [jax.ShapeDtypeStruct((T, c), dt) for c, dt in outs]
    out_shape += [jax.ShapeDtypeStruct(s, F32) for s in accs]
    out_specs = [pl.BlockSpec((tm, c), lambda b, i: (b * ns + i, 0)) for c, _ in outs]
    out_specs += [pl.BlockSpec(s, lambda b, i: (0, 0)) for s in accs]
    return pl.pallas_call(body, name=name, grid=(nb, ns), in_specs=in_specs, out_specs=out_specs,
                          out_shape=out_shape, compiler_params=_cparams("arbitrary", "arbitrary"))(*rows, *[
                              rows[idx] for idx, _ in halo_keys], *params)


def _rw_vjp(name, fn, rows, params, cts, n_out, d_rows=(), d_shifts=(), d_params=(), adds=(), ct_adds=(),
            shifts=(), seq=None, per_b=()):
    rows, params, cts = list(rows), list(params), list(cts)
    T = rows[0].shape[0]
    seq = seq or T
    row_bytes = sum(r.shape[1] * r.dtype.itemsize for r in rows + cts + [a for _, a, _ in list(adds) + list(ct_adds)])
    row_bytes += sum(rows[j].shape[1] * jnp.dtype(dt).itemsize for j, dt in d_rows)
    row_bytes += sum(rows[shifts[j][0]].shape[1] * jnp.dtype(dt).itemsize for j, dt in d_shifts)
    tm = _row_tile(seq, row_bytes)
    nb, ns = T // seq, seq // tm
    n_ga = len(adds)
    adds = list(adds) + list(ct_adds)
    add_rows = [a for _, a, _ in adds]
    add_shifts = [(j, k) for j, (_, _, k) in enumerate(adds) if k != 0]
    all_rows = rows + cts + add_rows
    off_ct, off_add = len(rows), len(rows) + len(cts)
    all_shifts = list(shifts) + [(off_add + j, k) for j, k in add_shifts]
    in_specs, halo_keys = _row_specs(all_rows, all_shifts, params, per_b, seq, tm)
    n_in = len(in_specs)
    n_dr, n_ds = len(d_rows), len(d_shifts)

    def body(*refs):
        pos, row_t, sh_t, par_t = _load_tiles(refs, all_rows, all_shifts, halo_keys, params, seq, tm)
        fn_rows, ct_t, add_t = row_t[:off_ct], row_t[off_ct:off_add], row_t[off_add:]
        fn_sh, add_sh = sh_t[:len(shifts)], sh_t[len(shifts):]
        diff = [fn_rows[j] for j, _ in d_rows] + [fn_sh[j] for j, _ in d_shifts] + [par_t[j] for j in d_params]

        def f(*d):
            r2, s2, p2 = list(fn_rows), list(fn_sh), list(par_t)
            for (j, _), v in zip(d_rows, d[:n_dr]):
                r2[j] = v
            for (j, _), v in zip(d_shifts, d[n_dr:n_dr + n_ds]):
                s2[j] = v
            for j, v in zip(d_params, d[n_dr + n_ds:]):
                p2[j] = v
            return tuple(fn(pos, *r2, *s2, *p2)[:n_out])

        sh_iter = iter(add_sh)
        add_v = [add_t[j] if k == 0 else next(sh_iter) for j, (_, _, k) in enumerate(adds)]
        ct_t = list(ct_t)
        for (tgt, _, _), v in zip(adds[n_ga:], add_v[n_ga:]):
            ct_t[tgt] = ct_t[tgt] + v
        outs_v, vjp_fn = jax.vjp(f, *diff)
        grads = list(vjp_fn(tuple(c.astype(o.dtype) for c, o in zip(ct_t, outs_v))))
        for (tgt, _, _), v in zip(adds[:n_ga], add_v[:n_ga]):
            grads[tgt] = grads[tgt] + v
        out_refs = refs[n_in:]
        for o_ref, g in zip(out_refs[:n_dr + n_ds], grads[:n_dr + n_ds]):
            o_ref[...] = g.astype(o_ref.dtype)
        for o_ref, g, j in zip(out_refs[n_dr + n_ds:], grads[n_dr + n_ds:], d_params):
            first = (pl.program_id(1) == 0) if j in per_b else ((pl.program_id(0) == 0) & (pl.program_id(1) == 0))

            @pl.when(first)
            def _(o_ref=o_ref):
                o_ref[...] = jnp.zeros_like(o_ref)
            o_ref[...] += g.astype(F32)

    out_shape, out_specs = [], []
    for j, dt in d_rows:
        out_shape.append(jax.ShapeDtypeStruct(rows[j].shape, dt))
        out_specs.append(pl.BlockSpec((tm, rows[j].shape[1]), lambda b, i: (b * ns + i, 0)))
    for j, dt in d_shifts:
        src = rows[shifts[j][0]]
        out_shape.append(jax.ShapeDtypeStruct(src.shape, dt))
        out_specs.append(pl.BlockSpec((tm, src.shape[1]), lambda b, i: (b * ns + i, 0)))
    for j in d_params:
        p = params[j]
        out_shape.append(jax.ShapeDtypeStruct(p.shape, F32))
        if j in per_b:
            out_specs.append(pl.BlockSpec((None,) + p.shape[1:], lambda b, i: (b, 0, 0)))
        else:
            out_specs.append(pl.BlockSpec(p.shape, lambda b, i, nd=p.ndim: (0,) * nd))
    return pl.pallas_call(body, name=name, grid=(nb, ns), in_specs=in_specs, out_specs=out_specs,
                          out_shape=out_shape, compiler_params=_cparams("arbitrary", "arbitrary"))(*all_rows, *[
                              all_rows[idx] for idx, _ in halo_keys], *params)


MM_TILE = 1024
MM_DEEP = 2048


def _mm_nn(name, x, w, out_dtype):
    M, K = x.shape
    N = w.shape[1]
    tm, tn = min(MM_TILE if K <= MM_DEEP else MM_TILE // 2, M), min(MM_TILE, N)

    def body(x_ref, w_ref, o_ref):
        o_ref[...] = jnp.dot(x_ref[...].astype(BF16), w_ref[...].astype(BF16),
                             preferred_element_type=F32).astype(o_ref.dtype)

    return pl.pallas_call(body, name=name, grid=(M // tm, N // tn),
                          in_specs=[pl.BlockSpec((tm, K), lambda i, j: (i, 0)), pl.BlockSpec((K, tn), lambda i, j: (0, j))],
                          out_specs=pl.BlockSpec((tm, tn), lambda i, j: (i, j)),
                          out_shape=jax.ShapeDtypeStruct((M, N), out_dtype),
                          compiler_params=_cparams("parallel", "parallel"))(x, w)


def _mm_up_act(name, x, w):
    M, K = x.shape
    N = w.shape[1]
    tm, tn = min(MM_TILE if K <= MM_DEEP else MM_TILE // 2, M), min(MM_TILE, N)

    def body(x_ref, w_ref, up_ref, act_ref):
        up = jnp.dot(x_ref[...].astype(BF16), w_ref[...].astype(BF16), preferred_element_type=F32)
        up_ref[...] = up.astype(BF16)
        act_ref[...] = jnp.square(jnp.maximum(up, 0.0)).astype(BF16)

    out = pl.BlockSpec((tm, tn), lambda i, j: (i, j))
    return pl.pallas_call(body, name=name, grid=(M // tm, N // tn),
                          in_specs=[pl.BlockSpec((tm, K), lambda i, j: (i, 0)), pl.BlockSpec((K, tn), lambda i, j: (0, j))],
                          out_specs=[out, out], out_shape=[jax.ShapeDtypeStruct((M, N), BF16)] * 2,
                          compiler_params=_cparams("parallel", "parallel"))(x, w)


def _mm_nt_act_bwd(name, dy, w, up):
    M, N = dy.shape
    K = w.shape[0]
    tm, tk = min(MM_TILE if N <= MM_DEEP else MM_TILE // 2, M), min(MM_TILE, K)

    def body(dy_ref, w_ref, up_ref, o_ref):
        dact = lax.dot_general(dy_ref[...].astype(BF16), w_ref[...].astype(BF16), (((1,), (1,)), ((), ())),
                               preferred_element_type=F32)
        o_ref[...] = (dact * (2.0 * jnp.maximum(up_ref[...].astype(F32), 0.0))).astype(o_ref.dtype)

    out = pl.BlockSpec((tm, tk), lambda i, j: (i, j))
    return pl.pallas_call(body, name=name, grid=(M // tm, K // tk),
                          in_specs=[pl.BlockSpec((tm, N), lambda i, j: (i, 0)), pl.BlockSpec((tk, N), lambda i, j: (j, 0)), out],
                          out_specs=out, out_shape=jax.ShapeDtypeStruct((M, K), BF16),
                          compiler_params=_cparams("parallel", "parallel"))(dy, w, up)


def _mm_nt(name, dy, w, out_dtype):
    M, N = dy.shape
    K = w.shape[0]
    tm, tk = min(MM_TILE if N <= MM_DEEP else MM_TILE // 2, M), min(MM_TILE, K)

    def body(dy_ref, w_ref, o_ref):
        o_ref[...] = lax.dot_general(dy_ref[...].astype(BF16), w_ref[...].astype(BF16), (((1,), (1,)), ((), ())),
                                     preferred_element_type=F32).astype(o_ref.dtype)

    return pl.pallas_call(body, name=name, grid=(M // tm, K // tk),
                          in_specs=[pl.BlockSpec((tm, N), lambda i, j: (i, 0)), pl.BlockSpec((tk, N), lambda i, j: (j, 0))],
                          out_specs=pl.BlockSpec((tm, tk), lambda i, j: (i, j)),
                          out_shape=jax.ShapeDtypeStruct((M, K), out_dtype),
                          compiler_params=_cparams("parallel", "parallel"))(dy, w)


def _mm_tn(name, x, dy):
    M, K = x.shape
    N = dy.shape[1]
    tm, tk, tn = min(MM_TILE, M), min(MM_TILE, K), min(MM_TILE, N)

    def body(x_ref, dy_ref, o_ref):
        @pl.when(pl.program_id(2) == 0)
        def _():
            o_ref[...] = jnp.zeros_like(o_ref)
        o_ref[...] += lax.dot_general(x_ref[...].astype(BF16), dy_ref[...].astype(BF16), (((0,), (0,)), ((), ())),
                                      preferred_element_type=F32)

    return pl.pallas_call(body, name=name, grid=(K // tk, N // tn, M // tm),
                          in_specs=[pl.BlockSpec((tm, tk), lambda i, j, m: (m, i)), pl.BlockSpec((tm, tn), lambda i, j, m: (m, j))],
                          out_specs=pl.BlockSpec((tk, tn), lambda i, j, m: (i, j)),
                          out_shape=jax.ShapeDtypeStruct((K, N), F32),
                          compiler_params=_cparams("parallel", "parallel", "arbitrary"))(x, dy)


def _lru_scan(name, a, b, seq, ct=128):
    T, C = a.shape

    def body(a_ref, b_ref, h_ref):
        A, Bv = a_ref[...], b_ref[...]
        row = lax.broadcasted_iota(jnp.int32, (seq, 1), 0)
        k = 1
        while k < seq:
            keep = row >= k
            Bv = A * jnp.where(keep, pltpu.roll(Bv, k, 0), 0.0) + Bv
            A = A * jnp.where(keep, pltpu.roll(A, k, 0), 1.0)
            k *= 2
        h_ref[...] = Bv

    spec = pl.BlockSpec((seq, ct), lambda b_, j: (b_, j))
    return pl.pallas_call(body, name=name, grid=(T // seq, C // ct), in_specs=[spec, spec], out_specs=spec,
                          out_shape=jax.ShapeDtypeStruct((T, C), F32),
                          compiler_params=_cparams("parallel", "parallel"))(a, b)


def _lru_scan_bwd(name, a, h, dh, seq, ct=128):
    T, C = a.shape

    def body(a_ref, h_ref, dh_ref, da_ref, db_ref):
        row = lax.broadcasted_iota(jnp.int32, (seq, 1), 0)
        Cf = jnp.where(row < seq - 1, pltpu.roll(a_ref[...], seq - 1, 0), 0.0)
        G = dh_ref[...]
        k = 1
        while k < seq:
            keep = row + k < seq
            G = Cf * jnp.where(keep, pltpu.roll(G, seq - k, 0), 0.0) + G
            Cf = Cf * jnp.where(keep, pltpu.roll(Cf, seq - k, 0), 1.0)
            k *= 2
        db_ref[...] = G
        da_ref[...] = G * jnp.where(row >= 1, pltpu.roll(h_ref[...], 1, 0), 0.0)

    spec = pl.BlockSpec((seq, ct), lambda b_, j: (b_, j))
    return pl.pallas_call(body, name=name, grid=(T // seq, C // ct), in_specs=[spec] * 3, out_specs=[spec] * 2,
                          out_shape=[jax.ShapeDtypeStruct((T, C), F32)] * 2,
                          compiler_params=_cparams("parallel", "parallel"))(a, h, dh)


def _pair_consts():
    jj = lax.broadcasted_iota(jnp.int32, (RWKV_N, LANES), 0)
    ll = lax.broadcasted_iota(jnp.int32, (RWKV_N, LANES), 1)
    pick = ((ll & (RWKV_N - 1)) == jj).astype(BF16)
    l0 = lax.broadcasted_iota(jnp.int32, (LANES, LANES), 0)
    l1 = lax.broadcasted_iota(jnp.int32, (LANES, LANES), 1)
    same = ((l0 >> 6) == (l1 >> 6)).astype(BF16)
    return pick, same


N_COL = 5


def _cols_raw(rows, pick, same):
    lhs = [pick * jnp.broadcast_to(x, (RWKV_N, LANES)).astype(BF16) for x in rows]
    out = jnp.dot(jnp.concatenate(lhs, axis=0), same, preferred_element_type=F32)
    return tuple(out[i * RWKV_N:(i + 1) * RWKV_N] for i in range(len(rows)))


def _cols_adjoint(g):
    n = len(g)
    g = list(g) + [jnp.zeros_like(g[0])] * (n % 2)
    head = ((lax.broadcasted_iota(jnp.int32, (SUB, LANES), 1) >> 6)
            == lax.broadcasted_iota(jnp.int32, (SUB, LANES), 0)).astype(BF16)
    low = lax.broadcasted_iota(jnp.int32, (1, LANES), 1) < RWKV_N
    sums = lax.dot_general(head, jnp.concatenate([t.astype(BF16) for t in g], axis=0),
                           (((1,), (1,)), ((), ())), preferred_element_type=F32)
    rows = []
    for i in range(0, n, 2):
        blk = sums[:, i * RWKV_N:(i + 2) * RWKV_N]
        swapped = pltpu.roll(blk, RWKV_N, 1)
        rows.append(jnp.where(low, blk[0:1], swapped[1:2]))
        rows.append(jnp.where(low, swapped[0:1], blk[1:2]))
    return tuple(rows[:n])


def _rwkv_cols(rows, pick, same):
    flat = []
    for r, w, k, v, a, b in rows:
        flat += [a, 1.0 - w, b, k, r]
    tiles = _cols_raw(tuple(flat), pick, same)
    return [tiles[N_COL * p:N_COL * (p + 1)] for p in range(len(rows))]


def _rwkv_step(s0, tiles, v):
    ca, ce, cb, ck, cr = tiles
    sa = jnp.sum(s0 * ca, axis=0, keepdims=True)
    s1 = s0 - s0 * ce + cb * sa + ck * v
    return s1, jnp.sum(s1 * cr, axis=0, keepdims=True), sa


def _rwkv_step_bwd(s0, s1, sa, tiles, v, dy, ds1):
    ca, ce, cb, ck, cr = tiles
    ds1 = ds1 + cr * dy
    dsa = jnp.sum(ds1 * cb, axis=0, keepdims=True)
    dv = jnp.sum(ds1 * ck, axis=0, keepdims=True)
    ds0 = ds1 - ds1 * ce + ca * dsa
    return ds0, (s0 * dsa, ds1 * s0, ds1 * sa, ds1 * v, s1 * dy), dv


def _rwkv_fwd(name, r, w, k, v, a, b, seq, tc=8, G=8, NB=4):
    T, C = r.shape
    B = T // seq
    tc = min(tc, seq)
    NB = NB if B % NB == 0 else 1
    NP = C // LANES
    nc = seq // tc
    row = pl.BlockSpec((NB, tc, LANES * G), lambda b_, g, c: (b_, c, g))
    st = pl.BlockSpec((NB, tc, G, RWKV_N, LANES), lambda b_, g, c: (b_, c, g, 0, 0))
    last = pl.BlockSpec((NB, G, RWKV_N, LANES), lambda b_, g, c: (b_, g, 0, 0))
    chains = [(q, p) for q in range(NB) for p in range(G)]

    def body(r_ref, w_ref, k_ref, v_ref, a_ref, b_ref, y_ref, st_ref, sa_ref, last_ref, s_scr):
        @pl.when(pl.program_id(2) == 0)
        def _():
            s_scr[...] = jnp.zeros_like(s_scr)
        pick, same = _pair_consts()
        rid = lax.broadcasted_iota(jnp.int32, (SUB, LANES), 0)

        def group(t8, carry):
            base = pl.multiple_of(t8 * SUB, SUB)
            rows8 = pl.ds(base, SUB)
            lanes = [pl.ds(LANES * p, LANES) for p in range(G)]
            blk = [[x[q, rows8, lanes[p]] for x in (r_ref, w_ref, k_ref, v_ref, a_ref, b_ref)] for q, p in chains]
            s = [s_scr[q, p] for q, p in chains]
            ys = [jnp.zeros((SUB, LANES), F32) for _ in chains]
            sas = [jnp.zeros((SUB, LANES), F32) for _ in chains]
            for i in range(SUB):
                tiles = _rwkv_cols([tuple(b_[i:i + 1] for b_ in blk[n_]) for n_ in range(len(chains))], pick, same)
                for n_, (q, p) in enumerate(chains):
                    st_ref[q, base + i, p] = s[n_]
                    s[n_], y, sa = _rwkv_step(s[n_], tiles[n_], blk[n_][3][i:i + 1])
                    ys[n_] = jnp.where(rid == i, y, ys[n_])
                    sas[n_] = jnp.where(rid == i, sa, sas[n_])
            for n_, (q, p) in enumerate(chains):
                s_scr[q, p] = s[n_]
                y_ref[q, rows8, lanes[p]] = ys[n_]
                sa_ref[q, rows8, lanes[p]] = sas[n_]
            return carry

        lax.fori_loop(0, tc // SUB, group, 0)

        @pl.when(pl.program_id(2) == nc - 1)
        def _():
            last_ref[...] = s_scr[...]

    seq3 = lambda x: x.reshape(B, seq, C)
    y, states, sa, final = pl.pallas_call(body,
        name=name, grid=(B // NB, NP // G, nc), in_specs=[row] * 6, out_specs=[row, st, row, last],
        out_shape=[jax.ShapeDtypeStruct((B, seq, C), F32), jax.ShapeDtypeStruct((B, seq, NP, RWKV_N, LANES), F32),
                   jax.ShapeDtypeStruct((B, seq, C), F32), jax.ShapeDtypeStruct((B, NP, RWKV_N, LANES), F32)],
        scratch_shapes=[pltpu.VMEM((NB, G, RWKV_N, LANES), F32)],
        compiler_params=_cparams("parallel", "parallel", "arbitrary"))(*[seq3(x) for x in (r, w, k, v, a, b)])
    return y.reshape(T, C), states.reshape(T, NP, RWKV_N, LANES), sa.reshape(T, C), final


def _rwkv_bwd(name, r, w, k, v, a, b, states, sa, last, dy, dr_p, dk_p, dv_p, seq, tc=8, G=8, NB=4):
    T, C = r.shape
    B = T // seq
    tc = min(tc, seq)
    NB = NB if B % NB == 0 else 1
    NP = C // LANES
    nc = seq // tc
    row = pl.BlockSpec((NB, tc, LANES * G), lambda b_, g, c: (b_, nc - 1 - c, g))
    st = pl.BlockSpec((NB, tc, G, RWKV_N, LANES), lambda b_, g, c: (b_, nc - 1 - c, g, 0, 0))
    last_spec = pl.BlockSpec((NB, G, RWKV_N, LANES), lambda b_, g, c: (b_, g, 0, 0))
    chains = [(q, p) for q in range(NB) for p in range(G)]

    def body(r_ref, w_ref, k_ref, v_ref, a_ref, b_ref, st_ref, sa_ref, last_ref, dy_ref, drp_ref, dkp_ref, dvp_ref,
             dr_ref, dw_ref, dk_ref, dv_ref, da_ref, db_ref, ds_scr, s1_scr):
        @pl.when(pl.program_id(2) == 0)
        def _():
            ds_scr[...] = jnp.zeros_like(ds_scr)
            s1_scr[...] = last_ref[...]
        pick, same = _pair_consts()
        rid = lax.broadcasted_iota(jnp.int32, (SUB, LANES), 0)
        out_refs = (da_ref, dw_ref, db_ref, dk_ref, dr_ref)

        def group(n, carry):
            base = pl.multiple_of((tc // SUB - 1 - n) * SUB, SUB)
            rows8 = pl.ds(base, SUB)
            lanes = [pl.ds(LANES * p, LANES) for p in range(G)]
            blk = [[x[q, rows8, lanes[p]] for x in (r_ref, w_ref, k_ref, v_ref, a_ref, b_ref)] for q, p in chains]
            dys = [dy_ref[q, rows8, lanes[p]] for q, p in chains]
            sas = [sa_ref[q, rows8, lanes[p]] for q, p in chains]
            ds = [ds_scr[q, p] for q, p in chains]
            s1 = [s1_scr[q, p] for q, p in chains]
            zero = jnp.zeros((SUB, LANES), F32)
            gs = [[zero, zero, zero, dkp_ref[q, rows8, lanes[p]], drp_ref[q, rows8, lanes[p]]] for q, p in chains]
            gv = [dvp_ref[q, rows8, lanes[p]] for q, p in chains]
            steps = tuple(reversed(range(SUB)))
            tiles_of = [_rwkv_cols([tuple(b_[i:i + 1] for b_ in blk[n_]) for n_ in range(len(chains))], pick, same)
                        for i in steps]
            adj_of = []
            for i, tiles in zip(steps, tiles_of):
                adj = []
                for n_, (q, p) in enumerate(chains):
                    s0 = st_ref[q, base + i, p]
                    ds[n_], g_tiles, dv = _rwkv_step_bwd(s0, s1[n_], sas[n_][i:i + 1], tiles[n_], blk[n_][3][i:i + 1],
                                                         dys[n_][i:i + 1], ds[n_])
                    s1[n_] = s0
                    adj += list(g_tiles)
                    gv[n_] = gv[n_] + jnp.where(rid == i, dv, 0.0)
                adj_of.append(adj)
            for i, adj in zip(steps, adj_of):
                rows = _cols_adjoint(adj)
                for n_ in range(len(chains)):
                    for j in range(N_COL):
                        gs[n_][j] = gs[n_][j] + jnp.where(rid == i, rows[N_COL * n_ + j], 0.0)
            for n_, (q, p) in enumerate(chains):
                ds_scr[q, p] = ds[n_]
                s1_scr[q, p] = s1[n_]
                dv_ref[q, rows8, lanes[p]] = gv[n_]
                for j in range(N_COL):
                    out_refs[j][q, rows8, lanes[p]] = gs[n_][j]
            return carry

        lax.fori_loop(0, tc // SUB, group, 0)

    seq3 = lambda x: x.reshape(B, seq, C)
    outs = pl.pallas_call(body, name=name, grid=(B // NB, NP // G, nc),
                          in_specs=[row] * 6 + [st, row, last_spec] + [row] * 4, out_specs=[row] * 6,
                          out_shape=[jax.ShapeDtypeStruct((B, seq, C), F32)] * 6,
                          scratch_shapes=[pltpu.VMEM((NB, G, RWKV_N, LANES), F32)] * 2,
                          compiler_params=_cparams("parallel", "parallel", "arbitrary"))(
                              *[seq3(x) for x in (r, w, k, v, a, b)], states.reshape(B, seq, NP, RWKV_N, LANES), seq3(sa),
                              last, *[seq3(x) for x in (dy, dr_p, dk_p, dv_p)])
    return [o.reshape(T, C) for o in outs]


def _all_gather(name, shard):
    def body(x_ref, o_ref, send_sems, recv_sems, local_sem):
        x, y, c = lax.axis_index("x"), lax.axis_index("y"), lax.axis_index("c")
        me, sibling = (x, y, c), (x, y, 1 - c)
        chips = [(1 - x, y), (x, 1 - y), (1 - x, 1 - y)]

        def copy(k, block, to, src=None):
            dst = o_ref.at[4 * block[0] + 2 * block[1] + block[2]]
            return pltpu.make_async_remote_copy(src_ref=dst if src is None else src, dst_ref=dst, send_sem=send_sems.at[k],
                                                recv_sem=recv_sems.at[k], device_id=to, device_id_type=MESH)

        mine = pltpu.make_async_copy(x_ref, o_ref.at[4 * x + 2 * y + c], local_sem)
        mine.start()
        first = [copy(0, me, sibling, src=x_ref)] + [copy(1 + j, me, (*chip, c), src=x_ref) for j, chip in enumerate(chips)]
        for cp in first:
            cp.start()
        passed = [copy(4 + j, (*chip, c), sibling) for j, chip in enumerate(chips)]
        for j, chip in enumerate(chips):
            copy(1 + j, (*chip, c), me).wait_recv()
            passed[j].start()
        copy(0, sibling, me).wait_recv()
        for j, chip in enumerate(chips):
            copy(4 + j, (*chip, 1 - c), me).wait_recv()
        for cp in first + passed:
            cp.wait_send()
        mine.wait()

    hbm = pl.BlockSpec(memory_space=pltpu.HBM)
    return pl.pallas_call(body, name=name, in_specs=[hbm], out_specs=hbm,
                          out_shape=jax.ShapeDtypeStruct((N_DEV,) + shard.shape, shard.dtype),
                          scratch_shapes=[pltpu.SemaphoreType.DMA((N_DEV - 1,)), pltpu.SemaphoreType.DMA((N_DEV - 1,)),
                                          pltpu.SemaphoreType.DMA])(shard)


N_CHIP = 4


def _swap_with_sibling(name, blocks, n_split):
    _, R, W = blocks.shape
    N = N_CHIP
    rq = R // n_split

    def body(x_ref, got_ref, send_sems, recv_sems):
        x, y, c = lax.axis_index("x"), lax.axis_index("y"), lax.axis_index("c")
        copies = []
        for j in range(N):
            for q in range(n_split):
                rows = pl.ds(q * rq, rq)
                cp = pltpu.make_async_remote_copy(src_ref=x_ref.at[2 * j + 1 - c, rows], dst_ref=got_ref.at[j, rows],
                                                  send_sem=send_sems.at[j * n_split + q], recv_sem=recv_sems.at[j * n_split + q],
                                                  device_id=(x, y, 1 - c), device_id_type=MESH)
                cp.start()
                copies.append(cp)
        for cp in copies:
            cp.wait()

    hbm = pl.BlockSpec(memory_space=pltpu.HBM)
    return pl.pallas_call(body, name=name, in_specs=[hbm], out_specs=hbm,
                          out_shape=jax.ShapeDtypeStruct((N, R, W), blocks.dtype),
                          scratch_shapes=[pltpu.SemaphoreType.DMA((N * n_split,)), pltpu.SemaphoreType.DMA((N * n_split,))])(blocks)


def _add_own_blocks(name, blocks, got, tr, out_dtype):
    N, R, W = got.shape

    def body(c_ref, a_ref, b_ref, o_ref):
        o_ref[...] = (a_ref[...] + b_ref[...]).astype(o_ref.dtype)

    grid_spec = pltpu.PrefetchScalarGridSpec(
        num_scalar_prefetch=1, grid=(N, R // tr),
        in_specs=[pl.BlockSpec((None, tr, W), lambda n, i, c: (2 * n + c[0], i, 0)),
                  pl.BlockSpec((None, tr, W), lambda n, i, c: (n, i, 0))],
        out_specs=pl.BlockSpec((None, tr, W), lambda n, i, c: (n, i, 0)))
    core = lax.axis_index("c").astype(jnp.int32).reshape(1)
    return pl.pallas_call(body, name=name, grid_spec=grid_spec, out_shape=jax.ShapeDtypeStruct(got.shape, out_dtype),
                          compiler_params=_cparams("parallel", "parallel"))(core, blocks, got)


def _exchange_chips(name, blocks):
    def body(z_ref, o_ref, send_sems, recv_sems, local_sem):
        x, y, c = lax.axis_index("x"), lax.axis_index("y"), lax.axis_index("c")
        chip = 2 * x + y
        mine = pltpu.make_async_copy(z_ref.at[chip], o_ref.at[chip], local_sem)
        mine.start()
        copies = []
        for k in range(1, N_CHIP):
            to = (x ^ (k >> 1), y ^ (k & 1), c)
            cp = pltpu.make_async_remote_copy(src_ref=z_ref.at[chip ^ k], dst_ref=o_ref.at[chip], send_sem=send_sems.at[k - 1],
                                              recv_sem=recv_sems.at[k - 1], device_id=to, device_id_type=MESH)
            cp.start()
            copies.append(cp)
        for k in range(1, N_CHIP):
            pltpu.make_async_remote_copy(src_ref=z_ref.at[chip], dst_ref=o_ref.at[chip ^ k], send_sem=send_sems.at[k - 1],
                                         recv_sem=recv_sems.at[k - 1], device_id=(x ^ (k >> 1), y ^ (k & 1), c),
                                         device_id_type=MESH).wait_recv()
        for cp in copies:
            cp.wait_send()
        mine.wait()

    hbm = pl.BlockSpec(memory_space=pltpu.HBM)
    return pl.pallas_call(body, name=name, in_specs=[hbm], out_specs=hbm,
                          out_shape=jax.ShapeDtypeStruct(blocks.shape, blocks.dtype),
                          scratch_shapes=[pltpu.SemaphoreType.DMA((N_CHIP - 1,)), pltpu.SemaphoreType.DMA((N_CHIP - 1,)),
                                          pltpu.SemaphoreType.DMA])(blocks)


def _adamw(name, parts, w, m, v, tr):
    R, W = w.shape
    n_parts = parts.shape[0]

    def body(p_ref, w_ref, m_ref, v_ref, g_ref, d_ref, nm_ref, nv_ref):
        g = p_ref[0].astype(F32)
        for s in range(1, n_parts):
            g = g + p_ref[s].astype(F32)
        nm = ADAM_B1 * m_ref[...] + (1.0 - ADAM_B1) * g
        nv = ADAM_B2 * v_ref[...] + (1.0 - ADAM_B2) * (g * g)
        m_hat = nm / (1.0 - ADAM_B1 ** ADAM_STEP)
        v_hat = nv / (1.0 - ADAM_B2 ** ADAM_STEP)
        g_ref[...] = g
        d_ref[...] = -ADAM_LR * (m_hat / (jnp.sqrt(v_hat) + ADAM_EPS) + ADAM_WD * w_ref[...])
        nm_ref[...] = nm
        nv_ref[...] = nv

    spec = pl.BlockSpec((tr, W), lambda i: (i, 0))
    return pl.pallas_call(body, name=name, grid=(R // tr,),
                          in_specs=[pl.BlockSpec((n_parts, tr, W), lambda i: (0, i, 0)), spec, spec, spec],
                          out_specs=[spec] * 4, out_shape=[jax.ShapeDtypeStruct((R, W), F32)] * 4,
                          compiler_params=_cparams("parallel"))(parts, w, m, v)


def _rms(x, g):
    return x * lax.rsqrt(jnp.mean(x * x, axis=-1, keepdims=True) + RMS_EPS) * g


def _softplus(z):
    return jnp.maximum(z, 0.0) + jnp.log(1.0 + jnp.exp(-jnp.abs(z)))


def _neg_expm1(z):
    series = -z * (1.0 + z * (0.5 + z * (1.0 / 6.0)))
    return jnp.where(z > -1e-3, series, 1.0 - jnp.exp(z))


def _gelu_tanh(x):
    return 0.5 * x * (1.0 + jnp.tanh(0.7978845608028654 * (x + 0.044715 * x * x * x)))


def _bdot(x, w):
    return jnp.dot(x.astype(BF16), w.astype(BF16), preferred_element_type=F32)


def _seg_consts(C):
    c0 = lax.broadcasted_iota(jnp.int32, (C, LANES), 0)
    h1 = lax.broadcasted_iota(jnp.int32, (C, LANES), 1)
    red = ((c0 >> 6) == h1).astype(BF16)
    h0 = lax.broadcasted_iota(jnp.int32, (LANES, C), 0)
    c1 = lax.broadcasted_iota(jnp.int32, (LANES, C), 1)
    exp = ((c1 >> 6) == h0).astype(BF16)
    return red, exp


def _split_dot(x, m):
    hi = x.astype(BF16)
    lo = (x - hi.astype(F32)).astype(BF16)
    return jnp.dot(hi, m, preferred_element_type=F32) + jnp.dot(lo, m, preferred_element_type=F32)


def _segsum_raw(x):
    red, exp = _seg_consts(x.shape[1])
    return _split_dot(_split_dot(x, red), exp)


@jax.custom_vjp
def _segsum(x):
    return _segsum_raw(x)


_segsum.defvjp(lambda x: (_segsum_raw(x), None), lambda _, g: (_segsum_raw(g),))


def _f_pre(out_dtype):
    def fn(pos, x, g):
        return (_rms(x, g).astype(out_dtype),)
    return fn


def _f_post(pos, x, t, g):
    return (x + _rms(t, g),)


def _f_post_pre(out_dtype, has_bias):
    def fn(pos, x, t, g_post, g_pre, *bias):
        x1 = x + _rms(t + bias[0] if has_bias else t, g_post)
        return x1, _rms(x1, g_pre).astype(out_dtype)
    return fn


def _f_conv(pos, proj, p1, p2, p3, b_u, w0, w1, w2, w3, cb):
    D = b_u.shape[1]
    u0 = proj[:, D:] + b_u
    u1 = jnp.where(pos >= 1, p1[:, D:] + b_u, 0.0)
    u2 = jnp.where(pos >= 2, p2[:, D:] + b_u, 0.0)
    u3 = jnp.where(pos >= 3, p3[:, D:] + b_u, 0.0)
    return (cb + u3 * w0 + u2 * w1 + u1 * w2 + u0 * w3,)


def _f_conv_bwd(pos, dconv, proj, dyb, n1, n2, n3, p1, p2, p3, b_u, w0, w1, w2, w3):
    D = b_u.shape[1]
    du = dconv * w3 + n1 * w2 + n2 * w1 + n3 * w0
    dproj = jnp.concatenate([dyb[:, :D], du], axis=1)
    u0 = proj[:, D:] + b_u
    u1 = jnp.where(pos >= 1, p1[:, D:] + b_u, 0.0)
    u2 = jnp.where(pos >= 2, p2[:, D:] + b_u, 0.0)
    u3 = jnp.where(pos >= 3, p3[:, D:] + b_u, 0.0)
    rs = lambda z: jnp.sum(z, axis=0, keepdims=True)
    return (dproj, rs(dproj), rs(dconv * u3), rs(dconv * u2), rs(dconv * u1), rs(dconv * u0), rs(dconv))


def _f_gates(pos, conv, proj, gate_w, gb_r, gb_i, lam, b_y):
    D = conv.shape[1]
    blk = D // LRU_HEADS
    cb = conv.astype(BF16)

    def gate(g, bias):
        z = [jnp.dot(cb[:, h * blk:(h + 1) * blk], gate_w[(g * LRU_HEADS + h) * blk:(g * LRU_HEADS + h + 1) * blk, :].astype(BF16),
                     preferred_element_type=F32) for h in range(LRU_HEADS)]
        return jax.nn.sigmoid(jnp.concatenate(z, axis=1) + bias)

    r_gate, i_gate = gate(0, gb_r), gate(1, gb_i)
    log_a = -LRU_C * r_gate * _softplus(-lam)
    a = jnp.exp(log_a)
    b = jnp.sqrt(_neg_expm1(2.0 * log_a)) * i_gate * conv
    yb = _gelu_tanh(proj[:, :D] + b_y)
    return a, b, yb


def _f_hy(pos, h, yb):
    return ((h * yb).astype(BF16),)


def _f_mix(pos, hn, hp, mu_r, mu_w, mu_k, mu_v, mu_a, mu_g, w0, w1, w2, a0, a1, a2, g1, g2):
    xx = hp - hn
    xr, xw, xk, xv, xa, xg = (hn + xx * m for m in (mu_r, mu_w, mu_k, mu_v, mu_a, mu_g))
    lw = w0 + _bdot(jnp.tanh(_bdot(xw, w1)), w2)
    decay = jnp.exp(-jnp.exp(-_softplus(-lw) - 0.5))
    a = jax.nn.sigmoid(a0 + _bdot(_bdot(xa, a1), a2))
    g = _bdot(jax.nn.sigmoid(_bdot(xg, g1)), g2)
    return xr.astype(BF16), xk.astype(BF16), xv.astype(BF16), decay, a, g


def _f_kk(pos, k, a, k_k, k_a):
    kk = k * k_k
    kk = kk / jnp.maximum(jnp.sqrt(_segsum(kk * kk)), 1e-12)
    return k * (1.0 + (a - 1.0) * k_a), -kk, kk * a


def _f_gn(pos, y, r, k2, v, g, gn_g, gn_b, r_k):
    inv_n = 1.0 / RWKV_N
    yc = y - _segsum(y) * inv_n
    var = _segsum(yc * yc) * inv_n
    yn = yc * lax.rsqrt(var + GN_EPS) * gn_g + gn_b
    bonus = _segsum(r * k2 * r_k) * v
    return (((yn + bonus) * g).astype(BF16),)


def _f_attn(pos, q, kv):
    D = q.shape[1]
    hd = D // XATTN_HEADS
    qb, kb, vb = q.astype(BF16), kv[:, :D].astype(BF16), kv[:, D:].astype(BF16)
    outs = []
    for h in range(XATTN_HEADS):
        sl = slice(h * hd, (h + 1) * hd)
        s = lax.dot_general(qb[:, sl], kb[:, sl], (((1,), (1,)), ((), ())), preferred_element_type=F32) * (hd ** -0.5)
        s = s - jnp.max(s, axis=-1, keepdims=True)
        e = jnp.exp(s)
        p = e / jnp.sum(e, axis=-1, keepdims=True)
        outs.append(jnp.dot(p.astype(BF16), vb[:, sl], preferred_element_type=F32))
    return (jnp.concatenate(outs, axis=1).astype(BF16),)


def _f_loss(pos, y, target):
    d = y - target
    inv = 1.0 / y.shape[1]
    part = 0.5 * inv * jnp.sum(jnp.sum(d * d, axis=1, keepdims=True), axis=0, keepdims=True)
    return d * inv, jnp.broadcast_to(part, (8, LANES))


def _full_from_blocks(blocks, shard_shape, ax):
    g = jnp.moveaxis(blocks.reshape((N_DEV,) + tuple(shard_shape)), 0, ax)
    return g.reshape(tuple(shard_shape[:ax]) + (N_DEV * shard_shape[ax],) + tuple(shard_shape[ax + 1:]))


def _blocks_from_full(full, shard_shape, ax):
    g = full.reshape(tuple(shard_shape[:ax]) + (N_DEV, shard_shape[ax]) + tuple(shard_shape[ax + 1:]))
    return jnp.moveaxis(g, ax, 0).reshape(N_DEV, -1)


def _numel(shape):
    n = 1
    for s in shape:
        n *= s
    return n


def _pack_flat(arrs, total):
    flat = jnp.concatenate([a.reshape(-1).astype(F32) for a in arrs])
    return jnp.pad(flat, (0, total - flat.shape[0]))


def _train_step(x3, mem3, target3, P, M_, V_):
    Bl, S, D = x3.shape
    T = Bl * S
    ML = mem3.shape[1]
    x = x3.reshape(T, D)
    mem = mem3.reshape(Bl * ML, D)
    target = target3.reshape(T, D)
    shp = {n: P[n].shape for n in WEIGHTS}

    n_big = sum(_numel(shp[n]) for n in BIG)
    n_small = sum(_numel(shp[n]) for n in SMALL)
    n_small_pad = -(-n_small // (PACK_ROWS * PACK_W)) * (PACK_ROWS * PACK_W)
    big_rows, small_rows = n_big // PACK_W, n_small_pad // PACK_W
    def pack_rows(src):
        big = [src[n].reshape(-1, PACK_W) for n in BIG]
        return jnp.concatenate(big + [_pack_flat([src[n] for n in SMALL], n_small_pad).reshape(small_rows, PACK_W)], axis=0)

    big_pack = jnp.concatenate([P[n].reshape(-1, PACK_W) for n in BIG], axis=0).astype(BF16)
    small_pack = _pack_flat([P[n] for n in SMALL], n_small_pad).reshape(small_rows, PACK_W)
    big_all = _all_gather("gather_matrices", big_pack)
    small_all = _all_gather("gather_vectors", small_pack).reshape(N_DEV, n_small_pad)
    Wf = {}
    big_row0 = {}
    off = 0
    for n in BIG:
        nr = _numel(shp[n]) // PACK_W
        big_row0[n] = off
        Wf[n] = _full_from_blocks(big_all[:, off:off + nr], shp[n], SHARD_AXIS[n])
        off += nr
    off = 0
    for n in SMALL:
        ne = _numel(shp[n])
        Wf[n] = _full_from_blocks(small_all[:, off:off + ne], shp[n], SHARD_AXIS[n])
        off += ne
    for n in REPL:
        Wf[n] = P[n]

    row = lambda v: v.reshape(1, -1).astype(F32)
    gains = [[row(Wf['ln_gains'][i, j]) for j in range(6)] for i in range(2)]
    LW = 128

    def pad_cols(w):
        return jnp.pad(w, ((0, 0), (0, LW - w.shape[1]))).astype(F32)

    def pad_rows(w):
        return jnp.pad(w, ((0, LW - w.shape[0]), (0, 0))).astype(F32)

    G = {}
    saved = {}

    mem_n, = _rw("mem_norm_fwd", _f_pre(BF16), [mem], [row(Wf['mem_norm'])], outs=[(D, BF16)])

    def post_pre(name, xin, t, g_post, g_pre, out_dtype, bias=()):
        return _rw(name, _f_post_pre(out_dtype, bool(bias)), [xin, t], [g_post, g_pre, *bias],
                   outs=[(D, F32), (D, out_dtype)], seq=S)

    def post_pre_bwd(name, xin, t, g_post, g_pre, dx_next, dhn, bias=(), ct_adds=()):
        return _rw_vjp(name, _f_post_pre(F32, bool(bias)), [xin, t], [g_post, g_pre, *bias], [dx_next, dhn], 2,
                       d_rows=[(0, F32), (1, BF16)], d_params=list(range(2 + len(bias))), ct_adds=ct_adds, seq=S)

    def xattn_fwd(i, hq):
        q = _mm_nn(f"xattn{i}_q", hq, Wf['c_w_q'][i], BF16)
        kv = _mm_nn(f"xattn{i}_kv", mem_n, Wf['c_w_kv'][i], F32)
        kv = kv.reshape(Bl, ML, 2 * D)
        o, = _rw(f"xattn{i}_attn", _f_attn, [q], [kv], outs=[(D, BF16)], seq=S, per_b=(0,))
        saved[f"xattn{i}"] = (hq, q, kv, o)
        return _mm_nn(f"xattn{i}_o", o, Wf['c_w_o'][i], F32)

    def mlp_fwd(i, hm):
        up, act = _mm_up_act(f"mlp{i}_up", hm, Wf['m_w_up'][i])
        saved[f"mlp{i}"] = (hm, up, act)
        return _mm_nn(f"mlp{i}_down", act, Wf['m_w_down'][i], F32)

    b_in = row(Wf['a_b_in'][0])
    b_y, b_u = b_in[:, :D], b_in[:, D:]
    cw = [row(Wf['a_conv_w'][0, t]) for t in range(4)]
    cb = row(Wf['a_conv_b'][0])
    gate_w = Wf['a_gate_w'][0].reshape(2 * LRU_HEADS * (D // LRU_HEADS), D // LRU_HEADS).astype(F32)
    gate_b = Wf['a_gate_b'][0].reshape(2, D).astype(F32)
    gb_r, gb_i = gate_b[0:1], gate_b[1:2]
    lam = row(Wf['a_lambda'][0])
    b_out = row(Wf['a_b_out'][0])

    hn0, = _rw("lru_pre", _f_pre(BF16), [x], [gains[0][0]], outs=[(D, BF16)], seq=S)
    proj = _mm_nn("lru_in", hn0, Wf['a_w_in'][0], F32)
    prev3 = [(0, 1), (0, 2), (0, 3)]
    conv, = _rw("lru_conv", _f_conv, [proj], [b_u] + cw + [cb], outs=[(D, F32)], shifts=prev3, seq=S)
    gate_par = [gate_w, gb_r, gb_i, lam, b_y]
    a_l, b_l, yb = _rw("lru_gates", _f_gates, [conv, proj], gate_par, outs=[(D, F32)] * 3, seq=S)
    h_l = _lru_scan("lru_scan", a_l, b_l, S)
    hy, = _rw("lru_hy", _f_hy, [h_l, yb], outs=[(D, BF16)], seq=S)
    t0 = _mm_nn("lru_out", hy, Wf['a_w_out'][0], F32)
    x1, hq0 = post_pre("lru_post", x, t0, gains[0][1], gains[0][2], BF16, bias=(b_out,))
    c0 = xattn_fwd(0, hq0)
    x2, hm0 = post_pre("xattn0_post", x1, c0, gains[0][3], gains[0][4], BF16)
    m0 = mlp_fwd(0, hm0)
    x3_, hn1 = post_pre("mlp0_post", x2, m0, gains[0][5], gains[1][0], F32)

    mu = [row(Wf['b_mu'][0, j]) for j in range(6)]
    lora = [row(Wf['b_w0'][0]), pad_cols(Wf['b_w1'][0]), pad_rows(Wf['b_w2'][0]),
            row(Wf['b_a0'][0]), pad_cols(Wf['b_a1'][0]), pad_rows(Wf['b_a2'][0]),
            Wf['b_g1'][0].astype(F32), Wf['b_g2'][0].astype(F32)]
    k_k, k_a = row(Wf['b_k_k'][0]), row(Wf['b_k_a'][0])
    gn_g, gn_b, r_k = row(Wf['b_gn_g'][0]), row(Wf['b_gn_b'][0]), row(Wf['b_r_k'][0])

    xr, xk, xv, decay, a_r, g_r = _rw("rwkv_mix", _f_mix, [hn1], mu + lora,
                                      outs=[(D, BF16)] * 3 + [(D, F32)] * 3, shifts=[(0, 1)], seq=S)
    r_ = _mm_nn("rwkv_r", xr, Wf['b_w_rkv'][0, 0], F32)
    k_ = _mm_nn("rwkv_k", xk, Wf['b_w_rkv'][0, 1], F32)
    v_ = _mm_nn("rwkv_v", xv, Wf['b_w_rkv'][0, 2], F32)
    k2, ra, rb = _rw("rwkv_kk", _f_kk, [k_, a_r], [k_k, k_a], outs=[(D, F32)] * 3, seq=S)
    y_r, states, sa_r, last_r = _rwkv_fwd("rwkv_scan", r_, decay, k2, v_, ra, rb, S)
    gn_par = [gn_g, gn_b, r_k]
    og, = _rw("rwkv_gn", _f_gn, [y_r, r_, k2, v_, g_r], gn_par, outs=[(D, BF16)], seq=S)
    t1 = _mm_nn("rwkv_out", og, Wf['b_w_o'][0], F32)
    x4, hq1 = post_pre("rwkv_post", x3_, t1, gains[1][1], gains[1][2], BF16)
    c1 = xattn_fwd(1, hq1)
    x5, hm1 = post_pre("xattn1_post", x4, c1, gains[1][3], gains[1][4], BF16)
    m1 = mlp_fwd(1, hm1)
    x6, = _rw("mlp1_post", _f_post, [x5, m1], [gains[1][5]], outs=[(D, F32)], seq=S)

    dx, loss_acc = _rw("loss", _f_loss, [x6, target], outs=[(D, F32)], accs=[(8, LANES)], seq=S)
    loss = lax.psum(loss_acc[0, 0], ("x", "y", "c"))

    dgain = [[None] * 6 for _ in range(2)]
    dmem_n = None

    def mlp_bwd(i, dm):
        hm, up, act = saved[f"mlp{i}"]
        G_down = _mm_tn(f"mlp{i}_down_dw", act, dm)
        dup = _mm_nt_act_bwd(f"mlp{i}_down_dx", dm, Wf['m_w_down'][i], up)
        G_up = _mm_tn(f"mlp{i}_up_dw", hm, dup)
        return _mm_nt(f"mlp{i}_up_dx", dup, Wf['m_w_up'][i], F32), G_up, G_down

    def xattn_bwd(i, dc):
        hq, q, kv, o = saved[f"xattn{i}"]
        G_o = _mm_tn(f"xattn{i}_o_dw", o, dc)
        do = _mm_nt(f"xattn{i}_o_dx", dc, Wf['c_w_o'][i], F32)
        dq, dkv = _rw_vjp(f"xattn{i}_attn_bwd", _f_attn, [q], [kv], [do], 1, d_rows=[(0, BF16)],
                          d_params=[0], seq=S, per_b=(0,))
        dkv = dkv.reshape(Bl * ML, 2 * D)
        G_kv = _mm_tn(f"xattn{i}_kv_dw", mem_n, dkv)
        dmn = _mm_nt(f"xattn{i}_kv_dx", dkv, Wf['c_w_kv'][i], F32)
        G_q = _mm_tn(f"xattn{i}_q_dw", hq, dq)
        dhq = _mm_nt(f"xattn{i}_q_dx", dq, Wf['c_w_q'][i], F32)
        return dhq, G_q, G_kv, G_o, dmn

    G_up, G_down, G_q, G_kv, G_o = [None] * 2, [None] * 2, [None] * 2, [None] * 2, [None] * 2
    dm1, dgain[1][5] = _rw_vjp("mlp1_post_bwd", _f_post, [x5, m1], [gains[1][5]], [dx], 1,
                               d_rows=[(1, BF16)], d_params=[0], seq=S)
    dhm1, G_up[1], G_down[1] = mlp_bwd(1, dm1)
    dx, dc1, dgain[1][3], dgain[1][4] = post_pre_bwd("xattn1_post_bwd", x4, c1, gains[1][3], gains[1][4], dx, dhm1)
    dhq1, G_q[1], G_kv[1], G_o[1], dmn1 = xattn_bwd(1, dc1)

    dx, dt1, dgain[1][1], dgain[1][2] = post_pre_bwd("rwkv_post_bwd", x3_, t1, gains[1][1], gains[1][2], dx, dhq1)
    G['b_w_o'] = _mm_tn("rwkv_out_dw", og, dt1)[None]
    dog = _mm_nt("rwkv_out_dx", dt1, Wf['b_w_o'][0], F32)
    dy_r, dr_p, dk2_p, dv_p, dg_r, d_gn_g, d_gn_b, d_r_k = _rw_vjp(
        "rwkv_gn_bwd", _f_gn, [y_r, r_, k2, v_, g_r], gn_par, [dog], 1,
        d_rows=[(j, F32) for j in range(5)], d_params=[0, 1, 2], seq=S)
    dr_, ddecay, dk2, dv_, dra, drb = _rwkv_bwd("rwkv_scan_bwd", r_, decay, k2, v_, ra, rb, states, sa_r, last_r, dy_r,
                                                dr_p, dk2_p, dv_p, S)
    dk_, da_r, d_k_k, d_k_a = _rw_vjp("rwkv_kk_bwd", _f_kk, [k_, a_r], [k_k, k_a], [dk2, dra, drb], 3,
                                      d_rows=[(0, F32), (1, F32)], d_params=[0, 1], seq=S)
    G_rkv = [_mm_tn("rwkv_r_dw", xr, dr_), _mm_tn("rwkv_k_dw", xk, dk_), _mm_tn("rwkv_v_dw", xv, dv_)]
    G['b_w_rkv'] = jnp.stack(G_rkv)[None]
    dxr = _mm_nt("rwkv_r_dx", dr_, Wf['b_w_rkv'][0, 0], F32)
    dxk = _mm_nt("rwkv_k_dx", dk_, Wf['b_w_rkv'][0, 1], F32)
    dxv = _mm_nt("rwkv_v_dx", dv_, Wf['b_w_rkv'][0, 2], F32)
    mix_out = _rw_vjp("rwkv_mix_bwd", _f_mix, [hn1], mu + lora, [dxr, dxk, dxv, ddecay, da_r, dg_r], 6,
                      d_rows=[(0, F32)], d_shifts=[(0, F32)], d_params=list(range(14)), shifts=[(0, 1)], seq=S)
    dhn1, dhp1 = mix_out[0], mix_out[1]
    d_mu = mix_out[2:8]
    d_lora = mix_out[8:16]
    dx, dm0, dgain[0][5], dgain[1][0] = post_pre_bwd("mlp0_post_bwd", x2, m0, gains[0][5], gains[1][0], dx, dhn1,
                                                     ct_adds=[(1, dhp1, -1)])
    G['b_mu'] = jnp.concatenate(d_mu, axis=0)[None]
    G['b_w0'], G['b_a0'] = d_lora[0], d_lora[3]
    G['b_w1'] = d_lora[1][:, :shp['b_w1'][2]][None]
    G['b_w2'] = d_lora[2][:shp['b_w2'][1]][None]
    G['b_a1'] = d_lora[4][:, :shp['b_a1'][2]][None]
    G['b_a2'] = d_lora[5][:shp['b_a2'][1]][None]
    G['b_g1'], G['b_g2'] = d_lora[6][None], d_lora[7][None]
    G['b_k_k'], G['b_k_a'], G['b_gn_g'], G['b_gn_b'] = d_k_k, d_k_a, d_gn_g, d_gn_b
    G['b_r_k'] = d_r_k.reshape(P['b_r_k'].shape)

    dhm0, G_up[0], G_down[0] = mlp_bwd(0, dm0)
    dx, dc0, dgain[0][3], dgain[0][4] = post_pre_bwd("xattn0_post_bwd", x1, c0, gains[0][3], gains[0][4], dx, dhm0)
    dhq0, G_q[0], G_kv[0], G_o[0], dmn0 = xattn_bwd(0, dc0)
    G['m_w_up'], G['m_w_down'] = jnp.stack(G_up), jnp.stack(G_down)
    G['c_w_q'], G['c_w_kv'], G['c_w_o'] = jnp.stack(G_q), jnp.stack(G_kv), jnp.stack(G_o)

    dx, dt0, dgain[0][1], dgain[0][2], G['a_b_out'] = post_pre_bwd("lru_post_bwd", x, t0, gains[0][1], gains[0][2],
                                                                  dx, dhq0, bias=(b_out,))
    G['a_w_out'] = _mm_tn("lru_out_dw", hy, dt0)[None]
    dhy = _mm_nt("lru_out_dx", dt0, Wf['a_w_out'][0], F32)
    dh_l, dyb = _rw_vjp("lru_hy_bwd", _f_hy, [h_l, yb], [], [dhy], 1, d_rows=[(0, F32), (1, F32)], seq=S)
    da_l, db_l = _lru_scan_bwd("lru_scan_bwd", a_l, h_l, dh_l, S)
    dconv, dproj_y, d_gate_w, d_gb_r, d_gb_i, G['a_lambda'] = _rw_vjp(
        "lru_gates_bwd", _f_gates, [conv, proj], gate_par, [da_l, db_l, dyb], 3,
        d_rows=[(0, F32), (1, F32)], d_params=[0, 1, 2, 3], seq=S)
    next3 = [(0, -1), (0, -2), (0, -3)]
    dproj, G['a_b_in'], dw0, dw1, dw2, dw3, G['a_conv_b'] = _rw(
        "lru_conv_bwd", _f_conv_bwd, [dconv, proj, dproj_y], [b_u] + cw, outs=[(2 * D, BF16)],
        accs=[(1, 2 * D)] + [(1, D)] * 5, shifts=next3 + [(1, 1), (1, 2), (1, 3)], seq=S)
    G['a_conv_w'] = jnp.concatenate([dw0, dw1, dw2, dw3], axis=0)[None]
    G['a_gate_w'] = d_gate_w.reshape(P['a_gate_w'].shape[:3] + (D // LRU_HEADS, D // LRU_HEADS))
    G['a_gate_b'] = jnp.concatenate([d_gb_r, d_gb_i], axis=0).reshape(1, 2, LRU_HEADS, D // LRU_HEADS)
    G['a_w_in'] = _mm_tn("lru_in_dw", hn0, dproj)[None]
    dhn0 = _mm_nt("lru_in_dx", dproj, Wf['a_w_in'][0], F32)
    grad_x, dgain[0][0] = _rw_vjp("lru_pre_bwd", _f_pre(F32), [x], [gains[0][0]], [dhn0], 1, d_rows=[(0, F32)],
                                  d_params=[0], adds=[(0, dx, 0)], seq=S)
    d_mem_norm, = _rw_vjp("mem_norm_bwd", _f_pre(F32), [mem], [row(Wf['mem_norm'])], [dmn0], 1, d_params=[0],
                          ct_adds=[(0, dmn1, 0)])
    G['mem_norm'] = d_mem_norm.reshape(-1)
    G['ln_gains'] = jnp.stack([jnp.concatenate(dgain[i], axis=0) for i in range(2)])

    sharded = BIG + SMALL
    rows_s = big_rows + small_rows
    tr = max(t for t in range(BF16_ROWS, 161, BF16_ROWS) if rows_s % t == 0)

    pieces = [_blocks_from_full(G[n], shp[n], SHARD_AXIS[n]).reshape(N_DEV, -1, PACK_W) for n in BIG]
    small = [_blocks_from_full(G[n], shp[n], SHARD_AXIS[n]) for n in SMALL] + [jnp.zeros((N_DEV, n_small_pad - n_small), F32)]
    pieces.append(jnp.concatenate(small, axis=1).reshape(N_DEV, small_rows, PACK_W))
    g_blocks = jnp.concatenate(pieces, axis=1)
    n_split = max(q for q in range(1, 10) if rows_s % (q * BF16_ROWS) == 0)
    got = _swap_with_sibling("exchange_grads_sibling", g_blocks, n_split)
    parts_s = _exchange_chips("exchange_grads_chips", _add_own_blocks("sum_with_sibling", g_blocks, got, tr, BF16))
    n_repl = sum(_numel(shp[n]) for n in REPL)
    n_repl_pad = -(-n_repl // (8 * PACK_W)) * (8 * PACK_W)
    rows_r = n_repl_pad // PACK_W
    parts_r = _all_gather("gather_replicated_grads", _pack_flat([G[n] for n in REPL], n_repl_pad).reshape(rows_r, PACK_W))

    def pack(src, names, total, rows_):
        return _pack_flat([src[n] for n in names], total).reshape(rows_, PACK_W)

    out_s = _adamw("adamw_sharded", parts_s, pack_rows(P), pack_rows(M_), pack_rows(V_), tr)
    out_r = _adamw("adamw_replicated", parts_r, pack(P, REPL, n_repl_pad, rows_r), pack(M_, REPL, n_repl_pad, rows_r),
                   pack(V_, REPL, n_repl_pad, rows_r), rows_r)

    def unpack(packed, names):
        flat = packed.reshape(-1)
        res, o = {}, 0
        for n in names:
            ne = _numel(shp[n])
            res[n] = flat[o:o + ne].reshape(shp[n])
            o += ne
        return res

    results = []
    for j in range(4):
        both = {n: out_s[j][big_row0[n]:big_row0[n] + _numel(shp[n]) // PACK_W].reshape(shp[n]) for n in BIG}
        both.update(unpack(out_s[j][big_rows:], SMALL))
        both.update(unpack(out_r[j], REPL))
        results += [both[n] for n in WEIGHTS]
    return (loss, grad_x.reshape(Bl, S, D), *results)


def kernel(x, mem, ln_gains, mem_norm, a_conv_w, a_conv_b, a_w_in, a_b_in, a_gate_w, a_gate_b, a_lambda, a_w_out, a_b_out, b_mu, b_w_rkv, b_w0, b_w1, b_w2, b_a0, b_a1, b_a2, b_g1, b_g2, b_k_k, b_k_a, b_r_k, b_gn_g, b_gn_b, b_w_o, c_w_q, c_w_kv, c_w_o, m_w_up, m_w_down, loss_target, m_ln_gains, m_mem_norm, m_a_conv_w, m_a_conv_b, m_a_w_in, m_a_b_in, m_a_gate_w, m_a_gate_b, m_a_lambda, m_a_w_out, m_a_b_out, m_b_mu, m_b_w_rkv, m_b_w0, m_b_w1, m_b_w2, m_b_a0, m_b_a1, m_b_a2, m_b_g1, m_b_g2, m_b_k_k, m_b_k_a, m_b_r_k, m_b_gn_g, m_b_gn_b, m_b_w_o, m_c_w_q, m_c_w_kv, m_c_w_o, m_m_w_up, m_m_w_down, v_ln_gains, v_mem_norm, v_a_conv_w, v_a_conv_b, v_a_w_in, v_a_b_in, v_a_gate_w, v_a_gate_b, v_a_lambda, v_a_w_out, v_a_b_out, v_b_mu, v_b_w_rkv, v_b_w0, v_b_w1, v_b_w2, v_b_a0, v_b_a1, v_b_a2, v_b_g1, v_b_g2, v_b_k_k, v_b_k_a, v_b_r_k, v_b_gn_g, v_b_gn_b, v_b_w_o, v_c_w_q, v_c_w_kv, v_c_w_o, v_m_w_up, v_m_w_down):
    weights = (ln_gains, mem_norm, a_conv_w, a_conv_b, a_w_in, a_b_in, a_gate_w, a_gate_b, a_lambda, a_w_out, a_b_out, b_mu, b_w_rkv, b_w0, b_w1, b_w2, b_a0, b_a1, b_a2, b_g1, b_g2, b_k_k, b_k_a, b_r_k, b_gn_g, b_gn_b, b_w_o, c_w_q, c_w_kv, c_w_o, m_w_up, m_w_down)
    moments1 = (m_ln_gains, m_mem_norm, m_a_conv_w, m_a_conv_b, m_a_w_in, m_a_b_in, m_a_gate_w, m_a_gate_b, m_a_lambda, m_a_w_out, m_a_b_out, m_b_mu, m_b_w_rkv, m_b_w0, m_b_w1, m_b_w2, m_b_a0, m_b_a1, m_b_a2, m_b_g1, m_b_g2, m_b_k_k, m_b_k_a, m_b_r_k, m_b_gn_g, m_b_gn_b, m_b_w_o, m_c_w_q, m_c_w_kv, m_c_w_o, m_m_w_up, m_m_w_down)
    moments2 = (v_ln_gains, v_mem_norm, v_a_conv_w, v_a_conv_b, v_a_w_in, v_a_b_in, v_a_gate_w, v_a_gate_b, v_a_lambda, v_a_w_out, v_a_b_out, v_b_mu, v_b_w_rkv, v_b_w0, v_b_w1, v_b_w2, v_b_a0, v_b_a1, v_b_a2, v_b_g1, v_b_g2, v_b_k_k, v_b_k_a, v_b_r_k, v_b_gn_g, v_b_gn_b, v_b_w_o, v_c_w_q, v_c_w_kv, v_c_w_o, v_m_w_up, v_m_w_down)
    return _train_step(x, mem, loss_target, dict(zip(WEIGHTS, weights)), dict(zip(WEIGHTS, moments1)),
                       dict(zip(WEIGHTS, moments2)))
```

```python
import jax
import jax.numpy as jnp
from jax import lax
from jax.experimental import pallas as pl
from jax.experimental.pallas import tpu as pltpu

F32 = jnp.float32
BF16 = jnp.bfloat16
N_DEV = 8
MESH = pl.DeviceIdType.MESH

V7X_VMEM_LIMIT_BYTES = 56 * 1024 * 1024
SUB = 8
HALO_ROWS = SUB
LANES = 128
RWKV_N = 64
HEAD_SHIFT = RWKV_N.bit_length() - 1
RMS_EPS = 1e-6
GN_EPS = 64e-5
LRU_C = 8.0
XATTN_HEADS = 4
LRU_HEADS = 4

ADAM_LR, ADAM_B1, ADAM_B2, ADAM_EPS, ADAM_WD, ADAM_STEP = 0.001, 0.9, 0.999, 1e-8, 0.01, 10

WEIGHTS = ['ln_gains', 'mem_norm', 'a_conv_w', 'a_conv_b', 'a_w_in', 'a_b_in', 'a_gate_w', 'a_gate_b',
           'a_lambda', 'a_w_out', 'a_b_out', 'b_mu', 'b_w_rkv', 'b_w0', 'b_w1', 'b_w2', 'b_a0', 'b_a1',
           'b_a2', 'b_g1', 'b_g2', 'b_k_k', 'b_k_a', 'b_r_k', 'b_gn_g', 'b_gn_b', 'b_w_o', 'c_w_q',
           'c_w_kv', 'c_w_o', 'm_w_up', 'm_w_down']
SHARD_AXIS = {'ln_gains': 2, 'mem_norm': None, 'a_conv_w': 2, 'a_conv_b': None, 'a_w_in': 2, 'a_b_in': None,
              'a_gate_w': 3, 'a_gate_b': 3, 'a_lambda': None, 'a_w_out': 1, 'a_b_out': None, 'b_mu': 2,
              'b_w_rkv': 2, 'b_w0': 1, 'b_w1': 1, 'b_w2': 2, 'b_a0': 1, 'b_a1': 1, 'b_a2': 2, 'b_g1': 1,
              'b_g2': 2, 'b_k_k': 1, 'b_k_a': 1, 'b_r_k': None, 'b_gn_g': 1, 'b_gn_b': 1, 'b_w_o': 1,
              'c_w_q': 1, 'c_w_kv': 2, 'c_w_o': 1, 'm_w_up': 2, 'm_w_down': 1}
BIG = ['a_w_in', 'a_gate_w', 'a_w_out', 'b_w_rkv', 'b_w1', 'b_w2', 'b_a1', 'b_a2', 'b_g1', 'b_g2', 'b_w_o',
       'c_w_q', 'c_w_kv', 'c_w_o', 'm_w_up', 'm_w_down']
SMALL = ['ln_gains', 'a_conv_w', 'a_gate_b', 'b_mu', 'b_w0', 'b_a0', 'b_k_k', 'b_k_a', 'b_gn_g', 'b_gn_b']
REPL = ['mem_norm', 'a_conv_b', 'a_b_in', 'a_lambda', 'a_b_out', 'b_r_k']
PACK_W = 1024
PACK_ROWS = 128
BF16_ROWS = 16


def _cparams(*sem):
    return pltpu.CompilerParams(dimension_semantics=sem, vmem_limit_bytes=V7X_VMEM_LIMIT_BYTES)


def _shift_tile(tile, halo, k, pos, seq, tm):
    if k > 0:
        ext = jnp.concatenate([halo, tile], axis=0)
        r = pltpu.roll(ext, k, 0)[HALO_ROWS:HALO_ROWS + tm]
        return jnp.where(pos >= k, r, 0.0)
    kk = -k
    ext = jnp.concatenate([tile, halo], axis=0)
    r = pltpu.roll(ext, tm + HALO_ROWS - kk, 0)[0:tm]
    return jnp.where(pos + kk < seq, r, 0.0)


def _row_specs(rows, shifts, params, per_b, seq, tm):
    T = rows[0].shape[0]
    ns = seq // tm
    specs = [pl.BlockSpec((tm, r.shape[1]), lambda b, i: (b * ns + i, 0)) for r in rows]
    halo_keys = []
    for idx, k in shifts:
        key = (idx, k > 0)
        if key not in halo_keys:
            halo_keys.append(key)
    per8 = tm // HALO_ROWS
    last8 = T // HALO_ROWS - 1
    for idx, prev in halo_keys:
        c = rows[idx].shape[1]
        if prev:
            specs.append(pl.BlockSpec((HALO_ROWS, c), lambda b, i: (jnp.maximum((b * ns + i) * per8 - 1, 0), 0)))
        else:
            specs.append(pl.BlockSpec((HALO_ROWS, c), lambda b, i: (jnp.minimum((b * ns + i + 1) * per8, last8), 0)))
    for j, p in enumerate(params):
        if j in per_b:
            specs.append(pl.BlockSpec((None,) + p.shape[1:], lambda b, i: (b, 0, 0)))
        else:
            specs.append(pl.BlockSpec(p.shape, lambda b, i, nd=p.ndim: (0,) * nd))
    return specs, halo_keys


def _load_tiles(refs, rows, shifts, halo_keys, params, seq, tm):
    nr, nh = len(rows), len(halo_keys)
    i = pl.program_id(1)
    pos = i * tm + lax.broadcasted_iota(jnp.int32, (tm, 1), 0)
    row_t = [r[...] for r in refs[:nr]]
    halo_t = {key: refs[nr + j][...] for j, key in enumerate(halo_keys)}
    sh_t = [_shift_tile(row_t[idx], halo_t[(idx, k > 0)], k, pos, seq, tm) for idx, k in shifts]
    par_t = [p[...] for p in refs[nr + nh:nr + nh + len(params)]]
    return pos, row_t, sh_t, par_t


ROW_TILES = (512, 256)
ROW_TILE_BUDGET_BYTES = 16 * 1024 * 1024


def _row_tile(seq, row_bytes):
    for tm in ROW_TILES:
        if 2 * tm * row_bytes <= ROW_TILE_BUDGET_BYTES:
            return min(tm, seq)
    return min(ROW_TILES[-1], seq)


def _rw(name, fn, rows, params=(), outs=(), accs=(), shifts=(), seq=None, per_b=()):
    rows, params = list(rows), list(params)
    T = rows[0].shape[0]
    seq = seq or T
    tm = _row_tile(seq, sum(r.shape[1] * r.dtype.itemsize for r in rows) + sum(c * jnp.dtype(dt).itemsize for c, dt in outs))
    nb, ns = T // seq, seq // tm
    in_specs, halo_keys = _row_specs(rows, shifts, params, per_b, seq, tm)
    n_in, no = len(in_specs), len(outs)

    def body(*refs):
        pos, row_t, sh_t, par_t = _load_tiles(refs, rows, shifts, halo_keys, params, seq, tm)
        res = fn(pos, *row_t, *sh_t, *par_t)
        out_refs = refs[n_in:n_in + no]
        acc_refs = refs[n_in + no:]
        for o_ref, v in zip(out_refs, res[:no]):
            o_ref[...] = v.astype(o_ref.dtype)
        if acc_refs:
            @pl.when((pl.program_id(0) == 0) & (pl.program_id(1) == 0))
            def _():
                for a_ref in acc_refs:
                    a_ref[...] = jnp.zeros_like(a_ref)
            for a_ref, v in zip(acc_refs, res[no:]):
                a_ref[...] += v

    out_shape = [jax.ShapeDtypeStruct((T, c), dt) for c, dt in outs]
    out_shape += [jax.ShapeDtypeStruct(s, F32) for s in accs]
    out_specs = [pl.BlockSpec((tm, c), lambda b, i: (b * ns + i, 0)) for c, _ in outs]
    out_specs += [pl.BlockSpec(s, lambda b, i: (0, 0)) for s in accs]
    return pl.pallas_call(body, name=name, grid=(nb, ns), in_specs=in_specs, out_specs=out_specs,
                          out_shape=out_shape, compiler_params=_cparams("arbitrary", "arbitrary"))(*rows, *[
                              rows[idx] for idx, _ in halo_keys], *params)


def _rw_vjp(name, fn, rows, params, cts, n_out, d_rows=(), d_shifts=(), d_params=(), adds=(), ct_adds=(),
            shifts=(), seq=None, per_b=()):
    rows, params, cts = list(rows), list(params), list(cts)
    T = rows[0].shape[0]
    seq = seq or T
    row_bytes = sum(r.shape[1] * r.dtype.itemsize for r in rows + cts + [a for _, a, _ in list(adds) + list(ct_adds)])
    row_bytes += sum(rows[j].shape[1] * jnp.dtype(dt).itemsize for j, dt in d_rows)
    row_bytes += sum(rows[shifts[j][0]].shape[1] * jnp.dtype(dt).itemsize for j, dt in d_shifts)
    tm = _row_tile(seq, row_bytes)
    nb, ns = T // seq, seq // tm
    n_ga = len(adds)
    adds = list(adds) + list(ct_adds)
    add_rows = [a for _, a, _ in adds]
    add_shifts = [(j, k) for j, (_, _, k) in enumerate(adds) if k != 0]
    all_rows = rows + cts + add_rows
    off_ct, off_add = len(rows), len(rows) + len(cts)
    all_shifts = list(shifts) + [(off_add + j, k) for j, k in add_shifts]
    in_specs, halo_keys = _row_specs(all_rows, all_shifts, params, per_b, seq, tm)
    n_in = len(in_specs)
    n_dr, n_ds = len(d_rows), len(d_shifts)

    def body(*refs):
        pos, row_t, sh_t, par_t = _load_tiles(refs, all_rows, all_shifts, halo_keys, params, seq, tm)
        fn_rows, ct_t, add_t = row_t[:off_ct], row_t[off_ct:off_add], row_t[off_add:]
        fn_sh, add_sh = sh_t[:len(shifts)], sh_t[len(shifts):]
        diff = [fn_rows[j] for j, _ in d_rows] + [fn_sh[j] for j, _ in d_shifts] + [par_t[j] for j in d_params]

        def f(*d):
            r2, s2, p2 = list(fn_rows), list(fn_sh), list(par_t)
            for (j, _), v in zip(d_rows, d[:n_dr]):
                r2[j] = v
            for (j, _), v in zip(d_shifts, d[n_dr:n_dr + n_ds]):
                s2[j] = v
            for j, v in zip(d_params, d[n_dr + n_ds:]):
                p2[j] = v
            return tuple(fn(pos, *r2, *s2, *p2)[:n_out])

        sh_iter = iter(add_sh)
        add_v = [add_t[j] if k == 0 else next(sh_iter) for j, (_, _, k) in enumerate(adds)]
        ct_t = list(ct_t)
        for (tgt, _, _), v in zip(adds[n_ga:], add_v[n_ga:]):
            ct_t[tgt] = ct_t[tgt] + v
        outs_v, vjp_fn = jax.vjp(f, *diff)
        grads = list(vjp_fn(tuple(c.astype(o.dtype) for c, o in zip(ct_t, outs_v))))
        for (tgt, _, _), v in zip(adds[:n_ga], add_v[:n_ga]):
            grads[tgt] = grads[tgt] + v
        out_refs = refs[n_in:]
        for o_ref, g in zip(out_refs[:n_dr + n_ds], grads[:n_dr + n_ds]):
            o_ref[...] = g.astype(o_ref.dtype)
        for o_ref, g, j in zip(out_refs[n_dr + n_ds:], grads[n_dr + n_ds:], d_params):
            first = (pl.program_id(1) == 0) if j in per_b else ((pl.program_id(0) == 0) & (pl.program_id(1) == 0))

            @pl.when(first)
            def _(o_ref=o_ref):
                o_ref[...] = jnp.zeros_like(o_ref)
            o_ref[...] += g.astype(F32)

    out_shape, out_specs = [], []
    for j, dt in d_rows:
        out_shape.append(jax.ShapeDtypeStruct(rows[j].shape, dt))
        out_specs.append(pl.BlockSpec((tm, rows[j].shape[1]), lambda b, i: (b * ns + i, 0)))
    for j, dt in d_shifts:
        src = rows[shifts[j][0]]
        out_shape.append(jax.ShapeDtypeStruct(src.shape, dt))
        out_specs.append(pl.BlockSpec((tm, src.shape[1]), lambda b, i: (b * ns + i, 0)))
    for j in d_params:
        p = params[j]
        out_shape.append(jax.ShapeDtypeStruct(p.shape, F32))
        if j in per_b:
            out_specs.append(pl.BlockSpec((None,) + p.shape[1:], lambda b, i: (b, 0, 0)))
        else:
            out_specs.append(pl.BlockSpec(p.shape, lambda b, i, nd=p.ndim: (0,) * nd))
    return pl.pallas_call(body, name=name, grid=(nb, ns), in_specs=in_specs, out_specs=out_specs,
                          out_shape=out_shape, compiler_params=_cparams("arbitrary", "arbitrary"))(*all_rows, *[
                              all_rows[idx] for idx, _ in halo_keys], *params)


MM_TILE = 1024
MM_DEEP = 2048


def _mm_nn(name, x, w, out_dtype):
    M, K = x.shape
    N = w.shape[1]
    tm, tn = min(MM_TILE if K <= MM_DEEP else MM_TILE // 2, M), min(MM_TILE, N)

    def body(x_ref, w_ref, o_ref):
        o_ref[...] = jnp.dot(x_ref[...].astype(BF16), w_ref[...].astype(BF16),
                             preferred_element_type=F32).astype(o_ref.dtype)

    return pl.pallas_call(body, name=name, grid=(M // tm, N // tn),
                          in_specs=[pl.BlockSpec((tm, K), lambda i, j: (i, 0)), pl.BlockSpec((K, tn), lambda i, j: (0, j))],
                          out_specs=pl.BlockSpec((tm, tn), lambda i, j: (i, j)),
                          out_shape=jax.ShapeDtypeStruct((M, N), out_dtype),
                          compiler_params=_cparams("parallel", "parallel"))(x, w)


def _mm_up_act(name, x, w):
    M, K = x.shape
    N = w.shape[1]
    tm, tn = min(MM_TILE if K <= MM_DEEP else MM_TILE // 2, M), min(MM_TILE, N)

    def body(x_ref, w_ref, up_ref, act_ref):
        up = jnp.dot(x_ref[...].astype(BF16), w_ref[...].astype(BF16), preferred_element_type=F32)
        up_ref[...] = up.astype(BF16)
        act_ref[...] = jnp.square(jnp.maximum(up, 0.0)).astype(BF16)

    out = pl.BlockSpec((tm, tn), lambda i, j: (i, j))
    return pl.pallas_call(body, name=name, grid=(M // tm, N // tn),
                          in_specs=[pl.BlockSpec((tm, K), lambda i, j: (i, 0)), pl.BlockSpec((K, tn), lambda i, j: (0, j))],
                          out_specs=[out, out], out_shape=[jax.ShapeDtypeStruct((M, N), BF16)] * 2,
                          compiler_params=_cparams("parallel", "parallel"))(x, w)


def _mm_nt_act_bwd(name, dy, w, up):
    M, N = dy.shape
    K = w.shape[0]
    tm, tk = min(MM_TILE if N <= MM_DEEP else MM_TILE // 2, M), min(MM_TILE, K)

    def body(dy_ref, w_ref, up_ref, o_ref):
        dact = lax.dot_general(dy_ref[...].astype(BF16), w_ref[...].astype(BF16), (((1,), (1,)), ((), ())),
                               preferred_element_type=F32)
        o_ref[...] = (dact * (2.0 * jnp.maximum(up_ref[...].astype(F32), 0.0))).astype(o_ref.dtype)

    out = pl.BlockSpec((tm, tk), lambda i, j: (i, j))
    return pl.pallas_call(body, name=name, grid=(M // tm, K // tk),
                          in_specs=[pl.BlockSpec((tm, N), lambda i, j: (i, 0)), pl.BlockSpec((tk, N), lambda i, j: (j, 0)), out],
                          out_specs=out, out_shape=jax.ShapeDtypeStruct((M, K), BF16),
                          compiler_params=_cparams("parallel", "parallel"))(dy, w, up)


def _mm_nt(name, dy, w, out_dtype):
    M, N = dy.shape
    K = w.shape[0]
    tm, tk = min(MM_TILE if N <= MM_DEEP else MM_TILE // 2, M), min(MM_TILE, K)

    def body(dy_ref, w_ref, o_ref):
        o_ref[...] = lax.dot_general(dy_ref[...].astype(BF16), w_ref[...].astype(BF16), (((1,), (1,)), ((), ())),
                                     preferred_element_type=F32).astype(o_ref.dtype)

    return pl.pallas_call(body, name=name, grid=(M // tm, K // tk),
                          in_specs=[pl.BlockSpec((tm, N), lambda i, j: (i, 0)), pl.BlockSpec((tk, N), lambda i, j: (j, 0))],
                          out_specs=pl.BlockSpec((tm, tk), lambda i, j: (i, j)),
                          out_shape=jax.ShapeDtypeStruct((M, K), out_dtype),
                          compiler_params=_cparams("parallel", "parallel"))(dy, w)


def _mm_tn(name, x, dy):
    M, K = x.shape
    N = dy.shape[1]
    tm, tk, tn = min(MM_TILE, M), min(MM_TILE, K), min(MM_TILE, N)

    def body(x_ref, dy_ref, o_ref):
        @pl.when(pl.program_id(2) == 0)
        def _():
            o_ref[...] = jnp.zeros_like(o_ref)
        o_ref[...] += lax.dot_general(x_ref[...].astype(BF16), dy_ref[...].astype(BF16), (((0,), (0,)), ((), ())),
                                      preferred_element_type=F32)

    return pl.pallas_call(body, name=name, grid=(K // tk, N // tn, M // tm),
                          in_specs=[pl.BlockSpec((tm, tk), lambda i, j, m: (m, i)), pl.BlockSpec((tm, tn), lambda i, j, m: (m, j))],
                          out_specs=pl.BlockSpec((tk, tn), lambda i, j, m: (i, j)),
                          out_shape=jax.ShapeDtypeStruct((K, N), F32),
                          compiler_params=_cparams("parallel", "parallel", "arbitrary"))(x, dy)


def _lru_scan(name, a, b, seq, ct=128):
    T, C = a.shape

    def body(a_ref, b_ref, h_ref):
        A, Bv = a_ref[...], b_ref[...]
        row = lax.broadcasted_iota(jnp.int32, (seq, 1), 0)
        k = 1
        while k < seq:
            keep = row >= k
            Bv = A * jnp.where(keep, pltpu.roll(Bv, k, 0), 0.0) + Bv
            A = A * jnp.where(keep, pltpu.roll(A, k, 0), 1.0)
            k *= 2
        h_ref[...] = Bv

    spec = pl.BlockSpec((seq, ct), lambda b_, j: (b_, j))
    return pl.pallas_call(body, name=name, grid=(T // seq, C // ct), in_specs=[spec, spec], out_specs=spec,
                          out_shape=jax.ShapeDtypeStruct((T, C), F32),
                          compiler_params=_cparams("parallel", "parallel"))(a, b)


def _lru_scan_bwd(name, a, h, dh, seq, ct=128):
    T, C = a.shape

    def body(a_ref, h_ref, dh_ref, da_ref, db_ref):
        row = lax.broadcasted_iota(jnp.int32, (seq, 1), 0)
        Cf = jnp.where(row < seq - 1, pltpu.roll(a_ref[...], seq - 1, 0), 0.0)
        G = dh_ref[...]
        k = 1
        while k < seq:
            keep = row + k < seq
            G = Cf * jnp.where(keep, pltpu.roll(G, seq - k, 0), 0.0) + G
            Cf = Cf * jnp.where(keep, pltpu.roll(Cf, seq - k, 0), 1.0)
            k *= 2
        db_ref[...] = G
        da_ref[...] = G * jnp.where(row >= 1, pltpu.roll(h_ref[...], 1, 0), 0.0)

    spec = pl.BlockSpec((seq, ct), lambda b_, j: (b_, j))
    return pl.pallas_call(body, name=name, grid=(T // seq, C // ct), in_specs=[spec] * 3, out_specs=[spec] * 2,
                          out_shape=[jax.ShapeDtypeStruct((T, C), F32)] * 2,
                          compiler_params=_cparams("parallel", "parallel"))(a, h, dh)


def _pair_consts():
    jj = lax.broadcasted_iota(jnp.int32, (RWKV_N, LANES), 0)
    ll = lax.broadcasted_iota(jnp.int32, (RWKV_N, LANES), 1)
    pick = ((ll & (RWKV_N - 1)) == jj).astype(BF16)
    l0 = lax.broadcasted_iota(jnp.int32, (LANES, LANES), 0)
    l1 = lax.broadcasted_iota(jnp.int32, (LANES, LANES), 1)
    same = ((l0 >> HEAD_SHIFT) == (l1 >> HEAD_SHIFT)).astype(BF16)
    return pick, same


N_COL = 5


def _cols_raw(rows, pick, same):
    lhs = [pick * jnp.broadcast_to(x, (RWKV_N, LANES)).astype(BF16) for x in rows]
    out = jnp.dot(jnp.concatenate(lhs, axis=0), same, preferred_element_type=F32)
    return tuple(out[i * RWKV_N:(i + 1) * RWKV_N] for i in range(len(rows)))


def _cols_adjoint(g):
    n = len(g)
    g = list(g) + [jnp.zeros_like(g[0])] * (n % 2)
    head = ((lax.broadcasted_iota(jnp.int32, (SUB, LANES), 1) >> HEAD_SHIFT)
            == lax.broadcasted_iota(jnp.int32, (SUB, LANES), 0)).astype(BF16)
    low = lax.broadcasted_iota(jnp.int32, (1, LANES), 1) < RWKV_N
    sums = lax.dot_general(head, jnp.concatenate([t.astype(BF16) for t in g], axis=0),
                           (((1,), (1,)), ((), ())), preferred_element_type=F32)
    rows = []
    for i in range(0, n, 2):
        blk = sums[:, i * RWKV_N:(i + 2) * RWKV_N]
        swapped = pltpu.roll(blk, RWKV_N, 1)
        rows.append(jnp.where(low, blk[0:1], swapped[1:2]))
        rows.append(jnp.where(low, swapped[0:1], blk[1:2]))
    return tuple(rows[:n])


def _rwkv_cols(rows, pick, same):
    flat = []
    for r, w, k, v, a, b in rows:
        flat += [a, 1.0 - w, b, k, r]
    tiles = _cols_raw(tuple(flat), pick, same)
    return [tiles[N_COL * p:N_COL * (p + 1)] for p in range(len(rows))]


def _rwkv_step(s0, tiles, v):
    ca, ce, cb, ck, cr = tiles
    sa = jnp.sum(s0 * ca, axis=0, keepdims=True)
    s1 = s0 - s0 * ce + cb * sa + ck * v
    return s1, jnp.sum(s1 * cr, axis=0, keepdims=True), sa


def _rwkv_step_bwd(s0, s1, sa, tiles, v, dy, ds1):
    ca, ce, cb, ck, cr = tiles
    ds1 = ds1 + cr * dy
    dsa = jnp.sum(ds1 * cb, axis=0, keepdims=True)
    dv = jnp.sum(ds1 * ck, axis=0, keepdims=True)
    ds0 = ds1 - ds1 * ce + ca * dsa
    return ds0, (s0 * dsa, ds1 * s0, ds1 * sa, ds1 * v, s1 * dy), dv


def _rwkv_fwd(name, r, w, k, v, a, b, seq, tc=16, G=8, NB=4):
    T, C = r.shape
    B = T // seq
    tc = min(tc, seq)
    NB = NB if B % NB == 0 else 1
    NP = C // LANES
    nc = seq // tc
    row = pl.BlockSpec((NB, tc, LANES * G), lambda b_, g, c: (b_, c, g))
    st = pl.BlockSpec((NB, tc, G, RWKV_N, LANES), lambda b_, g, c: (b_, c, g, 0, 0))
    last = pl.BlockSpec((NB, G, RWKV_N, LANES), lambda b_, g, c: (b_, g, 0, 0))
    chains = [(q, p) for q in range(NB) for p in range(G)]

    def body(r_ref, w_ref, k_ref, v_ref, a_ref, b_ref, y_ref, st_ref, sa_ref, last_ref, s_scr):
        @pl.when(pl.program_id(2) == 0)
        def _():
            s_scr[...] = jnp.zeros_like(s_scr)
        pick, same = _pair_consts()
        rid = lax.broadcasted_iota(jnp.int32, (SUB, LANES), 0)

        def group(t8, carry):
            base = pl.multiple_of(t8 * SUB, SUB)
            rows8 = pl.ds(base, SUB)
            lanes = [pl.ds(LANES * p, LANES) for p in range(G)]
            blk = [[x[q, rows8, lanes[p]] for x in (r_ref, w_ref, k_ref, v_ref, a_ref, b_ref)] for q, p in chains]
            s = [s_scr[q, p] for q, p in chains]
            ys = [jnp.zeros((SUB, LANES), F32) for _ in chains]
            sas = [jnp.zeros((SUB, LANES), F32) for _ in chains]
            for i in range(SUB):
                tiles = _rwkv_cols([tuple(b_[i:i + 1] for b_ in blk[n_]) for n_ in range(len(chains))], pick, same)
                for n_, (q, p) in enumerate(chains):
                    st_ref[q, base + i, p] = s[n_]
                    s[n_], y, sa = _rwkv_step(s[n_], tiles[n_], blk[n_][3][i:i + 1])
                    ys[n_] = jnp.where(rid == i, y, ys[n_])
                    sas[n_] = jnp.where(rid == i, sa, sas[n_])
            for n_, (q, p) in enumerate(chains):
                s_scr[q, p] = s[n_]
                y_ref[q, rows8, lanes[p]] = ys[n_]
                sa_ref[q, rows8, lanes[p]] = sas[n_]
            return carry

        lax.fori_loop(0, tc // SUB, group, 0)

        @pl.when(pl.program_id(2) == nc - 1)
        def _():
            last_ref[...] = s_scr[...]

    seq3 = lambda x: x.reshape(B, seq, C)
    y, states, sa, final = pl.pallas_call(body,
        name=name, grid=(B // NB, NP // G, nc), in_specs=[row] * 6, out_specs=[row, st, row, last],
        out_shape=[jax.ShapeDtypeStruct((B, seq, C), F32), jax.ShapeDtypeStruct((B, seq, NP, RWKV_N, LANES), F32),
                   jax.ShapeDtypeStruct((B, seq, C), F32), jax.ShapeDtypeStruct((B, NP, RWKV_N, LANES), F32)],
        scratch_shapes=[pltpu.VMEM((NB, G, RWKV_N, LANES), F32)],
        compiler_params=_cparams("parallel", "parallel", "arbitrary"))(*[seq3(x) for x in (r, w, k, v, a, b)])
    return y.reshape(T, C), states.reshape(T, NP, RWKV_N, LANES), sa.reshape(T, C), final


def _rwkv_bwd(name, r, w, k, v, a, b, states, sa, last, dy, dr_p, dk_p, dv_p, seq, tc=8, G=8, NB=4):
    T, C = r.shape
    B = T // seq
    tc = min(tc, seq)
    NB = NB if B % NB == 0 else 1
    NP = C // LANES
    nc = seq // tc
    row = pl.BlockSpec((NB, tc, LANES * G), lambda b_, g, c: (b_, nc - 1 - c, g))
    st = pl.BlockSpec((NB, tc, G, RWKV_N, LANES), lambda b_, g, c: (b_, nc - 1 - c, g, 0, 0))
    last_spec = pl.BlockSpec((NB, G, RWKV_N, LANES), lambda b_, g, c: (b_, g, 0, 0))
    chains = [(q, p) for q in range(NB) for p in range(G)]

    def body(r_ref, w_ref, k_ref, v_ref, a_ref, b_ref, st_ref, sa_ref, last_ref, dy_ref, drp_ref, dkp_ref, dvp_ref,
             dr_ref, dw_ref, dk_ref, dv_ref, da_ref, db_ref, ds_scr, s1_scr):
        @pl.when(pl.program_id(2) == 0)
        def _():
            ds_scr[...] = jnp.zeros_like(ds_scr)
            s1_scr[...] = last_ref[...]
        pick, same = _pair_consts()
        rid = lax.broadcasted_iota(jnp.int32, (SUB, LANES), 0)
        out_refs = (da_ref, dw_ref, db_ref, dk_ref, dr_ref)

        def group(n, carry):
            base = pl.multiple_of((tc // SUB - 1 - n) * SUB, SUB)
            rows8 = pl.ds(base, SUB)
            lanes = [pl.ds(LANES * p, LANES) for p in range(G)]
            blk = [[x[q, rows8, lanes[p]] for x in (r_ref, w_ref, k_ref, v_ref, a_ref, b_ref)] for q, p in chains]
            dys = [dy_ref[q, rows8, lanes[p]] for q, p in chains]
            sas = [sa_ref[q, rows8, lanes[p]] for q, p in chains]
            ds = [ds_scr[q, p] for q, p in chains]
            s1 = [s1_scr[q, p] for q, p in chains]
            zero = jnp.zeros((SUB, LANES), F32)
            gs = [[zero, zero, zero, dkp_ref[q, rows8, lanes[p]], drp_ref[q, rows8, lanes[p]]] for q, p in chains]
            gv = [dvp_ref[q, rows8, lanes[p]] for q, p in chains]
            steps = tuple(reversed(range(SUB)))
            tiles_of = [_rwkv_cols([tuple(b_[i:i + 1] for b_ in blk[n_]) for n_ in range(len(chains))], pick, same)
                        for i in steps]
            adj_of = []
            for i, tiles in zip(steps, tiles_of):
                adj = []
                for n_, (q, p) in enumerate(chains):
                    s0 = st_ref[q, base + i, p]
                    ds[n_], g_tiles, dv = _rwkv_step_bwd(s0, s1[n_], sas[n_][i:i + 1], tiles[n_], blk[n_][3][i:i + 1],
                                                         dys[n_][i:i + 1], ds[n_])
                    s1[n_] = s0
                    adj += list(g_tiles)
                    gv[n_] = gv[n_] + jnp.where(rid == i, dv, 0.0)
                adj_of.append(adj)
            for i, adj in zip(steps, adj_of):
                rows = _cols_adjoint(adj)
                for n_ in range(len(chains)):
                    for j in range(N_COL):
                        gs[n_][j] = gs[n_][j] + jnp.where(rid == i, rows[N_COL * n_ + j], 0.0)
            for n_, (q, p) in enumerate(chains):
                ds_scr[q, p] = ds[n_]
                s1_scr[q, p] = s1[n_]
                dv_ref[q, rows8, lanes[p]] = gv[n_]
                for j in range(N_COL):
                    out_refs[j][q, rows8, lanes[p]] = gs[n_][j]
            return carry

        lax.fori_loop(0, tc // SUB, group, 0)

    seq3 = lambda x: x.reshape(B, seq, C)
    outs = pl.pallas_call(body, name=name, grid=(B // NB, NP // G, nc),
                          in_specs=[row] * 6 + [st, row, last_spec] + [row] * 4, out_specs=[row] * 6,
                          out_shape=[jax.ShapeDtypeStruct((B, seq, C), F32)] * 6,
                          scratch_shapes=[pltpu.VMEM((NB, G, RWKV_N, LANES), F32)] * 2,
                          compiler_params=_cparams("parallel", "parallel", "arbitrary"))(
                              *[seq3(x) for x in (r, w, k, v, a, b)], states.reshape(B, seq, NP, RWKV_N, LANES), seq3(sa),
                              last, *[seq3(x) for x in (dy, dr_p, dk_p, dv_p)])
    return [o.reshape(T, C) for o in outs]


def _all_gather(name, shard):
    def body(x_ref, o_ref, send_sems, recv_sems, local_sem):
        x, y, c = lax.axis_index("x"), lax.axis_index("y"), lax.axis_index("c")
        me, sibling = (x, y, c), (x, y, 1 - c)
        chips = [(1 - x, y), (x, 1 - y), (1 - x, 1 - y)]

        def copy(k, block, to, src=None):
            dst = o_ref.at[4 * block[0] + 2 * block[1] + block[2]]
            return pltpu.make_async_remote_copy(src_ref=dst if src is None else src, dst_ref=dst, send_sem=send_sems.at[k],
                                                recv_sem=recv_sems.at[k], device_id=to, device_id_type=MESH)

        mine = pltpu.make_async_copy(x_ref, o_ref.at[4 * x + 2 * y + c], local_sem)
        mine.start()
        first = [copy(0, me, sibling, src=x_ref)] + [copy(1 + j, me, (*chip, c), src=x_ref) for j, chip in enumerate(chips)]
        for cp in first:
            cp.start()
        passed = [copy(4 + j, (*chip, c), sibling) for j, chip in enumerate(chips)]
        for j, chip in enumerate(chips):
            copy(1 + j, (*chip, c), me).wait_recv()
            passed[j].start()
        copy(0, sibling, me).wait_recv()
        for j, chip in enumerate(chips):
            copy(4 + j, (*chip, 1 - c), me).wait_recv()
        for cp in first + passed:
            cp.wait_send()
        mine.wait()

    hbm = pl.BlockSpec(memory_space=pltpu.HBM)
    return pl.pallas_call(body, name=name, in_specs=[hbm], out_specs=hbm,
                          out_shape=jax.ShapeDtypeStruct((N_DEV,) + shard.shape, shard.dtype),
                          scratch_shapes=[pltpu.SemaphoreType.DMA((N_DEV - 1,)), pltpu.SemaphoreType.DMA((N_DEV - 1,)),
                                          pltpu.SemaphoreType.DMA])(shard)


N_CHIP = 4


def _swap_with_sibling(name, blocks, n_split):
    _, R, W = blocks.shape
    N = N_CHIP
    rq = R // n_split

    def body(x_ref, got_ref, send_sems, recv_sems):
        x, y, c = lax.axis_index("x"), lax.axis_index("y"), lax.axis_index("c")
        copies = []
        for j in range(N):
            for q in range(n_split):
                rows = pl.ds(q * rq, rq)
                cp = pltpu.make_async_remote_copy(src_ref=x_ref.at[2 * j + 1 - c, rows], dst_ref=got_ref.at[j, rows],
                                                  send_sem=send_sems.at[j * n_split + q], recv_sem=recv_sems.at[j * n_split + q],
                                                  device_id=(x, y, 1 - c), device_id_type=MESH)
                cp.start()
                copies.append(cp)
        for cp in copies:
            cp.wait()

    hbm = pl.BlockSpec(memory_space=pltpu.HBM)
    return pl.pallas_call(body, name=name, in_specs=[hbm], out_specs=hbm,
                          out_shape=jax.ShapeDtypeStruct((N, R, W), blocks.dtype),
                          scratch_shapes=[pltpu.SemaphoreType.DMA((N * n_split,)), pltpu.SemaphoreType.DMA((N * n_split,))])(blocks)


def _add_own_blocks(name, blocks, got, tr, out_dtype):
    N, R, W = got.shape

    def body(c_ref, a_ref, b_ref, o_ref):
        o_ref[...] = (a_ref[...] + b_ref[...]).astype(o_ref.dtype)

    grid_spec = pltpu.PrefetchScalarGridSpec(
        num_scalar_prefetch=1, grid=(N, R // tr),
        in_specs=[pl.BlockSpec((None, tr, W), lambda n, i, c: (2 * n + c[0], i, 0)),
                  pl.BlockSpec((None, tr, W), lambda n, i, c: (n, i, 0))],
        out_specs=pl.BlockSpec((None, tr, W), lambda n, i, c: (n, i, 0)))
    core = lax.axis_index("c").astype(jnp.int32).reshape(1)
    return pl.pallas_call(body, name=name, grid_spec=grid_spec, out_shape=jax.ShapeDtypeStruct(got.shape, out_dtype),
                          compiler_params=_cparams("parallel", "parallel"))(core, blocks, got)


def _exchange_chips(name, blocks):
    def body(z_ref, o_ref, send_sems, recv_sems, local_sem):
        x, y, c = lax.axis_index("x"), lax.axis_index("y"), lax.axis_index("c")
        chip = 2 * x + y
        mine = pltpu.make_async_copy(z_ref.at[chip], o_ref.at[chip], local_sem)
        mine.start()
        copies = []
        for k in range(1, N_CHIP):
            to = (x ^ (k >> 1), y ^ (k & 1), c)
            cp = pltpu.make_async_remote_copy(src_ref=z_ref.at[chip ^ k], dst_ref=o_ref.at[chip], send_sem=send_sems.at[k - 1],
                                              recv_sem=recv_sems.at[k - 1], device_id=to, device_id_type=MESH)
            cp.start()
            copies.append(cp)
        for k in range(1, N_CHIP):
            pltpu.make_async_remote_copy(src_ref=z_ref.at[chip], dst_ref=o_ref.at[chip ^ k], send_sem=send_sems.at[k - 1],
                                         recv_sem=recv_sems.at[k - 1], device_id=(x ^ (k >> 1), y ^ (k & 1), c),
                                         device_id_type=MESH).wait_recv()
        for cp in copies:
            cp.wait_send()
        mine.wait()

    hbm = pl.BlockSpec(memory_space=pltpu.HBM)
    return pl.pallas_call(body, name=name, in_specs=[hbm], out_specs=hbm,
                          out_shape=jax.ShapeDtypeStruct(blocks.shape, blocks.dtype),
                          scratch_shapes=[pltpu.SemaphoreType.DMA((N_CHIP - 1,)), pltpu.SemaphoreType.DMA((N_CHIP - 1,)),
                                          pltpu.SemaphoreType.DMA])(blocks)


def _adamw(name, parts, w, m, v, tr):
    R, W = w.shape
    n_parts = parts.shape[0]

    def body(p_ref, w_ref, m_ref, v_ref, g_ref, d_ref, nm_ref, nv_ref):
        g = p_ref[0].astype(F32)
        for s in range(1, n_parts):
            g = g + p_ref[s].astype(F32)
        nm = ADAM_B1 * m_ref[...] + (1.0 - ADAM_B1) * g
        nv = ADAM_B2 * v_ref[...] + (1.0 - ADAM_B2) * (g * g)
        m_hat = nm / (1.0 - ADAM_B1 ** ADAM_STEP)
        v_hat = nv / (1.0 - ADAM_B2 ** ADAM_STEP)
        g_ref[...] = g
        d_ref[...] = -ADAM_LR * (m_hat / (jnp.sqrt(v_hat) + ADAM_EPS) + ADAM_WD * w_ref[...])
        nm_ref[...] = nm
        nv_ref[...] = nv

    spec = pl.BlockSpec((tr, W), lambda i: (i, 0))
    return pl.pallas_call(body, name=name, grid=(R // tr,),
                          in_specs=[pl.BlockSpec((n_parts, tr, W), lambda i: (0, i, 0)), spec, spec, spec],
                          out_specs=[spec] * 4, out_shape=[jax.ShapeDtypeStruct((R, W), F32)] * 4,
                          compiler_params=_cparams("parallel"))(parts, w, m, v)


def _rms(x, g):
    return x * lax.rsqrt(jnp.mean(x * x, axis=-1, keepdims=True) + RMS_EPS) * g


def _softplus(z):
    return jnp.maximum(z, 0.0) + jnp.log(1.0 + jnp.exp(-jnp.abs(z)))


def _neg_expm1(z):
    series = -z * (1.0 + z * (0.5 + z * (1.0 / 6.0)))
    return jnp.where(z > -1e-3, series, 1.0 - jnp.exp(z))


def _gelu_tanh(x):
    return 0.5 * x * (1.0 + jnp.tanh(0.7978845608028654 * (x + 0.044715 * x * x * x)))


def _bdot(x, w):
    return jnp.dot(x.astype(BF16), w.astype(BF16), preferred_element_type=F32)


def _seg_consts(C):
    c0 = lax.broadcasted_iota(jnp.int32, (C, LANES), 0)
    h1 = lax.broadcasted_iota(jnp.int32, (C, LANES), 1)
    red = ((c0 >> HEAD_SHIFT) == h1).astype(BF16)
    h0 = lax.broadcasted_iota(jnp.int32, (LANES, C), 0)
    c1 = lax.broadcasted_iota(jnp.int32, (LANES, C), 1)
    exp = ((c1 >> HEAD_SHIFT) == h0).astype(BF16)
    return red, exp


def _split_dot(x, m):
    hi = x.astype(BF16)
    lo = (x - hi.astype(F32)).astype(BF16)
    return jnp.dot(hi, m, preferred_element_type=F32) + jnp.dot(lo, m, preferred_element_type=F32)


def _segsum_raw(x):
    red, exp = _seg_consts(x.shape[1])
    return _split_dot(_split_dot(x, red), exp)


@jax.custom_vjp
def _segsum(x):
    return _segsum_raw(x)


_segsum.defvjp(lambda x: (_segsum_raw(x), None), lambda _, g: (_segsum_raw(g),))


def _f_pre(out_dtype):
    def fn(pos, x, g):
        return (_rms(x, g).astype(out_dtype),)
    return fn


def _f_post(pos, x, t, g):
    return (x + _rms(t, g),)


def _f_post_pre(out_dtype, has_bias):
    def fn(pos, x, t, g_post, g_pre, *bias):
        x1 = x + _rms(t + bias[0] if has_bias else t, g_post)
        return x1, _rms(x1, g_pre).astype(out_dtype)
    return fn


def _f_conv(pos, proj, p1, p2, p3, b_u, w0, w1, w2, w3, cb):
    D = b_u.shape[1]
    u0 = proj[:, D:] + b_u
    u1 = jnp.where(pos >= 1, p1[:, D:] + b_u, 0.0)
    u2 = jnp.where(pos >= 2, p2[:, D:] + b_u, 0.0)
    u3 = jnp.where(pos >= 3, p3[:, D:] + b_u, 0.0)
    return (cb + u3 * w0 + u2 * w1 + u1 * w2 + u0 * w3,)


def _f_conv_bwd(pos, dconv, proj, dyb, n1, n2, n3, p1, p2, p3, b_u, w0, w1, w2, w3):
    D = b_u.shape[1]
    du = dconv * w3 + n1 * w2 + n2 * w1 + n3 * w0
    dproj = jnp.concatenate([dyb[:, :D], du], axis=1)
    u0 = proj[:, D:] + b_u
    u1 = jnp.where(pos >= 1, p1[:, D:] + b_u, 0.0)
    u2 = jnp.where(pos >= 2, p2[:, D:] + b_u, 0.0)
    u3 = jnp.where(pos >= 3, p3[:, D:] + b_u, 0.0)
    rs = lambda z: jnp.sum(z, axis=0, keepdims=True)
    return (dproj, rs(dproj), rs(dconv * u3), rs(dconv * u2), rs(dconv * u1), rs(dconv * u0), rs(dconv))


def _f_gates(pos, conv, proj, gate_w, gb_r, gb_i, lam, b_y):
    D = conv.shape[1]
    blk = D // LRU_HEADS
    cb = conv.astype(BF16)

    def gate(g, bias):
        z = [jnp.dot(cb[:, h * blk:(h + 1) * blk], gate_w[(g * LRU_HEADS + h) * blk:(g * LRU_HEADS + h + 1) * blk, :].astype(BF16),
                     preferred_element_type=F32) for h in range(LRU_HEADS)]
        return jax.nn.sigmoid(jnp.concatenate(z, axis=1) + bias)

    r_gate, i_gate = gate(0, gb_r), gate(1, gb_i)
    log_a = -LRU_C * r_gate * _softplus(-lam)
    a = jnp.exp(log_a)
    b = jnp.sqrt(_neg_expm1(2.0 * log_a)) * i_gate * conv
    yb = _gelu_tanh(proj[:, :D] + b_y)
    return a, b, yb


def _f_hy(pos, h, yb):
    return ((h * yb).astype(BF16),)


def _f_mix(pos, hn, hp, mu_r, mu_w, mu_k, mu_v, mu_a, mu_g, w0, w1, w2, a0, a1, a2, g1, g2):
    xx = hp - hn
    xr, xw, xk, xv, xa, xg = (hn + xx * m for m in (mu_r, mu_w, mu_k, mu_v, mu_a, mu_g))
    lw = w0 + _bdot(jnp.tanh(_bdot(xw, w1)), w2)
    decay = jnp.exp(-jnp.exp(-_softplus(-lw) - 0.5))
    a = jax.nn.sigmoid(a0 + _bdot(_bdot(xa, a1), a2))
    g = _bdot(jax.nn.sigmoid(_bdot(xg, g1)), g2)
    return xr.astype(BF16), xk.astype(BF16), xv.astype(BF16), decay, a, g


def _f_kk(pos, k, a, k_k, k_a):
    kk = k * k_k
    kk = kk / jnp.maximum(jnp.sqrt(_segsum(kk * kk)), 1e-12)
    return k * (1.0 + (a - 1.0) * k_a), -kk, kk * a


def _f_gn(pos, y, r, k2, v, g, gn_g, gn_b, r_k):
    inv_n = 1.0 / RWKV_N
    yc = y - _segsum(y) * inv_n
    var = _segsum(yc * yc) * inv_n
    yn = yc * lax.rsqrt(var + GN_EPS) * gn_g + gn_b
    bonus = _segsum(r * k2 * r_k) * v
    return (((yn + bonus) * g).astype(BF16),)


def _f_attn(pos, q, kv):
    D = q.shape[1]
    hd = D // XATTN_HEADS
    qb, kb, vb = q.astype(BF16), kv[:, :D].astype(BF16), kv[:, D:].astype(BF16)
    outs = []
    for h in range(XATTN_HEADS):
        sl = slice(h * hd, (h + 1) * hd)
        s = lax.dot_general(qb[:, sl], kb[:, sl], (((1,), (1,)), ((), ())), preferred_element_type=F32) * (hd ** -0.5)
        s = s - jnp.max(s, axis=-1, keepdims=True)
        e = jnp.exp(s)
        p = e / jnp.sum(e, axis=-1, keepdims=True)
        outs.append(jnp.dot(p.astype(BF16), vb[:, sl], preferred_element_type=F32))
    return (jnp.concatenate(outs, axis=1).astype(BF16),)


def _f_loss(pos, y, target):
    d = y - target
    inv = 1.0 / y.shape[1]
    part = 0.5 * inv * jnp.sum(jnp.sum(d * d, axis=1, keepdims=True), axis=0, keepdims=True)
    return d * inv, jnp.broadcast_to(part, (8, LANES))


def _full_from_blocks(blocks, shard_shape, ax):
    g = jnp.moveaxis(blocks.reshape((N_DEV,) + tuple(shard_shape)), 0, ax)
    return g.reshape(tuple(shard_shape[:ax]) + (N_DEV * shard_shape[ax],) + tuple(shard_shape[ax + 1:]))


def _blocks_from_full(full, shard_shape, ax):
    g = full.reshape(tuple(shard_shape[:ax]) + (N_DEV, shard_shape[ax]) + tuple(shard_shape[ax + 1:]))
    return jnp.moveaxis(g, ax, 0).reshape(N_DEV, -1)


def _numel(shape):
    n = 1
    for s in shape:
        n *= s
    return n


def _pack_flat(arrs, total):
    flat = jnp.concatenate([a.reshape(-1).astype(F32) for a in arrs])
    return jnp.pad(flat, (0, total - flat.shape[0]))


def _train_step(x3, mem3, target3, P, M_, V_):
    Bl, S, D = x3.shape
    T = Bl * S
    ML = mem3.shape[1]
    x = x3.reshape(T, D)
    mem = mem3.reshape(Bl * ML, D)
    target = target3.reshape(T, D)
    shp = {n: P[n].shape for n in WEIGHTS}

    n_big = sum(_numel(shp[n]) for n in BIG)
    n_small = sum(_numel(shp[n]) for n in SMALL)
    n_small_pad = -(-n_small // (PACK_ROWS * PACK_W)) * (PACK_ROWS * PACK_W)
    big_rows, small_rows = n_big // PACK_W, n_small_pad // PACK_W
    def pack_rows(src):
        big = [src[n].reshape(-1, PACK_W) for n in BIG]
        return jnp.concatenate(big + [_pack_flat([src[n] for n in SMALL], n_small_pad).reshape(small_rows, PACK_W)], axis=0)

    big_pack = jnp.concatenate([P[n].reshape(-1, PACK_W) for n in BIG], axis=0).astype(BF16)
    small_pack = _pack_flat([P[n] for n in SMALL], n_small_pad).reshape(small_rows, PACK_W)
    big_all = _all_gather("gather_matrices", big_pack)
    small_all = _all_gather("gather_vectors", small_pack).reshape(N_DEV, n_small_pad)
    Wf = {}
    big_row0 = {}
    off = 0
    for n in BIG:
        nr = _numel(shp[n]) // PACK_W
        big_row0[n] = off
        Wf[n] = _full_from_blocks(big_all[:, off:off + nr], shp[n], SHARD_AXIS[n])
        off += nr
    off = 0
    for n in SMALL:
        ne = _numel(shp[n])
        Wf[n] = _full_from_blocks(small_all[:, off:off + ne], shp[n], SHARD_AXIS[n])
        off += ne
    for n in REPL:
        Wf[n] = P[n]

    row = lambda v: v.reshape(1, -1).astype(F32)
    gains = [[row(Wf['ln_gains'][i, j]) for j in range(6)] for i in range(2)]
    LW = LANES

    def pad_cols(w):
        return jnp.pad(w, ((0, 0), (0, LW - w.shape[1]))).astype(F32)

    def pad_rows(w):
        return jnp.pad(w, ((0, LW - w.shape[0]), (0, 0))).astype(F32)

    G = {}
    saved = {}

    mem_n, = _rw("mem_norm_fwd", _f_pre(BF16), [mem], [row(Wf['mem_norm'])], outs=[(D, BF16)])

    def post_pre(name, xin, t, g_post, g_pre, out_dtype, bias=()):
        return _rw(name, _f_post_pre(out_dtype, bool(bias)), [xin, t], [g_post, g_pre, *bias],
                   outs=[(D, F32), (D, out_dtype)], seq=S)

    def post_pre_bwd(name, xin, t, g_post, g_pre, dx_next, dhn, bias=(), ct_adds=()):
        return _rw_vjp(name, _f_post_pre(F32, bool(bias)), [xin, t], [g_post, g_pre, *bias], [dx_next, dhn], 2,
                       d_rows=[(0, F32), (1, BF16)], d_params=list(range(2 + len(bias))), ct_adds=ct_adds, seq=S)

    def xattn_fwd(i, hq):
        q = _mm_nn(f"xattn{i}_q", hq, Wf['c_w_q'][i], BF16)
        kv = _mm_nn(f"xattn{i}_kv", mem_n, Wf['c_w_kv'][i], F32)
        kv = kv.reshape(Bl, ML, 2 * D)
        o, = _rw(f"xattn{i}_attn", _f_attn, [q], [kv], outs=[(D, BF16)], seq=S, per_b=(0,))
        saved[f"xattn{i}"] = (hq, q, kv, o)
        return _mm_nn(f"xattn{i}_o", o, Wf['c_w_o'][i], F32)

    def mlp_fwd(i, hm):
        up, act = _mm_up_act(f"mlp{i}_up", hm, Wf['m_w_up'][i])
        saved[f"mlp{i}"] = (hm, up, act)
        return _mm_nn(f"mlp{i}_down", act, Wf['m_w_down'][i], F32)

    b_in = row(Wf['a_b_in'][0])
    b_y, b_u = b_in[:, :D], b_in[:, D:]
    cw = [row(Wf['a_conv_w'][0, t]) for t in range(4)]
    cb = row(Wf['a_conv_b'][0])
    gate_w = Wf['a_gate_w'][0].reshape(2 * LRU_HEADS * (D // LRU_HEADS), D // LRU_HEADS).astype(F32)
    gate_b = Wf['a_gate_b'][0].reshape(2, D).astype(F32)
    gb_r, gb_i = gate_b[0:1], gate_b[1:2]
    lam = row(Wf['a_lambda'][0])
    b_out = row(Wf['a_b_out'][0])

    hn0, = _rw("lru_pre", _f_pre(BF16), [x], [gains[0][0]], outs=[(D, BF16)], seq=S)
    proj = _mm_nn("lru_in", hn0, Wf['a_w_in'][0], F32)
    prev3 = [(0, 1), (0, 2), (0, 3)]
    conv, = _rw("lru_conv", _f_conv, [proj], [b_u] + cw + [cb], outs=[(D, F32)], shifts=prev3, seq=S)
    gate_par = [gate_w, gb_r, gb_i, lam, b_y]
    a_l, b_l, yb = _rw("lru_gates", _f_gates, [conv, proj], gate_par, outs=[(D, F32)] * 3, seq=S)
    h_l = _lru_scan("lru_scan", a_l, b_l, S)
    hy, = _rw("lru_hy", _f_hy, [h_l, yb], outs=[(D, BF16)], seq=S)
    t0 = _mm_nn("lru_out", hy, Wf['a_w_out'][0], F32)
    x1, hq0 = post_pre("lru_post", x, t0, gains[0][1], gains[0][2], BF16, bias=(b_out,))
    c0 = xattn_fwd(0, hq0)
    x2, hm0 = post_pre("xattn0_post", x1, c0, gains[0][3], gains[0][4], BF16)
    m0 = mlp_fwd(0, hm0)
    x3_, hn1 = post_pre("mlp0_post", x2, m0, gains[0][5], gains[1][0], F32)

    mu = [row(Wf['b_mu'][0, j]) for j in range(6)]
    lora = [row(Wf['b_w0'][0]), pad_cols(Wf['b_w1'][0]), pad_rows(Wf['b_w2'][0]),
            row(Wf['b_a0'][0]), pad_cols(Wf['b_a1'][0]), pad_rows(Wf['b_a2'][0]),
            Wf['b_g1'][0].astype(F32), Wf['b_g2'][0].astype(F32)]
    k_k, k_a = row(Wf['b_k_k'][0]), row(Wf['b_k_a'][0])
    gn_g, gn_b, r_k = row(Wf['b_gn_g'][0]), row(Wf['b_gn_b'][0]), row(Wf['b_r_k'][0])

    xr, xk, xv, decay, a_r, g_r = _rw("rwkv_mix", _f_mix, [hn1], mu + lora,
                                      outs=[(D, BF16)] * 3 + [(D, F32)] * 3, shifts=[(0, 1)], seq=S)
    r_ = _mm_nn("rwkv_r", xr, Wf['b_w_rkv'][0, 0], F32)
    k_ = _mm_nn("rwkv_k", xk, Wf['b_w_rkv'][0, 1], F32)
    v_ = _mm_nn("rwkv_v", xv, Wf['b_w_rkv'][0, 2], F32)
    k2, ra, rb = _rw("rwkv_kk", _f_kk, [k_, a_r], [k_k, k_a], outs=[(D, F32)] * 3, seq=S)
    y_r, states, sa_r, last_r = _rwkv_fwd("rwkv_scan", r_, decay, k2, v_, ra, rb, S)
    gn_par = [gn_g, gn_b, r_k]
    og, = _rw("rwkv_gn", _f_gn, [y_r, r_, k2, v_, g_r], gn_par, outs=[(D, BF16)], seq=S)
    t1 = _mm_nn("rwkv_out", og, Wf['b_w_o'][0], F32)
    x4, hq1 = post_pre("rwkv_post", x3_, t1, gains[1][1], gains[1][2], BF16)
    c1 = xattn_fwd(1, hq1)
    x5, hm1 = post_pre("xattn1_post", x4, c1, gains[1][3], gains[1][4], BF16)
    m1 = mlp_fwd(1, hm1)
    x6, = _rw("mlp1_post", _f_post, [x5, m1], [gains[1][5]], outs=[(D, F32)], seq=S)

    dx, loss_acc = _rw("loss", _f_loss, [x6, target], outs=[(D, F32)], accs=[(8, LANES)], seq=S)
    loss = lax.psum(loss_acc[0, 0], ("x", "y", "c"))

    dgain = [[None] * 6 for _ in range(2)]
    dmem_n = None

    def mlp_bwd(i, dm):
        hm, up, act = saved[f"mlp{i}"]
        G_down = _mm_tn(f"mlp{i}_down_dw", act, dm)
        dup = _mm_nt_act_bwd(f"mlp{i}_down_dx", dm, Wf['m_w_down'][i], up)
        G_up = _mm_tn(f"mlp{i}_up_dw", hm, dup)
        return _mm_nt(f"mlp{i}_up_dx", dup, Wf['m_w_up'][i], F32), G_up, G_down

    def xattn_bwd(i, dc):
        hq, q, kv, o = saved[f"xattn{i}"]
        G_o = _mm_tn(f"xattn{i}_o_dw", o, dc)
        do = _mm_nt(f"xattn{i}_o_dx", dc, Wf['c_w_o'][i], F32)
        dq, dkv = _rw_vjp(f"xattn{i}_attn_bwd", _f_attn, [q], [kv], [do], 1, d_rows=[(0, BF16)],
                          d_params=[0], seq=S, per_b=(0,))
        dkv = dkv.reshape(Bl * ML, 2 * D)
        G_kv = _mm_tn(f"xattn{i}_kv_dw", mem_n, dkv)
        dmn = _mm_nt(f"xattn{i}_kv_dx", dkv, Wf['c_w_kv'][i], F32)
        G_q = _mm_tn(f"xattn{i}_q_dw", hq, dq)
        dhq = _mm_nt(f"xattn{i}_q_dx", dq, Wf['c_w_q'][i], F32)
        return dhq, G_q, G_kv, G_o, dmn

    G_up, G_down, G_q, G_kv, G_o = [None] * 2, [None] * 2, [None] * 2, [None] * 2, [None] * 2
    dm1, dgain[1][5] = _rw_vjp("mlp1_post_bwd", _f_post, [x5, m1], [gains[1][5]], [dx], 1,
                               d_rows=[(1, BF16)], d_params=[0], seq=S)
    dhm1, G_up[1], G_down[1] = mlp_bwd(1, dm1)
    dx, dc1, dgain[1][3], dgain[1][4] = post_pre_bwd("xattn1_post_bwd", x4, c1, gains[1][3], gains[1][4], dx, dhm1)
    dhq1, G_q[1], G_kv[1], G_o[1], dmn1 = xattn_bwd(1, dc1)

    dx, dt1, dgain[1][1], dgain[1][2] = post_pre_bwd("rwkv_post_bwd", x3_, t1, gains[1][1], gains[1][2], dx, dhq1)
    G['b_w_o'] = _mm_tn("rwkv_out_dw", og, dt1)[None]
    dog = _mm_nt("rwkv_out_dx", dt1, Wf['b_w_o'][0], F32)
    dy_r, dr_p, dk2_p, dv_p, dg_r, d_gn_g, d_gn_b, d_r_k = _rw_vjp(
        "rwkv_gn_bwd", _f_gn, [y_r, r_, k2, v_, g_r], gn_par, [dog], 1,
        d_rows=[(j, F32) for j in range(5)], d_params=[0, 1, 2], seq=S)
    dr_, ddecay, dk2, dv_, dra, drb = _rwkv_bwd("rwkv_scan_bwd", r_, decay, k2, v_, ra, rb, states, sa_r, last_r, dy_r,
                                                dr_p, dk2_p, dv_p, S)
    dk_, da_r, d_k_k, d_k_a = _rw_vjp("rwkv_kk_bwd", _f_kk, [k_, a_r], [k_k, k_a], [dk2, dra, drb], 3,
                                      d_rows=[(0, F32), (1, F32)], d_params=[0, 1], seq=S)
    G_rkv = [_mm_tn("rwkv_r_dw", xr, dr_), _mm_tn("rwkv_k_dw", xk, dk_), _mm_tn("rwkv_v_dw", xv, dv_)]
    G['b_w_rkv'] = jnp.stack(G_rkv)[None]
    dxr = _mm_nt("rwkv_r_dx", dr_, Wf['b_w_rkv'][0, 0], F32)
    dxk = _mm_nt("rwkv_k_dx", dk_, Wf['b_w_rkv'][0, 1], F32)
    dxv = _mm_nt("rwkv_v_dx", dv_, Wf['b_w_rkv'][0, 2], F32)
    mix_out = _rw_vjp("rwkv_mix_bwd", _f_mix, [hn1], mu + lora, [dxr, dxk, dxv, ddecay, da_r, dg_r], 6,
                      d_rows=[(0, F32)], d_shifts=[(0, F32)], d_params=list(range(14)), shifts=[(0, 1)], seq=S)
    dhn1, dhp1 = mix_out[0], mix_out[1]
    d_mu = mix_out[2:8]
    d_lora = mix_out[8:16]
    dx, dm0, dgain[0][5], dgain[1][0] = post_pre_bwd("mlp0_post_bwd", x2, m0, gains[0][5], gains[1][0], dx, dhn1,
                                                     ct_adds=[(1, dhp1, -1)])
    G['b_mu'] = jnp.concatenate(d_mu, axis=0)[None]
    G['b_w0'], G['b_a0'] = d_lora[0], d_lora[3]
    G['b_w1'] = d_lora[1][:, :shp['b_w1'][2]][None]
    G['b_w2'] = d_lora[2][:shp['b_w2'][1]][None]
    G['b_a1'] = d_lora[4][:, :shp['b_a1'][2]][None]
    G['b_a2'] = d_lora[5][:shp['b_a2'][1]][None]
    G['b_g1'], G['b_g2'] = d_lora[6][None], d_lora[7][None]
    G['b_k_k'], G['b_k_a'], G['b_gn_g'], G['b_gn_b'] = d_k_k, d_k_a, d_gn_g, d_gn_b
    G['b_r_k'] = d_r_k.reshape(P['b_r_k'].shape)

    dhm0, G_up[0], G_down[0] = mlp_bwd(0, dm0)
    dx, dc0, dgain[0][3], dgain[0][4] = post_pre_bwd("xattn0_post_bwd", x1, c0, gains[0][3], gains[0][4], dx, dhm0)
    dhq0, G_q[0], G_kv[0], G_o[0], dmn0 = xattn_bwd(0, dc0)
    G['m_w_up'], G['m_w_down'] = jnp.stack(G_up), jnp.stack(G_down)
    G['c_w_q'], G['c_w_kv'], G['c_w_o'] = jnp.stack(G_q), jnp.stack(G_kv), jnp.stack(G_o)

    dx, dt0, dgain[0][1], dgain[0][2], G['a_b_out'] = post_pre_bwd("lru_post_bwd", x, t0, gains[0][1], gains[0][2],
                                                                  dx, dhq0, bias=(b_out,))
    G['a_w_out'] = _mm_tn("lru_out_dw", hy, dt0)[None]
    dhy = _mm_nt("lru_out_dx", dt0, Wf['a_w_out'][0], F32)
    dh_l, dyb = _rw_vjp("lru_hy_bwd", _f_hy, [h_l, yb], [], [dhy], 1, d_rows=[(0, F32), (1, F32)], seq=S)
    da_l, db_l = _lru_scan_bwd("lru_scan_bwd", a_l, h_l, dh_l, S)
    dconv, dproj_y, d_gate_w, d_gb_r, d_gb_i, G['a_lambda'] = _rw_vjp(
        "lru_gates_bwd", _f_gates, [conv, proj], gate_par, [da_l, db_l, dyb], 3,
        d_rows=[(0, F32), (1, F32)], d_params=[0, 1, 2, 3], seq=S)
    next3 = [(0, -1), (0, -2), (0, -3)]
    dproj, G['a_b_in'], dw0, dw1, dw2, dw3, G['a_conv_b'] = _rw(
        "lru_conv_bwd", _f_conv_bwd, [dconv, proj, dproj_y], [b_u] + cw, outs=[(2 * D, BF16)],
        accs=[(1, 2 * D)] + [(1, D)] * 5, shifts=next3 + [(1, 1), (1, 2), (1, 3)], seq=S)
    G['a_conv_w'] = jnp.concatenate([dw0, dw1, dw2, dw3], axis=0)[None]
    G['a_gate_w'] = d_gate_w.reshape(P['a_gate_w'].shape[:3] + (D // LRU_HEADS, D // LRU_HEADS))
    G['a_gate_b'] = jnp.concatenate([d_gb_r, d_gb_i], axis=0).reshape(1, 2, LRU_HEADS, D // LRU_HEADS)
    G['a_w_in'] = _mm_tn("lru_in_dw", hn0, dproj)[None]
    dhn0 = _mm_nt("lru_in_dx", dproj, Wf['a_w_in'][0], F32)
    grad_x, dgain[0][0] = _rw_vjp("lru_pre_bwd", _f_pre(F32), [x], [gains[0][0]], [dhn0], 1, d_rows=[(0, F32)],
                                  d_params=[0], adds=[(0, dx, 0)], seq=S)
    d_mem_norm, = _rw_vjp("mem_norm_bwd", _f_pre(F32), [mem], [row(Wf['mem_norm'])], [dmn0], 1, d_params=[0],
                          ct_adds=[(0, dmn1, 0)])
    G['mem_norm'] = d_mem_norm.reshape(-1)
    G['ln_gains'] = jnp.stack([jnp.concatenate(dgain[i], axis=0) for i in range(2)])

    sharded = BIG + SMALL
    rows_s = big_rows + small_rows
    tr = max(t for t in range(BF16_ROWS, 161, BF16_ROWS) if rows_s % t == 0)

    pieces = [_blocks_from_full(G[n], shp[n], SHARD_AXIS[n]).reshape(N_DEV, -1, PACK_W) for n in BIG]
    small = [_blocks_from_full(G[n], shp[n], SHARD_AXIS[n]) for n in SMALL] + [jnp.zeros((N_DEV, n_small_pad - n_small), F32)]
    pieces.append(jnp.concatenate(small, axis=1).reshape(N_DEV, small_rows, PACK_W))
    g_blocks = jnp.concatenate(pieces, axis=1)
    n_split = max(q for q in range(1, 10) if rows_s % (q * BF16_ROWS) == 0)
    got = _swap_with_sibling("exchange_grads_sibling", g_blocks, n_split)
    tr_add = max(t for t in range(BF16_ROWS, 641, BF16_ROWS) if rows_s % t == 0)
    parts_s = _exchange_chips("exchange_grads_chips", _add_own_blocks("sum_with_sibling", g_blocks, got, tr_add, BF16))
    n_repl = sum(_numel(shp[n]) for n in REPL)
    n_repl_pad = -(-n_repl // (8 * PACK_W)) * (8 * PACK_W)
    rows_r = n_repl_pad // PACK_W
    parts_r = _all_gather("gather_replicated_grads", _pack_flat([G[n] for n in REPL], n_repl_pad).reshape(rows_r, PACK_W))

    def pack(src, names, total, rows_):
        return _pack_flat([src[n] for n in names], total).reshape(rows_, PACK_W)

    out_s = _adamw("adamw_sharded", parts_s, pack_rows(P), pack_rows(M_), pack_rows(V_), tr)
    out_r = _adamw("adamw_replicated", parts_r, pack(P, REPL, n_repl_pad, rows_r), pack(M_, REPL, n_repl_pad, rows_r),
                   pack(V_, REPL, n_repl_pad, rows_r), rows_r)

    def unpack(packed, names):
        flat = packed.reshape(-1)
        res, o = {}, 0
        for n in names:
            ne = _numel(shp[n])
            res[n] = flat[o:o + ne].reshape(shp[n])
            o += ne
        return res

    results = []
    for j in range(4):
        both = {n: out_s[j][big_row0[n]:big_row0[n] + _numel(shp[n]) // PACK_W].reshape(shp[n]) for n in BIG}
        both.update(unpack(out_s[j][big_rows:], SMALL))
        both.update(unpack(out_r[j], REPL))
        results += [both[n] for n in WEIGHTS]
    return (loss, grad_x.reshape(Bl, S, D), *results)


def kernel(x, mem, ln_gains, mem_norm, a_conv_w, a_conv_b, a_w_in, a_b_in, a_gate_w, a_gate_b, a_lambda, a_w_out, a_b_out, b_mu, b_w_rkv, b_w0, b_w1, b_w2, b_a0, b_a1, b_a2, b_g1, b_g2, b_k_k, b_k_a, b_r_k, b_gn_g, b_gn_b, b_w_o, c_w_q, c_w_kv, c_w_o, m_w_up, m_w_down, loss_target, m_ln_gains, m_mem_norm, m_a_conv_w, m_a_conv_b, m_a_w_in, m_a_b_in, m_a_gate_w, m_a_gate_b, m_a_lambda, m_a_w_out, m_a_b_out, m_b_mu, m_b_w_rkv, m_b_w0, m_b_w1, m_b_w2, m_b_a0, m_b_a1, m_b_a2, m_b_g1, m_b_g2, m_b_k_k, m_b_k_a, m_b_r_k, m_b_gn_g, m_b_gn_b, m_b_w_o, m_c_w_q, m_c_w_kv, m_c_w_o, m_m_w_up, m_m_w_down, v_ln_gains, v_mem_norm, v_a_conv_w, v_a_conv_b, v_a_w_in, v_a_b_in, v_a_gate_w, v_a_gate_b, v_a_lambda, v_a_w_out, v_a_b_out, v_b_mu, v_b_w_rkv, v_b_w0, v_b_w1, v_b_w2, v_b_a0, v_b_a1, v_b_a2, v_b_g1, v_b_g2, v_b_k_k, v_b_k_a, v_b_r_k, v_b_gn_g, v_b_gn_b, v_b_w_o, v_c_w_q, v_c_w_kv, v_c_w_o, v_m_w_up, v_m_w_down):
    weights = (ln_gains, mem_norm, a_conv_w, a_conv_b, a_w_in, a_b_in, a_gate_w, a_gate_b, a_lambda, a_w_out, a_b_out, b_mu, b_w_rkv, b_w0, b_w1, b_w2, b_a0, b_a1, b_a2, b_g1, b_g2, b_k_k, b_k_a, b_r_k, b_gn_g, b_gn_b, b_w_o, c_w_q, c_w_kv, c_w_o, m_w_up, m_w_down)
    moments1 = (m_ln_gains, m_mem_norm, m_a_conv_w, m_a_conv_b, m_a_w_in, m_a_b_in, m_a_gate_w, m_a_gate_b, m_a_lambda, m_a_w_out, m_a_b_out, m_b_mu, m_b_w_rkv, m_b_w0, m_b_w1, m_b_w2, m_b_a0, m_b_a1, m_b_a2, m_b_g1, m_b_g2, m_b_k_k, m_b_k_a, m_b_r_k, m_b_gn_g, m_b_gn_b, m_b_w_o, m_c_w_q, m_c_w_kv, m_c_w_o, m_m_w_up, m_m_w_down)
    moments2 = (v_ln_gains, v_mem_norm, v_a_conv_w, v_a_conv_b, v_a_w_in, v_a_b_in, v_a_gate_w, v_a_gate_b, v_a_lambda, v_a_w_out, v_a_b_out, v_b_mu, v_b_w_rkv, v_b_w0, v_b_w1, v_b_w2, v_b_a0, v_b_a1, v_b_a2, v_b_g1, v_b_g2, v_b_k_k, v_b_k_a, v_b_r_k, v_b_gn_g, v_b_gn_b, v_b_w_o, v_c_w_q, v_c_w_kv, v_c_w_o, v_m_w_up, v_m_w_down)
    return _train_step(x, mem, loss_target, dict(zip(WEIGHTS, weights)), dict(zip(WEIGHTS, moments1)),
                       dict(zip(WEIGHTS, moments2)))
```

```python
import jax
import jax.numpy as jnp
from jax import lax
from jax.experimental import pallas as pl
from jax.experimental.pallas import tpu as pltpu

F32 = jnp.float32
BF16 = jnp.bfloat16
N_DEV = 8
MESH = pl.DeviceIdType.MESH

V7X_VMEM_LIMIT_BYTES = 56 * 1024 * 1024
SUB = 8
HALO_ROWS = SUB
LANES = 128
RWKV_N = 64
HEAD_SHIFT = RWKV_N.bit_length() - 1
RMS_EPS = 1e-6
GN_EPS = 64e-5
LRU_C = 8.0
XATTN_HEADS = 4
LRU_HEADS = 4

ADAM_LR, ADAM_B1, ADAM_B2, ADAM_EPS, ADAM_WD, ADAM_STEP = 0.001, 0.9, 0.999, 1e-8, 0.01, 10

WEIGHTS = ['ln_gains', 'mem_norm', 'a_conv_w', 'a_conv_b', 'a_w_in', 'a_b_in', 'a_gate_w', 'a_gate_b',
           'a_lambda', 'a_w_out', 'a_b_out', 'b_mu', 'b_w_rkv', 'b_w0', 'b_w1', 'b_w2', 'b_a0', 'b_a1',
           'b_a2', 'b_g1', 'b_g2', 'b_k_k', 'b_k_a', 'b_r_k', 'b_gn_g', 'b_gn_b', 'b_w_o', 'c_w_q',
           'c_w_kv', 'c_w_o', 'm_w_up', 'm_w_down']
SHARD_AXIS = {'ln_gains': 2, 'mem_norm': None, 'a_conv_w': 2, 'a_conv_b': None, 'a_w_in': 2, 'a_b_in': None,
              'a_gate_w': 3, 'a_gate_b': 3, 'a_lambda': None, 'a_w_out': 1, 'a_b_out': None, 'b_mu': 2,
              'b_w_rkv': 2, 'b_w0': 1, 'b_w1': 1, 'b_w2': 2, 'b_a0': 1, 'b_a1': 1, 'b_a2': 2, 'b_g1': 1,
              'b_g2': 2, 'b_k_k': 1, 'b_k_a': 1, 'b_r_k': None, 'b_gn_g': 1, 'b_gn_b': 1, 'b_w_o': 1,
              'c_w_q': 1, 'c_w_kv': 2, 'c_w_o': 1, 'm_w_up': 2, 'm_w_down': 1}
BIG = ['a_w_in', 'a_gate_w', 'a_w_out', 'b_w_rkv', 'b_w1', 'b_w2', 'b_a1', 'b_a2', 'b_g1', 'b_g2', 'b_w_o',
       'c_w_q', 'c_w_kv', 'c_w_o', 'm_w_up', 'm_w_down']
SMALL = ['ln_gains', 'a_conv_w', 'a_gate_b', 'b_mu', 'b_w0', 'b_a0', 'b_k_k', 'b_k_a', 'b_gn_g', 'b_gn_b']
REPL = ['mem_norm', 'a_conv_b', 'a_b_in', 'a_lambda', 'a_b_out', 'b_r_k']
PACK_W = 1024
PACK_ROWS = 128
BF16_ROWS = 16


def _cparams(*sem):
    return pltpu.CompilerParams(dimension_semantics=sem, vmem_limit_bytes=V7X_VMEM_LIMIT_BYTES)


def _shift_tile(tile, halo, k, pos, seq, tm):
    if k > 0:
        ext = jnp.concatenate([halo, tile], axis=0)
        r = pltpu.roll(ext, k, 0)[HALO_ROWS:HALO_ROWS + tm]
        return jnp.where(pos >= k, r, 0.0)
    kk = -k
    ext = jnp.concatenate([tile, halo], axis=0)
    r = pltpu.roll(ext, tm + HALO_ROWS - kk, 0)[0:tm]
    return jnp.where(pos + kk < seq, r, 0.0)


def _row_specs(rows, shifts, params, per_b, seq, tm):
    T = rows[0].shape[0]
    ns = seq // tm
    specs = [pl.BlockSpec((tm, r.shape[1]), lambda b, i: (b * ns + i, 0)) for r in rows]
    halo_keys = []
    for idx, k in shifts:
        key = (idx, k > 0)
        if key not in halo_keys:
            halo_keys.append(key)
    per8 = tm // HALO_ROWS
    last8 = T // HALO_ROWS - 1
    for idx, prev in halo_keys:
        c = rows[idx].shape[1]
        if prev:
            specs.append(pl.BlockSpec((HALO_ROWS, c), lambda b, i: (jnp.maximum((b * ns + i) * per8 - 1, 0), 0)))
        else:
            specs.append(pl.BlockSpec((HALO_ROWS, c), lambda b, i: (jnp.minimum((b * ns + i + 1) * per8, last8), 0)))
    for j, p in enumerate(params):
        if j in per_b:
            specs.append(pl.BlockSpec((None,) + p.shape[1:], lambda b, i: (b, 0, 0)))
        else:
            specs.append(pl.BlockSpec(p.shape, lambda b, i, nd=p.ndim: (0,) * nd))
    return specs, halo_keys


def _load_tiles(refs, rows, shifts, halo_keys, params, seq, tm):
    nr, nh = len(rows), len(halo_keys)
    i = pl.program_id(1)
    pos = i * tm + lax.broadcasted_iota(jnp.int32, (tm, 1), 0)
    row_t = [r[...] for r in refs[:nr]]
    halo_t = {key: refs[nr + j][...] for j, key in enumerate(halo_keys)}
    sh_t = [_shift_tile(row_t[idx], halo_t[(idx, k > 0)], k, pos, seq, tm) for idx, k in shifts]
    par_t = [p[...] for p in refs[nr + nh:nr + nh + len(params)]]
    return pos, row_t, sh_t, par_t


ROW_TILES = (512, 256)
ROW_TILE_BUDGET_BYTES = 16 * 1024 * 1024


def _row_tile(seq, row_bytes):
    for tm in ROW_TILES:
        if 2 * tm * row_bytes <= ROW_TILE_BUDGET_BYTES:
            return min(tm, seq)
    return min(ROW_TILES[-1], seq)


def _rw(name, fn, rows, params=(), outs=(), accs=(), shifts=(), seq=None, per_b=()):
    rows, params = list(rows), list(params)
    T = rows[0].shape[0]
    seq = seq or T
    tm = _row_tile(seq, sum(r.shape[1] * r.dtype.itemsize for r in rows) + sum(c * jnp.dtype(dt).itemsize for c, dt in outs))
    nb, ns = T // seq, seq // tm
    in_specs, halo_keys = _row_specs(rows, shifts, params, per_b, seq, tm)
    n_in, no = len(in_specs), len(outs)

    def body(*refs):
        pos, row_t, sh_t, par_t = _load_tiles(refs, rows, shifts, halo_keys, params, seq, tm)
        res = fn(pos, *row_t, *sh_t, *par_t)
        out_refs = refs[n_in:n_in + no]
        acc_refs = refs[n_in + no:]
        for o_ref, v in zip(out_refs, res[:no]):
            o_ref[...] = v.astype(o_ref.dtype)
        if acc_refs:
            @pl.when((pl.program_id(0) == 0) & (pl.program_id(1) == 0))
            def _():
                for a_ref in acc_refs:
                    a_ref[...] = jnp.zeros_like(a_ref)
            for a_ref, v in zip(acc_refs, res[no:]):
                a_ref[...] += v

    out_shape = [jax.ShapeDtypeStruct((T, c), dt) for c, dt in outs]
    out_shape += [jax.ShapeDtypeStruct(s, F32) for s in accs]
    out_specs = [pl.BlockSpec((tm, c), lambda b, i: (b * ns + i, 0)) for c, _ in outs]
    out_specs += [pl.BlockSpec(s, lambda b, i: (0, 0)) for s in accs]
    return pl.pallas_call(body, name=name, grid=(nb, ns), in_specs=in_specs, out_specs=out_specs,
                          out_shape=out_shape, compiler_params=_cparams("arbitrary", "arbitrary"))(*rows, *[
                              rows[idx] for idx, _ in halo_keys], *params)


def _rw_vjp(name, fn, rows, params, cts, n_out, d_rows=(), d_shifts=(), d_params=(), adds=(), ct_adds=(),
            shifts=(), seq=None, per_b=()):
    rows, params, cts = list(rows), list(params), list(cts)
    T = rows[0].shape[0]
    seq = seq or T
    row_bytes = sum(r.shape[1] * r.dtype.itemsize for r in rows + cts + [a for _, a, _ in list(adds) + list(ct_adds)])
    row_bytes += sum(rows[j].shape[1] * jnp.dtype(dt).itemsize for j, dt in d_rows)
    row_bytes += sum(rows[shifts[j][0]].shape[1] * jnp.dtype(dt).itemsize for j, dt in d_shifts)
    tm = _row_tile(seq, row_bytes)
    nb, ns = T // seq, seq // tm
    n_ga = len(adds)
    adds = list(adds) + list(ct_adds)
    add_rows = [a for _, a, _ in adds]
    add_shifts = [(j, k) for j, (_, _, k) in enumerate(adds) if k != 0]
    all_rows = rows + cts + add_rows
    off_ct, off_add = len(rows), len(rows) + len(cts)
    all_shifts = list(shifts) + [(off_add + j, k) for j, k in add_shifts]
    in_specs, halo_keys = _row_specs(all_rows, all_shifts, params, per_b, seq, tm)
    n_in = len(in_specs)
    n_dr, n_ds = len(d_rows), len(d_shifts)

    def body(*refs):
        pos, row_t, sh_t, par_t = _load_tiles(refs, all_rows, all_shifts, halo_keys, params, seq, tm)
        fn_rows, ct_t, add_t = row_t[:off_ct], row_t[off_ct:off_add], row_t[off_add:]
        fn_sh, add_sh = sh_t[:len(shifts)], sh_t[len(shifts):]
        diff = [fn_rows[j] for j, _ in d_rows] + [fn_sh[j] for j, _ in d_shifts] + [par_t[j] for j in d_params]

        def f(*d):
            r2, s2, p2 = list(fn_rows), list(fn_sh), list(par_t)
            for (j, _), v in zip(d_rows, d[:n_dr]):
                r2[j] = v
            for (j, _), v in zip(d_shifts, d[n_dr:n_dr + n_ds]):
                s2[j] = v
            for j, v in zip(d_params, d[n_dr + n_ds:]):
                p2[j] = v
            return tuple(fn(pos, *r2, *s2, *p2)[:n_out])

        sh_iter = iter(add_sh)
        add_v = [add_t[j] if k == 0 else next(sh_iter) for j, (_, _, k) in enumerate(adds)]
        ct_t = list(ct_t)
        for (tgt, _, _), v in zip(adds[n_ga:], add_v[n_ga:]):
            ct_t[tgt] = ct_t[tgt] + v
        outs_v, vjp_fn = jax.vjp(f, *diff)
        grads = list(vjp_fn(tuple(c.astype(o.dtype) for c, o in zip(ct_t, outs_v))))
        for (tgt, _, _), v in zip(adds[:n_ga], add_v[:n_ga]):
            grads[tgt] = grads[tgt] + v
        out_refs = refs[n_in:]
        for o_ref, g in zip(out_refs[:n_dr + n_ds], grads[:n_dr + n_ds]):
            o_ref[...] = g.astype(o_ref.dtype)
        for o_ref, g, j in zip(out_refs[n_dr + n_ds:], grads[n_dr + n_ds:], d_params):
            first = (pl.program_id(1) == 0) if j in per_b else ((pl.program_id(0) == 0) & (pl.program_id(1) == 0))

            @pl.when(first)
            def _(o_ref=o_ref):
                o_ref[...] = jnp.zeros_like(o_ref)
            o_ref[...] += g.astype(F32)

    out_shape, out_specs = [], []
    for j, dt in d_rows:
        out_shape.append(jax.ShapeDtypeStruct(rows[j].shape, dt))
        out_specs.append(pl.BlockSpec((tm, rows[j].shape[1]), lambda b, i: (b * ns + i, 0)))
    for j, dt in d_shifts:
        src = rows[shifts[j][0]]
        out_shape.append(jax.ShapeDtypeStruct(src.shape, dt))
        out_specs.append(pl.BlockSpec((tm, src.shape[1]), lambda b, i: (b * ns + i, 0)))
    for j in d_params:
        p = params[j]
        out_shape.append(jax.ShapeDtypeStruct(p.shape, F32))
        if j in per_b:
            out_specs.append(pl.BlockSpec((None,) + p.shape[1:], lambda b, i: (b, 0, 0)))
        else:
            out_specs.append(pl.BlockSpec(p.shape, lambda b, i, nd=p.ndim: (0,) * nd))
    return pl.pallas_call(body, name=name, grid=(nb, ns), in_specs=in_specs, out_specs=out_specs,
                          out_shape=out_shape, compiler_params=_cparams("arbitrary", "arbitrary"))(*all_rows, *[
                              all_rows[idx] for idx, _ in halo_keys], *params)


MM_TILE = 1024
MM_DEEP = 2048


def _mm_nn(name, x, w, out_dtype):
    M, K = x.shape
    N = w.shape[1]
    tm, tn = min(MM_TILE if K <= MM_DEEP else MM_TILE // 2, M), min(MM_TILE, N)

    def body(x_ref, w_ref, o_ref):
        o_ref[...] = jnp.dot(x_ref[...].astype(BF16), w_ref[...].astype(BF16),
                             preferred_element_type=F32).astype(o_ref.dtype)

    return pl.pallas_call(body, name=name, grid=(M // tm, N // tn),
                          in_specs=[pl.BlockSpec((tm, K), lambda i, j: (i, 0)), pl.BlockSpec((K, tn), lambda i, j: (0, j))],
                          out_specs=pl.BlockSpec((tm, tn), lambda i, j: (i, j)),
                          out_shape=jax.ShapeDtypeStruct((M, N), out_dtype),
                          compiler_params=_cparams("parallel", "parallel"))(x, w)


def _mm_up_act(name, x, w):
    M, K = x.shape
    N = w.shape[1]
    tm, tn = min(MM_TILE if K <= MM_DEEP else MM_TILE // 2, M), min(MM_TILE, N)

    def body(x_ref, w_ref, up_ref, act_ref):
        up = jnp.dot(x_ref[...].astype(BF16), w_ref[...].astype(BF16), preferred_element_type=F32)
        up_ref[...] = up.astype(BF16)
        act_ref[...] = jnp.square(jnp.maximum(up, 0.0)).astype(BF16)

    out = pl.BlockSpec((tm, tn), lambda i, j: (i, j))
    return pl.pallas_call(body, name=name, grid=(M // tm, N // tn),
                          in_specs=[pl.BlockSpec((tm, K), lambda i, j: (i, 0)), pl.BlockSpec((K, tn), lambda i, j: (0, j))],
                          out_specs=[out, out], out_shape=[jax.ShapeDtypeStruct((M, N), BF16)] * 2,
                          compiler_params=_cparams("parallel", "parallel"))(x, w)


def _mm_nt_act_bwd(name, dy, w, up):
    M, N = dy.shape
    K = w.shape[0]
    tm, tk = min(MM_TILE if N <= MM_DEEP else MM_TILE // 2, M), min(MM_TILE, K)

    def body(dy_ref, w_ref, up_ref, o_ref):
        dact = lax.dot_general(dy_ref[...].astype(BF16), w_ref[...].astype(BF16), (((1,), (1,)), ((), ())),
                               preferred_element_type=F32)
        o_ref[...] = (dact * (2.0 * jnp.maximum(up_ref[...].astype(F32), 0.0))).astype(o_ref.dtype)

    out = pl.BlockSpec((tm, tk), lambda i, j: (i, j))
    return pl.pallas_call(body, name=name, grid=(M // tm, K // tk),
                          in_specs=[pl.BlockSpec((tm, N), lambda i, j: (i, 0)), pl.BlockSpec((tk, N), lambda i, j: (j, 0)), out],
                          out_specs=out, out_shape=jax.ShapeDtypeStruct((M, K), BF16),
                          compiler_params=_cparams("parallel", "parallel"))(dy, w, up)


def _mm_nt(name, dy, w, out_dtype):
    M, N = dy.shape
    K = w.shape[0]
    tm, tk = min(MM_TILE if N <= MM_DEEP else MM_TILE // 2, M), min(MM_TILE, K)

    def body(dy_ref, w_ref, o_ref):
        o_ref[...] = lax.dot_general(dy_ref[...].astype(BF16), w_ref[...].astype(BF16), (((1,), (1,)), ((), ())),
                                     preferred_element_type=F32).astype(o_ref.dtype)

    return pl.pallas_call(body, name=name, grid=(M // tm, K // tk),
                          in_specs=[pl.BlockSpec((tm, N), lambda i, j: (i, 0)), pl.BlockSpec((tk, N), lambda i, j: (j, 0))],
                          out_specs=pl.BlockSpec((tm, tk), lambda i, j: (i, j)),
                          out_shape=jax.ShapeDtypeStruct((M, K), out_dtype),
                          compiler_params=_cparams("parallel", "parallel"))(dy, w)


def _mm_tn(name, x, dy):
    M, K = x.shape
    N = dy.shape[1]
    tm, tk, tn = min(MM_TILE, M), min(MM_TILE, K), min(MM_TILE, N)

    def body(x_ref, dy_ref, o_ref):
        @pl.when(pl.program_id(2) == 0)
        def _():
            o_ref[...] = jnp.zeros_like(o_ref)
        o_ref[...] += lax.dot_general(x_ref[...].astype(BF16), dy_ref[...].astype(BF16), (((0,), (0,)), ((), ())),
                                      preferred_element_type=F32)

    return pl.pallas_call(body, name=name, grid=(K // tk, N // tn, M // tm),
                          in_specs=[pl.BlockSpec((tm, tk), lambda i, j, m: (m, i)), pl.BlockSpec((tm, tn), lambda i, j, m: (m, j))],
                          out_specs=pl.BlockSpec((tk, tn), lambda i, j, m: (i, j)),
                          out_shape=jax.ShapeDtypeStruct((K, N), F32),
                          compiler_params=_cparams("parallel", "parallel", "arbitrary"))(x, dy)


def _lru_scan(name, a, b, seq, ct=128):
    T, C = a.shape

    def body(a_ref, b_ref, h_ref):
        A, Bv = a_ref[...], b_ref[...]
        row = lax.broadcasted_iota(jnp.int32, (seq, 1), 0)
        k = 1
        while k < seq:
            keep = row >= k
            Bv = A * jnp.where(keep, pltpu.roll(Bv, k, 0), 0.0) + Bv
            A = A * jnp.where(keep, pltpu.roll(A, k, 0), 1.0)
            k *= 2
        h_ref[...] = Bv

    spec = pl.BlockSpec((seq, ct), lambda b_, j: (b_, j))
    return pl.pallas_call(body, name=name, grid=(T // seq, C // ct), in_specs=[spec, spec], out_specs=spec,
                          out_shape=jax.ShapeDtypeStruct((T, C), F32),
                          compiler_params=_cparams("parallel", "parallel"))(a, b)


def _lru_scan_bwd(name, a, h, dh, seq, ct=128):
    T, C = a.shape

    def body(a_ref, h_ref, dh_ref, da_ref, db_ref):
        row = lax.broadcasted_iota(jnp.int32, (seq, 1), 0)
        Cf = jnp.where(row < seq - 1, pltpu.roll(a_ref[...], seq - 1, 0), 0.0)
        G = dh_ref[...]
        k = 1
        while k < seq:
            keep = row + k < seq
            G = Cf * jnp.where(keep, pltpu.roll(G, seq - k, 0), 0.0) + G
            Cf = Cf * jnp.where(keep, pltpu.roll(Cf, seq - k, 0), 1.0)
            k *= 2
        db_ref[...] = G
        da_ref[...] = G * jnp.where(row >= 1, pltpu.roll(h_ref[...], 1, 0), 0.0)

    spec = pl.BlockSpec((seq, ct), lambda b_, j: (b_, j))
    return pl.pallas_call(body, name=name, grid=(T // seq, C // ct), in_specs=[spec] * 3, out_specs=[spec] * 2,
                          out_shape=[jax.ShapeDtypeStruct((T, C), F32)] * 2,
                          compiler_params=_cparams("parallel", "parallel"))(a, h, dh)


def _pair_consts():
    jj = lax.broadcasted_iota(jnp.int32, (RWKV_N, LANES), 0)
    ll = lax.broadcasted_iota(jnp.int32, (RWKV_N, LANES), 1)
    pick = ((ll & (RWKV_N - 1)) == jj).astype(BF16)
    l0 = lax.broadcasted_iota(jnp.int32, (LANES, LANES), 0)
    l1 = lax.broadcasted_iota(jnp.int32, (LANES, LANES), 1)
    same = ((l0 >> HEAD_SHIFT) == (l1 >> HEAD_SHIFT)).astype(BF16)
    return pick, same


N_COL = 5


def _cols_raw(rows, pick, same):
    lhs = [pick * jnp.broadcast_to(x, (RWKV_N, LANES)).astype(BF16) for x in rows]
    out = jnp.dot(jnp.concatenate(lhs, axis=0), same, preferred_element_type=F32)
    return tuple(out[i * RWKV_N:(i + 1) * RWKV_N] for i in range(len(rows)))


def _cols_adjoint(g):
    n = len(g)
    g = list(g) + [jnp.zeros_like(g[0])] * (n % 2)
    head = ((lax.broadcasted_iota(jnp.int32, (SUB, LANES), 1) >> HEAD_SHIFT)
            == lax.broadcasted_iota(jnp.int32, (SUB, LANES), 0)).astype(BF16)
    low = lax.broadcasted_iota(jnp.int32, (1, LANES), 1) < RWKV_N
    sums = lax.dot_general(head, jnp.concatenate([t.astype(BF16) for t in g], axis=0),
                           (((1,), (1,)), ((), ())), preferred_element_type=F32)
    rows = []
    for i in range(0, n, 2):
        blk = sums[:, i * RWKV_N:(i + 2) * RWKV_N]
        swapped = pltpu.roll(blk, RWKV_N, 1)
        rows.append(jnp.where(low, blk[0:1], swapped[1:2]))
        rows.append(jnp.where(low, swapped[0:1], blk[1:2]))
    return tuple(rows[:n])


def _rwkv_cols(rows, pick, same):
    flat = []
    for r, w, k, v, a, b in rows:
        flat += [a, 1.0 - w, b, k, r]
    tiles = _cols_raw(tuple(flat), pick, same)
    return [tiles[N_COL * p:N_COL * (p + 1)] for p in range(len(rows))]


def _rwkv_step(s0, tiles, v):
    ca, ce, cb, ck, cr = tiles
    sa = jnp.sum(s0 * ca, axis=0, keepdims=True)
    s1 = s0 - s0 * ce + cb * sa + ck * v
    return s1, jnp.sum(s1 * cr, axis=0, keepdims=True), sa


def _rwkv_step_bwd(s0, s1, sa, tiles, v, dy, ds1):
    ca, ce, cb, ck, cr = tiles
    ds1 = ds1 + cr * dy
    dsa = jnp.sum(ds1 * cb, axis=0, keepdims=True)
    dv = jnp.sum(ds1 * ck, axis=0, keepdims=True)
    ds0 = ds1 - ds1 * ce + ca * dsa
    return ds0, (s0 * dsa, ds1 * s0, ds1 * sa, ds1 * v, s1 * dy), dv


def _rwkv_fwd(name, r, w, k, v, a, b, seq, tc=16, G=8, NB=4):
    T, C = r.shape
    B = T // seq
    tc = min(tc, seq)
    NB = NB if B % NB == 0 else 1
    NP = C // LANES
    nc = seq // tc
    row = pl.BlockSpec((NB, tc, LANES * G), lambda b_, g, c: (b_, c, g))
    st = pl.BlockSpec((NB, tc, G, RWKV_N, LANES), lambda b_, g, c: (b_, c, g, 0, 0))
    last = pl.BlockSpec((NB, G, RWKV_N, LANES), lambda b_, g, c: (b_, g, 0, 0))
    chains = [(q, p) for q in range(NB) for p in range(G)]

    def body(r_ref, w_ref, k_ref, v_ref, a_ref, b_ref, y_ref, st_ref, sa_ref, last_ref, s_scr):
        @pl.when(pl.program_id(2) == 0)
        def _():
            s_scr[...] = jnp.zeros_like(s_scr)
        pick, same = _pair_consts()
        rid = lax.broadcasted_iota(jnp.int32, (SUB, LANES), 0)

        def group(t8, carry):
            base = pl.multiple_of(t8 * SUB, SUB)
            rows8 = pl.ds(base, SUB)
            lanes = [pl.ds(LANES * p, LANES) for p in range(G)]
            blk = [[x[q, rows8, lanes[p]] for x in (r_ref, w_ref, k_ref, v_ref, a_ref, b_ref)] for q, p in chains]
            s = [s_scr[q, p] for q, p in chains]
            ys = [jnp.zeros((SUB, LANES), F32) for _ in chains]
            sas = [jnp.zeros((SUB, LANES), F32) for _ in chains]
            for i in range(SUB):
                tiles = _rwkv_cols([tuple(b_[i:i + 1] for b_ in blk[n_]) for n_ in range(len(chains))], pick, same)
                for n_, (q, p) in enumerate(chains):
                    st_ref[q, base + i, p] = s[n_]
                    s[n_], y, sa = _rwkv_step(s[n_], tiles[n_], blk[n_][3][i:i + 1])
                    ys[n_] = jnp.where(rid == i, y, ys[n_])
                    sas[n_] = jnp.where(rid == i, sa, sas[n_])
            for n_, (q, p) in enumerate(chains):
                s_scr[q, p] = s[n_]
                y_ref[q, rows8, lanes[p]] = ys[n_]
                sa_ref[q, rows8, lanes[p]] = sas[n_]
            return carry

        lax.fori_loop(0, tc // SUB, group, 0)

        @pl.when(pl.program_id(2) == nc - 1)
        def _():
            last_ref[...] = s_scr[...]

    seq3 = lambda x: x.reshape(B, seq, C)
    y, states, sa, final = pl.pallas_call(body,
        name=name, grid=(B // NB, NP // G, nc), in_specs=[row] * 6, out_specs=[row, st, row, last],
        out_shape=[jax.ShapeDtypeStruct((B, seq, C), F32), jax.ShapeDtypeStruct((B, seq, NP, RWKV_N, LANES), F32),
                   jax.ShapeDtypeStruct((B, seq, C), F32), jax.ShapeDtypeStruct((B, NP, RWKV_N, LANES), F32)],
        scratch_shapes=[pltpu.VMEM((NB, G, RWKV_N, LANES), F32)],
        compiler_params=_cparams("parallel", "parallel", "arbitrary"))(*[seq3(x) for x in (r, w, k, v, a, b)])
    return y.reshape(T, C), states.reshape(T, NP, RWKV_N, LANES), sa.reshape(T, C), final


def _rwkv_bwd(name, r, w, k, v, a, b, states, sa, last, dy, dr_p, dk_p, dv_p, seq, tc=8, G=8, NB=4):
    T, C = r.shape
    B = T // seq
    tc = min(tc, seq)
    NB = NB if B % NB == 0 else 1
    NP = C // LANES
    nc = seq // tc
    row = pl.BlockSpec((NB, tc, LANES * G), lambda b_, g, c: (b_, nc - 1 - c, g))
    st = pl.BlockSpec((NB, tc, G, RWKV_N, LANES), lambda b_, g, c: (b_, nc - 1 - c, g, 0, 0))
    last_spec = pl.BlockSpec((NB, G, RWKV_N, LANES), lambda b_, g, c: (b_, g, 0, 0))
    chains = [(q, p) for q in range(NB) for p in range(G)]

    def body(r_ref, w_ref, k_ref, v_ref, a_ref, b_ref, st_ref, sa_ref, last_ref, dy_ref, drp_ref, dkp_ref, dvp_ref,
             dr_ref, dw_ref, dk_ref, dv_ref, da_ref, db_ref, ds_scr, s1_scr):
        @pl.when(pl.program_id(2) == 0)
        def _():
            ds_scr[...] = jnp.zeros_like(ds_scr)
            s1_scr[...] = last_ref[...]
        pick, same = _pair_consts()
        rid = lax.broadcasted_iota(jnp.int32, (SUB, LANES), 0)
        out_refs = (da_ref, dw_ref, db_ref, dk_ref, dr_ref)

        def group(n, carry):
            base = pl.multiple_of((tc // SUB - 1 - n) * SUB, SUB)
            rows8 = pl.ds(base, SUB)
            lanes = [pl.ds(LANES * p, LANES) for p in range(G)]
            blk = [[x[q, rows8, lanes[p]] for x in (r_ref, w_ref, k_ref, v_ref, a_ref, b_ref)] for q, p in chains]
            dys = [dy_ref[q, rows8, lanes[p]] for q, p in chains]
            sas = [sa_ref[q, rows8, lanes[p]] for q, p in chains]
            ds = [ds_scr[q, p] for q, p in chains]
            s1 = [s1_scr[q, p] for q, p in chains]
            zero = jnp.zeros((SUB, LANES), F32)
            gs = [[zero, zero, zero, dkp_ref[q, rows8, lanes[p]], drp_ref[q, rows8, lanes[p]]] for q, p in chains]
            gv = [dvp_ref[q, rows8, lanes[p]] for q, p in chains]
            steps = tuple(reversed(range(SUB)))
            tiles_of = [_rwkv_cols([tuple(b_[i:i + 1] for b_ in blk[n_]) for n_ in range(len(chains))], pick, same)
                        for i in steps]
            adj_of = []
            for i, tiles in zip(steps, tiles_of):
                adj = []
                for n_, (q, p) in enumerate(chains):
                    s0 = st_ref[q, base + i, p]
                    ds[n_], g_tiles, dv = _rwkv_step_bwd(s0, s1[n_], sas[n_][i:i + 1], tiles[n_], blk[n_][3][i:i + 1],
                                                         dys[n_][i:i + 1], ds[n_])
                    s1[n_] = s0
                    adj += list(g_tiles)
                    gv[n_] = gv[n_] + jnp.where(rid == i, dv, 0.0)
                adj_of.append(adj)
            for i, adj in zip(steps, adj_of):
                rows = _cols_adjoint(adj)
                for n_ in range(len(chains)):
                    for j in range(N_COL):
                        gs[n_][j] = gs[n_][j] + jnp.where(rid == i, rows[N_COL * n_ + j], 0.0)
            for n_, (q, p) in enumerate(chains):
                ds_scr[q, p] = ds[n_]
                s1_scr[q, p] = s1[n_]
                dv_ref[q, rows8, lanes[p]] = gv[n_]
                for j in range(N_COL):
                    out_refs[j][q, rows8, lanes[p]] = gs[n_][j]
            return carry

        lax.fori_loop(0, tc // SUB, group, 0)

    seq3 = lambda x: x.reshape(B, seq, C)
    outs = pl.pallas_call(body, name=name, grid=(B // NB, NP // G, nc),
                          in_specs=[row] * 6 + [st, row, last_spec] + [row] * 4, out_specs=[row] * 6,
                          out_shape=[jax.ShapeDtypeStruct((B, seq, C), F32)] * 6,
                          scratch_shapes=[pltpu.VMEM((NB, G, RWKV_N, LANES), F32)] * 2,
                          compiler_params=_cparams("parallel", "parallel", "arbitrary"))(
                              *[seq3(x) for x in (r, w, k, v, a, b)], states.reshape(B, seq, NP, RWKV_N, LANES), seq3(sa),
                              last, *[seq3(x) for x in (dy, dr_p, dk_p, dv_p)])
    return [o.reshape(T, C) for o in outs]


def _all_gather(name, shard):
    def body(x_ref, o_ref, send_sems, recv_sems, local_sem):
        x, y, c = lax.axis_index("x"), lax.axis_index("y"), lax.axis_index("c")
        me, sibling = (x, y, c), (x, y, 1 - c)
        chips = [(1 - x, y), (x, 1 - y), (1 - x, 1 - y)]

        def copy(k, block, to, src=None):
            dst = o_ref.at[4 * block[0] + 2 * block[1] + block[2]]
            return pltpu.make_async_remote_copy(src_ref=dst if src is None else src, dst_ref=dst, send_sem=send_sems.at[k],
                                                recv_sem=recv_sems.at[k], device_id=to, device_id_type=MESH)

        mine = pltpu.make_async_copy(x_ref, o_ref.at[4 * x + 2 * y + c], local_sem)
        mine.start()
        first = [copy(0, me, sibling, src=x_ref)] + [copy(1 + j, me, (*chip, c), src=x_ref) for j, chip in enumerate(chips)]
        for cp in first:
            cp.start()
        passed = [copy(4 + j, (*chip, c), sibling) for j, chip in enumerate(chips)]
        for j, chip in enumerate(chips):
            copy(1 + j, (*chip, c), me).wait_recv()
            passed[j].start()
        copy(0, sibling, me).wait_recv()
        for j, chip in enumerate(chips):
            copy(4 + j, (*chip, 1 - c), me).wait_recv()
        for cp in first + passed:
            cp.wait_send()
        mine.wait()

    hbm = pl.BlockSpec(memory_space=pltpu.HBM)
    return pl.pallas_call(body, name=name, in_specs=[hbm], out_specs=hbm,
                          out_shape=jax.ShapeDtypeStruct((N_DEV,) + shard.shape, shard.dtype),
                          scratch_shapes=[pltpu.SemaphoreType.DMA((N_DEV - 1,)), pltpu.SemaphoreType.DMA((N_DEV - 1,)),
                                          pltpu.SemaphoreType.DMA])(shard)


N_CHIP = 4


def _swap_with_sibling(name, blocks, n_split):
    _, R, W = blocks.shape
    N = N_CHIP
    rq = R // n_split

    def body(x_ref, got_ref, send_sems, recv_sems):
        x, y, c = lax.axis_index("x"), lax.axis_index("y"), lax.axis_index("c")
        copies = []
        for j in range(N):
            for q in range(n_split):
                rows = pl.ds(q * rq, rq)
                cp = pltpu.make_async_remote_copy(src_ref=x_ref.at[2 * j + 1 - c, rows], dst_ref=got_ref.at[j, rows],
                                                  send_sem=send_sems.at[j * n_split + q], recv_sem=recv_sems.at[j * n_split + q],
                                                  device_id=(x, y, 1 - c), device_id_type=MESH)
                cp.start()
                copies.append(cp)
        for cp in copies:
            cp.wait()

    hbm = pl.BlockSpec(memory_space=pltpu.HBM)
    return pl.pallas_call(body, name=name, in_specs=[hbm], out_specs=hbm,
                          out_shape=jax.ShapeDtypeStruct((N, R, W), blocks.dtype),
                          scratch_shapes=[pltpu.SemaphoreType.DMA((N * n_split,)), pltpu.SemaphoreType.DMA((N * n_split,))])(blocks)


def _add_own_blocks(name, blocks, got, tr, out_dtype):
    N, R, W = got.shape

    def body(c_ref, a_ref, b_ref, o_ref):
        o_ref[...] = (a_ref[...] + b_ref[...]).astype(o_ref.dtype)

    grid_spec = pltpu.PrefetchScalarGridSpec(
        num_scalar_prefetch=1, grid=(N, R // tr),
        in_specs=[pl.BlockSpec((None, tr, W), lambda n, i, c: (2 * n + c[0], i, 0)),
                  pl.BlockSpec((None, tr, W), lambda n, i, c: (n, i, 0))],
        out_specs=pl.BlockSpec((None, tr, W), lambda n, i, c: (n, i, 0)))
    core = lax.axis_index("c").astype(jnp.int32).reshape(1)
    return pl.pallas_call(body, name=name, grid_spec=grid_spec, out_shape=jax.ShapeDtypeStruct(got.shape, out_dtype),
                          compiler_params=_cparams("parallel", "parallel"))(core, blocks, got)


def _exchange_chips(name, blocks):
    def body(z_ref, o_ref, send_sems, recv_sems, local_sem):
        x, y, c = lax.axis_index("x"), lax.axis_index("y"), lax.axis_index("c")
        chip = 2 * x + y
        mine = pltpu.make_async_copy(z_ref.at[chip], o_ref.at[chip], local_sem)
        mine.start()
        copies = []
        for k in range(1, N_CHIP):
            to = (x ^ (k >> 1), y ^ (k & 1), c)
            cp = pltpu.make_async_remote_copy(src_ref=z_ref.at[chip ^ k], dst_ref=o_ref.at[chip], send_sem=send_sems.at[k - 1],
                                              recv_sem=recv_sems.at[k - 1], device_id=to, device_id_type=MESH)
            cp.start()
            copies.append(cp)
        for k in range(1, N_CHIP):
            pltpu.make_async_remote_copy(src_ref=z_ref.at[chip], dst_ref=o_ref.at[chip ^ k], send_sem=send_sems.at[k - 1],
                                         recv_sem=recv_sems.at[k - 1], device_id=(x ^ (k >> 1), y ^ (k & 1), c),
                                         device_id_type=MESH).wait_recv()
        for cp in copies:
            cp.wait_send()
        mine.wait()

    hbm = pl.BlockSpec(memory_space=pltpu.HBM)
    return pl.pallas_call(body, name=name, in_specs=[hbm], out_specs=hbm,
                          out_shape=jax.ShapeDtypeStruct(blocks.shape, blocks.dtype),
                          scratch_shapes=[pltpu.SemaphoreType.DMA((N_CHIP - 1,)), pltpu.SemaphoreType.DMA((N_CHIP - 1,)),
                                          pltpu.SemaphoreType.DMA])(blocks)


def _adamw(name, parts, w, m, v, tr):
    R, W = w.shape
    n_parts = parts.shape[0]

    def body(p_ref, w_ref, m_ref, v_ref, g_ref, d_ref, nm_ref, nv_ref):
        g = p_ref[0].astype(F32)
        for s in range(1, n_parts):
            g = g + p_ref[s].astype(F32)
        nm = ADAM_B1 * m_ref[...] + (1.0 - ADAM_B1) * g
        nv = ADAM_B2 * v_ref[...] + (1.0 - ADAM_B2) * (g * g)
        m_hat = nm / (1.0 - ADAM_B1 ** ADAM_STEP)
        v_hat = nv / (1.0 - ADAM_B2 ** ADAM_STEP)
        g_ref[...] = g
        d_ref[...] = -ADAM_LR * (m_hat / (jnp.sqrt(v_hat) + ADAM_EPS) + ADAM_WD * w_ref[...])
        nm_ref[...] = nm
        nv_ref[...] = nv

    spec = pl.BlockSpec((tr, W), lambda i: (i, 0))
    return pl.pallas_call(body, name=name, grid=(R // tr,),
                          in_specs=[pl.BlockSpec((n_parts, tr, W), lambda i: (0, i, 0)), spec, spec, spec],
                          out_specs=[spec] * 4, out_shape=[jax.ShapeDtypeStruct((R, W), F32)] * 4,
                          compiler_params=_cparams("parallel"))(parts, w, m, v)


def _rms(x, g):
    return x * lax.rsqrt(jnp.mean(x * x, axis=-1, keepdims=True) + RMS_EPS) * g


def _softplus(z):
    return jnp.maximum(z, 0.0) + jnp.log(1.0 + jnp.exp(-jnp.abs(z)))


def _neg_expm1(z):
    series = -z * (1.0 + z * (0.5 + z * (1.0 / 6.0)))
    return jnp.where(z > -1e-3, series, 1.0 - jnp.exp(z))


def _gelu_tanh(x):
    return 0.5 * x * (1.0 + jnp.tanh(0.7978845608028654 * (x + 0.044715 * x * x * x)))


def _bdot(x, w):
    return jnp.dot(x.astype(BF16), w.astype(BF16), preferred_element_type=F32)


def _seg_consts(C):
    c0 = lax.broadcasted_iota(jnp.int32, (C, LANES), 0)
    h1 = lax.broadcasted_iota(jnp.int32, (C, LANES), 1)
    red = ((c0 >> HEAD_SHIFT) == h1).astype(BF16)
    h0 = lax.broadcasted_iota(jnp.int32, (LANES, C), 0)
    c1 = lax.broadcasted_iota(jnp.int32, (LANES, C), 1)
    exp = ((c1 >> HEAD_SHIFT) == h0).astype(BF16)
    return red, exp


def _split_dot(x, m):
    hi = x.astype(BF16)
    lo = (x - hi.astype(F32)).astype(BF16)
    return jnp.dot(hi, m, preferred_element_type=F32) + jnp.dot(lo, m, preferred_element_type=F32)


def _segsum_raw(x):
    red, exp = _seg_consts(x.shape[1])
    return _split_dot(_split_dot(x, red), exp)


@jax.custom_vjp
def _segsum(x):
    return _segsum_raw(x)


_segsum.defvjp(lambda x: (_segsum_raw(x), None), lambda _, g: (_segsum_raw(g),))


def _f_pre(out_dtype):
    def fn(pos, x, g):
        return (_rms(x, g).astype(out_dtype),)
    return fn


def _f_post(pos, x, t, g):
    return (x + _rms(t, g),)


def _f_post_pre(out_dtype, has_bias):
    def fn(pos, x, t, g_post, g_pre, *bias):
        x1 = x + _rms(t + bias[0] if has_bias else t, g_post)
        return x1, _rms(x1, g_pre).astype(out_dtype)
    return fn


def _f_conv(pos, proj, p1, p2, p3, b_u, w0, w1, w2, w3, cb):
    D = b_u.shape[1]
    u0 = proj[:, D:] + b_u
    u1 = jnp.where(pos >= 1, p1[:, D:] + b_u, 0.0)
    u2 = jnp.where(pos >= 2, p2[:, D:] + b_u, 0.0)
    u3 = jnp.where(pos >= 3, p3[:, D:] + b_u, 0.0)
    return (cb + u3 * w0 + u2 * w1 + u1 * w2 + u0 * w3,)


def _f_conv_bwd(pos, dconv, proj, dyb, n1, n2, n3, p1, p2, p3, b_u, w0, w1, w2, w3):
    D = b_u.shape[1]
    du = dconv * w3 + n1 * w2 + n2 * w1 + n3 * w0
    dproj = jnp.concatenate([dyb[:, :D], du], axis=1)
    u0 = proj[:, D:] + b_u
    u1 = jnp.where(pos >= 1, p1[:, D:] + b_u, 0.0)
    u2 = jnp.where(pos >= 2, p2[:, D:] + b_u, 0.0)
    u3 = jnp.where(pos >= 3, p3[:, D:] + b_u, 0.0)
    rs = lambda z: jnp.sum(z, axis=0, keepdims=True)
    return (dproj, rs(dproj), rs(dconv * u3), rs(dconv * u2), rs(dconv * u1), rs(dconv * u0), rs(dconv))


def _f_gates(pos, conv, proj, gate_w, gb_r, gb_i, lam, b_y):
    D = conv.shape[1]
    blk = D // LRU_HEADS
    cb = conv.astype(BF16)

    def gate(g, bias):
        z = [jnp.dot(cb[:, h * blk:(h + 1) * blk], gate_w[(g * LRU_HEADS + h) * blk:(g * LRU_HEADS + h + 1) * blk, :].astype(BF16),
                     preferred_element_type=F32) for h in range(LRU_HEADS)]
        return jax.nn.sigmoid(jnp.concatenate(z, axis=1) + bias)

    r_gate, i_gate = gate(0, gb_r), gate(1, gb_i)
    log_a = -LRU_C * r_gate * _softplus(-lam)
    a = jnp.exp(log_a)
    b = jnp.sqrt(_neg_expm1(2.0 * log_a)) * i_gate * conv
    yb = _gelu_tanh(proj[:, :D] + b_y)
    return a, b, yb


def _f_hy(pos, h, yb):
    return ((h * yb).astype(BF16),)


def _f_mix(pos, hn, hp, mu_r, mu_w, mu_k, mu_v, mu_a, mu_g, w0, w1, w2, a0, a1, a2, g1, g2):
    xx = hp - hn
    xr, xw, xk, xv, xa, xg = (hn + xx * m for m in (mu_r, mu_w, mu_k, mu_v, mu_a, mu_g))
    lw = w0 + _bdot(jnp.tanh(_bdot(xw, w1)), w2)
    decay = jnp.exp(-jnp.exp(-_softplus(-lw) - 0.5))
    a = jax.nn.sigmoid(a0 + _bdot(_bdot(xa, a1), a2))
    g = _bdot(jax.nn.sigmoid(_bdot(xg, g1)), g2)
    return xr.astype(BF16), xk.astype(BF16), xv.astype(BF16), decay, a, g


def _f_kk(pos, k, a, k_k, k_a):
    kk = k * k_k
    kk = kk / jnp.maximum(jnp.sqrt(_segsum(kk * kk)), 1e-12)
    return k * (1.0 + (a - 1.0) * k_a), -kk, kk * a


def _f_gn(pos, y, r, k2, v, g, gn_g, gn_b, r_k):
    inv_n = 1.0 / RWKV_N
    yc = y - _segsum(y) * inv_n
    var = _segsum(yc * yc) * inv_n
    yn = yc * lax.rsqrt(var + GN_EPS) * gn_g + gn_b
    bonus = _segsum(r * k2 * r_k) * v
    return (((yn + bonus) * g).astype(BF16),)


def _f_attn(pos, q, kv):
    D = q.shape[1]
    hd = D // XATTN_HEADS
    qb, kb, vb = q.astype(BF16), kv[:, :D].astype(BF16), kv[:, D:].astype(BF16)
    outs = []
    for h in range(XATTN_HEADS):
        sl = slice(h * hd, (h + 1) * hd)
        s = lax.dot_general(qb[:, sl], kb[:, sl], (((1,), (1,)), ((), ())), preferred_element_type=F32) * (hd ** -0.5)
        s = s - jnp.max(s, axis=-1, keepdims=True)
        e = jnp.exp(s)
        p = e / jnp.sum(e, axis=-1, keepdims=True)
        outs.append(jnp.dot(p.astype(BF16), vb[:, sl], preferred_element_type=F32))
    return (jnp.concatenate(outs, axis=1).astype(BF16),)


def _f_loss(pos, y, target):
    d = y - target
    inv = 1.0 / y.shape[1]
    part = 0.5 * inv * jnp.sum(jnp.sum(d * d, axis=1, keepdims=True), axis=0, keepdims=True)
    return d * inv, jnp.broadcast_to(part, (8, LANES))


def _full_from_blocks(blocks, shard_shape, ax):
    g = jnp.moveaxis(blocks.reshape((N_DEV,) + tuple(shard_shape)), 0, ax)
    return g.reshape(tuple(shard_shape[:ax]) + (N_DEV * shard_shape[ax],) + tuple(shard_shape[ax + 1:]))


def _blocks_from_full(full, shard_shape, ax):
    g = full.reshape(tuple(shard_shape[:ax]) + (N_DEV, shard_shape[ax]) + tuple(shard_shape[ax + 1:]))
    return jnp.moveaxis(g, ax, 0).reshape(N_DEV, -1)


def _numel(shape):
    n = 1
    for s in shape:
        n *= s
    return n


def _pack_flat(arrs, total):
    flat = jnp.concatenate([a.reshape(-1).astype(F32) for a in arrs])
    return jnp.pad(flat, (0, total - flat.shape[0]))


def _train_step(x3, mem3, target3, P, M_, V_):
    Bl, S, D = x3.shape
    T = Bl * S
    ML = mem3.shape[1]
    x = x3.reshape(T, D)
    mem = mem3.reshape(Bl * ML, D)
    target = target3.reshape(T, D)
    shp = {n: P[n].shape for n in WEIGHTS}

    n_big = sum(_numel(shp[n]) for n in BIG)
    n_small = sum(_numel(shp[n]) for n in SMALL)
    n_small_pad = -(-n_small // (PACK_ROWS * PACK_W)) * (PACK_ROWS * PACK_W)
    big_rows, small_rows = n_big // PACK_W, n_small_pad // PACK_W
    def pack_rows(src):
        big = [src[n].reshape(-1, PACK_W) for n in BIG]
        return jnp.concatenate(big + [_pack_flat([src[n] for n in SMALL], n_small_pad).reshape(small_rows, PACK_W)], axis=0)

    big_pack = jnp.concatenate([P[n].reshape(-1, PACK_W) for n in BIG], axis=0).astype(BF16)
    n_small_g = -(-n_small // (SUB * PACK_W)) * (SUB * PACK_W)
    small_pack = _pack_flat([P[n] for n in SMALL], n_small_g).reshape(n_small_g // PACK_W, PACK_W)
    big_all = _all_gather("gather_matrices", big_pack)
    small_all = _all_gather("gather_vectors", small_pack).reshape(N_DEV, n_small_g)
    Wf = {}
    big_row0 = {}
    off = 0
    for n in BIG:
        nr = _numel(shp[n]) // PACK_W
        big_row0[n] = off
        Wf[n] = _full_from_blocks(big_all[:, off:off + nr], shp[n], SHARD_AXIS[n])
        off += nr
    off = 0
    for n in SMALL:
        ne = _numel(shp[n])
        Wf[n] = _full_from_blocks(small_all[:, off:off + ne], shp[n], SHARD_AXIS[n])
        off += ne
    for n in REPL:
        Wf[n] = P[n]

    row = lambda v: v.reshape(1, -1).astype(F32)
    gains = [[row(Wf['ln_gains'][i, j]) for j in range(6)] for i in range(2)]
    LW = LANES

    def pad_cols(w):
        return jnp.pad(w, ((0, 0), (0, LW - w.shape[1]))).astype(F32)

    def pad_rows(w):
        return jnp.pad(w, ((0, LW - w.shape[0]), (0, 0))).astype(F32)

    G = {}
    saved = {}

    mem_n, = _rw("mem_norm_fwd", _f_pre(BF16), [mem], [row(Wf['mem_norm'])], outs=[(D, BF16)])

    def post_pre(name, xin, t, g_post, g_pre, out_dtype, bias=()):
        return _rw(name, _f_post_pre(out_dtype, bool(bias)), [xin, t], [g_post, g_pre, *bias],
                   outs=[(D, F32), (D, out_dtype)], seq=S)

    def post_pre_bwd(name, xin, t, g_post, g_pre, dx_next, dhn, bias=(), ct_adds=()):
        return _rw_vjp(name, _f_post_pre(F32, bool(bias)), [xin, t], [g_post, g_pre, *bias], [dx_next, dhn], 2,
                       d_rows=[(0, F32), (1, BF16)], d_params=list(range(2 + len(bias))), ct_adds=ct_adds, seq=S)

    def xattn_fwd(i, hq):
        q = _mm_nn(f"xattn{i}_q", hq, Wf['c_w_q'][i], BF16)
        kv = _mm_nn(f"xattn{i}_kv", mem_n, Wf['c_w_kv'][i], F32)
        kv = kv.reshape(Bl, ML, 2 * D)
        o, = _rw(f"xattn{i}_attn", _f_attn, [q], [kv], outs=[(D, BF16)], seq=S, per_b=(0,))
        saved[f"xattn{i}"] = (hq, q, kv, o)
        return _mm_nn(f"xattn{i}_o", o, Wf['c_w_o'][i], F32)

    def mlp_fwd(i, hm):
        up, act = _mm_up_act(f"mlp{i}_up", hm, Wf['m_w_up'][i])
        saved[f"mlp{i}"] = (hm, up, act)
        return _mm_nn(f"mlp{i}_down", act, Wf['m_w_down'][i], F32)

    b_in = row(Wf['a_b_in'][0])
    b_y, b_u = b_in[:, :D], b_in[:, D:]
    cw = [row(Wf['a_conv_w'][0, t]) for t in range(4)]
    cb = row(Wf['a_conv_b'][0])
    gate_w = Wf['a_gate_w'][0].reshape(2 * LRU_HEADS * (D // LRU_HEADS), D // LRU_HEADS).astype(F32)
    gate_b = Wf['a_gate_b'][0].reshape(2, D).astype(F32)
    gb_r, gb_i = gate_b[0:1], gate_b[1:2]
    lam = row(Wf['a_lambda'][0])
    b_out = row(Wf['a_b_out'][0])

    hn0, = _rw("lru_pre", _f_pre(BF16), [x], [gains[0][0]], outs=[(D, BF16)], seq=S)
    proj = _mm_nn("lru_in", hn0, Wf['a_w_in'][0], F32)
    prev3 = [(0, 1), (0, 2), (0, 3)]
    conv, = _rw("lru_conv", _f_conv, [proj], [b_u] + cw + [cb], outs=[(D, F32)], shifts=prev3, seq=S)
    gate_par = [gate_w, gb_r, gb_i, lam, b_y]
    a_l, b_l, yb = _rw("lru_gates", _f_gates, [conv, proj], gate_par, outs=[(D, F32)] * 3, seq=S)
    h_l = _lru_scan("lru_scan", a_l, b_l, S)
    hy, = _rw("lru_hy", _f_hy, [h_l, yb], outs=[(D, BF16)], seq=S)
    t0 = _mm_nn("lru_out", hy, Wf['a_w_out'][0], F32)
    x1, hq0 = post_pre("lru_post", x, t0, gains[0][1], gains[0][2], BF16, bias=(b_out,))
    c0 = xattn_fwd(0, hq0)
    x2, hm0 = post_pre("xattn0_post", x1, c0, gains[0][3], gains[0][4], BF16)
    m0 = mlp_fwd(0, hm0)
    x3_, hn1 = post_pre("mlp0_post", x2, m0, gains[0][5], gains[1][0], F32)

    mu = [row(Wf['b_mu'][0, j]) for j in range(6)]
    lora = [row(Wf['b_w0'][0]), pad_cols(Wf['b_w1'][0]), pad_rows(Wf['b_w2'][0]),
            row(Wf['b_a0'][0]), pad_cols(Wf['b_a1'][0]), pad_rows(Wf['b_a2'][0]),
            Wf['b_g1'][0].astype(F32), Wf['b_g2'][0].astype(F32)]
    k_k, k_a = row(Wf['b_k_k'][0]), row(Wf['b_k_a'][0])
    gn_g, gn_b, r_k = row(Wf['b_gn_g'][0]), row(Wf['b_gn_b'][0]), row(Wf['b_r_k'][0])

    xr, xk, xv, decay, a_r, g_r = _rw("rwkv_mix", _f_mix, [hn1], mu + lora,
                                      outs=[(D, BF16)] * 3 + [(D, F32)] * 3, shifts=[(0, 1)], seq=S)
    r_ = _mm_nn("rwkv_r", xr, Wf['b_w_rkv'][0, 0], F32)
    k_ = _mm_nn("rwkv_k", xk, Wf['b_w_rkv'][0, 1], F32)
    v_ = _mm_nn("rwkv_v", xv, Wf['b_w_rkv'][0, 2], F32)
    k2, ra, rb = _rw("rwkv_kk", _f_kk, [k_, a_r], [k_k, k_a], outs=[(D, F32)] * 3, seq=S)
    y_r, states, sa_r, last_r = _rwkv_fwd("rwkv_scan", r_, decay, k2, v_, ra, rb, S)
    gn_par = [gn_g, gn_b, r_k]
    og, = _rw("rwkv_gn", _f_gn, [y_r, r_, k2, v_, g_r], gn_par, outs=[(D, BF16)], seq=S)
    t1 = _mm_nn("rwkv_out", og, Wf['b_w_o'][0], F32)
    x4, hq1 = post_pre("rwkv_post", x3_, t1, gains[1][1], gains[1][2], BF16)
    c1 = xattn_fwd(1, hq1)
    x5, hm1 = post_pre("xattn1_post", x4, c1, gains[1][3], gains[1][4], BF16)
    m1 = mlp_fwd(1, hm1)
    x6, = _rw("mlp1_post", _f_post, [x5, m1], [gains[1][5]], outs=[(D, F32)], seq=S)

    dx, loss_acc = _rw("loss", _f_loss, [x6, target], outs=[(D, F32)], accs=[(8, LANES)], seq=S)
    loss = lax.psum(loss_acc[0, 0], ("x", "y", "c"))

    dgain = [[None] * 6 for _ in range(2)]
    dmem_n = None

    def mlp_bwd(i, dm):
        hm, up, act = saved[f"mlp{i}"]
        G_down = _mm_tn(f"mlp{i}_down_dw", act, dm)
        dup = _mm_nt_act_bwd(f"mlp{i}_down_dx", dm, Wf['m_w_down'][i], up)
        G_up = _mm_tn(f"mlp{i}_up_dw", hm, dup)
        return _mm_nt(f"mlp{i}_up_dx", dup, Wf['m_w_up'][i], F32), G_up, G_down

    def xattn_bwd(i, dc):
        hq, q, kv, o = saved[f"xattn{i}"]
        G_o = _mm_tn(f"xattn{i}_o_dw", o, dc)
        do = _mm_nt(f"xattn{i}_o_dx", dc, Wf['c_w_o'][i], F32)
        dq, dkv = _rw_vjp(f"xattn{i}_attn_bwd", _f_attn, [q], [kv], [do], 1, d_rows=[(0, BF16)],
                          d_params=[0], seq=S, per_b=(0,))
        dkv = dkv.reshape(Bl * ML, 2 * D)
        G_kv = _mm_tn(f"xattn{i}_kv_dw", mem_n, dkv)
        dmn = _mm_nt(f"xattn{i}_kv_dx", dkv, Wf['c_w_kv'][i], F32)
        G_q = _mm_tn(f"xattn{i}_q_dw", hq, dq)
        dhq = _mm_nt(f"xattn{i}_q_dx", dq, Wf['c_w_q'][i], F32)
        return dhq, G_q, G_kv, G_o, dmn

    G_up, G_down, G_q, G_kv, G_o = [None] * 2, [None] * 2, [None] * 2, [None] * 2, [None] * 2
    dm1, dgain[1][5] = _rw_vjp("mlp1_post_bwd", _f_post, [x5, m1], [gains[1][5]], [dx], 1,
                               d_rows=[(1, BF16)], d_params=[0], seq=S)
    dhm1, G_up[1], G_down[1] = mlp_bwd(1, dm1)
    dx, dc1, dgain[1][3], dgain[1][4] = post_pre_bwd("xattn1_post_bwd", x4, c1, gains[1][3], gains[1][4], dx, dhm1)
    dhq1, G_q[1], G_kv[1], G_o[1], dmn1 = xattn_bwd(1, dc1)

    dx, dt1, dgain[1][1], dgain[1][2] = post_pre_bwd("rwkv_post_bwd", x3_, t1, gains[1][1], gains[1][2], dx, dhq1)
    G['b_w_o'] = _mm_tn("rwkv_out_dw", og, dt1)[None]
    dog = _mm_nt("rwkv_out_dx", dt1, Wf['b_w_o'][0], F32)
    dy_r, dr_p, dk2_p, dv_p, dg_r, d_gn_g, d_gn_b, d_r_k = _rw_vjp(
        "rwkv_gn_bwd", _f_gn, [y_r, r_, k2, v_, g_r], gn_par, [dog], 1,
        d_rows=[(j, F32) for j in range(5)], d_params=[0, 1, 2], seq=S)
    dr_, ddecay, dk2, dv_, dra, drb = _rwkv_bwd("rwkv_scan_bwd", r_, decay, k2, v_, ra, rb, states, sa_r, last_r, dy_r,
                                                dr_p, dk2_p, dv_p, S)
    dk_, da_r, d_k_k, d_k_a = _rw_vjp("rwkv_kk_bwd", _f_kk, [k_, a_r], [k_k, k_a], [dk2, dra, drb], 3,
                                      d_rows=[(0, F32), (1, F32)], d_params=[0, 1], seq=S)
    G_rkv = [_mm_tn("rwkv_r_dw", xr, dr_), _mm_tn("rwkv_k_dw", xk, dk_), _mm_tn("rwkv_v_dw", xv, dv_)]
    G['b_w_rkv'] = jnp.stack(G_rkv)[None]
    dxr = _mm_nt("rwkv_r_dx", dr_, Wf['b_w_rkv'][0, 0], F32)
    dxk = _mm_nt("rwkv_k_dx", dk_, Wf['b_w_rkv'][0, 1], F32)
    dxv = _mm_nt("rwkv_v_dx", dv_, Wf['b_w_rkv'][0, 2], F32)
    mix_out = _rw_vjp("rwkv_mix_bwd", _f_mix, [hn1], mu + lora, [dxr, dxk, dxv, ddecay, da_r, dg_r], 6,
                      d_rows=[(0, F32)], d_shifts=[(0, F32)], d_params=list(range(14)), shifts=[(0, 1)], seq=S)
    dhn1, dhp1 = mix_out[0], mix_out[1]
    d_mu = mix_out[2:8]
    d_lora = mix_out[8:16]
    dx, dm0, dgain[0][5], dgain[1][0] = post_pre_bwd("mlp0_post_bwd", x2, m0, gains[0][5], gains[1][0], dx, dhn1,
                                                     ct_adds=[(1, dhp1, -1)])
    G['b_mu'] = jnp.concatenate(d_mu, axis=0)[None]
    G['b_w0'], G['b_a0'] = d_lora[0], d_lora[3]
    G['b_w1'] = d_lora[1][:, :shp['b_w1'][2]][None]
    G['b_w2'] = d_lora[2][:shp['b_w2'][1]][None]
    G['b_a1'] = d_lora[4][:, :shp['b_a1'][2]][None]
    G['b_a2'] = d_lora[5][:shp['b_a2'][1]][None]
    G['b_g1'], G['b_g2'] = d_lora[6][None], d_lora[7][None]
    G['b_k_k'], G['b_k_a'], G['b_gn_g'], G['b_gn_b'] = d_k_k, d_k_a, d_gn_g, d_gn_b
    G['b_r_k'] = d_r_k.reshape(P['b_r_k'].shape)

    dhm0, G_up[0], G_down[0] = mlp_bwd(0, dm0)
    dx, dc0, dgain[0][3], dgain[0][4] = post_pre_bwd("xattn0_post_bwd", x1, c0, gains[0][3], gains[0][4], dx, dhm0)
    dhq0, G_q[0], G_kv[0], G_o[0], dmn0 = xattn_bwd(0, dc0)
    G['m_w_up'], G['m_w_down'] = jnp.stack(G_up), jnp.stack(G_down)
    G['c_w_q'], G['c_w_kv'], G['c_w_o'] = jnp.stack(G_q), jnp.stack(G_kv), jnp.stack(G_o)

    dx, dt0, dgain[0][1], dgain[0][2], G['a_b_out'] = post_pre_bwd("lru_post_bwd", x, t0, gains[0][1], gains[0][2],
                                                                  dx, dhq0, bias=(b_out,))
    G['a_w_out'] = _mm_tn("lru_out_dw", hy, dt0)[None]
    dhy = _mm_nt("lru_out_dx", dt0, Wf['a_w_out'][0], F32)
    dh_l, dyb = _rw_vjp("lru_hy_bwd", _f_hy, [h_l, yb], [], [dhy], 1, d_rows=[(0, F32), (1, F32)], seq=S)
    da_l, db_l = _lru_scan_bwd("lru_scan_bwd", a_l, h_l, dh_l, S)
    dconv, dproj_y, d_gate_w, d_gb_r, d_gb_i, G['a_lambda'] = _rw_vjp(
        "lru_gates_bwd", _f_gates, [conv, proj], gate_par, [da_l, db_l, dyb], 3,
        d_rows=[(0, F32), (1, F32)], d_params=[0, 1, 2, 3], seq=S)
    next3 = [(0, -1), (0, -2), (0, -3)]
    dproj, G['a_b_in'], dw0, dw1, dw2, dw3, G['a_conv_b'] = _rw(
        "lru_conv_bwd", _f_conv_bwd, [dconv, proj, dproj_y], [b_u] + cw, outs=[(2 * D, BF16)],
        accs=[(1, 2 * D)] + [(1, D)] * 5, shifts=next3 + [(1, 1), (1, 2), (1, 3)], seq=S)
    G['a_conv_w'] = jnp.concatenate([dw0, dw1, dw2, dw3], axis=0)[None]
    G['a_gate_w'] = d_gate_w.reshape(P['a_gate_w'].shape[:3] + (D // LRU_HEADS, D // LRU_HEADS))
    G['a_gate_b'] = jnp.concatenate([d_gb_r, d_gb_i], axis=0).reshape(1, 2, LRU_HEADS, D // LRU_HEADS)
    G['a_w_in'] = _mm_tn("lru_in_dw", hn0, dproj)[None]
    dhn0 = _mm_nt("lru_in_dx", dproj, Wf['a_w_in'][0], F32)
    grad_x, dgain[0][0] = _rw_vjp("lru_pre_bwd", _f_pre(F32), [x], [gains[0][0]], [dhn0], 1, d_rows=[(0, F32)],
                                  d_params=[0], adds=[(0, dx, 0)], seq=S)
    d_mem_norm, = _rw_vjp("mem_norm_bwd", _f_pre(F32), [mem], [row(Wf['mem_norm'])], [dmn0], 1, d_params=[0],
                          ct_adds=[(0, dmn1, 0)])
    G['mem_norm'] = d_mem_norm.reshape(-1)
    G['ln_gains'] = jnp.stack([jnp.concatenate(dgain[i], axis=0) for i in range(2)])

    sharded = BIG + SMALL
    rows_s = big_rows + small_rows
    tr = max(t for t in range(BF16_ROWS, 161, BF16_ROWS) if rows_s % t == 0)

    pieces = [_blocks_from_full(G[n], shp[n], SHARD_AXIS[n]).reshape(N_DEV, -1, PACK_W) for n in BIG]
    small = [_blocks_from_full(G[n], shp[n], SHARD_AXIS[n]) for n in SMALL] + [jnp.zeros((N_DEV, n_small_pad - n_small), F32)]
    pieces.append(jnp.concatenate(small, axis=1).reshape(N_DEV, small_rows, PACK_W))
    g_blocks = jnp.concatenate(pieces, axis=1)
    n_split = max(q for q in range(1, 10) if rows_s % (q * BF16_ROWS) == 0)
    got = _swap_with_sibling("exchange_grads_sibling", g_blocks, n_split)
    tr_add = max(t for t in range(BF16_ROWS, 641, BF16_ROWS) if rows_s % t == 0)
    parts_s = _exchange_chips("exchange_grads_chips", _add_own_blocks("sum_with_sibling", g_blocks, got, tr_add, BF16))
    n_repl = sum(_numel(shp[n]) for n in REPL)
    n_repl_pad = -(-n_repl // (8 * PACK_W)) * (8 * PACK_W)
    rows_r = n_repl_pad // PACK_W
    parts_r = _all_gather("gather_replicated_grads", _pack_flat([G[n] for n in REPL], n_repl_pad).reshape(rows_r, PACK_W))

    def pack(src, names, total, rows_):
        return _pack_flat([src[n] for n in names], total).reshape(rows_, PACK_W)

    out_s = _adamw("adamw_sharded", parts_s, pack_rows(P), pack_rows(M_), pack_rows(V_), tr_add)
    out_r = _adamw("adamw_replicated", parts_r, pack(P, REPL, n_repl_pad, rows_r), pack(M_, REPL, n_repl_pad, rows_r),
                   pack(V_, REPL, n_repl_pad, rows_r), rows_r)

    def unpack(packed, names):
        flat = packed.reshape(-1)
        res, o = {}, 0
        for n in names:
            ne = _numel(shp[n])
            res[n] = flat[o:o + ne].reshape(shp[n])
            o += ne
        return res

    results = []
    for j in range(4):
        both = {n: out_s[j][big_row0[n]:big_row0[n] + _numel(shp[n]) // PACK_W].reshape(shp[n]) for n in BIG}
        both.update(unpack(out_s[j][big_rows:], SMALL))
        both.update(unpack(out_r[j], REPL))
        results += [both[n] for n in WEIGHTS]
    return (loss, grad_x.reshape(Bl, S, D), *results)


def kernel(x, mem, ln_gains, mem_norm, a_conv_w, a_conv_b, a_w_in, a_b_in, a_gate_w, a_gate_b, a_lambda, a_w_out, a_b_out, b_mu, b_w_rkv, b_w0, b_w1, b_w2, b_a0, b_a1, b_a2, b_g1, b_g2, b_k_k, b_k_a, b_r_k, b_gn_g, b_gn_b, b_w_o, c_w_q, c_w_kv, c_w_o, m_w_up, m_w_down, loss_target, m_ln_gains, m_mem_norm, m_a_conv_w, m_a_conv_b, m_a_w_in, m_a_b_in, m_a_gate_w, m_a_gate_b, m_a_lambda, m_a_w_out, m_a_b_out, m_b_mu, m_b_w_rkv, m_b_w0, m_b_w1, m_b_w2, m_b_a0, m_b_a1, m_b_a2, m_b_g1, m_b_g2, m_b_k_k, m_b_k_a, m_b_r_k, m_b_gn_g, m_b_gn_b, m_b_w_o, m_c_w_q, m_c_w_kv, m_c_w_o, m_m_w_up, m_m_w_down, v_ln_gains, v_mem_norm, v_a_conv_w, v_a_conv_b, v_a_w_in, v_a_b_in, v_a_gate_w, v_a_gate_b, v_a_lambda, v_a_w_out, v_a_b_out, v_b_mu, v_b_w_rkv, v_b_w0, v_b_w1, v_b_w2, v_b_a0, v_b_a1, v_b_a2, v_b_g1, v_b_g2, v_b_k_k, v_b_k_a, v_b_r_k, v_b_gn_g, v_b_gn_b, v_b_w_o, v_c_w_q, v_c_w_kv, v_c_w_o, v_m_w_up, v_m_w_down):
    weights = (ln_gains, mem_norm, a_conv_w, a_conv_b, a_w_in, a_b_in, a_gate_w, a_gate_b, a_lambda, a_w_out, a_b_out, b_mu, b_w_rkv, b_w0, b_w1, b_w2, b_a0, b_a1, b_a2, b_g1, b_g2, b_k_k, b_k_a, b_r_k, b_gn_g, b_gn_b, b_w_o, c_w_q, c_w_kv, c_w_o, m_w_up, m_w_down)
    moments1 = (m_ln_gains, m_mem_norm, m_a_conv_w, m_a_conv_b, m_a_w_in, m_a_b_in, m_a_gate_w, m_a_gate_b, m_a_lambda, m_a_w_out, m_a_b_out, m_b_mu, m_b_w_rkv, m_b_w0, m_b_w1, m_b_w2, m_b_a0, m_b_a1, m_b_a2, m_b_g1, m_b_g2, m_b_k_k, m_b_k_a, m_b_r_k, m_b_gn_g, m_b_gn_b, m_b_w_o, m_c_w_q, m_c_w_kv, m_c_w_o, m_m_w_up, m_m_w_down)
    moments2 = (v_ln_gains, v_mem_norm, v_a_conv_w, v_a_conv_b, v_a_w_in, v_a_b_in, v_a_gate_w, v_a_gate_b, v_a_lambda, v_a_w_out, v_a_b_out, v_b_mu, v_b_w_rkv, v_b_w0, v_b_w1, v_b_w2, v_b_a0, v_b_a1, v_b_a2, v_b_g1, v_b_g2, v_b_k_k, v_b_k_a, v_b_r_k, v_b_gn_g, v_b_gn_b, v_b_w_o, v_c_w_q, v_c_w_kv, v_c_w_o, v_m_w_up, v_m_w_down)
    return _train_step(x, mem, loss_target, dict(zip(WEIGHTS, weights)), dict(zip(WEIGHTS, moments1)),
                       dict(zip(WEIGHTS, moments2)))
```
